```python
import math
import jax, jax.numpy as jnp
from jax import lax
import numpy as np

D_MODEL = 2048
BATCH = 2
SEQ = 8192
DEPTH = 1
DEC_BATCH = 128
DEC_SEQ = 1
PAST_LEN = 16384
PAGE_SIZE = 128

PLE_DIM = 256
MIX_WIDTH = D_MODEL
ATTN_WIDTH = MIX_WIDTH // 2
HEAD_DIM = 64
N_HEADS = ATTN_WIDTH // HEAD_DIM
N_KV_HEADS = 4
Q_PER_KV = N_HEADS // N_KV_HEADS
WINDOW = 128
ROPE_DIM = HEAD_DIM // 4
ROPE_THETA = 500000.0
SSM_INNER = MIX_WIDTH - ATTN_WIDTH
SSM_HEAD_DIM = 64
SSM_HEADS = SSM_INNER // SSM_HEAD_DIM
SSM_GROUPS = 4
SSM_HPG = SSM_HEADS // SSM_GROUPS
SSM_STATE = 128
CONV_WIDTH = 4
CONV_DIM = SSM_INNER + 2 * SSM_GROUPS * SSM_STATE
SSD_CHUNK = 128
FFN_HIDDEN = -(-8 * D_MODEL // 768) * 256
Q_DIM = N_HEADS * HEAD_DIM
KV_DIM = N_KV_HEADS * HEAD_DIM
IN_DIM = Q_DIM + 2 * KV_DIM + SSM_INNER + CONV_DIM + SSM_HEADS
RMS_EPS = 1e-6

kernel_name = "hybrid_swa_sink_ssd_step"


def rmsnorm(x, g):
    xf = x.astype(jnp.float32)
    xf = xf * lax.rsqrt(jnp.mean(xf * xf, axis=-1, keepdims=True) + RMS_EPS)
    return xf.astype(x.dtype) * g


def partial_rope(x, pos):
    half = ROPE_DIM // 2
    inv = ROPE_THETA ** (-jnp.arange(half, dtype=jnp.float32) * (2.0 / ROPE_DIM))
    ang = pos.astype(jnp.float32)[:, None] * inv[None, :]
    cos = jnp.cos(ang)[:, None, :].astype(x.dtype)
    sin = jnp.sin(ang)[:, None, :].astype(x.dtype)
    x1, x2, rest = x[..., :half], x[..., half:ROPE_DIM], x[..., ROPE_DIM:]
    return jnp.concatenate([x1 * cos - x2 * sin, x2 * cos + x1 * sin, rest], axis=-1)


def sink_attend(q, k, v, mask, sinks):
    s = jnp.einsum('...qhrd,...khd->...hrqk', q, k).astype(jnp.float32) * (HEAD_DIM ** -0.5)
    s = jnp.where(mask, s, -jnp.inf)
    sink = sinks.astype(jnp.float32).reshape(N_KV_HEADS, Q_PER_KV)[:, :, None, None]
    m = jnp.maximum(jnp.max(s, axis=-1, keepdims=True), sink)
    e = jnp.exp(s - m)
    p = e / (jnp.sum(e, axis=-1, keepdims=True) + jnp.exp(sink - m))
    return jnp.einsum('...hrqk,...khd->...qhrd', p.astype(v.dtype), v)


def window_attention_prompt(q, k, v, sinks):
    b, L = q.shape[:2]
    nb = L // WINDOW
    qb = q.reshape(b, nb, WINDOW, N_KV_HEADS, Q_PER_KV, HEAD_DIM)
    kb = k.reshape(b, nb, WINDOW, N_KV_HEADS, HEAD_DIM)
    vb = v.reshape(b, nb, WINDOW, N_KV_HEADS, HEAD_DIM)
    kk = jnp.concatenate([jnp.concatenate([jnp.zeros_like(kb[:, :1]), kb[:, :-1]], axis=1), kb], axis=2)
    vv = jnp.concatenate([jnp.concatenate([jnp.zeros_like(vb[:, :1]), vb[:, :-1]], axis=1), vb], axis=2)
    blk = jnp.arange(nb)[:, None, None]
    i = jnp.arange(WINDOW)[None, :, None]
    j = jnp.arange(2 * WINDOW)[None, None, :]
    rel = i + WINDOW - j
    mask = (rel >= 0) & (rel < WINDOW) & ((blk > 0) | (j >= WINDOW))
    o = sink_attend(qb, kk, vv, mask[:, None, None], sinks)
    return o.reshape(b, L, Q_DIM), k[:, -WINDOW:], v[:, -WINDOW:]


def window_attention_sample(q, k, v, cache_k, cache_v, sinks):
    T = q.shape[1]
    qg = q.reshape(q.shape[0], T, N_KV_HEADS, Q_PER_KV, HEAD_DIM)
    kk = jnp.concatenate([cache_k, k], axis=1)
    vv = jnp.concatenate([cache_v, v], axis=1)
    qpos = PAST_LEN + jnp.arange(T)
    kpos = jnp.concatenate([PAST_LEN - WINDOW + jnp.arange(WINDOW), PAST_LEN + jnp.arange(T)])
    rel = qpos[:, None] - kpos[None, :]
    mask = (rel >= 0) & (rel < WINDOW) & (kpos[None, :] >= 0)
    o = sink_attend(qg, kk, vv, mask, sinks)
    return o.reshape(q.shape[0], T, Q_DIM), kk[:, -WINDOW:], vv[:, -WINDOW:]


def causal_conv(xbc, conv_state, w, bias):
    L = xbc.shape[1]
    xp = jnp.concatenate([conv_state.astype(xbc.dtype), xbc], axis=1)
    out = sum(xp[:, t:t + L] * w[t] for t in range(CONV_WIDTH)) + bias
    return jax.nn.silu(out), xp[:, -(CONV_WIDTH - 1):]


def ssd_scan(x, dt, a, b_in, c_in, d_skip, state0):
    f32 = jnp.float32
    in_dtype, st_dtype = x.dtype, state0.dtype
    x, dt, b_in, c_in = x.astype(f32), dt.astype(f32), b_in.astype(f32), c_in.astype(f32)
    bsz, L = x.shape[:2]
    Q = min(SSD_CHUNK, L)
    nc = -(-L // Q)
    pad = nc * Q - L
    if pad:
        padf = lambda t: jnp.pad(t, [(0, 0), (0, pad)] + [(0, 0)] * (t.ndim - 2))
        x, dt, b_in, c_in = padf(x), padf(dt), padf(b_in), padf(c_in)
    xc = x.reshape(bsz, nc, Q, SSM_GROUPS, SSM_HPG, SSM_HEAD_DIM)
    dtc = dt.reshape(bsz, nc, Q, SSM_GROUPS, SSM_HPG)
    bc = b_in.reshape(bsz, nc, Q, SSM_GROUPS, SSM_STATE)
    cc = c_in.reshape(bsz, nc, Q, SSM_GROUPS, SSM_STATE)
    cs = jnp.cumsum(dtc * a.astype(f32).reshape(SSM_GROUPS, SSM_HPG), axis=2)
    xdt = xc * dtc[..., None]
    cs_t = jnp.moveaxis(cs, 2, -1)
    causal = jnp.tril(jnp.ones((Q, Q), dtype=bool))
    decay = jnp.exp(jnp.where(causal, cs_t[..., :, None] - cs_t[..., None, :], -jnp.inf))
    cb = jnp.einsum('bcign,bcjgn->bcgij', cc, bc)
    y_diag = jnp.einsum('bcgij,bcgrij,bcjgrp->bcigrp', cb, decay, xdt)
    decay_end = jnp.exp(cs[:, :, -1:] - cs)
    chunk_states = jnp.einsum('bcjgn,bcjgr,bcjgrp->bcgrpn', bc, decay_end, xdt)
    chunk_decay = jnp.exp(cs[:, :, -1])

    def step(carry, inp):
        st, dec = inp
        return carry * dec[..., None, None] + st, carry

    s0 = state0.astype(f32).reshape(bsz, SSM_GROUPS, SSM_HPG, SSM_HEAD_DIM, SSM_STATE)
    final, prev = lax.scan(step, s0, (jnp.moveaxis(chunk_states, 1, 0), jnp.moveaxis(chunk_decay, 1, 0)))
    prev = jnp.moveaxis(prev, 0, 1)
    y_off = jnp.einsum('bcign,bcgrpn,bcigr->bcigrp', cc, prev, jnp.exp(cs))
    y = y_diag + y_off + xc * d_skip.astype(f32).reshape(SSM_GROUPS, SSM_HPG)[..., None]
    y = y.reshape(bsz, nc * Q, SSM_INNER)[:, :L]
    return y.astype(in_dtype), final.reshape(bsz, SSM_HEADS, SSM_HEAD_DIM, SSM_STATE).astype(st_dtype)


def gated_group_rmsnorm(y, z, g):
    h = (y * jax.nn.silu(z)).astype(jnp.float32)
    hg = h.reshape(h.shape[:-1] + (SSM_GROUPS, SSM_INNER // SSM_GROUPS))
    hg = hg * lax.rsqrt(jnp.mean(hg * hg, axis=-1, keepdims=True) + RMS_EPS)
    return hg.reshape(h.shape).astype(y.dtype) * g


def trunk_layer(h, p, pos, ck, cv, ssm0, conv0, prompt,
                w_in, conv_w, conv_b, dt_bias, a_log, d_skip, ssm_norm_g, attn_sinks, w_out,
                g_mix, g_ffn, w_ffn_gate, w_ffn_up, w_ffn_down, g_ple, w_ple, w_ple_gate):
    b, L = h.shape[:2]
    u = rmsnorm(h, g_mix)
    proj = u @ w_in
    offs = np.cumsum([Q_DIM, KV_DIM, KV_DIM, SSM_INNER, CONV_DIM]).tolist()
    q, k, v, z, xbc, dt_raw = jnp.split(proj, offs, axis=-1)
    q = partial_rope(q.reshape(b, L, N_HEADS, HEAD_DIM), pos)
    k = partial_rope(k.reshape(b, L, N_KV_HEADS, HEAD_DIM), pos)
    v = v.reshape(b, L, N_KV_HEADS, HEAD_DIM)
    if prompt:
        attn, new_k, new_v = window_attention_prompt(q, k, v, attn_sinks)
    else:
        attn, new_k, new_v = window_attention_sample(q, k, v, ck, cv, attn_sinks)
    xbc, new_conv = causal_conv(xbc, conv0, conv_w, conv_b)
    xs, bs, cs = jnp.split(xbc, [SSM_INNER, SSM_INNER + SSM_GROUPS * SSM_STATE], axis=-1)
    dt = jax.nn.softplus(dt_raw + dt_bias)
    a = -jnp.exp(a_log)
    y, new_ssm = ssd_scan(xs.reshape(b, L, SSM_HEADS, SSM_HEAD_DIM), dt, a,
                          bs.reshape(b, L, SSM_GROUPS, SSM_STATE), cs.reshape(b, L, SSM_GROUPS, SSM_STATE),
                          d_skip, ssm0)
    y = gated_group_rmsnorm(y, z, ssm_norm_g)
    h = h + jnp.concatenate([attn, y], axis=-1) @ w_out
    f = rmsnorm(h, g_ffn)
    h = h + (jax.nn.silu(f @ w_ffn_gate) * (f @ w_ffn_up)) @ w_ffn_down
    h = h + (p @ w_ple) * jax.nn.sigmoid(rmsnorm(h, g_ple) @ w_ple_gate)
    return h, new_k, new_v, new_ssm, new_conv


def setup_inputs(seed: int = 0) -> dict:
    key = jax.random.key(seed)
    ks = jax.random.split(key, 32)
    nrm = lambda i, shape, s: jax.random.normal(ks[i], shape, jnp.float32) * s
    dt0 = jnp.exp(jax.random.uniform(ks[13], (DEPTH, SSM_HEADS), jnp.float32, math.log(1e-3), math.log(1e-1)))
    return {
        "x_prompt": nrm(0, (BATCH, SEQ, D_MODEL), 1.0),
        "x_sample": nrm(1, (DEC_BATCH, DEC_SEQ, D_MODEL), 1.0),
        "cache_k": nrm(2, (DEPTH, DEC_BATCH, WINDOW, N_KV_HEADS, HEAD_DIM), 1.0),
        "cache_v": nrm(3, (DEPTH, DEC_BATCH, WINDOW, N_KV_HEADS, HEAD_DIM), 1.0),
        "state_ssm": nrm(4, (DEPTH, DEC_BATCH, SSM_HEADS, SSM_HEAD_DIM, SSM_STATE), 0.5),
        "state_conv": nrm(5, (DEPTH, DEC_BATCH, CONV_WIDTH - 1, CONV_DIM), 1.0),
        "p_prompt": nrm(6, (DEPTH, BATCH, SEQ, PLE_DIM), 1.0),
        "p_sample": nrm(7, (DEPTH, DEC_BATCH, DEC_SEQ, PLE_DIM), 1.0),
        "w_in": nrm(8, (DEPTH, D_MODEL, IN_DIM), D_MODEL ** -0.5),
        "conv_w": nrm(9, (DEPTH, CONV_WIDTH, CONV_DIM), CONV_WIDTH ** -0.5),
        "conv_b": nrm(10, (DEPTH, CONV_DIM), 0.01),
        "dt_bias": dt0 + jnp.log(-jnp.expm1(-dt0)),
        "a_log": jnp.log(jax.random.uniform(ks[11], (DEPTH, SSM_HEADS), jnp.float32, 1.0, 16.0)),
        "d_skip": 1.0 + nrm(12, (DEPTH, SSM_HEADS), 0.1),
        "ssm_norm_g": 1.0 + nrm(14, (DEPTH, SSM_INNER), 0.05),
        "attn_sinks": nrm(15, (DEPTH, N_HEADS), 0.5),
        "w_out": nrm(16, (DEPTH, MIX_WIDTH, D_MODEL), MIX_WIDTH ** -0.5),
        "g_mix": 1.0 + nrm(17, (DEPTH, D_MODEL), 0.05),
        "g_ffn": 1.0 + nrm(18, (DEPTH, D_MODEL), 0.05),
        "w_ffn_gate": nrm(19, (DEPTH, D_MODEL, FFN_HIDDEN), D_MODEL ** -0.5),
        "w_ffn_up": nrm(20, (DEPTH, D_MODEL, FFN_HIDDEN), D_MODEL ** -0.5),
        "w_ffn_down": nrm(21, (DEPTH, FFN_HIDDEN, D_MODEL), FFN_HIDDEN ** -0.5),
        "g_ple": 1.0 + nrm(22, (DEPTH, D_MODEL), 0.05),
        "w_ple": nrm(23, (DEPTH, PLE_DIM, D_MODEL), PLE_DIM ** -0.5),
        "w_ple_gate": nrm(24, (DEPTH, D_MODEL, D_MODEL), D_MODEL ** -0.5),
        "g_final": 1.0 + nrm(25, (D_MODEL,), 0.05),
    }


def reference(x_prompt, x_sample, cache_k, cache_v, state_ssm, state_conv, p_prompt, p_sample,
              w_in, conv_w, conv_b, dt_bias, a_log, d_skip, ssm_norm_g, attn_sinks, w_out,
              g_mix, g_ffn, w_ffn_gate, w_ffn_up, w_ffn_down, g_ple, w_ple, w_ple_gate, g_final):
    pos_prompt = jnp.arange(SEQ)
    pos_sample = PAST_LEN + jnp.arange(DEC_SEQ)
    hp, hs = x_prompt, x_sample
    kp, vp, sp, cp = [], [], [], []
    ksm, vsm, ssm_s, csm = [], [], [], []
    for i in range(DEPTH):
        params = (w_in[i], conv_w[i], conv_b[i], dt_bias[i], a_log[i], d_skip[i], ssm_norm_g[i],
                  attn_sinks[i], w_out[i], g_mix[i], g_ffn[i], w_ffn_gate[i], w_ffn_up[i],
                  w_ffn_down[i], g_ple[i], w_ple[i], w_ple_gate[i])
        zero_ssm = jnp.zeros((BATCH, SSM_HEADS, SSM_HEAD_DIM, SSM_STATE), hp.dtype)
        zero_conv = jnp.zeros((BATCH, CONV_WIDTH - 1, CONV_DIM), hp.dtype)
        hp, k1, v1, s1, c1 = trunk_layer(hp, p_prompt[i], pos_prompt, None, None, zero_ssm, zero_conv,
                                         True, *params)
        hs, k2, v2, s2, c2 = trunk_layer(hs, p_sample[i], pos_sample, cache_k[i], cache_v[i],
                                         state_ssm[i], state_conv[i], False, *params)
        kp.append(k1); vp.append(v1); sp.append(s1); cp.append(c1)
        ksm.append(k2); vsm.append(v2); ssm_s.append(s2); csm.append(c2)
    y_prompt = rmsnorm(hp, g_final)
    y_sample = rmsnorm(hs, g_final)
    return (y_prompt, y_sample,
            jnp.stack(kp), jnp.stack(vp), jnp.stack(sp), jnp.stack(cp),
            jnp.stack(ksm), jnp.stack(vsm), jnp.stack(ssm_s), jnp.stack(csm))
```

```python
import functools

import jax
import jax.numpy as jnp
from jax import lax
from jax.experimental import pallas as pl
from jax.experimental.pallas import tpu as pltpu

F32 = jnp.float32
BF16 = jnp.bfloat16

HEAD_DIM = 64
N_HEADS = 16
N_KV_HEADS = 4
WINDOW = 128
ROPE_DIM = 16
ROPE_THETA = 500000.0
SSM_HEADS = 16
SSM_HEAD_DIM = 64
SSM_GROUPS = 4
SSM_STATE = 128
CONV_WIDTH = 4
SSD_CHUNK = 128
RMS_EPS = 1e-6
PAST_LEN = 16384

Q_DIM = N_HEADS * HEAD_DIM
KV_DIM = N_KV_HEADS * HEAD_DIM
SSM_INNER = SSM_HEADS * SSM_HEAD_DIM
BC_DIM = SSM_GROUPS * SSM_STATE
CONV_DIM = SSM_INNER + 2 * BC_DIM
LANES = 128
HEADS_PAD = LANES

COL_Q = 0
COL_Z = Q_DIM
COL_XBC = Q_DIM + SSM_INNER
COL_K = COL_XBC + CONV_DIM
COL_V = COL_K + KV_DIM
PROJ_DIM = COL_V + KV_DIM

VMEM_LIMIT = 56 * 1024 * 1024


def _cparams(n_axes):
    return pltpu.CompilerParams(
        dimension_semantics=("arbitrary",) * n_axes, vmem_limit_bytes=VMEM_LIMIT)


def _rms(x):
    return x * lax.rsqrt(jnp.mean(x * x, axis=-1, keepdims=True) + RMS_EPS)


def _sigmoid(x):
    return 1.0 / (1.0 + jnp.exp(-x))


def _silu(x):
    return x * _sigmoid(x)


def _dot(a, b):
    return jnp.dot(a, b, preferred_element_type=F32)


def _dot_nt(a, b):
    return lax.dot_general(a, b, (((1,), (1,)), ((), ())), preferred_element_type=F32)


def _dot_tn(a, b):
    return lax.dot_general(a, b, (((0,), (0,)), ((), ())), preferred_element_type=F32)


def _inproj_kernel(x_ref, g_ref, w_ref, wdt_ref, o_ref, odt_ref, u_ref):
    @pl.when(pl.program_id(1) == 0)
    def _():
        u = (_rms(x_ref[...]) * g_ref[...]).astype(BF16)
        u_ref[...] = u
        odt_ref[...] = _dot(u, wdt_ref[...])

    o_ref[...] = _dot(u_ref[...], w_ref[...])


def _inproj(x, g, w, wdt, tm, tn):
    t, d = x.shape
    n = w.shape[1]
    return pl.pallas_call(
        _inproj_kernel,
        grid=(t // tm, n // tn),
        in_specs=[
            pl.BlockSpec((tm, d), lambda i, j: (i, 0)),
            pl.BlockSpec((1, d), lambda i, j: (0, 0)),
            pl.BlockSpec((d, tn), lambda i, j: (0, j)),
            pl.BlockSpec((d, HEADS_PAD), lambda i, j: (0, 0)),
        ],
        out_specs=[
            pl.BlockSpec((tm, tn), lambda i, j: (i, j)),
            pl.BlockSpec((tm, HEADS_PAD), lambda i, j: (i, 0)),
        ],
        out_shape=[
            jax.ShapeDtypeStruct((t, n), F32),
            jax.ShapeDtypeStruct((t, HEADS_PAD), F32),
        ],
        scratch_shapes=[pltpu.VMEM((tm, d), BF16)],
        compiler_params=_cparams(2),
        name="inproj",
    )(x, g, w, wdt)


def _rope_tables(pos):
    half = ROPE_DIM // 2
    inv = ROPE_THETA ** (-jnp.arange(half, dtype=F32) * (2.0 / ROPE_DIM))
    ang = pos.astype(F32)[:, None] * inv[None, :]
    cos, sin = jnp.cos(ang), jnp.sin(ang)
    n = pos.shape[0]
    rest = HEAD_DIM - ROPE_DIM
    zh = jnp.zeros((n, half), F32)
    c = jnp.concatenate([cos, cos, jnp.ones((n, rest), F32)], axis=1)
    sa = jnp.concatenate([-sin, zh, jnp.zeros((n, rest), F32)], axis=1)
    sb = jnp.concatenate([zh, sin, jnp.zeros((n, rest), F32)], axis=1)
    rep = LANES // HEAD_DIM
    return jnp.concatenate([jnp.tile(c, (1, rep)), jnp.tile(sa, (1, rep)),
                            jnp.tile(sb, (1, rep))], axis=1)


def _rope(x, tab):
    half = ROPE_DIM // 2
    c, sa, sb = tab[:, :LANES], tab[:, LANES:2 * LANES], tab[:, 2 * LANES:]
    return x * c + pltpu.roll(x, LANES - half, 1) * sa + pltpu.roll(x, half, 1) * sb


def _rope_wide(x, tab):
    return jnp.concatenate(
        [_rope(x[:, c * LANES:(c + 1) * LANES], tab) for c in range(x.shape[1] // LANES)],
        axis=1)


def _softmax_sink(s, valid, sink):
    s = jnp.where(valid, s, -jnp.inf)
    m = jnp.maximum(jnp.max(s, axis=-1, keepdims=True), sink)
    e = jnp.exp(s - m)
    den = jnp.sum(e, axis=-1, keepdims=True) + jnp.exp(sink - m)
    return e.astype(BF16), den


def _attn_prompt_kernel(sink_ref, q_ref, kc_ref, kp_ref, vc_ref, vp_ref, tc_ref, tp_ref,
                        o_ref, nk_ref, nv_ref):
    i = pl.program_id(1)
    nb = pl.num_programs(1)
    w = WINDOW
    tc = tc_ref[...]
    kcr = _rope_wide(kc_ref[...], tc)
    kpr = _rope_wide(kp_ref[...], tp_ref[...])
    vc = vc_ref[...]

    @pl.when(i == nb - 1)
    def _():
        nk_ref[...] = kcr
        nv_ref[...] = vc

    kk = jnp.concatenate([kpr, kcr], axis=0)
    vv = jnp.concatenate([vp_ref[...], vc], axis=0)
    lo = lax.broadcasted_iota(jnp.int32, (2 * w, LANES), 1) < HEAD_DIM
    lo_q = lax.broadcasted_iota(jnp.int32, (w, LANES), 1) < HEAD_DIM
    qi = lax.broadcasted_iota(jnp.int32, (w, 2 * w), 0)
    kj = lax.broadcasted_iota(jnp.int32, (w, 2 * w), 1)
    first = (i == 0).astype(jnp.int32)
    valid = (kj > qi) & (kj <= qi + w) & (kj >= w * first)
    scale = HEAD_DIM ** -0.5

    for g in range(N_KV_HEADS):
        col, odd = g // 2, g % 2
        kg = kk[:, col * LANES:(col + 1) * LANES]
        vg = vv[:, col * LANES:(col + 1) * LANES]
        kg_sw = pltpu.roll(kg, HEAD_DIM, 1)
        vg_sw = pltpu.roll(vg, HEAD_DIM, 1)
        k_lo = jnp.where(lo, kg_sw if odd else kg, 0.0).astype(BF16)
        k_hi = jnp.where(lo, 0.0, kg if odd else kg_sw).astype(BF16)
        v_lo = jnp.where(lo, vg_sw if odd else vg, 0.0).astype(BF16)
        v_hi = jnp.where(lo, 0.0, vg if odd else vg_sw).astype(BF16)
        qa = _rope(q_ref[:, (2 * g) * LANES:(2 * g + 1) * LANES], tc) * scale
        qb = _rope(q_ref[:, (2 * g + 1) * LANES:(2 * g + 2) * LANES], tc) * scale
        qst = jnp.concatenate([qa, qb], axis=0).astype(BF16)
        s_lo = _dot_nt(qst, k_lo)
        s_hi = _dot_nt(qst, k_hi)
        e0, d0 = _softmax_sink(s_lo[:w], valid, sink_ref[4 * g])
        e1, d1 = _softmax_sink(s_hi[:w], valid, sink_ref[4 * g + 1])
        e2, d2 = _softmax_sink(s_lo[w:], valid, sink_ref[4 * g + 2])
        e3, d3 = _softmax_sink(s_hi[w:], valid, sink_ref[4 * g + 3])
        p = jnp.concatenate([jnp.concatenate([e0, e1], axis=1),
                             jnp.concatenate([e2, e3], axis=1)], axis=0)
        vcat = jnp.concatenate([v_lo, v_hi], axis=0)
        o = _dot(p, vcat)
        oa = o[:w] * jnp.where(lo_q, 1.0 / d0, 1.0 / d1)
        ob = o[w:] * jnp.where(lo_q, 1.0 / d2, 1.0 / d3)
        o_ref[:, (2 * g) * LANES:(2 * g + 1) * LANES] = oa.astype(o_ref.dtype)
        o_ref[:, (2 * g + 1) * LANES:(2 * g + 2) * LANES] = ob.astype(o_ref.dtype)


def _attn_prompt(proj, tab, sinks):
    b, l, _ = proj.shape
    w = WINDOW
    nb = l // w
    kcol, vcol = COL_K // KV_DIM, COL_V // KV_DIM
    prev = lambda bi, i: jnp.maximum(i - 1, 0)
    return pl.pallas_call(
        _attn_prompt_kernel,
        grid=(b, nb),
        in_specs=[
            pl.BlockSpec(memory_space=pltpu.SMEM),
            pl.BlockSpec((None, w, Q_DIM), lambda bi, i: (bi, i, COL_Q // Q_DIM)),
            pl.BlockSpec((None, w, KV_DIM), lambda bi, i: (bi, i, kcol)),
            pl.BlockSpec((None, w, KV_DIM), lambda bi, i: (bi, prev(bi, i), kcol)),
            pl.BlockSpec((None, w, KV_DIM), lambda bi, i: (bi, i, vcol)),
            pl.BlockSpec((None, w, KV_DIM), lambda bi, i: (bi, prev(bi, i), vcol)),
            pl.BlockSpec((w, 3 * LANES), lambda bi, i: (i, 0)),
            pl.BlockSpec((w, 3 * LANES), lambda bi, i: (prev(bi, i), 0)),
        ],
        out_specs=[
            pl.BlockSpec((None, w, Q_DIM), lambda bi, i: (bi, i, 0)),
            pl.BlockSpec((None, w, KV_DIM), lambda bi, i: (bi, 0, 0)),
            pl.BlockSpec((None, w, KV_DIM), lambda bi, i: (bi, 0, 0)),
        ],
        out_shape=[
            jax.ShapeDtypeStruct((b, l, Q_DIM), BF16),
            jax.ShapeDtypeStruct((b, w, KV_DIM), F32),
            jax.ShapeDtypeStruct((b, w, KV_DIM), F32),
        ],
        compiler_params=_cparams(2),
        name="attn_prompt",
    )(sinks, proj, proj, proj, proj, proj, tab, tab)


def _softplus(v):
    return jnp.maximum(v, 0.0) + jnp.log1p(jnp.exp(-jnp.abs(v)))


def _split3(x):
    hi = x.astype(BF16)
    r = x - hi.astype(F32)
    mid = r.astype(BF16)
    lo = (r - mid.astype(F32)).astype(BF16)
    return hi, mid, lo


def _dot_exact_lhs01(m01, x):
    hi, mid, lo = _split3(x)
    return _dot(m01, hi) + _dot(m01, mid) + _dot(m01, lo)


def _head_expand(col_vals, pair):
    rows = col_vals.shape[0]
    lo = lax.broadcasted_iota(jnp.int32, (rows, LANES), 1) < SSM_HEAD_DIM
    a = jnp.broadcast_to(col_vals[:, 2 * pair:2 * pair + 1], (rows, LANES))
    b = jnp.broadcast_to(col_vals[:, 2 * pair + 1:2 * pair + 2], (rows, LANES))
    return jnp.where(lo, a, b)


def _ssd_prompt_kernel(z_ref, x_ref, dt_ref, cw_ref, cb_ref, dtb_ref, alog_ref, dsk_ref, gn_ref,
                       y_ref, nssm_ref, nconv_ref, state_ref, carry_ref):
    i = pl.program_id(1)
    nc = pl.num_programs(1)
    q = SSD_CHUNK

    @pl.when(i == 0)
    def _():
        state_ref[...] = jnp.zeros_like(state_ref)
        carry_ref[...] = jnp.zeros_like(carry_ref)

    x = x_ref[...]
    prev = carry_ref[...]
    row8 = lax.broadcasted_iota(jnp.int32, (8, CONV_DIM), 0)

    def shifted(k):
        r = pltpu.roll(x, k, 0)
        head = jnp.where(row8 < k, pltpu.roll(prev, k, 0), r[:8])
        return jnp.concatenate([head, r[8:]], axis=0)

    conv = x * cw_ref[CONV_WIDTH - 1:CONV_WIDTH, :] + cb_ref[...]
    for k in range(1, CONV_WIDTH):
        conv = conv + shifted(k) * cw_ref[CONV_WIDTH - 1 - k:CONV_WIDTH - k, :]
    carry_ref[...] = x[q - 8:, :]

    @pl.when(i == nc - 1)
    def _():
        nconv_ref[...] = x[q - 8:, :]

    xc = _silu(conv)
    xs = xc[:, :SSM_INNER]
    bm = xc[:, SSM_INNER:SSM_INNER + BC_DIM]
    cm = xc[:, SSM_INNER + BC_DIM:]

    dt = _softplus(dt_ref[...] + dtb_ref[...])
    a = -jnp.exp(alog_ref[...])
    da = dt * a
    ri = lax.broadcasted_iota(jnp.int32, (q, q), 0)
    cj = lax.broadcasted_iota(jnp.int32, (q, q), 1)
    tri = ri >= cj
    cs = _dot_exact_lhs01(tri.astype(BF16), da)
    cs_t = cs.T
    dt_t = dt.T
    cs_last = cs[q - 1:q, :]
    ecs = jnp.exp(cs)
    wgt = dt * jnp.exp(cs_last - cs)
    cdec_t = jnp.exp(cs_t[:, q - 1:q])
    lo = lax.broadcasted_iota(jnp.int32, (q, LANES), 1) < SSM_HEAD_DIM

    hpg = SSM_HEADS // SSM_GROUPS
    gw = hpg * SSM_HEAD_DIM
    for g in range(SSM_GROUPS):
        bg = bm[:, g * SSM_STATE:(g + 1) * SSM_STATE].astype(BF16)
        cg = cm[:, g * SSM_STATE:(g + 1) * SSM_STATE].astype(BF16)
        cb = _dot_nt(cg, bg)
        st = state_ref[g * gw:(g + 1) * gw, :]
        yoff = _dot_nt(cg, st.astype(BF16))
        ys = []
        for pr in range(2):
            pair = 2 * g + pr
            ms = []
            for h in (2 * pair, 2 * pair + 1):
                diff = cs[:, h:h + 1] - cs_t[h:h + 1, :]
                lm = jnp.exp(jnp.where(tri, diff, -jnp.inf))
                ms.append((cb * lm * dt_t[h:h + 1, :]).astype(BF16))
            xp = xs[:, pair * LANES:(pair + 1) * LANES]
            x2 = jnp.concatenate([jnp.where(lo, xp, 0.0), jnp.where(lo, 0.0, xp)],
                                 axis=0).astype(BF16)
            yd = _dot(jnp.concatenate(ms, axis=1), x2)
            yo = yoff[:, pr * LANES:(pr + 1) * LANES] * _head_expand(ecs, pair)
            ys.append(yd + yo + xp * dsk_ref[:, pair * LANES:(pair + 1) * LANES])
        yg = jnp.concatenate(ys, axis=1)
        wx = jnp.concatenate(
            [xs[:, (2 * g + pr) * LANES:(2 * g + pr + 1) * LANES] * _head_expand(wgt, 2 * g + pr)
             for pr in range(2)], axis=1)
        s_new = _dot_tn(wx.astype(BF16), bg)
        for r in range(hpg):
            h = hpg * g + r
            rows = slice(g * gw + r * SSM_HEAD_DIM, g * gw + (r + 1) * SSM_HEAD_DIM)
            state_ref[rows, :] = (st[r * SSM_HEAD_DIM:(r + 1) * SSM_HEAD_DIM, :] * cdec_t[h:h + 1, :]
                                  + s_new[r * SSM_HEAD_DIM:(r + 1) * SSM_HEAD_DIM, :])
        hg = yg * _silu(z_ref[:, g * gw:(g + 1) * gw])
        y_ref[:, g * gw:(g + 1) * gw] = (_rms(hg) * gn_ref[:, g * gw:(g + 1) * gw]).astype(y_ref.dtype)

    @pl.when(i == nc - 1)
    def _():
        nssm_ref[...] = state_ref[...]


def _ssd_prompt(proj, dtraw, conv_w, conv_b, dt_bias, a_log, d_skip_x, norm_g):
    b, l, _ = proj.shape
    q = SSD_CHUNK
    nc = l // q
    const = lambda bi, i: (0, 0)
    return pl.pallas_call(
        _ssd_prompt_kernel,
        grid=(b, nc),
        in_specs=[
            pl.BlockSpec((None, q, SSM_INNER), lambda bi, i: (bi, i, COL_Z // SSM_INNER)),
            pl.BlockSpec((None, q, CONV_DIM), lambda bi, i: (bi, i, COL_XBC // CONV_DIM)),
            pl.BlockSpec((None, q, HEADS_PAD), lambda bi, i: (bi, i, 0)),
            pl.BlockSpec((CONV_WIDTH, CONV_DIM), const),
            pl.BlockSpec((1, CONV_DIM), const),
            pl.BlockSpec((1, HEADS_PAD), const),
            pl.BlockSpec((1, HEADS_PAD), const),
            pl.BlockSpec((1, SSM_INNER), const),
            pl.BlockSpec((1, SSM_INNER), const),
        ],
        out_specs=[
            pl.BlockSpec((None, q, SSM_INNER), lambda bi, i: (bi, i, 0)),
            pl.BlockSpec((None, SSM_INNER, SSM_STATE), lambda bi, i: (bi, 0, 0)),
            pl.BlockSpec((None, 8, CONV_DIM), lambda bi, i: (bi, 0, 0)),
        ],
        out_shape=[
            jax.ShapeDtypeStruct((b, l, SSM_INNER), BF16),
            jax.ShapeDtypeStruct((b, SSM_INNER, SSM_STATE), F32),
            jax.ShapeDtypeStruct((b, 8, CONV_DIM), F32),
        ],
        scratch_shapes=[pltpu.VMEM((SSM_INNER, SSM_STATE), F32), pltpu.VMEM((8, CONV_DIM), F32)],
        compiler_params=_cparams(2),
        name="ssd_prompt",
    )(proj, proj, dtraw, conv_w, conv_b, dt_bias, a_log, d_skip_x, norm_g)


def _outproj_kernel(x_ref, a_ref, y_ref, wa_ref, wy_ref, o_ref):
    o_ref[...] = (x_ref[...] + _dot(a_ref[...].astype(BF16), wa_ref[...])
                  + _dot(y_ref[...].astype(BF16), wy_ref[...]))


def _outproj(x, attn, y, w_out, tm):
    t, d = x.shape
    half = w_out.shape[0] // 2
    return pl.pallas_call(
        _outproj_kernel,
        grid=(t // tm,),
        in_specs=[
            pl.BlockSpec((tm, d), lambda i: (i, 0)),
            pl.BlockSpec((tm, half), lambda i: (i, 0)),
            pl.BlockSpec((tm, half), lambda i: (i, 0)),
            pl.BlockSpec((half, d), lambda i: (0, 0)),
            pl.BlockSpec((half, d), lambda i: (1, 0)),
        ],
        out_specs=pl.BlockSpec((tm, d), lambda i: (i, 0)),
        out_shape=jax.ShapeDtypeStruct((t, d), F32),
        compiler_params=_cparams(1),
        name="outproj",
    )(x, attn, y, w_out, w_out)


def _ffn_kernel(h_ref, g_ref, wg_ref, wu_ref, wd_ref, o_ref, f_ref):
    @pl.when(pl.program_id(1) == 0)
    def _():
        h = h_ref[...]
        f_ref[...] = (_rms(h) * g_ref[...]).astype(BF16)
        o_ref[...] = h

    f = f_ref[...]
    hid = (_silu(_dot(f, wg_ref[...])) * _dot(f, wu_ref[...])).astype(BF16)
    o_ref[...] += _dot(hid, wd_ref[...])


def _ffn(h, g, wg, wu, wd, tm, th):
    t, d = h.shape
    hidden = wg.shape[1]
    return pl.pallas_call(
        _ffn_kernel,
        grid=(t // tm, hidden // th),
        in_specs=[
            pl.BlockSpec((tm, d), lambda i, j: (i, 0)),
            pl.BlockSpec((1, d), lambda i, j: (0, 0)),
            pl.BlockSpec((d, th), lambda i, j: (0, j)),
            pl.BlockSpec((d, th), lambda i, j: (0, j)),
            pl.BlockSpec((th, d), lambda i, j: (j, 0)),
        ],
        out_specs=pl.BlockSpec((tm, d), lambda i, j: (i, 0)),
        out_shape=jax.ShapeDtypeStruct((t, d), F32),
        scratch_shapes=[pltpu.VMEM((tm, d), BF16)],
        compiler_params=_cparams(2),
        name="ffn",
    )(h, g, wg, wu, wd)


def _ple_kernel(h_ref, p_ref, gp_ref, wp_ref, wg_ref, gf_ref, o_ref, *, tn):
    h = h_ref[...]
    d = h.shape[1]
    n = (_rms(h) * gp_ref[...]).astype(BF16)
    pb = p_ref[...].astype(BF16)
    ss = jnp.zeros((h.shape[0], 1), F32)
    for c in range(d // tn):
        cols = slice(c * tn, (c + 1) * tn)
        gate = _dot(n, wg_ref[:, cols])
        h3 = h_ref[:, cols] + _dot(pb, wp_ref[:, cols]) * _sigmoid(gate)
        o_ref[:, cols] = h3
        ss = ss + jnp.sum(h3 * h3, axis=-1, keepdims=True)
    inv = lax.rsqrt(ss * (1.0 / d) + RMS_EPS)
    o_ref[...] = o_ref[...] * inv * gf_ref[...]


def _ple_final(h, p, g_ple, w_ple, w_gate, g_final, tm):
    t, d = h.shape
    pd = p.shape[1]
    const = lambda i: (0, 0)
    return pl.pallas_call(
        functools.partial(_ple_kernel, tn=512),
        grid=(t // tm,),
        in_specs=[
            pl.BlockSpec((tm, d), lambda i: (i, 0)),
            pl.BlockSpec((tm, pd), lambda i: (i, 0)),
            pl.BlockSpec((1, d), const),
            pl.BlockSpec((pd, d), const),
            pl.BlockSpec((d, d), const),
            pl.BlockSpec((1, d), const),
        ],
        out_specs=pl.BlockSpec((tm, d), lambda i: (i, 0)),
        out_shape=jax.ShapeDtypeStruct((t, d), F32),
        compiler_params=_cparams(1),
        name="ple_final",
    )(h, p, g_ple, w_ple, w_gate, g_final)


def _sample_pre_kernel(q_ref, k_ref, x_ref, dt_ref, sc_ref, tab_ref, cw_ref, cb_ref, dtb_ref,
                       alog_ref, dsk_ref, exp_ref,
                       qr_ref, kr_ref, xdt_ref, b_ref, c_ref, dec_ref, yp_ref, dech_ref):
    tab = tab_ref[...]
    qr_ref[...] = _rope_wide(q_ref[...], tab) * (HEAD_DIM ** -0.5)
    kr_ref[...] = _rope_wide(k_ref[...], tab)
    conv = x_ref[...] * cw_ref[CONV_WIDTH - 1:CONV_WIDTH, :] + cb_ref[...]
    for k in range(CONV_WIDTH - 1):
        conv = conv + sc_ref[k] * cw_ref[k:k + 1, :]
    xc = _silu(conv)
    xs = xc[:, :SSM_INNER]
    bm = xc[:, SSM_INNER:SSM_INNER + BC_DIM]
    cm = xc[:, SSM_INNER + BC_DIM:]
    b_ref[...] = bm
    c_ref[...] = cm
    dt = _softplus(dt_ref[...] + dtb_ref[...])
    dec = jnp.exp(dt * (-jnp.exp(alog_ref[...])))
    ex = exp_ref[...]
    dtx = _dot_exact_rhs01(dt, ex)
    dec_ref[...] = _dot_exact_rhs01(dec, ex)
    dech_ref[...] = dec
    xdt = xs * dtx
    xdt_ref[...] = xdt
    gw = SSM_INNER // SSM_GROUPS
    cbs = []
    for g in range(SSM_GROUPS):
        prod = cm[:, g * SSM_STATE:(g + 1) * SSM_STATE] * bm[:, g * SSM_STATE:(g + 1) * SSM_STATE]
        cbs.append(jnp.broadcast_to(jnp.sum(prod, axis=-1, keepdims=True), (prod.shape[0], gw)))
    yp_ref[...] = xdt * jnp.concatenate(cbs, axis=1) + xs * dsk_ref[...]


def _dot_exact_rhs01(x, m01):
    hi, mid, lo = _split3(x)
    return _dot(hi, m01) + _dot(mid, m01) + _dot(lo, m01)


def _sample_pre(q, k, xbc, dtraw, sconv_t, tab, conv_w, conv_b, dt_bias, a_log, d_skip_x, expand):
    nb = q.shape[0]
    shapes = [(nb, Q_DIM), (nb, KV_DIM), (nb, SSM_INNER), (nb, BC_DIM), (nb, BC_DIM),
              (nb, SSM_INNER), (nb, SSM_INNER), (nb, HEADS_PAD)]
    return pl.pallas_call(
        _sample_pre_kernel,
        out_shape=[jax.ShapeDtypeStruct(s, F32) for s in shapes],
        compiler_params=pltpu.CompilerParams(vmem_limit_bytes=VMEM_LIMIT),
        name="sample_pre",
    )(q, k, xbc, dtraw, sconv_t, tab, conv_w, conv_b, dt_bias, a_log, d_skip_x, expand)


def _attn_sample_kernel(sink_ref, q_ref, kn_ref, vn_ref, ck_ref, cv_ref, o_ref, nk_ref, nv_ref, *, bb):
    w = WINDOW
    row = lax.broadcasted_iota(jnp.int32, (w, KV_DIM), 0)
    hrow = lax.broadcasted_iota(jnp.int32, (N_HEADS, KV_DIM), 0) // (N_HEADS // N_KV_HEADS)
    hgrp = lax.broadcasted_iota(jnp.int32, (N_HEADS, KV_DIM), 1) // HEAD_DIM
    own = hrow == hgrp
    sink = sink_ref[...]
    for b in range(bb):
        kk = jnp.where(row == w - 1, kn_ref[b:b + 1, :], pltpu.roll(ck_ref[b], w - 1, 0))
        vv = jnp.where(row == w - 1, vn_ref[b:b + 1, :], pltpu.roll(cv_ref[b], w - 1, 0))
        nk_ref[b] = kk
        nv_ref[b] = vv
        qb = q_ref[b]
        qrow = jnp.where(own, jnp.concatenate([qb] * N_KV_HEADS, axis=1), 0.0)
        s = _dot_nt(qrow.astype(BF16), kk.astype(BF16))
        m = jnp.maximum(jnp.max(s, axis=-1, keepdims=True), sink)
        e = jnp.exp(s - m)
        den = jnp.sum(e, axis=-1, keepdims=True) + jnp.exp(sink - m)
        of = jnp.where(own, _dot(e.astype(BF16), vv.astype(BF16)), 0.0)
        o = of[:, :HEAD_DIM]
        for g in range(1, N_KV_HEADS):
            o = o + of[:, g * HEAD_DIM:(g + 1) * HEAD_DIM]
        o_ref[b] = o / den


def _attn_sample(sinks_col, q3, knew, vnew, cache_k, cache_v, bb):
    nb = q3.shape[0]
    w = WINDOW
    return pl.pallas_call(
        functools.partial(_attn_sample_kernel, bb=bb),
        grid=(nb // bb,),
        in_specs=[
            pl.BlockSpec((N_HEADS, 1), lambda i: (0, 0)),
            pl.BlockSpec((bb, N_HEADS, HEAD_DIM), lambda i: (i, 0, 0)),
            pl.BlockSpec((bb, KV_DIM), lambda i: (i, 0)),
            pl.BlockSpec((bb, KV_DIM), lambda i: (i, 0)),
            pl.BlockSpec((bb, w, KV_DIM), lambda i: (i, 0, 0)),
            pl.BlockSpec((bb, w, KV_DIM), lambda i: (i, 0, 0)),
        ],
        out_specs=[
            pl.BlockSpec((bb, N_HEADS, HEAD_DIM), lambda i: (i, 0, 0)),
            pl.BlockSpec((bb, w, KV_DIM), lambda i: (i, 0, 0)),
            pl.BlockSpec((bb, w, KV_DIM), lambda i: (i, 0, 0)),
        ],
        out_shape=[
            jax.ShapeDtypeStruct((nb, N_HEADS, HEAD_DIM), F32),
            jax.ShapeDtypeStruct((nb, w, KV_DIM), F32),
            jax.ShapeDtypeStruct((nb, w, KV_DIM), F32),
        ],
        compiler_params=_cparams(1),
        name="attn_sample",
    )(sinks_col, q3, knew, vnew, cache_k, cache_v)


def _ssd_sample_kernel(st_ref, xdt_ref, b_ref, c_ref, dec_ref, dech_ref, yp_ref, z_ref, gn_ref,
                       y_ref, ns_ref, *, bb):
    gw = SSM_INNER // SSM_GROUPS
    grow = lax.broadcasted_iota(jnp.int32, (8, SSM_INNER), 0)
    glane = lax.broadcasted_iota(jnp.int32, (8, SSM_INNER), 1) // gw
    own = grow == glane
    pad = jnp.zeros((8 - SSM_GROUPS, SSM_STATE), F32)
    yoffs = []
    for b in range(bb):
        st = st_ref[b]
        cmat = jnp.concatenate([c_ref[b], pad], axis=0).astype(BF16)
        bmat = jnp.concatenate([b_ref[b], pad], axis=0).astype(BF16)
        r = _dot_nt(cmat, st.astype(BF16))
        yoffs.append(jnp.sum(jnp.where(own, r, 0.0), axis=0, keepdims=True))
        amat = jnp.where(own, jnp.broadcast_to(xdt_ref[b:b + 1, :], (8, SSM_INNER)), 0.0)
        outer = _dot_tn(amat.astype(BF16), bmat)
        for h in range(SSM_HEADS):
            rows = slice(h * SSM_HEAD_DIM, (h + 1) * SSM_HEAD_DIM)
            ns_ref[b, rows, :] = st[rows, :] * dech_ref[b:b + 1, h:h + 1] + outer[rows, :]
    y = yp_ref[...] + jnp.concatenate(yoffs, axis=0) * dec_ref[...]
    hg = y * _silu(z_ref[...])
    outs = []
    for g in range(SSM_GROUPS):
        outs.append(_rms(hg[:, g * gw:(g + 1) * gw]))
    y_ref[...] = jnp.concatenate(outs, axis=1) * gn_ref[...]


def _ssd_sample(state, xdt, b3, c3, decx, dech, ypart, z, norm_g, bb):
    nb = state.shape[0]
    row = lambda i: (i, 0)
    return pl.pallas_call(
        functools.partial(_ssd_sample_kernel, bb=bb),
        grid=(nb // bb,),
        in_specs=[
            pl.BlockSpec((bb, SSM_INNER, SSM_STATE), lambda i: (i, 0, 0)),
            pl.BlockSpec((bb, SSM_INNER), row),
            pl.BlockSpec((bb, SSM_GROUPS, SSM_STATE), lambda i: (i, 0, 0)),
            pl.BlockSpec((bb, SSM_GROUPS, SSM_STATE), lambda i: (i, 0, 0)),
            pl.BlockSpec((bb, SSM_INNER), row),
            pl.BlockSpec((bb, HEADS_PAD), row),
            pl.BlockSpec((bb, SSM_INNER), row),
            pl.BlockSpec((bb, SSM_INNER), row),
            pl.BlockSpec((1, SSM_INNER), lambda i: (0, 0)),
        ],
        out_specs=[
            pl.BlockSpec((bb, SSM_INNER), row),
            pl.BlockSpec((bb, SSM_INNER, SSM_STATE), lambda i: (i, 0, 0)),
        ],
        out_shape=[
            jax.ShapeDtypeStruct((nb, SSM_INNER), F32),
            jax.ShapeDtypeStruct((nb, SSM_INNER, SSM_STATE), F32),
        ],
        compiler_params=_cparams(1),
        name="ssd_sample",
    )(state, xdt, b3, c3, decx, dech, ypart, z, norm_g)


def _row_tile(t, want):
    return want if t % want == 0 else t


def _dense_tail(x, attn, y, p, wts, tm):
    h1 = _outproj(x, attn, y, wts["w_out"], tm)
    h2 = _ffn(h1, wts["g_ffn"], wts["w_gate"], wts["w_up"], wts["w_down"], tm, 512)
    return _ple_final(h2, p, wts["g_ple"], wts["w_ple"], wts["w_ple_gate"], wts["g_final"], tm)


def kernel(x_prompt, x_sample, cache_k, cache_v, state_ssm, state_conv, p_prompt, p_sample, w_in, conv_w, conv_b, dt_bias, a_log, d_skip, ssm_norm_g, attn_sinks, w_out, g_mix, g_ffn, w_ffn_gate, w_ffn_up, w_ffn_down, g_ple, w_ple, w_ple_gate, g_final):
    depth = w_in.shape[0]
    assert depth == 1, "single-layer step only"
    bsz, seq, d = x_prompt.shape
    nb, dseq, _ = x_sample.shape
    assert dseq == 1 and seq % SSD_CHUNK == 0 and seq % WINDOW == 0

    w = w_in[0]
    o1, o2, o3, o4, o5 = (Q_DIM, Q_DIM + KV_DIM, Q_DIM + 2 * KV_DIM,
                          Q_DIM + 2 * KV_DIM + SSM_INNER, Q_DIM + 2 * KV_DIM + SSM_INNER + CONV_DIM)
    w_main = jnp.concatenate([w[:, :o1], w[:, o3:o4], w[:, o4:o5], w[:, o1:o2], w[:, o2:o3]],
                             axis=1).astype(BF16)
    w_dt = jnp.pad(w[:, o5:], ((0, 0), (0, HEADS_PAD - SSM_HEADS))).astype(BF16)
    padh = lambda v: jnp.pad(v, (0, HEADS_PAD - SSM_HEADS)).reshape(1, HEADS_PAD)
    dtb, alog = padh(dt_bias[0]), padh(a_log[0])
    dsk_x = jnp.repeat(d_skip[0], SSM_HEAD_DIM).reshape(1, SSM_INNER)
    gn = ssm_norm_g[0].reshape(1, SSM_INNER)
    cw, cb = conv_w[0], conv_b[0].reshape(1, CONV_DIM)
    wts = dict(
        w_out=w_out[0].astype(BF16), g_ffn=g_ffn[0].reshape(1, d),
        w_gate=w_ffn_gate[0].astype(BF16), w_up=w_ffn_up[0].astype(BF16),
        w_down=w_ffn_down[0].astype(BF16), g_ple=g_ple[0].reshape(1, d),
        w_ple=w_ple[0].astype(BF16), w_ple_gate=w_ple_gate[0].astype(BF16),
        g_final=g_final.reshape(1, d))
    gmix = g_mix[0].reshape(1, d)
    sinks = attn_sinks[0]

    tp = bsz * seq
    xp = x_prompt.reshape(tp, d)
    tm_in = _row_tile(tp, 1024)
    proj, dtraw = _inproj(xp, gmix, w_main, w_dt, tm_in, 512)
    proj3 = proj.reshape(bsz, seq, PROJ_DIM)
    tab_p = _rope_tables(jnp.arange(seq))
    attn, nk_p, nv_p = _attn_prompt(proj3, tab_p, sinks)
    yp, nssm_p, nconv_p = _ssd_prompt(proj3, dtraw.reshape(bsz, seq, HEADS_PAD), cw, cb, dtb, alog,
                                      dsk_x, gn)
    tm = _row_tile(tp, 512)
    y_prompt = _dense_tail(xp, attn.reshape(tp, Q_DIM), yp.reshape(tp, SSM_INNER),
                           p_prompt[0].reshape(tp, -1), wts, tm).reshape(bsz, seq, d)

    xs = x_sample.reshape(nb, d)
    proj_s, dtraw_s = _inproj(xs, gmix, w_main, w_dt, nb, 512)
    window = cache_k.shape[2]
    tab_s = _rope_tables(jnp.full((1,), PAST_LEN, jnp.int32))
    sconv_t = jnp.transpose(state_conv[0], (1, 0, 2))
    expand = (jnp.arange(HEADS_PAD)[:, None] == (jnp.arange(SSM_INNER) // SSM_HEAD_DIM)[None, :]
              ).astype(BF16)
    q_s = proj_s[:, COL_Q:COL_Q + Q_DIM]
    k_s = proj_s[:, COL_K:COL_K + KV_DIM]
    v_s = proj_s[:, COL_V:COL_V + KV_DIM]
    z_s = proj_s[:, COL_Z:COL_Z + SSM_INNER]
    xbc_s = proj_s[:, COL_XBC:COL_XBC + CONV_DIM]
    qr, kr, xdt, bm, cm, decx, ypart, dech = _sample_pre(
        q_s, k_s, xbc_s, dtraw_s, sconv_t, tab_s, cw, cb, dtb, alog, dsk_x, expand)
    bb = 8 if nb % 8 == 0 else nb
    attn_s, nk_s, nv_s = _attn_sample(
        sinks.reshape(N_HEADS, 1), qr.reshape(nb, N_HEADS, HEAD_DIM), kr, v_s,
        cache_k[0].reshape(nb, window, KV_DIM), cache_v[0].reshape(nb, window, KV_DIM), bb)
    ys, nssm_s = _ssd_sample(
        state_ssm[0].reshape(nb, SSM_INNER, SSM_STATE), xdt,
        bm.reshape(nb, SSM_GROUPS, SSM_STATE), cm.reshape(nb, SSM_GROUPS, SSM_STATE),
        decx, dech, ypart, z_s, gn, bb)
    y_sample = _dense_tail(xs, attn_s.reshape(nb, Q_DIM), ys, p_sample[0].reshape(nb, -1), wts,
                           nb).reshape(nb, 1, d)
    nconv_s = jnp.concatenate([state_conv[0][:, 1:], xbc_s[:, None, :]], axis=1)

    kv5 = lambda t: t.reshape(1, t.shape[0], window, N_KV_HEADS, HEAD_DIM)
    return (y_prompt, y_sample,
            kv5(nk_p), kv5(nv_p),
            nssm_p.reshape(1, bsz, SSM_HEADS, SSM_HEAD_DIM, SSM_STATE),
            nconv_p[None, :, 8 - (CONV_WIDTH - 1):, :],
            kv5(nk_s), kv5(nv_s),
            nssm_s.reshape(1, nb, SSM_HEADS, SSM_HEAD_DIM, SSM_STATE),
            nconv_s[None])
```

```python
import functools

import jax
import jax.numpy as jnp
from jax import lax
from jax.experimental import pallas as pl
from jax.experimental.pallas import tpu as pltpu

F32 = jnp.float32
BF16 = jnp.bfloat16

HEAD_DIM = 64
N_HEADS = 16
N_KV_HEADS = 4
WINDOW = 128
ROPE_DIM = 16
ROPE_THETA = 500000.0
SSM_HEADS = 16
SSM_HEAD_DIM = 64
SSM_GROUPS = 4
SSM_STATE = 128
CONV_WIDTH = 4
SSD_CHUNK = 128
RMS_EPS = 1e-6
PAST_LEN = 16384
LOG2E = 1.4426950408889634

Q_DIM = N_HEADS * HEAD_DIM
KV_DIM = N_KV_HEADS * HEAD_DIM
SSM_INNER = SSM_HEADS * SSM_HEAD_DIM
BC_DIM = SSM_GROUPS * SSM_STATE
CONV_DIM = SSM_INNER + 2 * BC_DIM
LANES = 128
HEADS_PAD = LANES

COL_Q = 0
COL_Z = Q_DIM
COL_XBC = Q_DIM + SSM_INNER
COL_K = COL_XBC + CONV_DIM
COL_V = COL_K + KV_DIM
PROJ_DIM = COL_V + KV_DIM

INPROJ_TN = 1536
ATTN_BLOCKS = 2
SSD_CHUNKS = 4
VMEM_LIMIT = 56 * 1024 * 1024


def _cparams(n_axes):
    return pltpu.CompilerParams(
        dimension_semantics=("arbitrary",) * n_axes, vmem_limit_bytes=VMEM_LIMIT)


def _rms(x):
    return x * lax.rsqrt(jnp.mean(x * x, axis=-1, keepdims=True) + RMS_EPS)


def _sigmoid(x):
    return 1.0 / (1.0 + jnp.exp(-x))


def _silu(x):
    return x * _sigmoid(x)


def _dot(a, b):
    return jnp.dot(a, b, preferred_element_type=F32)


def _dot_nt(a, b):
    return lax.dot_general(a, b, (((1,), (1,)), ((), ())), preferred_element_type=F32)


def _dot_tn(a, b):
    return lax.dot_general(a, b, (((0,), (0,)), ((), ())), preferred_element_type=F32)


def _inproj_kernel(x_ref, g_ref, w_ref, wdt_ref, o_ref, odt_ref, u_ref):
    @pl.when(pl.program_id(1) == 0)
    def _():
        u = (_rms(x_ref[...]) * g_ref[...]).astype(BF16)
        u_ref[...] = u
        odt_ref[...] = _dot(u, wdt_ref[...])

    o_ref[...] = _dot(u_ref[...], w_ref[...])


def _inproj(x, g, w, wdt, tm):
    t, d = x.shape
    nj, _, tn = w.shape
    n = nj * tn
    return pl.pallas_call(
        _inproj_kernel,
        grid=(t // tm, nj),
        in_specs=[
            pl.BlockSpec((tm, d), lambda i, j: (i, 0)),
            pl.BlockSpec((1, d), lambda i, j: (0, 0)),
            pl.BlockSpec((None, d, tn), lambda i, j: (j, 0, 0)),
            pl.BlockSpec((d, HEADS_PAD), lambda i, j: (0, 0)),
        ],
        out_specs=[
            pl.BlockSpec((tm, tn), lambda i, j: (i, j)),
            pl.BlockSpec((tm, HEADS_PAD), lambda i, j: (i, 0)),
        ],
        out_shape=[
            jax.ShapeDtypeStruct((t, n), F32),
            jax.ShapeDtypeStruct((t, HEADS_PAD), F32),
        ],
        scratch_shapes=[pltpu.VMEM((tm, d), BF16)],
        compiler_params=_cparams(2),
        name="inproj",
    )(x, g, w, wdt)


def _rope_tables(pos):
    half = ROPE_DIM // 2
    inv = ROPE_THETA ** (-jnp.arange(half, dtype=F32) * (2.0 / ROPE_DIM))
    lane = jnp.arange(LANES) % HEAD_DIM
    inv_lane = jnp.where(lane < ROPE_DIM, inv[lane % half], 0.0)
    ang = pos.astype(F32)[:, None] * inv_lane[None, :]
    cos, sin = jnp.cos(ang), jnp.sin(ang)
    sa = jnp.where(lane < half, -sin, 0.0)
    sb = jnp.where((lane >= half) & (lane < ROPE_DIM), sin, 0.0)
    return jnp.concatenate([cos, sa, sb], axis=1)


def _rope(x, tab):
    half = ROPE_DIM // 2
    c, sa, sb = tab[:, :LANES], tab[:, LANES:2 * LANES], tab[:, 2 * LANES:]
    return x * c + pltpu.roll(x, LANES - half, 1) * sa + pltpu.roll(x, half, 1) * sb


def _rope_wide(x, tab):
    return jnp.concatenate(
        [_rope(x[:, c * LANES:(c + 1) * LANES], tab) for c in range(x.shape[1] // LANES)],
        axis=1)


def _softmax_fold(s, band, prev_bias, sink2):
    sp = s[:, :WINDOW] if prev_bias is None else s[:, :WINDOW] + prev_bias
    t = jnp.where(band, sp, s[:, WINDOW:])
    m = jnp.maximum(jnp.max(t, axis=-1, keepdims=True), sink2)
    e = jnp.exp2(t - m)
    den = jnp.sum(e, axis=-1, keepdims=True) + jnp.exp2(sink2 - m)
    p = jnp.concatenate([jnp.where(band, e, 0.0), jnp.where(band, 0.0, e)], axis=1)
    return p.astype(BF16), den


def _attn_prompt_kernel(sink_ref, q_ref, kc_ref, kp_ref, vc_ref, vp_ref, tc_ref, tp_ref,
                        o_ref, nk_ref, nv_ref, *, nq):
    i = pl.program_id(1)
    nsteps = pl.num_programs(1)
    w = WINDOW
    tc = tc_ref[...]
    kcr = _rope_wide(kc_ref[...], tc)
    kpr = _rope_wide(kp_ref[...], tp_ref[...])
    vc = vc_ref[...]

    @pl.when(i == nsteps - 1)
    def _():
        nk_ref[...] = kcr[(nq - 1) * w:]
        nv_ref[...] = vc[(nq - 1) * w:]

    kall = jnp.concatenate([kpr, kcr], axis=0)
    vall = jnp.concatenate([vp_ref[...], vc], axis=0)
    lo = lax.broadcasted_iota(jnp.int32, ((nq + 1) * w, LANES), 1) < HEAD_DIM
    lo_q = lax.broadcasted_iota(jnp.int32, (w, LANES), 1) < HEAD_DIM
    band = (lax.broadcasted_iota(jnp.int32, (w, w), 1) > lax.broadcasted_iota(jnp.int32, (w, w), 0))
    first_bias = jnp.where(i == 0, -jnp.inf, 0.0)
    scale = HEAD_DIM ** -0.5 * LOG2E

    for g in range(N_KV_HEADS):
        col, odd = g // 2, g % 2
        kg = kall[:, col * LANES:(col + 1) * LANES]
        vg = vall[:, col * LANES:(col + 1) * LANES]
        kg_sw = pltpu.roll(kg, HEAD_DIM, 1)
        vg_sw = pltpu.roll(vg, HEAD_DIM, 1)
        k_lo = jnp.where(lo, kg_sw if odd else kg, 0.0).astype(BF16)
        k_hi = jnp.where(lo, 0.0, kg if odd else kg_sw).astype(BF16)
        v_lo = jnp.where(lo, vg_sw if odd else vg, 0.0).astype(BF16)
        v_hi = jnp.where(lo, 0.0, vg if odd else vg_sw).astype(BF16)
        sinks2 = [sink_ref[4 * g + r] * LOG2E for r in range(4)]
        for s in range(nq):
            rows = slice(s * w, (s + 1) * w)
            keys = slice(s * w, (s + 2) * w)
            tcs = tc[rows]
            pb = first_bias if s == 0 else None
            qa = _rope(q_ref[rows, (2 * g) * LANES:(2 * g + 1) * LANES], tcs) * scale
            qb = _rope(q_ref[rows, (2 * g + 1) * LANES:(2 * g + 2) * LANES], tcs) * scale
            qst = jnp.concatenate([qa, qb], axis=0).astype(BF16)
            s_lo = _dot_nt(qst, k_lo[keys])
            s_hi = _dot_nt(qst, k_hi[keys])
            e0, d0 = _softmax_fold(s_lo[:w], band, pb, sinks2[0])
            e1, d1 = _softmax_fold(s_hi[:w], band, pb, sinks2[1])
            e2, d2 = _softmax_fold(s_lo[w:], band, pb, sinks2[2])
            e3, d3 = _softmax_fold(s_hi[w:], band, pb, sinks2[3])
            p = jnp.concatenate([jnp.concatenate([e0, e1], axis=1),
                                 jnp.concatenate([e2, e3], axis=1)], axis=0)
            vcat = jnp.concatenate([v_lo[keys], v_hi[keys]], axis=0)
            o = _dot(p, vcat)
            oa = o[:w] * jnp.where(lo_q, 1.0 / d0, 1.0 / d1)
            ob = o[w:] * jnp.where(lo_q, 1.0 / d2, 1.0 / d3)
            o_ref[rows, (2 * g) * LANES:(2 * g + 1) * LANES] = oa.astype(o_ref.dtype)
            o_ref[rows, (2 * g + 1) * LANES:(2 * g + 2) * LANES] = ob.astype(o_ref.dtype)


def _attn_prompt(proj, tab, sinks, nq):
    b, l, _ = proj.shape
    w = WINDOW
    nsteps = l // (nq * w)
    kcol, vcol = COL_K // KV_DIM, COL_V // KV_DIM
    prev = lambda bi, i: jnp.maximum(nq * i - 1, 0)
    return pl.pallas_call(
        functools.partial(_attn_prompt_kernel, nq=nq),
        grid=(b, nsteps),
        in_specs=[
            pl.BlockSpec(memory_space=pltpu.SMEM),
            pl.BlockSpec((None, nq * w, Q_DIM), lambda bi, i: (bi, i, COL_Q // Q_DIM)),
            pl.BlockSpec((None, nq * w, KV_DIM), lambda bi, i: (bi, i, kcol)),
            pl.BlockSpec((None, w, KV_DIM), lambda bi, i: (bi, prev(bi, i), kcol)),
            pl.BlockSpec((None, nq * w, KV_DIM), lambda bi, i: (bi, i, vcol)),
            pl.BlockSpec((None, w, KV_DIM), lambda bi, i: (bi, prev(bi, i), vcol)),
            pl.BlockSpec((nq * w, 3 * LANES), lambda bi, i: (i, 0)),
            pl.BlockSpec((w, 3 * LANES), lambda bi, i: (prev(bi, i), 0)),
        ],
        out_specs=[
            pl.BlockSpec((None, nq * w, Q_DIM), lambda bi, i: (bi, i, 0)),
            pl.BlockSpec((None, w, KV_DIM), lambda bi, i: (bi, 0, 0)),
            pl.BlockSpec((None, w, KV_DIM), lambda bi, i: (bi, 0, 0)),
        ],
        out_shape=[
            jax.ShapeDtypeStruct((b, l, Q_DIM), BF16),
            jax.ShapeDtypeStruct((b, w, KV_DIM), F32),
            jax.ShapeDtypeStruct((b, w, KV_DIM), F32),
        ],
        compiler_params=_cparams(2),
        name="attn_prompt",
    )(sinks, proj, proj, proj, proj, proj, tab, tab)


def _softplus(v):
    return jnp.maximum(v, 0.0) + jnp.log1p(jnp.exp(-jnp.abs(v)))


def _split3(x):
    hi = x.astype(BF16)
    r = x - hi.astype(F32)
    mid = r.astype(BF16)
    lo = (r - mid.astype(F32)).astype(BF16)
    return hi, mid, lo


def _dot_exact_lhs01(m01, x):
    hi, mid, lo = _split3(x)
    return _dot(m01, hi) + _dot(m01, mid) + _dot(m01, lo)


def _head_expand(col_vals, pair):
    rows = col_vals.shape[0]
    lo = lax.broadcasted_iota(jnp.int32, (rows, LANES), 1) < SSM_HEAD_DIM
    a = jnp.broadcast_to(col_vals[:, 2 * pair:2 * pair + 1], (rows, LANES))
    b = jnp.broadcast_to(col_vals[:, 2 * pair + 1:2 * pair + 2], (rows, LANES))
    return jnp.where(lo, a, b)


def _ssd_prompt_kernel(z_ref, x_ref, dt_ref, cw_ref, cb_ref, dtb_ref, alog_ref, dsk_ref, gn_ref,
                       y_ref, nssm_ref, nconv_ref, state_ref, carry_ref, *, nsub):
    i = pl.program_id(1)
    nc = pl.num_programs(1)
    q = SSD_CHUNK

    @pl.when(i == 0)
    def _():
        state_ref[...] = jnp.zeros_like(state_ref)
        carry_ref[...] = jnp.zeros_like(carry_ref)

    for s in range(nsub):
        _ssd_chunk(s, z_ref, x_ref, dt_ref, cw_ref, cb_ref, dtb_ref, alog_ref, dsk_ref, gn_ref,
                   y_ref, state_ref, carry_ref)
    carry_ref[...] = x_ref[nsub * q - 8:, :]

    @pl.when(i == nc - 1)
    def _():
        nconv_ref[...] = x_ref[nsub * q - 8:, :]
        nssm_ref[...] = state_ref[...]


def _ssd_chunk(s, z_ref, x_ref, dt_ref, cw_ref, cb_ref, dtb_ref, alog_ref, dsk_ref, gn_ref,
               y_ref, state_ref, carry_ref):
    q = SSD_CHUNK
    trows = slice(s * q, (s + 1) * q)
    x = x_ref[trows, :]
    prev = carry_ref[...] if s == 0 else x_ref[s * q - 8:s * q, :]
    row8 = lax.broadcasted_iota(jnp.int32, (8, CONV_DIM), 0)

    def shifted(k):
        r = pltpu.roll(x, k, 0)
        head = jnp.where(row8 < k, pltpu.roll(prev, k, 0), r[:8])
        return jnp.concatenate([head, r[8:]], axis=0)

    conv = x * cw_ref[CONV_WIDTH - 1:CONV_WIDTH, :] + cb_ref[...]
    for k in range(1, CONV_WIDTH):
        conv = conv + shifted(k) * cw_ref[CONV_WIDTH - 1 - k:CONV_WIDTH - k, :]

    xc = _silu(conv)
    xs = xc[:, :SSM_INNER]
    bm = xc[:, SSM_INNER:SSM_INNER + BC_DIM]
    cm = xc[:, SSM_INNER + BC_DIM:]

    dt = _softplus(dt_ref[trows, :] + dtb_ref[...])
    a = -jnp.exp(alog_ref[...])
    da = dt * a
    ri = lax.broadcasted_iota(jnp.int32, (q, q), 0)
    cj = lax.broadcasted_iota(jnp.int32, (q, q), 1)
    tri = ri >= cj
    cs = _dot_exact_lhs01(tri.astype(BF16), da)
    cs_t = cs.T
    dt_t = dt.T
    cs_last = cs[q - 1:q, :]
    ecs = jnp.exp(cs)
    wgt = dt * jnp.exp(cs_last - cs)
    cdec_t = jnp.exp(cs_t[:, q - 1:q])
    lo = lax.broadcasted_iota(jnp.int32, (q, LANES), 1) < SSM_HEAD_DIM

    hpg = SSM_HEADS // SSM_GROUPS
    gw = hpg * SSM_HEAD_DIM
    for g in range(SSM_GROUPS):
        bg = bm[:, g * SSM_STATE:(g + 1) * SSM_STATE].astype(BF16)
        cg = cm[:, g * SSM_STATE:(g + 1) * SSM_STATE].astype(BF16)
        cb = _dot_nt(cg, bg)
        st = state_ref[g * gw:(g + 1) * gw, :]
        yoff = _dot_nt(cg, st.astype(BF16))
        ys = []
        for pr in range(2):
            pair = 2 * g + pr
            ms = []
            for h in (2 * pair, 2 * pair + 1):
                diff = cs[:, h:h + 1] - cs_t[h:h + 1, :]
                lm = jnp.exp(jnp.where(tri, diff, -jnp.inf))
                ms.append((cb * lm * dt_t[h:h + 1, :]).astype(BF16))
            xp = xs[:, pair * LANES:(pair + 1) * LANES]
            x2 = jnp.concatenate([jnp.where(lo, xp, 0.0), jnp.where(lo, 0.0, xp)],
                                 axis=0).astype(BF16)
            yd = _dot(jnp.concatenate(ms, axis=1), x2)
            yo = yoff[:, pr * LANES:(pr + 1) * LANES] * _head_expand(ecs, pair)
            ys.append(yd + yo + xp * dsk_ref[:, pair * LANES:(pair + 1) * LANES])
        yg = jnp.concatenate(ys, axis=1)
        wx = jnp.concatenate(
            [xs[:, (2 * g + pr) * LANES:(2 * g + pr + 1) * LANES] * _head_expand(wgt, 2 * g + pr)
             for pr in range(2)], axis=1)
        s_new = _dot_tn(wx.astype(BF16), bg)
        for r in range(hpg):
            h = hpg * g + r
            rows = slice(g * gw + r * SSM_HEAD_DIM, g * gw + (r + 1) * SSM_HEAD_DIM)
            state_ref[rows, :] = (st[r * SSM_HEAD_DIM:(r + 1) * SSM_HEAD_DIM, :] * cdec_t[h:h + 1, :]
                                  + s_new[r * SSM_HEAD_DIM:(r + 1) * SSM_HEAD_DIM, :])
        hg = yg * _silu(z_ref[trows, g * gw:(g + 1) * gw])
        y_ref[trows, g * gw:(g + 1) * gw] = (_rms(hg) * gn_ref[:, g * gw:(g + 1) * gw]
                                            ).astype(y_ref.dtype)


def _ssd_prompt(proj, dtraw, conv_w, conv_b, dt_bias, a_log, d_skip_x, norm_g, nsub):
    b, l, _ = proj.shape
    q = SSD_CHUNK * nsub
    nc = l // q
    const = lambda bi, i: (0, 0)
    return pl.pallas_call(
        functools.partial(_ssd_prompt_kernel, nsub=nsub),
        grid=(b, nc),
        in_specs=[
            pl.BlockSpec((None, q, SSM_INNER), lambda bi, i: (bi, i, COL_Z // SSM_INNER)),
            pl.BlockSpec((None, q, CONV_DIM), lambda bi, i: (bi, i, COL_XBC // CONV_DIM)),
            pl.BlockSpec((None, q, HEADS_PAD), lambda bi, i: (bi, i, 0)),
            pl.BlockSpec((CONV_WIDTH, CONV_DIM), const),
            pl.BlockSpec((1, CONV_DIM), const),
            pl.BlockSpec((1, HEADS_PAD), const),
            pl.BlockSpec((1, HEADS_PAD), const),
            pl.BlockSpec((1, SSM_INNER), const),
            pl.BlockSpec((1, SSM_INNER), const),
        ],
        out_specs=[
            pl.BlockSpec((None, q, SSM_INNER), lambda bi, i: (bi, i, 0)),
            pl.BlockSpec((None, SSM_INNER, SSM_STATE), lambda bi, i: (bi, 0, 0)),
            pl.BlockSpec((None, 8, CONV_DIM), lambda bi, i: (bi, 0, 0)),
        ],
        out_shape=[
            jax.ShapeDtypeStruct((b, l, SSM_INNER), BF16),
            jax.ShapeDtypeStruct((b, SSM_INNER, SSM_STATE), F32),
            jax.ShapeDtypeStruct((b, 8, CONV_DIM), F32),
        ],
        scratch_shapes=[pltpu.VMEM((SSM_INNER, SSM_STATE), F32),
                        pltpu.VMEM((8, CONV_DIM), F32)],
        compiler_params=_cparams(2),
        name="ssd_prompt",
    )(proj, proj, dtraw, conv_w, conv_b, dt_bias, a_log, d_skip_x, norm_g)


def _outproj_kernel(x_ref, a_ref, y_ref, wa_ref, wy_ref, o_ref):
    o_ref[...] = (x_ref[...] + _dot(a_ref[...].astype(BF16), wa_ref[...])
                  + _dot(y_ref[...].astype(BF16), wy_ref[...]))


def _outproj(x, attn, y, w_out, tm):
    t, d = x.shape
    half = w_out.shape[0] // 2
    return pl.pallas_call(
        _outproj_kernel,
        grid=(t // tm,),
        in_specs=[
            pl.BlockSpec((tm, d), lambda i: (i, 0)),
            pl.BlockSpec((tm, half), lambda i: (i, 0)),
            pl.BlockSpec((tm, half), lambda i: (i, 0)),
            pl.BlockSpec((half, d), lambda i: (0, 0)),
            pl.BlockSpec((half, d), lambda i: (1, 0)),
        ],
        out_specs=pl.BlockSpec((tm, d), lambda i: (i, 0)),
        out_shape=jax.ShapeDtypeStruct((t, d), F32),
        compiler_params=_cparams(1),
        name="outproj",
    )(x, attn, y, w_out, w_out)


def _ffn_kernel(h_ref, g_ref, wg_ref, wu_ref, wd_ref, o_ref, f_ref):
    @pl.when(pl.program_id(1) == 0)
    def _():
        h = h_ref[...]
        f_ref[...] = (_rms(h) * g_ref[...]).astype(BF16)
        o_ref[...] = h

    f = f_ref[...]
    hid = (_silu(_dot(f, wg_ref[...])) * _dot(f, wu_ref[...])).astype(BF16)
    o_ref[...] += _dot(hid, wd_ref[...])


def _ffn(h, g, wg, wu, wd, tm, th):
    t, d = h.shape
    hidden = wg.shape[1]
    return pl.pallas_call(
        _ffn_kernel,
        grid=(t // tm, hidden // th),
        in_specs=[
            pl.BlockSpec((tm, d), lambda i, j: (i, 0)),
            pl.BlockSpec((1, d), lambda i, j: (0, 0)),
            pl.BlockSpec((d, th), lambda i, j: (0, j)),
            pl.BlockSpec((d, th), lambda i, j: (0, j)),
            pl.BlockSpec((th, d), lambda i, j: (j, 0)),
        ],
        out_specs=pl.BlockSpec((tm, d), lambda i, j: (i, 0)),
        out_shape=jax.ShapeDtypeStruct((t, d), F32),
        scratch_shapes=[pltpu.VMEM((tm, d), BF16)],
        compiler_params=_cparams(2),
        name="ffn",
    )(h, g, wg, wu, wd)


def _ple_kernel(h_ref, p_ref, gp_ref, wp_ref, wg_ref, gf_ref, o_ref, *, tn, row_parts):
    tm, d = h_ref.shape
    rp = tm // row_parts
    for r in range(row_parts):
        rows = slice(r * rp, (r + 1) * rp)
        n = (_rms(h_ref[rows, :]) * gp_ref[...]).astype(BF16)
        pb = p_ref[rows, :].astype(BF16)
        ss = jnp.zeros((rp, 1), F32)
        for c in range(d // tn):
            cols = slice(c * tn, (c + 1) * tn)
            gate = _dot(n, wg_ref[:, cols])
            h3 = h_ref[rows, cols] + _dot(pb, wp_ref[:, cols]) * _sigmoid(gate)
            o_ref[rows, cols] = h3
            ss = ss + jnp.sum(h3 * h3, axis=-1, keepdims=True)
        inv = lax.rsqrt(ss * (1.0 / d) + RMS_EPS)
        o_ref[rows, :] = o_ref[rows, :] * inv * gf_ref[...]


def _ple_final(h, p, g_ple, w_ple, w_gate, g_final, tm):
    t, d = h.shape
    pd = p.shape[1]
    const = lambda i: (0, 0)
    return pl.pallas_call(
        functools.partial(_ple_kernel, tn=512, row_parts=2 if tm % 256 == 0 else 1),
        grid=(t // tm,),
        in_specs=[
            pl.BlockSpec((tm, d), lambda i: (i, 0)),
            pl.BlockSpec((tm, pd), lambda i: (i, 0)),
            pl.BlockSpec((1, d), const),
            pl.BlockSpec((pd, d), const),
            pl.BlockSpec((d, d), const),
            pl.BlockSpec((1, d), const),
        ],
        out_specs=pl.BlockSpec((tm, d), lambda i: (i, 0)),
        out_shape=jax.ShapeDtypeStruct((t, d), F32),
        compiler_params=_cparams(1),
        name="ple_final",
    )(h, p, g_ple, w_ple, w_gate, g_final)


def _sample_pre_kernel(q_ref, k_ref, x_ref, dt_ref, sc_ref, tab_ref, cw_ref, cb_ref, dtb_ref,
                       alog_ref, dsk_ref, exp_ref,
                       qr_ref, kr_ref, xdt_ref, b_ref, c_ref, dec_ref, yp_ref, dech_ref):
    tab = tab_ref[...]
    qr_ref[...] = _rope_wide(q_ref[...], tab) * (HEAD_DIM ** -0.5)
    kr_ref[...] = _rope_wide(k_ref[...], tab)
    conv = x_ref[...] * cw_ref[CONV_WIDTH - 1:CONV_WIDTH, :] + cb_ref[...]
    for k in range(CONV_WIDTH - 1):
        conv = conv + sc_ref[k] * cw_ref[k:k + 1, :]
    xc = _silu(conv)
    xs = xc[:, :SSM_INNER]
    bm = xc[:, SSM_INNER:SSM_INNER + BC_DIM]
    cm = xc[:, SSM_INNER + BC_DIM:]
    b_ref[...] = bm
    c_ref[...] = cm
    dt = _softplus(dt_ref[...] + dtb_ref[...])
    dec = jnp.exp(dt * (-jnp.exp(alog_ref[...])))
    ex = exp_ref[...]
    dtx = _dot_exact_rhs01(dt, ex)
    dec_ref[...] = _dot_exact_rhs01(dec, ex)
    dech_ref[...] = dec
    xdt = xs * dtx
    xdt_ref[...] = xdt
    gw = SSM_INNER // SSM_GROUPS
    cbs = []
    for g in range(SSM_GROUPS):
        prod = cm[:, g * SSM_STATE:(g + 1) * SSM_STATE] * bm[:, g * SSM_STATE:(g + 1) * SSM_STATE]
        cbs.append(jnp.broadcast_to(jnp.sum(prod, axis=-1, keepdims=True), (prod.shape[0], gw)))
    yp_ref[...] = xdt * jnp.concatenate(cbs, axis=1) + xs * dsk_ref[...]


def _dot_exact_rhs01(x, m01):
    hi, mid, lo = _split3(x)
    return _dot(hi, m01) + _dot(mid, m01) + _dot(lo, m01)


def _sample_pre(q, k, xbc, dtraw, sconv_t, tab, conv_w, conv_b, dt_bias, a_log, d_skip_x, expand):
    nb = q.shape[0]
    shapes = [(nb, Q_DIM), (nb, KV_DIM), (nb, SSM_INNER), (nb, BC_DIM), (nb, BC_DIM),
              (nb, SSM_INNER), (nb, SSM_INNER), (nb, HEADS_PAD)]
    return pl.pallas_call(
        _sample_pre_kernel,
        out_shape=[jax.ShapeDtypeStruct(s, F32) for s in shapes],
        compiler_params=pltpu.CompilerParams(vmem_limit_bytes=VMEM_LIMIT),
        name="sample_pre",
    )(q, k, xbc, dtraw, sconv_t, tab, conv_w, conv_b, dt_bias, a_log, d_skip_x, expand)


def _attn_sample_kernel(sink_ref, q_ref, kn_ref, vn_ref, ck_ref, cv_ref, o_ref, nk_ref, nv_ref, *, bb):
    w = WINDOW
    row = lax.broadcasted_iota(jnp.int32, (w, KV_DIM), 0)
    hrow = lax.broadcasted_iota(jnp.int32, (N_HEADS, KV_DIM), 0) // (N_HEADS // N_KV_HEADS)
    hgrp = lax.broadcasted_iota(jnp.int32, (N_HEADS, KV_DIM), 1) // HEAD_DIM
    own = hrow == hgrp
    sink = sink_ref[...]
    for b in range(bb):
        kk = jnp.where(row == w - 1, kn_ref[b:b + 1, :], pltpu.roll(ck_ref[b], w - 1, 0))
        vv = jnp.where(row == w - 1, vn_ref[b:b + 1, :], pltpu.roll(cv_ref[b], w - 1, 0))
        nk_ref[b] = kk
        nv_ref[b] = vv
        qb = q_ref[b]
        qrow = jnp.where(own, jnp.concatenate([qb] * N_KV_HEADS, axis=1), 0.0)
        s = _dot_nt(qrow.astype(BF16), kk.astype(BF16))
        m = jnp.maximum(jnp.max(s, axis=-1, keepdims=True), sink)
        e = jnp.exp(s - m)
        den = jnp.sum(e, axis=-1, keepdims=True) + jnp.exp(sink - m)
        of = jnp.where(own, _dot(e.astype(BF16), vv.astype(BF16)), 0.0)
        o = of[:, :HEAD_DIM]
        for g in range(1, N_KV_HEADS):
            o = o + of[:, g * HEAD_DIM:(g + 1) * HEAD_DIM]
        o_ref[b] = o / den


def _attn_sample(sinks_col, q3, knew, vnew, cache_k, cache_v, bb):
    nb = q3.shape[0]
    w = WINDOW
    return pl.pallas_call(
        functools.partial(_attn_sample_kernel, bb=bb),
        grid=(nb // bb,),
        in_specs=[
            pl.BlockSpec((N_HEADS, 1), lambda i: (0, 0)),
            pl.BlockSpec((bb, N_HEADS, HEAD_DIM), lambda i: (i, 0, 0)),
            pl.BlockSpec((bb, KV_DIM), lambda i: (i, 0)),
            pl.BlockSpec((bb, KV_DIM), lambda i: (i, 0)),
            pl.BlockSpec((bb, w, KV_DIM), lambda i: (i, 0, 0)),
            pl.BlockSpec((bb, w, KV_DIM), lambda i: (i, 0, 0)),
        ],
        out_specs=[
            pl.BlockSpec((bb, N_HEADS, HEAD_DIM), lambda i: (i, 0, 0)),
            pl.BlockSpec((bb, w, KV_DIM), lambda i: (i, 0, 0)),
            pl.BlockSpec((bb, w, KV_DIM), lambda i: (i, 0, 0)),
        ],
        out_shape=[
            jax.ShapeDtypeStruct((nb, N_HEADS, HEAD_DIM), F32),
            jax.ShapeDtypeStruct((nb, w, KV_DIM), F32),
            jax.ShapeDtypeStruct((nb, w, KV_DIM), F32),
        ],
        compiler_params=_cparams(1),
        name="attn_sample",
    )(sinks_col, q3, knew, vnew, cache_k, cache_v)


def _ssd_sample_kernel(st_ref, xdt_ref, b_ref, c_ref, dec_ref, dech_ref, yp_ref, z_ref, gn_ref,
                       y_ref, ns_ref, *, bb):
    gw = SSM_INNER // SSM_GROUPS
    grow = lax.broadcasted_iota(jnp.int32, (8, SSM_INNER), 0)
    glane = lax.broadcasted_iota(jnp.int32, (8, SSM_INNER), 1) // gw
    own = grow == glane
    pad = jnp.zeros((8 - SSM_GROUPS, SSM_STATE), F32)
    yoffs = []
    for b in range(bb):
        st = st_ref[b]
        cmat = jnp.concatenate([c_ref[b], pad], axis=0).astype(BF16)
        bmat = jnp.concatenate([b_ref[b], pad], axis=0).astype(BF16)
        r = _dot_nt(cmat, st.astype(BF16))
        yoffs.append(jnp.sum(jnp.where(own, r, 0.0), axis=0, keepdims=True))
        amat = jnp.where(own, jnp.broadcast_to(xdt_ref[b:b + 1, :], (8, SSM_INNER)), 0.0)
        outer = _dot_tn(amat.astype(BF16), bmat)
        for h in range(SSM_HEADS):
            rows = slice(h * SSM_HEAD_DIM, (h + 1) * SSM_HEAD_DIM)
            ns_ref[b, rows, :] = st[rows, :] * dech_ref[b:b + 1, h:h + 1] + outer[rows, :]
    y = yp_ref[...] + jnp.concatenate(yoffs, axis=0) * dec_ref[...]
    hg = y * _silu(z_ref[...])
    outs = []
    for g in range(SSM_GROUPS):
        outs.append(_rms(hg[:, g * gw:(g + 1) * gw]))
    y_ref[...] = jnp.concatenate(outs, axis=1) * gn_ref[...]


def _ssd_sample(state, xdt, b3, c3, decx, dech, ypart, z, norm_g, bb):
    nb = state.shape[0]
    row = lambda i: (i, 0)
    return pl.pallas_call(
        functools.partial(_ssd_sample_kernel, bb=bb),
        grid=(nb // bb,),
        in_specs=[
            pl.BlockSpec((bb, SSM_INNER, SSM_STATE), lambda i: (i, 0, 0)),
            pl.BlockSpec((bb, SSM_INNER), row),
            pl.BlockSpec((bb, SSM_GROUPS, SSM_STATE), lambda i: (i, 0, 0)),
            pl.BlockSpec((bb, SSM_GROUPS, SSM_STATE), lambda i: (i, 0, 0)),
            pl.BlockSpec((bb, SSM_INNER), row),
            pl.BlockSpec((bb, HEADS_PAD), row),
            pl.BlockSpec((bb, SSM_INNER), row),
            pl.BlockSpec((bb, SSM_INNER), row),
            pl.BlockSpec((1, SSM_INNER), lambda i: (0, 0)),
        ],
        out_specs=[
            pl.BlockSpec((bb, SSM_INNER), row),
            pl.BlockSpec((bb, SSM_INNER, SSM_STATE), lambda i: (i, 0, 0)),
        ],
        out_shape=[
            jax.ShapeDtypeStruct((nb, SSM_INNER), F32),
            jax.ShapeDtypeStruct((nb, SSM_INNER, SSM_STATE), F32),
        ],
        compiler_params=_cparams(1),
        name="ssd_sample",
    )(state, xdt, b3, c3, decx, dech, ypart, z, norm_g)


def _row_tile(t, want):
    return want if t % want == 0 else t


def _dense_tail(x, attn, y, p, wts, tm):
    h1 = _outproj(x, attn, y, wts["w_out"], tm)
    h2 = _ffn(h1, wts["g_ffn"], wts["w_gate"], wts["w_up"], wts["w_down"], tm, 512)
    return _ple_final(h2, p, wts["g_ple"], wts["w_ple"], wts["w_ple_gate"], wts["g_final"], tm)


def kernel(x_prompt, x_sample, cache_k, cache_v, state_ssm, state_conv, p_prompt, p_sample, w_in, conv_w, conv_b, dt_bias, a_log, d_skip, ssm_norm_g, attn_sinks, w_out, g_mix, g_ffn, w_ffn_gate, w_ffn_up, w_ffn_down, g_ple, w_ple, w_ple_gate, g_final):
    depth = w_in.shape[0]
    assert depth == 1, "single-layer step only"
    bsz, seq, d = x_prompt.shape
    nb, dseq, _ = x_sample.shape
    assert dseq == 1 and seq % SSD_CHUNK == 0 and seq % WINDOW == 0

    w = w_in[0]
    o1, o2, o3, o4, o5 = (Q_DIM, Q_DIM + KV_DIM, Q_DIM + 2 * KV_DIM,
                          Q_DIM + 2 * KV_DIM + SSM_INNER, Q_DIM + 2 * KV_DIM + SSM_INNER + CONV_DIM)
    w_main = jnp.concatenate([w[:, :o1], w[:, o3:o4], w[:, o4:o5], w[:, o1:o2], w[:, o2:o3]],
                             axis=1).astype(BF16)
    w_main = w_main.reshape(d, PROJ_DIM // INPROJ_TN, INPROJ_TN).transpose(1, 0, 2)
    w_dt = jnp.pad(w[:, o5:], ((0, 0), (0, HEADS_PAD - SSM_HEADS))).astype(BF16)
    padh = lambda v: jnp.pad(v, (0, HEADS_PAD - SSM_HEADS)).reshape(1, HEADS_PAD)
    dtb, alog = padh(dt_bias[0]), padh(a_log[0])
    dsk_x = jnp.repeat(d_skip[0], SSM_HEAD_DIM).reshape(1, SSM_INNER)
    gn = ssm_norm_g[0].reshape(1, SSM_INNER)
    cw, cb = conv_w[0], conv_b[0].reshape(1, CONV_DIM)
    wts = dict(
        w_out=w_out[0].astype(BF16), g_ffn=g_ffn[0].reshape(1, d),
        w_gate=w_ffn_gate[0].astype(BF16), w_up=w_ffn_up[0].astype(BF16),
        w_down=w_ffn_down[0].astype(BF16), g_ple=g_ple[0].reshape(1, d),
        w_ple=w_ple[0].astype(BF16), w_ple_gate=w_ple_gate[0].astype(BF16),
        g_final=g_final.reshape(1, d))
    gmix = g_mix[0].reshape(1, d)
    sinks = attn_sinks[0]

    tp = bsz * seq
    xp = x_prompt.reshape(tp, d)
    tm_in = _row_tile(tp, 1024)
    proj, dtraw = _inproj(xp, gmix, w_main, w_dt, tm_in)
    proj3 = proj.reshape(bsz, seq, PROJ_DIM)
    tab_p = _rope_tables(jnp.arange(seq))
    attn, nk_p, nv_p = _attn_prompt(proj3, tab_p, sinks, ATTN_BLOCKS)
    yp, nssm_p, nconv_p = _ssd_prompt(proj3, dtraw.reshape(bsz, seq, HEADS_PAD), cw, cb, dtb, alog,
                                      dsk_x, gn, SSD_CHUNKS)
    tm = _row_tile(tp, 512)
    y_prompt = _dense_tail(xp, attn.reshape(tp, Q_DIM), yp.reshape(tp, SSM_INNER),
                           p_prompt[0].reshape(tp, -1), wts, tm).reshape(bsz, seq, d)

    xs = x_sample.reshape(nb, d)
    proj_s, dtraw_s = _inproj(xs, gmix, w_main, w_dt, nb)
    window = cache_k.shape[2]
    tab_s = _rope_tables(jnp.full((1,), PAST_LEN, jnp.int32))
    sconv_t = jnp.transpose(state_conv[0], (1, 0, 2))
    expand = (jnp.arange(HEADS_PAD)[:, None] == (jnp.arange(SSM_INNER) // SSM_HEAD_DIM)[None, :]
              ).astype(BF16)
    q_s = proj_s[:, COL_Q:COL_Q + Q_DIM]
    k_s = proj_s[:, COL_K:COL_K + KV_DIM]
    v_s = proj_s[:, COL_V:COL_V + KV_DIM]
    z_s = proj_s[:, COL_Z:COL_Z + SSM_INNER]
    xbc_s = proj_s[:, COL_XBC:COL_XBC + CONV_DIM]
    qr, kr, xdt, bm, cm, decx, ypart, dech = _sample_pre(
        q_s, k_s, xbc_s, dtraw_s, sconv_t, tab_s, cw, cb, dtb, alog, dsk_x, expand)
    bb = 8 if nb % 8 == 0 else nb
    attn_s, nk_s, nv_s = _attn_sample(
        sinks.reshape(N_HEADS, 1), qr.reshape(nb, N_HEADS, HEAD_DIM), kr, v_s,
        cache_k[0].reshape(nb, window, KV_DIM), cache_v[0].reshape(nb, window, KV_DIM), bb)
    ys, nssm_s = _ssd_sample(
        state_ssm[0].reshape(nb, SSM_INNER, SSM_STATE), xdt,
        bm.reshape(nb, SSM_GROUPS, SSM_STATE), cm.reshape(nb, SSM_GROUPS, SSM_STATE),
        decx, dech, ypart, z_s, gn, bb)
    y_sample = _dense_tail(xs, attn_s.reshape(nb, Q_DIM), ys, p_sample[0].reshape(nb, -1), wts,
                           nb).reshape(nb, 1, d)
    nconv_s = jnp.concatenate([state_conv[0][:, 1:], xbc_s[:, None, :]], axis=1)

    kv5 = lambda t: t.reshape(1, t.shape[0], window, N_KV_HEADS, HEAD_DIM)
    return (y_prompt, y_sample,
            kv5(nk_p), kv5(nv_p),
            nssm_p.reshape(1, bsz, SSM_HEADS, SSM_HEAD_DIM, SSM_STATE),
            nconv_p[None, :, 8 - (CONV_WIDTH - 1):, :],
            kv5(nk_s), kv5(nv_s),
            nssm_s.reshape(1, nb, SSM_HEADS, SSM_HEAD_DIM, SSM_STATE),
            nconv_s[None])
```

```python
import functools

import numpy as np
import jax
import jax.numpy as jnp
from jax import lax
from jax.experimental import pallas as pl
from jax.experimental.pallas import tpu as pltpu

F32 = jnp.float32
BF16 = jnp.bfloat16

HEAD_DIM = 64
N_HEADS = 16
N_KV_HEADS = 4
WINDOW = 128
ROPE_DIM = 16
ROPE_THETA = 500000.0
SSM_HEADS = 16
SSM_HEAD_DIM = 64
SSM_GROUPS = 4
SSM_STATE = 128
CONV_WIDTH = 4
SSD_CHUNK = 128
RMS_EPS = 1e-6
PAST_LEN = 16384
LOG2E = 1.4426950408889634

Q_DIM = N_HEADS * HEAD_DIM
KV_DIM = N_KV_HEADS * HEAD_DIM
SSM_INNER = SSM_HEADS * SSM_HEAD_DIM
BC_DIM = SSM_GROUPS * SSM_STATE
CONV_DIM = SSM_INNER + 2 * BC_DIM
LANES = 128
HEADS_PAD = LANES

COL_Q = 0
COL_K = Q_DIM
COL_V = COL_K + KV_DIM
COL_Z = COL_V + KV_DIM
COL_XBC = COL_Z + SSM_INNER
PROJ_DIM = COL_XBC + CONV_DIM
PIECE = 512

INPROJ_TN = Q_DIM + 2 * KV_DIM
Q_SCALE = HEAD_DIM ** -0.5 * LOG2E
ATTN_BLOCKS = 2
SSD_CHUNKS = 4
VMEM_LIMIT = 56 * 1024 * 1024


def _cparams(n_axes):
    return pltpu.CompilerParams(
        dimension_semantics=("arbitrary",) * n_axes, vmem_limit_bytes=VMEM_LIMIT)


def _rms(x):
    return x * lax.rsqrt(jnp.mean(x * x, axis=-1, keepdims=True) + RMS_EPS)


def _sigmoid(x):
    return 1.0 / (1.0 + jnp.exp(-x))


def _silu(x):
    return x * _sigmoid(x)


def _dot(a, b):
    return jnp.dot(a, b, preferred_element_type=F32)


def _dot_nt(a, b):
    return lax.dot_general(a, b, (((1,), (1,)), ((), ())), preferred_element_type=F32)


def _dot_tn(a, b):
    return lax.dot_general(a, b, (((0,), (0,)), ((), ())), preferred_element_type=F32)


def _inproj_kernel(x_ref, g_ref, w_ref, wdt_ref, tab_ref, o_ref, odt_ref, u_ref):
    j = pl.program_id(1)

    @pl.when(j == 0)
    def _():
        u = (_rms(x_ref[...]) * g_ref[...]).astype(BF16)
        u_ref[...] = u
        odt_ref[...] = _dot(u, wdt_ref[...])
        res = _dot(u, w_ref[...])
        tab = tab_ref[...]
        for c in range((Q_DIM + KV_DIM) // LANES):
            cols = slice(c * LANES, (c + 1) * LANES)
            r = _rope(res[:, cols], tab)
            o_ref[:, cols] = r * Q_SCALE if c < Q_DIM // LANES else r
        o_ref[:, Q_DIM + KV_DIM:] = res[:, Q_DIM + KV_DIM:]

    @pl.when(j > 0)
    def _():
        o_ref[...] = _dot(u_ref[...], w_ref[...])


def _inproj(x, g, w, wdt, tab, tm):
    t, d = x.shape
    n = w.shape[1]
    tn = INPROJ_TN
    nt = tab.shape[0] // tm
    return pl.pallas_call(
        _inproj_kernel,
        grid=(t // tm, n // tn),
        in_specs=[
            pl.BlockSpec((tm, d), lambda i, j: (i, 0)),
            pl.BlockSpec((1, d), lambda i, j: (0, 0)),
            pl.BlockSpec((d, tn), lambda i, j: (0, j)),
            pl.BlockSpec((d, HEADS_PAD), lambda i, j: (0, 0)),
            pl.BlockSpec((tm, 3 * LANES), lambda i, j: (i % nt, 0)),
        ],
        out_specs=[
            pl.BlockSpec((tm, tn), lambda i, j: (i, j)),
            pl.BlockSpec((tm, HEADS_PAD), lambda i, j: (i, 0)),
        ],
        out_shape=[
            jax.ShapeDtypeStruct((t, n), F32),
            jax.ShapeDtypeStruct((t, HEADS_PAD), F32),
        ],
        scratch_shapes=[pltpu.VMEM((tm, d), BF16)],
        compiler_params=_cparams(2),
        name="inproj",
    )(x, g, w, wdt, tab)


def _rope_tables(pos):
    half = ROPE_DIM // 2
    inv = ROPE_THETA ** (-jnp.arange(half, dtype=F32) * (2.0 / ROPE_DIM))
    ang = pos.astype(F32)[:, None] * inv[None, :]
    cs = jnp.concatenate([jnp.cos(ang), jnp.sin(ang)], axis=1)
    expand = np.zeros((2 * half, 3 * LANES), np.float32)
    base = np.zeros((1, 3 * LANES), np.float32)
    for lane in range(LANES):
        m = lane % HEAD_DIM
        if m >= ROPE_DIM:
            base[0, lane] = 1.0
            continue
        expand[m % half, lane] = 1.0
        if m < half:
            expand[half + m, LANES + lane] = -1.0
        else:
            expand[m, 2 * LANES + lane] = 1.0
    return jnp.dot(cs, expand, precision=lax.Precision.HIGHEST) + base


def _rope(x, tab):
    half = ROPE_DIM // 2
    c, sa, sb = tab[:, :LANES], tab[:, LANES:2 * LANES], tab[:, 2 * LANES:]
    return x * c + pltpu.roll(x, LANES - half, 1) * sa + pltpu.roll(x, half, 1) * sb


def _softmax_fold(s, band, prev_bias, sink2):
    sp = s[:, :WINDOW] if prev_bias is None else s[:, :WINDOW] + prev_bias
    t = jnp.where(band, sp, s[:, WINDOW:])
    m = jnp.maximum(jnp.max(t, axis=-1, keepdims=True), sink2)
    e = jnp.exp2(t - m)
    den = jnp.sum(e, axis=-1, keepdims=True) + jnp.exp2(sink2 - m)
    p = jnp.concatenate([jnp.where(band, e, 0.0), jnp.where(band, 0.0, e)], axis=1)
    return p.astype(BF16), den


def _attn_prompt_kernel(sink_ref, q_ref, kc_ref, kp_ref, vc_ref, vp_ref,
                        o_ref, nk_ref, nv_ref, *, nq):
    i = pl.program_id(1)
    nsteps = pl.num_programs(1)
    w = WINDOW
    kcr = kc_ref[...]
    kpr = kp_ref[...]
    vc = vc_ref[...]

    @pl.when(i == nsteps - 1)
    def _():
        nk_ref[...] = kcr[(nq - 1) * w:]
        nv_ref[...] = vc[(nq - 1) * w:]

    kall = jnp.concatenate([kpr, kcr], axis=0)
    vall = jnp.concatenate([vp_ref[...], vc], axis=0)
    lo = lax.broadcasted_iota(jnp.int32, ((nq + 1) * w, LANES), 1) < HEAD_DIM
    lo_q = lax.broadcasted_iota(jnp.int32, (w, LANES), 1) < HEAD_DIM
    band = (lax.broadcasted_iota(jnp.int32, (w, w), 1) > lax.broadcasted_iota(jnp.int32, (w, w), 0))
    first_bias = jnp.where(i == 0, -jnp.inf, 0.0)

    for g in range(N_KV_HEADS):
        col, odd = g // 2, g % 2
        kg = kall[:, col * LANES:(col + 1) * LANES]
        vg = vall[:, col * LANES:(col + 1) * LANES]
        kg_sw = pltpu.roll(kg, HEAD_DIM, 1)
        vg_sw = pltpu.roll(vg, HEAD_DIM, 1)
        k_lo = jnp.where(lo, kg_sw if odd else kg, 0.0).astype(BF16)
        k_hi = jnp.where(lo, 0.0, kg if odd else kg_sw).astype(BF16)
        v_lo = jnp.where(lo, vg_sw if odd else vg, 0.0).astype(BF16)
        v_hi = jnp.where(lo, 0.0, vg if odd else vg_sw).astype(BF16)
        sinks2 = [sink_ref[4 * g + r] * LOG2E for r in range(4)]
        for s in range(nq):
            rows = slice(s * w, (s + 1) * w)
            keys = slice(s * w, (s + 2) * w)
            pb = first_bias if s == 0 else None
            qst = jnp.concatenate([q_ref[rows, (2 * g) * LANES:(2 * g + 1) * LANES],
                                   q_ref[rows, (2 * g + 1) * LANES:(2 * g + 2) * LANES]],
                                  axis=0).astype(BF16)
            s_lo = _dot_nt(qst, k_lo[keys])
            s_hi = _dot_nt(qst, k_hi[keys])
            e0, d0 = _softmax_fold(s_lo[:w], band, pb, sinks2[0])
            e1, d1 = _softmax_fold(s_hi[:w], band, pb, sinks2[1])
            e2, d2 = _softmax_fold(s_lo[w:], band, pb, sinks2[2])
            e3, d3 = _softmax_fold(s_hi[w:], band, pb, sinks2[3])
            p = jnp.concatenate([jnp.concatenate([e0, e1], axis=1),
                                 jnp.concatenate([e2, e3], axis=1)], axis=0)
            vcat = jnp.concatenate([v_lo[keys], v_hi[keys]], axis=0)
            o = _dot(p, vcat)
            oa = o[:w] * jnp.where(lo_q, 1.0 / d0, 1.0 / d1)
            ob = o[w:] * jnp.where(lo_q, 1.0 / d2, 1.0 / d3)
            o_ref[rows, (2 * g) * LANES:(2 * g + 1) * LANES] = oa.astype(o_ref.dtype)
            o_ref[rows, (2 * g + 1) * LANES:(2 * g + 2) * LANES] = ob.astype(o_ref.dtype)


def _attn_prompt(proj, sinks, nq):
    b, l, _ = proj.shape
    w = WINDOW
    nsteps = l // (nq * w)
    kcol, vcol = COL_K // KV_DIM, COL_V // KV_DIM
    prev = lambda bi, i: jnp.maximum(nq * i - 1, 0)
    return pl.pallas_call(
        functools.partial(_attn_prompt_kernel, nq=nq),
        grid=(b, nsteps),
        in_specs=[
            pl.BlockSpec(memory_space=pltpu.SMEM),
            pl.BlockSpec((None, nq * w, Q_DIM), lambda bi, i: (bi, i, COL_Q // Q_DIM)),
            pl.BlockSpec((None, nq * w, KV_DIM), lambda bi, i: (bi, i, kcol)),
            pl.BlockSpec((None, w, KV_DIM), lambda bi, i: (bi, prev(bi, i), kcol)),
            pl.BlockSpec((None, nq * w, KV_DIM), lambda bi, i: (bi, i, vcol)),
            pl.BlockSpec((None, w, KV_DIM), lambda bi, i: (bi, prev(bi, i), vcol)),
        ],
        out_specs=[
            pl.BlockSpec((None, nq * w, Q_DIM), lambda bi, i: (bi, i, 0)),
            pl.BlockSpec((None, w, KV_DIM), lambda bi, i: (bi, 0, 0)),
            pl.BlockSpec((None, w, KV_DIM), lambda bi, i: (bi, 0, 0)),
        ],
        out_shape=[
            jax.ShapeDtypeStruct((b, l, Q_DIM), BF16),
            jax.ShapeDtypeStruct((b, w, KV_DIM), F32),
            jax.ShapeDtypeStruct((b, w, KV_DIM), F32),
        ],
        compiler_params=_cparams(2),
        name="attn_prompt",
    )(sinks, proj, proj, proj, proj, proj)


def _softplus(v):
    return jnp.maximum(v, 0.0) + jnp.log1p(jnp.exp(-jnp.abs(v)))


def _split3(x):
    hi = x.astype(BF16)
    r = x - hi.astype(F32)
    mid = r.astype(BF16)
    lo = (r - mid.astype(F32)).astype(BF16)
    return hi, mid, lo


def _dot_exact_lhs01(m01, x):
    hi, mid, lo = _split3(x)
    return _dot(m01, hi) + _dot(m01, mid) + _dot(m01, lo)


def _head_expand(col_vals, pair):
    rows = col_vals.shape[0]
    lo = lax.broadcasted_iota(jnp.int32, (rows, LANES), 1) < SSM_HEAD_DIM
    a = jnp.broadcast_to(col_vals[:, 2 * pair:2 * pair + 1], (rows, LANES))
    b = jnp.broadcast_to(col_vals[:, 2 * pair + 1:2 * pair + 2], (rows, LANES))
    return jnp.where(lo, a, b)


def _ssd_prompt_kernel(*refs, nsub):
    nz, nx = SSM_INNER // PIECE, CONV_DIM // PIECE
    z_refs, x_refs = refs[:nz], refs[nz:nz + nx]
    (dt_ref, cw_ref, cb_ref, dtb_ref, alog_ref, dsk_ref, gn_ref,
     y_ref, nssm_ref, nconv_ref, state_ref, carry_ref) = refs[nz + nx:]
    i = pl.program_id(1)
    nc = pl.num_programs(1)
    q = SSD_CHUNK

    @pl.when(i == 0)
    def _():
        state_ref[...] = jnp.zeros_like(state_ref)
        carry_ref[...] = jnp.zeros_like(carry_ref)

    rows_of = lambda rs, sl: jnp.concatenate([r[sl, :] for r in rs], axis=1)
    for s in range(nsub):
        _ssd_chunk(s, rows_of, z_refs, x_refs, dt_ref, cw_ref, cb_ref, dtb_ref, alog_ref, dsk_ref,
                   gn_ref, y_ref, state_ref, carry_ref)
    tail = rows_of(x_refs, slice(nsub * q - 8, nsub * q))
    carry_ref[...] = tail

    @pl.when(i == nc - 1)
    def _():
        nconv_ref[...] = tail
        nssm_ref[...] = state_ref[...]


def _ssd_chunk(s, rows_of, z_refs, x_refs, dt_ref, cw_ref, cb_ref, dtb_ref, alog_ref, dsk_ref,
               gn_ref, y_ref, state_ref, carry_ref):
    q = SSD_CHUNK
    trows = slice(s * q, (s + 1) * q)
    x = rows_of(x_refs, trows)
    z = rows_of(z_refs, trows)
    prev = carry_ref[...] if s == 0 else rows_of(x_refs, slice(s * q - 8, s * q))
    row8 = lax.broadcasted_iota(jnp.int32, (8, CONV_DIM), 0)

    def shifted(k):
        r = pltpu.roll(x, k, 0)
        head = jnp.where(row8 < k, pltpu.roll(prev, k, 0), r[:8])
        return jnp.concatenate([head, r[8:]], axis=0)

    conv = x * cw_ref[CONV_WIDTH - 1:CONV_WIDTH, :] + cb_ref[...]
    for k in range(1, CONV_WIDTH):
        conv = conv + shifted(k) * cw_ref[CONV_WIDTH - 1 - k:CONV_WIDTH - k, :]

    xc = _silu(conv)
    xs = xc[:, :SSM_INNER]
    bm = xc[:, SSM_INNER:SSM_INNER + BC_DIM]
    cm = xc[:, SSM_INNER + BC_DIM:]

    dt = _softplus(dt_ref[trows, :] + dtb_ref[...])
    a = -jnp.exp(alog_ref[...])
    da = dt * a
    ri = lax.broadcasted_iota(jnp.int32, (q, q), 0)
    cj = lax.broadcasted_iota(jnp.int32, (q, q), 1)
    tri = ri >= cj
    cs = _dot_exact_lhs01(tri.astype(BF16), da)
    cs_t = cs.T
    dt_t = dt.T
    cs_last = cs[q - 1:q, :]
    ecs = jnp.exp(cs)
    wgt = dt * jnp.exp(cs_last - cs)
    cdec_t = jnp.exp(cs_t[:, q - 1:q])
    lo = lax.broadcasted_iota(jnp.int32, (q, LANES), 1) < SSM_HEAD_DIM

    hpg = SSM_HEADS // SSM_GROUPS
    gw = hpg * SSM_HEAD_DIM
    for g in range(SSM_GROUPS):
        bg = bm[:, g * SSM_STATE:(g + 1) * SSM_STATE].astype(BF16)
        cg = cm[:, g * SSM_STATE:(g + 1) * SSM_STATE].astype(BF16)
        cb = _dot_nt(cg, bg)
        st = state_ref[g * gw:(g + 1) * gw, :]
        yoff = _dot_nt(cg, st.astype(BF16))
        ys = []
        for pr in range(2):
            pair = 2 * g + pr
            ms = []
            for h in (2 * pair, 2 * pair + 1):
                diff = cs[:, h:h + 1] - cs_t[h:h + 1, :]
                lm = jnp.exp(jnp.where(tri, diff, -jnp.inf))
                ms.append((cb * lm * dt_t[h:h + 1, :]).astype(BF16))
            xp = xs[:, pair * LANES:(pair + 1) * LANES]
            x2 = jnp.concatenate([jnp.where(lo, xp, 0.0), jnp.where(lo, 0.0, xp)],
                                 axis=0).astype(BF16)
            yd = _dot(jnp.concatenate(ms, axis=1), x2)
            yo = yoff[:, pr * LANES:(pr + 1) * LANES] * _head_expand(ecs, pair)
            ys.append(yd + yo + xp * dsk_ref[:, pair * LANES:(pair + 1) * LANES])
        yg = jnp.concatenate(ys, axis=1)
        wx = jnp.concatenate(
            [xs[:, (2 * g + pr) * LANES:(2 * g + pr + 1) * LANES] * _head_expand(wgt, 2 * g + pr)
             for pr in range(2)], axis=1)
        s_new = _dot_tn(wx.astype(BF16), bg)
        for r in range(hpg):
            h = hpg * g + r
            rows = slice(g * gw + r * SSM_HEAD_DIM, g * gw + (r + 1) * SSM_HEAD_DIM)
            state_ref[rows, :] = (st[r * SSM_HEAD_DIM:(r + 1) * SSM_HEAD_DIM, :] * cdec_t[h:h + 1, :]
                                  + s_new[r * SSM_HEAD_DIM:(r + 1) * SSM_HEAD_DIM, :])
        hg = yg * _silu(z[:, g * gw:(g + 1) * gw])
        y_ref[trows, g * gw:(g + 1) * gw] = (_rms(hg) * gn_ref[:, g * gw:(g + 1) * gw]
                                            ).astype(y_ref.dtype)


def _ssd_prompt(proj, dtraw, conv_w, conv_b, dt_bias, a_log, d_skip_x, norm_g, nsub):
    b, l, _ = proj.shape
    q = SSD_CHUNK * nsub
    nc = l // q
    const = lambda bi, i: (0, 0)
    piece = lambda c: pl.BlockSpec((None, q, PIECE), lambda bi, i: (bi, i, c))
    n_pieces = (SSM_INNER + CONV_DIM) // PIECE
    return pl.pallas_call(
        functools.partial(_ssd_prompt_kernel, nsub=nsub),
        grid=(b, nc),
        in_specs=[piece(COL_Z // PIECE + c) for c in range(SSM_INNER // PIECE)] + [
            piece(COL_XBC // PIECE + c) for c in range(CONV_DIM // PIECE)] + [
            pl.BlockSpec((None, q, HEADS_PAD), lambda bi, i: (bi, i, 0)),
            pl.BlockSpec((CONV_WIDTH, CONV_DIM), const),
            pl.BlockSpec((1, CONV_DIM), const),
            pl.BlockSpec((1, HEADS_PAD), const),
            pl.BlockSpec((1, HEADS_PAD), const),
            pl.BlockSpec((1, SSM_INNER), const),
            pl.BlockSpec((1, SSM_INNER), const),
        ],
        out_specs=[
            pl.BlockSpec((None, q, SSM_INNER), lambda bi, i: (bi, i, 0)),
            pl.BlockSpec((None, SSM_INNER, SSM_STATE), lambda bi, i: (bi, 0, 0)),
            pl.BlockSpec((None, 8, CONV_DIM), lambda bi, i: (bi, 0, 0)),
        ],
        out_shape=[
            jax.ShapeDtypeStruct((b, l, SSM_INNER), BF16),
            jax.ShapeDtypeStruct((b, SSM_INNER, SSM_STATE), F32),
            jax.ShapeDtypeStruct((b, 8, CONV_DIM), F32),
        ],
        scratch_shapes=[pltpu.VMEM((SSM_INNER, SSM_STATE), F32),
                        pltpu.VMEM((8, CONV_DIM), F32)],
        compiler_params=_cparams(2),
        name="ssd_prompt",
    )(*([proj] * n_pieces), dtraw, conv_w, conv_b, dt_bias, a_log, d_skip_x, norm_g)


def _outproj_kernel(x_ref, a_ref, y_ref, wa_ref, wy_ref, o_ref):
    o_ref[...] = (x_ref[...] + _dot(a_ref[...].astype(BF16), wa_ref[...])
                  + _dot(y_ref[...].astype(BF16), wy_ref[...]))


def _outproj(x, attn, y, w_out, tm):
    t, d = x.shape
    half = w_out.shape[0] // 2
    return pl.pallas_call(
        _outproj_kernel,
        grid=(t // tm,),
        in_specs=[
            pl.BlockSpec((tm, d), lambda i: (i, 0)),
            pl.BlockSpec((tm, half), lambda i: (i, 0)),
            pl.BlockSpec((tm, half), lambda i: (i, 0)),
            pl.BlockSpec((half, d), lambda i: (0, 0)),
            pl.BlockSpec((half, d), lambda i: (1, 0)),
        ],
        out_specs=pl.BlockSpec((tm, d), lambda i: (i, 0)),
        out_shape=jax.ShapeDtypeStruct((t, d), F32),
        compiler_params=_cparams(1),
        name="outproj",
    )(x, attn, y, w_out, w_out)


def _ffn_kernel(h_ref, g_ref, wg_ref, wu_ref, wd_ref, o_ref, f_ref):
    @pl.when(pl.program_id(1) == 0)
    def _():
        h = h_ref[...]
        f_ref[...] = (_rms(h) * g_ref[...]).astype(BF16)
        o_ref[...] = h

    f = f_ref[...]
    hid = (_silu(_dot(f, wg_ref[...])) * _dot(f, wu_ref[...])).astype(BF16)
    o_ref[...] += _dot(hid, wd_ref[...])


def _ffn(h, g, wg, wu, wd, tm, th):
    t, d = h.shape
    hidden = wg.shape[1]
    return pl.pallas_call(
        _ffn_kernel,
        grid=(t // tm, hidden // th),
        in_specs=[
            pl.BlockSpec((tm, d), lambda i, j: (i, 0)),
            pl.BlockSpec((1, d), lambda i, j: (0, 0)),
            pl.BlockSpec((d, th), lambda i, j: (0, j)),
            pl.BlockSpec((d, th), lambda i, j: (0, j)),
            pl.BlockSpec((th, d), lambda i, j: (j, 0)),
        ],
        out_specs=pl.BlockSpec((tm, d), lambda i, j: (i, 0)),
        out_shape=jax.ShapeDtypeStruct((t, d), F32),
        scratch_shapes=[pltpu.VMEM((tm, d), BF16)],
        compiler_params=_cparams(2),
        name="ffn",
    )(h, g, wg, wu, wd)


def _ple_kernel(h_ref, p_ref, gp_ref, wp_ref, wg_ref, gf_ref, o_ref, *, tn, row_parts):
    tm, d = h_ref.shape
    rp = tm // row_parts
    for r in range(row_parts):
        rows = slice(r * rp, (r + 1) * rp)
        n = (_rms(h_ref[rows, :]) * gp_ref[...]).astype(BF16)
        pb = p_ref[rows, :].astype(BF16)
        ss = jnp.zeros((rp, 1), F32)
        for c in range(d // tn):
            cols = slice(c * tn, (c + 1) * tn)
            gate = _dot(n, wg_ref[:, cols])
            h3 = h_ref[rows, cols] + _dot(pb, wp_ref[:, cols]) * _sigmoid(gate)
            o_ref[rows, cols] = h3
            ss = ss + jnp.sum(h3 * h3, axis=-1, keepdims=True)
        inv = lax.rsqrt(ss * (1.0 / d) + RMS_EPS)
        o_ref[rows, :] = o_ref[rows, :] * inv * gf_ref[...]


def _ple_final(h, p, g_ple, w_ple, w_gate, g_final, tm):
    t, d = h.shape
    pd = p.shape[1]
    const = lambda i: (0, 0)
    resident = dict(pipeline_mode=pl.Buffered(1))
    return pl.pallas_call(
        functools.partial(_ple_kernel, tn=512, row_parts=max(tm // 256, 1)),
        grid=(t // tm,),
        in_specs=[
            pl.BlockSpec((tm, d), lambda i: (i, 0)),
            pl.BlockSpec((tm, pd), lambda i: (i, 0)),
            pl.BlockSpec((1, d), const),
            pl.BlockSpec((pd, d), const, **resident),
            pl.BlockSpec((d, d), const, **resident),
            pl.BlockSpec((1, d), const),
        ],
        out_specs=pl.BlockSpec((tm, d), lambda i: (i, 0)),
        out_shape=jax.ShapeDtypeStruct((t, d), F32),
        compiler_params=_cparams(1),
        name="ple_final",
    )(h, p, g_ple, w_ple, w_gate, g_final)


def _sample_pre_kernel(x_ref, dt_ref, sc_ref, cw_ref, cb_ref, dtb_ref,
                       alog_ref, dsk_ref, exp_ref,
                       xdt_ref, b_ref, c_ref, dec_ref, yp_ref, dech_ref):
    conv = x_ref[...] * cw_ref[CONV_WIDTH - 1:CONV_WIDTH, :] + cb_ref[...]
    for k in range(CONV_WIDTH - 1):
        conv = conv + sc_ref[k] * cw_ref[k:k + 1, :]
    xc = _silu(conv)
    xs = xc[:, :SSM_INNER]
    bm = xc[:, SSM_INNER:SSM_INNER + BC_DIM]
    cm = xc[:, SSM_INNER + BC_DIM:]
    b_ref[...] = bm
    c_ref[...] = cm
    dt = _softplus(dt_ref[...] + dtb_ref[...])
    dec = jnp.exp(dt * (-jnp.exp(alog_ref[...])))
    ex = exp_ref[...]
    dtx = _dot_exact_rhs01(dt, ex)
    dec_ref[...] = _dot_exact_rhs01(dec, ex)
    dech_ref[...] = dec
    xdt = xs * dtx
    xdt_ref[...] = xdt
    gw = SSM_INNER // SSM_GROUPS
    cbs = []
    for g in range(SSM_GROUPS):
        prod = cm[:, g * SSM_STATE:(g + 1) * SSM_STATE] * bm[:, g * SSM_STATE:(g + 1) * SSM_STATE]
        cbs.append(jnp.broadcast_to(jnp.sum(prod, axis=-1, keepdims=True), (prod.shape[0], gw)))
    yp_ref[...] = xdt * jnp.concatenate(cbs, axis=1) + xs * dsk_ref[...]


def _dot_exact_rhs01(x, m01):
    hi, mid, lo = _split3(x)
    return _dot(hi, m01) + _dot(mid, m01) + _dot(lo, m01)


def _sample_pre(xbc, dtraw, sconv_t, conv_w, conv_b, dt_bias, a_log, d_skip_x, expand):
    nb = xbc.shape[0]
    shapes = [(nb, SSM_INNER), (nb, BC_DIM), (nb, BC_DIM),
              (nb, SSM_INNER), (nb, SSM_INNER), (nb, HEADS_PAD)]
    return pl.pallas_call(
        _sample_pre_kernel,
        out_shape=[jax.ShapeDtypeStruct(s, F32) for s in shapes],
        compiler_params=pltpu.CompilerParams(vmem_limit_bytes=VMEM_LIMIT),
        name="sample_pre",
    )(xbc, dtraw, sconv_t, conv_w, conv_b, dt_bias, a_log, d_skip_x, expand)


def _attn_sample_kernel(sink_ref, q_ref, kn_ref, vn_ref, ck_ref, cv_ref, o_ref, nk_ref, nv_ref, *, bb):
    w = WINDOW
    row = lax.broadcasted_iota(jnp.int32, (w, KV_DIM), 0)
    hrow = lax.broadcasted_iota(jnp.int32, (N_HEADS, KV_DIM), 0) // (N_HEADS // N_KV_HEADS)
    hgrp = lax.broadcasted_iota(jnp.int32, (N_HEADS, KV_DIM), 1) // HEAD_DIM
    own = hrow == hgrp
    sink = sink_ref[...] * LOG2E
    for b in range(bb):
        kk = jnp.where(row == w - 1, kn_ref[b:b + 1, :], pltpu.roll(ck_ref[b], w - 1, 0))
        vv = jnp.where(row == w - 1, vn_ref[b:b + 1, :], pltpu.roll(cv_ref[b], w - 1, 0))
        nk_ref[b] = kk
        nv_ref[b] = vv
        qb = q_ref[b]
        qrow = jnp.where(own, jnp.concatenate([qb] * N_KV_HEADS, axis=1), 0.0)
        s = _dot_nt(qrow.astype(BF16), kk.astype(BF16))
        m = jnp.maximum(jnp.max(s, axis=-1, keepdims=True), sink)
        e = jnp.exp2(s - m)
        den = jnp.sum(e, axis=-1, keepdims=True) + jnp.exp2(sink - m)
        of =jnp.where(own, _dot(e.astype(BF16), vv.astype(BF16)), 0.0)
        o = of[:, :HEAD_DIM]
        for g in range(1, N_KV_HEADS):
            o = o + of[:, g * HEAD_DIM:(g + 1) * HEAD_DIM]
        o_ref[b] = o / den


def _attn_sample(sinks_col, q3, knew, vnew, cache_k, cache_v, bb):
    nb = q3.shape[0]
    w = WINDOW
    return pl.pallas_call(
        functools.partial(_attn_sample_kernel, bb=bb),
        grid=(nb // bb,),
        in_specs=[
            pl.BlockSpec((N_HEADS, 1), lambda i: (0, 0)),
            pl.BlockSpec((bb, N_HEADS, HEAD_DIM), lambda i: (i, 0, 0)),
            pl.BlockSpec((bb, KV_DIM), lambda i: (i, 0)),
            pl.BlockSpec((bb, KV_DIM), lambda i: (i, 0)),
            pl.BlockSpec((bb, w, KV_DIM), lambda i: (i, 0, 0)),
            pl.BlockSpec((bb, w, KV_DIM), lambda i: (i, 0, 0)),
        ],
        out_specs=[
            pl.BlockSpec((bb, N_HEADS, HEAD_DIM), lambda i: (i, 0, 0)),
            pl.BlockSpec((bb, w, KV_DIM), lambda i: (i, 0, 0)),
            pl.BlockSpec((bb, w, KV_DIM), lambda i: (i, 0, 0)),
        ],
        out_shape=[
            jax.ShapeDtypeStruct((nb, N_HEADS, HEAD_DIM), F32),
            jax.ShapeDtypeStruct((nb, w, KV_DIM), F32),
            jax.ShapeDtypeStruct((nb, w, KV_DIM), F32),
        ],
        compiler_params=_cparams(1),
        name="attn_sample",
    )(sinks_col, q3, knew, vnew, cache_k, cache_v)


def _ssd_sample_kernel(st_ref, xdt_ref, b_ref, c_ref, dec_ref, dech_ref, yp_ref, z_ref, gn_ref,
                       y_ref, ns_ref, *, bb):
    gw = SSM_INNER // SSM_GROUPS
    grow = lax.broadcasted_iota(jnp.int32, (8, SSM_INNER), 0)
    glane = lax.broadcasted_iota(jnp.int32, (8, SSM_INNER), 1) // gw
    own = grow == glane
    pad = jnp.zeros((8 - SSM_GROUPS, SSM_STATE), F32)
    yoffs = []
    for b in range(bb):
        st = st_ref[b]
        cmat = jnp.concatenate([c_ref[b], pad], axis=0).astype(BF16)
        bmat = jnp.concatenate([b_ref[b], pad], axis=0).astype(BF16)
        r = _dot_nt(cmat, st.astype(BF16))
        yoffs.append(jnp.sum(jnp.where(own, r, 0.0), axis=0, keepdims=True))
        amat = jnp.where(own, jnp.broadcast_to(xdt_ref[b:b + 1, :], (8, SSM_INNER)), 0.0)
        outer = _dot_tn(amat.astype(BF16), bmat)
        for h in range(SSM_HEADS):
            rows = slice(h * SSM_HEAD_DIM, (h + 1) * SSM_HEAD_DIM)
            ns_ref[b, rows, :] = st[rows, :] * dech_ref[b:b + 1, h:h + 1] + outer[rows, :]
    y = yp_ref[...] + jnp.concatenate(yoffs, axis=0) * dec_ref[...]
    hg = y * _silu(z_ref[...])
    outs = []
    for g in range(SSM_GROUPS):
        outs.append(_rms(hg[:, g * gw:(g + 1) * gw]))
    y_ref[...] = jnp.concatenate(outs, axis=1) * gn_ref[...]


def _ssd_sample(state, xdt, b3, c3, decx, dech, ypart, z, norm_g, bb):
    nb = state.shape[0]
    row = lambda i: (i, 0)
    return pl.pallas_call(
        functools.partial(_ssd_sample_kernel, bb=bb),
        grid=(nb // bb,),
        in_specs=[
            pl.BlockSpec((bb, SSM_INNER, SSM_STATE), lambda i: (i, 0, 0)),
            pl.BlockSpec((bb, SSM_INNER), row),
            pl.BlockSpec((bb, SSM_GROUPS, SSM_STATE), lambda i: (i, 0, 0)),
            pl.BlockSpec((bb, SSM_GROUPS, SSM_STATE), lambda i: (i, 0, 0)),
            pl.BlockSpec((bb, SSM_INNER), row),
            pl.BlockSpec((bb, HEADS_PAD), row),
            pl.BlockSpec((bb, SSM_INNER), row),
            pl.BlockSpec((bb, SSM_INNER), row),
            pl.BlockSpec((1, SSM_INNER), lambda i: (0, 0)),
        ],
        out_specs=[
            pl.BlockSpec((bb, SSM_INNER), row),
            pl.BlockSpec((bb, SSM_INNER, SSM_STATE), lambda i: (i, 0, 0)),
        ],
        out_shape=[
            jax.ShapeDtypeStruct((nb, SSM_INNER), F32),
            jax.ShapeDtypeStruct((nb, SSM_INNER, SSM_STATE), F32),
        ],
        compiler_params=_cparams(1),
        name="ssd_sample",
    )(state, xdt, b3, c3, decx, dech, ypart, z, norm_g)


def _row_tile(t, want):
    return want if t % want == 0 else t


def _dense_tail(x, attn, y, p, wts, tm):
    h1 = _outproj(x, attn, y, wts["w_out"], tm)
    h2 = _ffn(h1, wts["g_ffn"], wts["w_gate"], wts["w_up"], wts["w_down"], tm, 512)
    return _ple_final(h2, p, wts["g_ple"], wts["w_ple"], wts["w_ple_gate"], wts["g_final"],
                      _row_tile(x.shape[0], 1024))


def kernel(x_prompt, x_sample, cache_k, cache_v, state_ssm, state_conv, p_prompt, p_sample, w_in, conv_w, conv_b, dt_bias, a_log, d_skip, ssm_norm_g, attn_sinks, w_out, g_mix, g_ffn, w_ffn_gate, w_ffn_up, w_ffn_down, g_ple, w_ple, w_ple_gate, g_final):
    depth = w_in.shape[0]
    assert depth == 1, "single-layer step only"
    bsz, seq, d = x_prompt.shape
    nb, dseq, _ = x_sample.shape
    assert dseq == 1 and seq % SSD_CHUNK == 0 and seq % WINDOW == 0

    w = w_in[0]
    w_main = w[:, :PROJ_DIM].astype(BF16)
    w_dt = jnp.pad(w[:, PROJ_DIM:], ((0, 0), (0, HEADS_PAD - SSM_HEADS))).astype(BF16)
    padh = lambda v: jnp.pad(v, (0, HEADS_PAD - SSM_HEADS)).reshape(1, HEADS_PAD)
    dtb, alog = padh(dt_bias[0]), padh(a_log[0])
    dsk_x = jnp.repeat(d_skip[0], SSM_HEAD_DIM).reshape(1, SSM_INNER)
    gn = ssm_norm_g[0].reshape(1, SSM_INNER)
    cw, cb = conv_w[0], conv_b[0].reshape(1, CONV_DIM)
    wts = dict(
        w_out=w_out[0].astype(BF16), g_ffn=g_ffn[0].reshape(1, d),
        w_gate=w_ffn_gate[0].astype(BF16), w_up=w_ffn_up[0].astype(BF16),
        w_down=w_ffn_down[0].astype(BF16), g_ple=g_ple[0].reshape(1, d),
        w_ple=w_ple[0].astype(BF16), w_ple_gate=w_ple_gate[0].astype(BF16),
        g_final=g_final.reshape(1, d))
    gmix = g_mix[0].reshape(1, d)
    sinks = attn_sinks[0]

    tp = bsz * seq
    xp = x_prompt.reshape(tp, d)
    tm_in = _row_tile(seq, 1024)
    proj, dtraw = _inproj(xp, gmix, w_main, w_dt, _rope_tables(jnp.arange(seq)), tm_in)
    proj3 = proj.reshape(bsz, seq, PROJ_DIM)
    attn, nk_p, nv_p = _attn_prompt(proj3, sinks, ATTN_BLOCKS)
    yp, nssm_p, nconv_p = _ssd_prompt(proj3, dtraw.reshape(bsz, seq, HEADS_PAD), cw, cb, dtb, alog,
                                      dsk_x, gn, SSD_CHUNKS)
    tm = _row_tile(tp, 512)
    y_prompt = _dense_tail(xp, attn.reshape(tp, Q_DIM), yp.reshape(tp, SSM_INNER),
                           p_prompt[0].reshape(tp, -1), wts, tm).reshape(bsz, seq, d)

    xs = x_sample.reshape(nb, d)
    tab_s = _rope_tables(jnp.full((nb,), PAST_LEN, jnp.int32))
    proj_s, dtraw_s = _inproj(xs, gmix, w_main, w_dt, tab_s, nb)
    window = cache_k.shape[2]
    sconv_t = jnp.transpose(state_conv[0], (1, 0, 2))
    expand = (jnp.arange(HEADS_PAD)[:, None] == (jnp.arange(SSM_INNER) // SSM_HEAD_DIM)[None, :]
              ).astype(BF16)
    q_s = proj_s[:, COL_Q:COL_Q + Q_DIM]
    k_s = proj_s[:, COL_K:COL_K + KV_DIM]
    v_s = proj_s[:, COL_V:COL_V + KV_DIM]
    z_s = proj_s[:, COL_Z:COL_Z + SSM_INNER]
    xbc_s = proj_s[:, COL_XBC:COL_XBC + CONV_DIM]
    xdt, bm, cm, decx, ypart, dech = _sample_pre(
        xbc_s, dtraw_s, sconv_t, cw, cb, dtb, alog, dsk_x, expand)
    bb = 8 if nb % 8 == 0 else nb
    attn_s, nk_s, nv_s = _attn_sample(
        sinks.reshape(N_HEADS, 1), q_s.reshape(nb, N_HEADS, HEAD_DIM), k_s, v_s,
        cache_k[0].reshape(nb, window, KV_DIM), cache_v[0].reshape(nb, window, KV_DIM), bb)
    ys, nssm_s = _ssd_sample(
        state_ssm[0].reshape(nb, SSM_INNER, SSM_STATE), xdt,
        bm.reshape(nb, SSM_GROUPS, SSM_STATE), cm.reshape(nb, SSM_GROUPS, SSM_STATE),
        decx, dech, ypart, z_s, gn, bb)
    y_sample = _dense_tail(xs, attn_s.reshape(nb, Q_DIM), ys, p_sample[0].reshape(nb, -1), wts,
                           nb).reshape(nb, 1, d)
    nconv_s = jnp.concatenate([state_conv[0][:, 1:], xbc_s[:, None, :]], axis=1)

    kv5 = lambda t: t.reshape(1, t.shape[0], window, N_KV_HEADS, HEAD_DIM)
    return (y_prompt, y_sample,
            kv5(nk_p), kv5(nv_p),
            nssm_p.reshape(1, bsz, SSM_HEADS, SSM_HEAD_DIM, SSM_STATE),
            nconv_p[None, :, 8 - (CONV_WIDTH - 1):, :],
            kv5(nk_s), kv5(nv_s),
            nssm_s.reshape(1, nb, SSM_HEADS, SSM_HEAD_DIM, SSM_STATE),
            nconv_s[None])
```

```python
import functools

import numpy as np
import jax
import jax.numpy as jnp
from jax import lax
from jax.experimental import pallas as pl
from jax.experimental.pallas import tpu as pltpu

F32 = jnp.float32
BF16 = jnp.bfloat16

HEAD_DIM = 64
N_HEADS = 16
N_KV_HEADS = 4
WINDOW = 128
ROPE_DIM = 16
ROPE_THETA = 500000.0
SSM_HEADS = 16
SSM_HEAD_DIM = 64
SSM_GROUPS = 4
SSM_STATE = 128
CONV_WIDTH = 4
SSD_CHUNK = 128
RMS_EPS = 1e-6
PAST_LEN = 16384
LOG2E = 1.4426950408889634

Q_DIM = N_HEADS * HEAD_DIM
KV_DIM = N_KV_HEADS * HEAD_DIM
SSM_INNER = SSM_HEADS * SSM_HEAD_DIM
BC_DIM = SSM_GROUPS * SSM_STATE
CONV_DIM = SSM_INNER + 2 * BC_DIM
LANES = 128
HEADS_PAD = LANES

COL_Q = 0
COL_K = Q_DIM
COL_V = COL_K + KV_DIM
COL_Z = COL_V + KV_DIM
COL_XBC = COL_Z + SSM_INNER
PROJ_DIM = COL_XBC + CONV_DIM
PIECE = 512

INPROJ_TN = Q_DIM + 2 * KV_DIM
Q_SCALE = HEAD_DIM ** -0.5 * LOG2E
ATTN_BLOCKS = 2
SSD_CHUNKS = 4
VMEM_LIMIT = 56 * 1024 * 1024


def _cparams(n_axes):
    return pltpu.CompilerParams(
        dimension_semantics=("arbitrary",) * n_axes, vmem_limit_bytes=VMEM_LIMIT)


def _rms(x):
    return x * lax.rsqrt(jnp.mean(x * x, axis=-1, keepdims=True) + RMS_EPS)


def _sigmoid(x):
    return 1.0 / (1.0 + jnp.exp(-x))


def _silu(x):
    return x * _sigmoid(x)


def _dot(a, b):
    return jnp.dot(a, b, preferred_element_type=F32)


def _dot_nt(a, b):
    return lax.dot_general(a, b, (((1,), (1,)), ((), ())), preferred_element_type=F32)


def _dot_tn(a, b):
    return lax.dot_general(a, b, (((0,), (0,)), ((), ())), preferred_element_type=F32)


def _split3(x):
    hi = x.astype(BF16)
    r = x - hi.astype(F32)
    mid = r.astype(BF16)
    lo = (r - mid.astype(F32)).astype(BF16)
    return hi, mid, lo


def _dot_exact_lhs01(m01, x):
    hi, mid, lo = _split3(x)
    return _dot(m01, hi) + _dot(m01, mid) + _dot(m01, lo)


def _dot_exact_rhs01(x, m01):
    hi, mid, lo = _split3(x)
    return _dot(hi, m01) + _dot(mid, m01) + _dot(lo, m01)


def _inproj_kernel(x_ref, g_ref, w_ref, wdt_ref, tab_ref, o_ref, odt_ref, u_ref):
    j = pl.program_id(1)

    @pl.when(j == 0)
    def _():
        u = (_rms(x_ref[...]) * g_ref[...]).astype(BF16)
        u_ref[...] = u
        odt_ref[...] = _dot(u, wdt_ref[...])
        res = _dot(u, w_ref[...])
        tab = tab_ref[...]
        for c in range((Q_DIM + KV_DIM) // LANES):
            cols = slice(c * LANES, (c + 1) * LANES)
            r = _rope(res[:, cols], tab)
            o_ref[:, cols] = r * Q_SCALE if c < Q_DIM // LANES else r
        o_ref[:, Q_DIM + KV_DIM:] = res[:, Q_DIM + KV_DIM:]

    @pl.when(j > 0)
    def _():
        o_ref[...] = _dot(u_ref[...], w_ref[...])


def _inproj(x, g, w, wdt, tab, tm):
    t, d = x.shape
    n = PROJ_DIM
    tn = INPROJ_TN
    nt = tab.shape[0] // tm
    return pl.pallas_call(
        _inproj_kernel,
        grid=(t // tm, n // tn),
        in_specs=[
            pl.BlockSpec((tm, d), lambda i, j: (i, 0)),
            pl.BlockSpec((1, d), lambda i, j: (0, 0)),
            pl.BlockSpec((d, tn), lambda i, j: (0, j)),
            pl.BlockSpec((d, HEADS_PAD), lambda i, j: (0, 0)),
            pl.BlockSpec((tm, 3 * LANES), lambda i, j: (i % nt, 0)),
        ],
        out_specs=[
            pl.BlockSpec((tm, tn), lambda i, j: (i, j)),
            pl.BlockSpec((tm, HEADS_PAD), lambda i, j: (i, 0)),
        ],
        out_shape=[
            jax.ShapeDtypeStruct((t, n), F32),
            jax.ShapeDtypeStruct((t, HEADS_PAD), F32),
        ],
        scratch_shapes=[pltpu.VMEM((tm, d), BF16)],
        compiler_params=_cparams(2),
        name="inproj",
    )(x, g, w, wdt, tab)


def _rope_tables(pos):
    half = ROPE_DIM // 2
    inv = ROPE_THETA ** (-jnp.arange(half, dtype=F32) * (2.0 / ROPE_DIM))
    ang = pos.astype(F32)[:, None] * inv[None, :]
    cs = jnp.concatenate([jnp.cos(ang), jnp.sin(ang)], axis=1)
    expand = np.zeros((2 * half, 3 * LANES), np.float32)
    base = np.zeros((1, 3 * LANES), np.float32)
    for lane in range(LANES):
        m = lane % HEAD_DIM
        if m >= ROPE_DIM:
            base[0, lane] = 1.0
            continue
        expand[m % half, lane] = 1.0
        if m < half:
            expand[half + m, LANES + lane] = -1.0
        else:
            expand[m, 2 * LANES + lane] = 1.0
    cs3 = jnp.concatenate(_split3(cs), axis=1)
    expand3 = jnp.asarray(np.concatenate([expand] * 3, axis=0), BF16)
    return jnp.dot(cs3, expand3, preferred_element_type=F32) + base


def _rope(x, tab):
    half = ROPE_DIM // 2
    c, sa, sb = tab[:, :LANES], tab[:, LANES:2 * LANES], tab[:, 2 * LANES:]
    return x * c + pltpu.roll(x, LANES - half, 1) * sa + pltpu.roll(x, half, 1) * sb


def _softmax_fold(s, band, prev_bias, sink2):
    sp = s[:, :WINDOW] if prev_bias is None else s[:, :WINDOW] + prev_bias
    t = jnp.where(band, sp, s[:, WINDOW:])
    m = jnp.maximum(jnp.max(t, axis=-1, keepdims=True), sink2)
    e = jnp.exp2(t - m)
    den = jnp.sum(e, axis=-1, keepdims=True) + jnp.exp2(sink2 - m)
    p = jnp.concatenate([jnp.where(band, e, 0.0), jnp.where(band, 0.0, e)], axis=1)
    return p.astype(BF16), den


def _attn_prompt_kernel(sink_ref, q_ref, kc_ref, kp_ref, vc_ref, vp_ref,
                        o_ref, nk_ref, nv_ref, *, nq):
    i = pl.program_id(1)
    nsteps = pl.num_programs(1)
    w = WINDOW
    kcr = kc_ref[...]
    kpr = kp_ref[...]
    vc = vc_ref[...]

    @pl.when(i == nsteps - 1)
    def _():
        nk_ref[...] = kcr[(nq - 1) * w:]
        nv_ref[...] = vc[(nq - 1) * w:]

    kall = jnp.concatenate([kpr, kcr], axis=0)
    vall = jnp.concatenate([vp_ref[...], vc], axis=0)
    lo = lax.broadcasted_iota(jnp.int32, ((nq + 1) * w, LANES), 1) < HEAD_DIM
    lo_q = lax.broadcasted_iota(jnp.int32, (w, LANES), 1) < HEAD_DIM
    band = (lax.broadcasted_iota(jnp.int32, (w, w), 1) > lax.broadcasted_iota(jnp.int32, (w, w), 0))
    first_bias = jnp.where(i == 0, -jnp.inf, 0.0)

    for g in range(N_KV_HEADS):
        col, odd = g // 2, g % 2
        kg = kall[:, col * LANES:(col + 1) * LANES]
        vg = vall[:, col * LANES:(col + 1) * LANES]
        kg_sw = pltpu.roll(kg, HEAD_DIM, 1)
        vg_sw = pltpu.roll(vg, HEAD_DIM, 1)
        k_lo = jnp.where(lo, kg_sw if odd else kg, 0.0).astype(BF16)
        k_hi = jnp.where(lo, 0.0, kg if odd else kg_sw).astype(BF16)
        v_lo = jnp.where(lo, vg_sw if odd else vg, 0.0).astype(BF16)
        v_hi = jnp.where(lo, 0.0, vg if odd else vg_sw).astype(BF16)
        sinks2 = [sink_ref[4 * g + r] * LOG2E for r in range(4)]
        for s in range(nq):
            rows = slice(s * w, (s + 1) * w)
            keys = slice(s * w, (s + 2) * w)
            pb = first_bias if s == 0 else None
            qst = jnp.concatenate([q_ref[rows, (2 * g) * LANES:(2 * g + 1) * LANES],
                                   q_ref[rows, (2 * g + 1) * LANES:(2 * g + 2) * LANES]],
                                  axis=0).astype(BF16)
            s_lo = _dot_nt(qst, k_lo[keys])
            s_hi = _dot_nt(qst, k_hi[keys])
            e0, d0 = _softmax_fold(s_lo[:w], band, pb, sinks2[0])
            e1, d1 = _softmax_fold(s_hi[:w], band, pb, sinks2[1])
            e2, d2 = _softmax_fold(s_lo[w:], band, pb, sinks2[2])
            e3, d3 = _softmax_fold(s_hi[w:], band, pb, sinks2[3])
            p = jnp.concatenate([jnp.concatenate([e0, e1], axis=1),
                                 jnp.concatenate([e2, e3], axis=1)], axis=0)
            vcat = jnp.concatenate([v_lo[keys], v_hi[keys]], axis=0)
            o = _dot(p, vcat)
            oa = o[:w] / jnp.where(lo_q, d0, d1)
            ob = o[w:] / jnp.where(lo_q, d2, d3)
            o_ref[rows, (2 * g) * LANES:(2 * g + 1) * LANES] = oa.astype(o_ref.dtype)
            o_ref[rows, (2 * g + 1) * LANES:(2 * g + 2) * LANES] = ob.astype(o_ref.dtype)


def _attn_prompt(proj, sinks, nq):
    b, l, _ = proj.shape
    w = WINDOW
    nsteps = l // (nq * w)
    kcol, vcol = COL_K // KV_DIM, COL_V // KV_DIM
    prev = lambda bi, i: jnp.maximum(nq * i - 1, 0)
    return pl.pallas_call(
        functools.partial(_attn_prompt_kernel, nq=nq),
        grid=(b, nsteps),
        in_specs=[
            pl.BlockSpec(memory_space=pltpu.SMEM),
            pl.BlockSpec((None, nq * w, Q_DIM), lambda bi, i: (bi, i, COL_Q // Q_DIM)),
            pl.BlockSpec((None, nq * w, KV_DIM), lambda bi, i: (bi, i, kcol)),
            pl.BlockSpec((None, w, KV_DIM), lambda bi, i: (bi, prev(bi, i), kcol)),
            pl.BlockSpec((None, nq * w, KV_DIM), lambda bi, i: (bi, i, vcol)),
            pl.BlockSpec((None, w, KV_DIM), lambda bi, i: (bi, prev(bi, i), vcol)),
        ],
        out_specs=[
            pl.BlockSpec((None, nq * w, Q_DIM), lambda bi, i: (bi, i, 0)),
            pl.BlockSpec((None, w, KV_DIM), lambda bi, i: (bi, 0, 0)),
            pl.BlockSpec((None, w, KV_DIM), lambda bi, i: (bi, 0, 0)),
        ],
        out_shape=[
            jax.ShapeDtypeStruct((b, l, Q_DIM), BF16),
            jax.ShapeDtypeStruct((b, w, KV_DIM), F32),
            jax.ShapeDtypeStruct((b, w, KV_DIM), F32),
        ],
        compiler_params=_cparams(2),
        name="attn_prompt",
    )(sinks, proj, proj, proj, proj, proj)


def _softplus(v):
    return jnp.maximum(v, 0.0) + jnp.log1p(jnp.exp(-jnp.abs(v)))


def _head_expand(vals, ex2_ref):
    hi = vals.astype(BF16)
    mid = (vals - hi.astype(F32)).astype(BF16)
    return _dot(jnp.concatenate([hi, mid], axis=1), ex2_ref[...])


def _ssd_prompt_kernel(*refs, nsub):
    nz, nx = SSM_INNER // PIECE, CONV_DIM // PIECE
    z_refs, x_refs = refs[:nz], refs[nz:nz + nx]
    (dt_ref, cw_ref, cb_ref, dtb_ref, alog_ref, dsk_ref, gn_ref, ex2_ref,
     y_ref, nssm_ref, nconv_ref, state_ref, carry_ref) = refs[nz + nx:]
    i = pl.program_id(1)
    nc = pl.num_programs(1)
    q = SSD_CHUNK

    @pl.when(i == 0)
    def _():
        state_ref[...] = jnp.zeros_like(state_ref)
        carry_ref[...] = jnp.zeros_like(carry_ref)

    dtp = dt_ref[:q, :]
    for s in range(1, nsub):
        dtp = dtp + pltpu.roll(dt_ref[s * q:(s + 1) * q, :], s * SSM_HEADS, 1)
    dt = _softplus(dtp + dtb_ref[...])
    da = dt * (-jnp.exp(alog_ref[...]))
    tri = (lax.broadcasted_iota(jnp.int32, (q, q), 0) >= lax.broadcasted_iota(jnp.int32, (q, q), 1))
    cs = _dot_exact_lhs01(tri.astype(BF16), da)
    cs_t = cs.T
    sc = dict(
        tri=tri, cs=cs, cs_t=cs_t, dt_t=dt.T,
        cdec_t=jnp.exp(cs_t[:, q - 1:q]),
        ecs_x=_head_expand(jnp.exp(cs), ex2_ref),
        wgt_x=_head_expand(dt * jnp.exp(cs[q - 1:q, :] - cs), ex2_ref))

    rows_of = lambda rs, sl: jnp.concatenate([r[sl, :] for r in rs], axis=1)
    for s in range(nsub):
        _ssd_chunk(s, sc, rows_of, z_refs, x_refs, cw_ref, cb_ref, dsk_ref,
                   gn_ref, y_ref, state_ref, carry_ref)
    tail = rows_of(x_refs, slice(nsub * q - 8, nsub * q))
    carry_ref[...] = tail

    @pl.when(i == nc - 1)
    def _():
        nconv_ref[...] = tail
        nssm_ref[...] = state_ref[...]


def _ssd_chunk(s, sc, rows_of, z_refs, x_refs, cw_ref, cb_ref, dsk_ref,
               gn_ref, y_ref, state_ref, carry_ref):
    q = SSD_CHUNK
    tri, cs, cs_t, dt_t, cdec_t = sc["tri"], sc["cs"], sc["cs_t"], sc["dt_t"], sc["cdec_t"]
    hoff = s * SSM_HEADS
    xoff = s * SSM_INNER
    trows = slice(s * q, (s + 1) * q)
    x = rows_of(x_refs, trows)
    z = rows_of(z_refs, trows)
    prev = carry_ref[...] if s == 0 else rows_of(x_refs, slice(s * q - 8, s * q))
    row8 = lax.broadcasted_iota(jnp.int32, (8, CONV_DIM), 0)

    def shifted(k):
        r = pltpu.roll(x, k, 0)
        head = jnp.where(row8 < k, pltpu.roll(prev, k, 0), r[:8])
        return jnp.concatenate([head, r[8:]], axis=0)

    conv = x * cw_ref[CONV_WIDTH - 1:CONV_WIDTH, :] + cb_ref[...]
    for k in range(1, CONV_WIDTH):
        conv = conv + shifted(k) * cw_ref[CONV_WIDTH - 1 - k:CONV_WIDTH - k, :]

    xc = _silu(conv)
    xs = xc[:, :SSM_INNER]
    bm = xc[:, SSM_INNER:SSM_INNER + BC_DIM]
    cm = xc[:, SSM_INNER + BC_DIM:]

    lo = lax.broadcasted_iota(jnp.int32, (q, LANES), 1) < SSM_HEAD_DIM

    hpg = SSM_HEADS // SSM_GROUPS
    gw = hpg * SSM_HEAD_DIM
    for g in range(SSM_GROUPS):
        bg = bm[:, g * SSM_STATE:(g + 1) * SSM_STATE].astype(BF16)
        cg = cm[:, g * SSM_STATE:(g + 1) * SSM_STATE].astype(BF16)
        cb = _dot_nt(cg, bg)
        st = state_ref[g * gw:(g + 1) * gw, :]
        yoff = _dot_nt(cg, st.astype(BF16))
        ys = []
        for pr in range(2):
            pair = 2 * g + pr
            ms = []
            for h in (hoff + 2 * pair, hoff + 2 * pair + 1):
                diff = cs[:, h:h + 1] - cs_t[h:h + 1, :]
                lm = jnp.exp(jnp.where(tri, diff, -jnp.inf))
                ms.append((cb * lm * dt_t[h:h + 1, :]).astype(BF16))
            xp = xs[:, pair * LANES:(pair + 1) * LANES]
            x2 = jnp.concatenate([jnp.where(lo, xp, 0.0), jnp.where(lo, 0.0, xp)],
                                 axis=0).astype(BF16)
            yd = _dot(jnp.concatenate(ms, axis=1), x2)
            yo = yoff[:, pr * LANES:(pr + 1) * LANES] * sc["ecs_x"][
                :, xoff + pair * LANES:xoff + (pair + 1) * LANES]
            ys.append(yd + yo + xp * dsk_ref[:, pair * LANES:(pair + 1) * LANES])
        yg = jnp.concatenate(ys, axis=1)
        wx = xs[:, g * gw:(g + 1) * gw] * sc["wgt_x"][:, xoff + g * gw:xoff + (g + 1) * gw]
        s_new = _dot_tn(wx.astype(BF16), bg)
        for r in range(hpg):
            h = hoff + hpg * g + r
            rows = slice(g * gw + r * SSM_HEAD_DIM, g * gw + (r + 1) * SSM_HEAD_DIM)
            state_ref[rows, :] = (st[r * SSM_HEAD_DIM:(r + 1) * SSM_HEAD_DIM, :] * cdec_t[h:h + 1, :]
                                  + s_new[r * SSM_HEAD_DIM:(r + 1) * SSM_HEAD_DIM, :])
        hg = yg * _silu(z[:, g * gw:(g + 1) * gw])
        y_ref[trows, g * gw:(g + 1) * gw] = (_rms(hg) * gn_ref[:, g * gw:(g + 1) * gw]
                                            ).astype(y_ref.dtype)


def _ssd_prompt(proj, dtraw, conv_w, conv_b, dt_bias_t, a_log_t, d_skip_x, norm_g, nsub):
    assert nsub * SSM_HEADS <= LANES
    src = np.arange(nsub * SSM_INNER) // SSM_HEAD_DIM
    ex = (np.arange(LANES)[:, None] == src[None, :]).astype(np.float32)
    ex2 = jnp.asarray(np.concatenate([ex, ex], axis=0), BF16)
    b, l, _ = proj.shape
    q = SSD_CHUNK * nsub
    nc = l // q
    const = lambda bi, i: (0, 0)
    piece = lambda c: pl.BlockSpec((None, q, PIECE), lambda bi, i: (bi, i, c))
    n_pieces = (SSM_INNER + CONV_DIM) // PIECE
    return pl.pallas_call(
        functools.partial(_ssd_prompt_kernel, nsub=nsub),
        grid=(b, nc),
        in_specs=[piece(COL_Z // PIECE + c) for c in range(SSM_INNER // PIECE)] + [
            piece(COL_XBC // PIECE + c) for c in range(CONV_DIM // PIECE)] + [
            pl.BlockSpec((None, q, HEADS_PAD), lambda bi, i: (bi, i, 0)),
            pl.BlockSpec((CONV_WIDTH, CONV_DIM), const),
            pl.BlockSpec((1, CONV_DIM), const),
            pl.BlockSpec((1, HEADS_PAD), const),
            pl.BlockSpec((1, HEADS_PAD), const),
            pl.BlockSpec((1, SSM_INNER), const),
            pl.BlockSpec((1, SSM_INNER), const),
            pl.BlockSpec((2 * LANES, nsub * SSM_INNER), const),
        ],
        out_specs=[
            pl.BlockSpec((None, q, SSM_INNER), lambda bi, i: (bi, i, 0)),
            pl.BlockSpec((None, SSM_INNER, SSM_STATE), lambda bi, i: (bi, 0, 0)),
            pl.BlockSpec((None, 8, CONV_DIM), lambda bi, i: (bi, 0, 0)),
        ],
        out_shape=[
            jax.ShapeDtypeStruct((b, l, SSM_INNER), BF16),
            jax.ShapeDtypeStruct((b, SSM_INNER, SSM_STATE), F32),
            jax.ShapeDtypeStruct((b, 8, CONV_DIM), F32),
        ],
        scratch_shapes=[pltpu.VMEM((SSM_INNER, SSM_STATE), F32),
                        pltpu.VMEM((8, CONV_DIM), F32)],
        compiler_params=_cparams(2),
        name="ssd_prompt",
    )(*([proj] * n_pieces), dtraw, conv_w, conv_b, dt_bias_t, a_log_t, d_skip_x, norm_g, ex2)


def _outproj_kernel(x_ref, a_ref, y_ref, wa_ref, wy_ref, o_ref):
    o_ref[...] = (x_ref[...] + _dot(a_ref[...].astype(BF16), wa_ref[...])
                  + _dot(y_ref[...].astype(BF16), wy_ref[...]))


def _outproj(x, attn, y, w_out, tm):
    t, d = x.shape
    half = w_out.shape[0] // 2
    return pl.pallas_call(
        _outproj_kernel,
        grid=(t // tm,),
        in_specs=[
            pl.BlockSpec((tm, d), lambda i: (i, 0)),
            pl.BlockSpec((tm, half), lambda i: (i, 0)),
            pl.BlockSpec((tm, half), lambda i: (i, 0)),
            pl.BlockSpec((half, d), lambda i: (0, 0)),
            pl.BlockSpec((half, d), lambda i: (1, 0)),
        ],
        out_specs=pl.BlockSpec((tm, d), lambda i: (i, 0)),
        out_shape=jax.ShapeDtypeStruct((t, d), F32),
        compiler_params=_cparams(1),
        name="outproj",
    )(x, attn, y, w_out, w_out)


def _ffn_kernel(h_ref, g_ref, wg_ref, wu_ref, wd_ref, o_ref, f_ref):
    @pl.when(pl.program_id(1) == 0)
    def _():
        h = h_ref[...]
        f_ref[...] = (_rms(h) * g_ref[...]).astype(BF16)
        o_ref[...] = h

    f = f_ref[...]
    hid = (_silu(_dot(f, wg_ref[...])) * _dot(f, wu_ref[...])).astype(BF16)
    o_ref[...] += _dot(hid, wd_ref[...])


def _ffn(h, g, wg, wu, wd, tm, th):
    t, d = h.shape
    hidden = wg.shape[1]
    return pl.pallas_call(
        _ffn_kernel,
        grid=(t // tm, hidden // th),
        in_specs=[
            pl.BlockSpec((tm, d), lambda i, j: (i, 0)),
            pl.BlockSpec((1, d), lambda i, j: (0, 0)),
            pl.BlockSpec((d, th), lambda i, j: (0, j)),
            pl.BlockSpec((d, th), lambda i, j: (0, j)),
            pl.BlockSpec((th, d), lambda i, j: (j, 0)),
        ],
        out_specs=pl.BlockSpec((tm, d), lambda i, j: (i, 0)),
        out_shape=jax.ShapeDtypeStruct((t, d), F32),
        scratch_shapes=[pltpu.VMEM((tm, d), BF16)],
        compiler_params=_cparams(2),
        name="ffn",
    )(h, g, wg, wu, wd)


def _ple_kernel(h_ref, p_ref, gp_ref, wp_ref, wg_ref, gf_ref, o_ref, *, tn, row_parts):
    tm, d = h_ref.shape
    rp = tm // row_parts
    for r in range(row_parts):
        rows = slice(r * rp, (r + 1) * rp)
        n = (_rms(h_ref[rows, :]) * gp_ref[...]).astype(BF16)
        pb = p_ref[rows, :].astype(BF16)
        ss = jnp.zeros((rp, 1), F32)
        for c in range(d // tn):
            cols = slice(c * tn, (c + 1) * tn)
            gate = _dot(n, wg_ref[:, cols])
            h3 = h_ref[rows, cols] + _dot(pb, wp_ref[:, cols]) * _sigmoid(gate)
            o_ref[rows, cols] = h3
            ss = ss + jnp.sum(h3 * h3, axis=-1, keepdims=True)
        inv = lax.rsqrt(ss * (1.0 / d) + RMS_EPS)
        o_ref[rows, :] = o_ref[rows, :] * inv * gf_ref[...]


def _ple_final(h, p, g_ple, w_ple, w_gate, g_final, tm):
    t, d = h.shape
    pd = p.shape[1]
    const = lambda i: (0, 0)
    resident = dict(pipeline_mode=pl.Buffered(1))
    return pl.pallas_call(
        functools.partial(_ple_kernel, tn=512, row_parts=max(tm // 256, 1)),
        grid=(t // tm,),
        in_specs=[
            pl.BlockSpec((tm, d), lambda i: (i, 0)),
            pl.BlockSpec((tm, pd), lambda i: (i, 0)),
            pl.BlockSpec((1, d), const),
            pl.BlockSpec((pd, d), const, **resident),
            pl.BlockSpec((d, d), const, **resident),
            pl.BlockSpec((1, d), const),
        ],
        out_specs=pl.BlockSpec((tm, d), lambda i: (i, 0)),
        out_shape=jax.ShapeDtypeStruct((t, d), F32),
        compiler_params=_cparams(1),
        name="ple_final",
    )(h, p, g_ple, w_ple, w_gate, g_final)


def _sample_pre_kernel(x_ref, dt_ref, sc_ref, cw_ref, cb_ref, dtb_ref,
                       alog_ref, dsk_ref, exp_ref,
                       xdt_ref, b_ref, c_ref, dec_ref, yp_ref, dech_ref):
    conv = x_ref[...] * cw_ref[CONV_WIDTH - 1:CONV_WIDTH, :] + cb_ref[...]
    for k in range(CONV_WIDTH - 1):
        conv = conv + sc_ref[k] * cw_ref[k:k + 1, :]
    xc = _silu(conv)
    xs = xc[:, :SSM_INNER]
    bm = xc[:, SSM_INNER:SSM_INNER + BC_DIM]
    cm = xc[:, SSM_INNER + BC_DIM:]
    b_ref[...] = bm
    c_ref[...] = cm
    dt = _softplus(dt_ref[...] + dtb_ref[...])
    dec = jnp.exp(dt * (-jnp.exp(alog_ref[...])))
    ex = exp_ref[...]
    dtx = _dot_exact_rhs01(dt, ex)
    dec_ref[...] = _dot_exact_rhs01(dec, ex)
    dech_ref[...] = dec
    xdt = xs * dtx
    xdt_ref[...] = xdt
    gw = SSM_INNER // SSM_GROUPS
    cbs = []
    for g in range(SSM_GROUPS):
        prod = cm[:, g * SSM_STATE:(g + 1) * SSM_STATE] * bm[:, g * SSM_STATE:(g + 1) * SSM_STATE]
        cbs.append(jnp.broadcast_to(jnp.sum(prod, axis=-1, keepdims=True), (prod.shape[0], gw)))
    yp_ref[...] = xdt * jnp.concatenate(cbs, axis=1) + xs * dsk_ref[...]


def _sample_pre(xbc, dtraw, sconv_t, conv_w, conv_b, dt_bias, a_log, d_skip_x, expand):
    nb = xbc.shape[0]
    shapes = [(nb, SSM_INNER), (nb, BC_DIM), (nb, BC_DIM),
              (nb, SSM_INNER), (nb, SSM_INNER), (nb, HEADS_PAD)]
    return pl.pallas_call(
        _sample_pre_kernel,
        out_shape=[jax.ShapeDtypeStruct(s, F32) for s in shapes],
        compiler_params=pltpu.CompilerParams(vmem_limit_bytes=VMEM_LIMIT),
        name="sample_pre",
    )(xbc, dtraw, sconv_t, conv_w, conv_b, dt_bias, a_log, d_skip_x, expand)


def _attn_sample_kernel(sink_ref, q_ref, kn_ref, vn_ref, ck_ref, cv_ref, o_ref, nk_ref, nv_ref, *, bb):
    w = WINDOW
    row = lax.broadcasted_iota(jnp.int32, (w, KV_DIM), 0)
    hrow = lax.broadcasted_iota(jnp.int32, (N_HEADS, KV_DIM), 0) // (N_HEADS // N_KV_HEADS)
    hgrp = lax.broadcasted_iota(jnp.int32, (N_HEADS, KV_DIM), 1) // HEAD_DIM
    own = hrow == hgrp
    sink = sink_ref[...] * LOG2E
    for b in range(bb):
        kk = jnp.where(row == w - 1, kn_ref[b:b + 1, :], pltpu.roll(ck_ref[b], w - 1, 0))
        vv = jnp.where(row == w - 1, vn_ref[b:b + 1, :], pltpu.roll(cv_ref[b], w - 1, 0))
        nk_ref[b] = kk
        nv_ref[b] = vv
        qb = q_ref[b]
        qrow = jnp.where(own, jnp.concatenate([qb] * N_KV_HEADS, axis=1), 0.0)
        s = _dot_nt(qrow.astype(BF16), kk.astype(BF16))
        m = jnp.maximum(jnp.max(s, axis=-1, keepdims=True), sink)
        e = jnp.exp2(s - m)
        den = jnp.sum(e, axis=-1, keepdims=True) + jnp.exp2(sink - m)
        of = jnp.where(own, _dot(e.astype(BF16), vv.astype(BF16)), 0.0)
        o = of[:, :HEAD_DIM]
        for g in range(1, N_KV_HEADS):
            o = o + of[:, g * HEAD_DIM:(g + 1) * HEAD_DIM]
        o_ref[b] = o / den


def _attn_sample(sinks_col, q3, knew, vnew, cache_k, cache_v, bb):
    nb = q3.shape[0]
    w = WINDOW
    return pl.pallas_call(
        functools.partial(_attn_sample_kernel, bb=bb),
        grid=(nb // bb,),
        in_specs=[
            pl.BlockSpec((N_HEADS, 1), lambda i: (0, 0)),
            pl.BlockSpec((bb, N_HEADS, HEAD_DIM), lambda i: (i, 0, 0)),
            pl.BlockSpec((bb, KV_DIM), lambda i: (i, 0)),
            pl.BlockSpec((bb, KV_DIM), lambda i: (i, 0)),
            pl.BlockSpec((bb, w, KV_DIM), lambda i: (i, 0, 0)),
            pl.BlockSpec((bb, w, KV_DIM), lambda i: (i, 0, 0)),
        ],
        out_specs=[
            pl.BlockSpec((bb, N_HEADS, HEAD_DIM), lambda i: (i, 0, 0)),
            pl.BlockSpec((bb, w, KV_DIM), lambda i: (i, 0, 0)),
            pl.BlockSpec((bb, w, KV_DIM), lambda i: (i, 0, 0)),
        ],
        out_shape=[
            jax.ShapeDtypeStruct((nb, N_HEADS, HEAD_DIM), F32),
            jax.ShapeDtypeStruct((nb, w, KV_DIM), F32),
            jax.ShapeDtypeStruct((nb, w, KV_DIM), F32),
        ],
        compiler_params=_cparams(1),
        name="attn_sample",
    )(sinks_col, q3, knew, vnew, cache_k, cache_v)


def _ssd_sample_kernel(st_ref, xdt_ref, b_ref, c_ref, dec_ref, dech_ref, yp_ref, z_ref, gn_ref,
                       y_ref, ns_ref, *, bb):
    gw = SSM_INNER // SSM_GROUPS
    grow = lax.broadcasted_iota(jnp.int32, (8, SSM_INNER), 0)
    glane = lax.broadcasted_iota(jnp.int32, (8, SSM_INNER), 1) // gw
    own = grow == glane
    pad = jnp.zeros((8 - SSM_GROUPS, SSM_STATE), F32)
    yoffs = []
    for b in range(bb):
        st = st_ref[b]
        cmat = jnp.concatenate([c_ref[b], pad], axis=0).astype(BF16)
        bmat = jnp.concatenate([b_ref[b], pad], axis=0).astype(BF16)
        r = _dot_nt(cmat, st.astype(BF16))
        yoffs.append(jnp.sum(jnp.where(own, r, 0.0), axis=0, keepdims=True))
        amat = jnp.where(own, jnp.broadcast_to(xdt_ref[b:b + 1, :], (8, SSM_INNER)), 0.0)
        outer = _dot_tn(amat.astype(BF16), bmat)
        for h in range(SSM_HEADS):
            rows = slice(h * SSM_HEAD_DIM, (h + 1) * SSM_HEAD_DIM)
            ns_ref[b, rows, :] = st[rows, :] * dech_ref[b:b + 1, h:h + 1] + outer[rows, :]
    y = yp_ref[...] + jnp.concatenate(yoffs, axis=0) * dec_ref[...]
    hg = y * _silu(z_ref[...])
    outs = []
    for g in range(SSM_GROUPS):
        outs.append(_rms(hg[:, g * gw:(g + 1) * gw]))
    y_ref[...] = jnp.concatenate(outs, axis=1) * gn_ref[...]


def _ssd_sample(state, xdt, b3, c3, decx, dech, ypart, z, norm_g, bb):
    nb = state.shape[0]
    row = lambda i: (i, 0)
    return pl.pallas_call(
        functools.partial(_ssd_sample_kernel, bb=bb),
        grid=(nb // bb,),
        in_specs=[
            pl.BlockSpec((bb, SSM_INNER, SSM_STATE), lambda i: (i, 0, 0)),
            pl.BlockSpec((bb, SSM_INNER), row),
            pl.BlockSpec((bb, SSM_GROUPS, SSM_STATE), lambda i: (i, 0, 0)),
            pl.BlockSpec((bb, SSM_GROUPS, SSM_STATE), lambda i: (i, 0, 0)),
            pl.BlockSpec((bb, SSM_INNER), row),
            pl.BlockSpec((bb, HEADS_PAD), row),
            pl.BlockSpec((bb, SSM_INNER), row),
            pl.BlockSpec((bb, SSM_INNER), row),
            pl.BlockSpec((1, SSM_INNER), lambda i: (0, 0)),
        ],
        out_specs=[
            pl.BlockSpec((bb, SSM_INNER), row),
            pl.BlockSpec((bb, SSM_INNER, SSM_STATE), lambda i: (i, 0, 0)),
        ],
        out_shape=[
            jax.ShapeDtypeStruct((nb, SSM_INNER), F32),
            jax.ShapeDtypeStruct((nb, SSM_INNER, SSM_STATE), F32),
        ],
        compiler_params=_cparams(1),
        name="ssd_sample",
    )(state, xdt, b3, c3, decx, dech, ypart, z, norm_g)


def _row_tile(t, want):
    return want if t % want == 0 else t


def _dense_tail(x, attn, y, p, wts, tm):
    h1 = _outproj(x, attn, y, wts["w_out"], tm)
    h2 = _ffn(h1, wts["g_ffn"], wts["w_gate"], wts["w_up"], wts["w_down"], tm, 512)
    return _ple_final(h2, p, wts["g_ple"], wts["w_ple"], wts["w_ple_gate"], wts["g_final"],
                      _row_tile(x.shape[0], 1024))


def kernel(x_prompt, x_sample, cache_k, cache_v, state_ssm, state_conv, p_prompt, p_sample, w_in, conv_w, conv_b, dt_bias, a_log, d_skip, ssm_norm_g, attn_sinks, w_out, g_mix, g_ffn, w_ffn_gate, w_ffn_up, w_ffn_down, g_ple, w_ple, w_ple_gate, g_final):
    depth = w_in.shape[0]
    assert depth == 1, "single-layer step only"
    bsz, seq, d = x_prompt.shape
    nb, dseq, _ = x_sample.shape
    assert dseq == 1 and seq % SSD_CHUNK == 0 and seq % WINDOW == 0

    w = w_in[0]
    w_main = w.astype(BF16)
    w_dt = jnp.pad(w[:, PROJ_DIM:], ((0, 0), (0, HEADS_PAD - SSM_HEADS))).astype(BF16)
    padh = lambda v: jnp.pad(v, (0, HEADS_PAD - SSM_HEADS)).reshape(1, HEADS_PAD)
    tileh = lambda v: jnp.tile(v, HEADS_PAD // SSM_HEADS).reshape(1, HEADS_PAD)
    dtb, alog = padh(dt_bias[0]), padh(a_log[0])
    dsk_x = jnp.repeat(d_skip[0], SSM_HEAD_DIM).reshape(1, SSM_INNER)
    gn = ssm_norm_g[0].reshape(1, SSM_INNER)
    cw, cb = conv_w[0], conv_b[0].reshape(1, CONV_DIM)
    wts = dict(
        w_out=w_out[0].astype(BF16), g_ffn=g_ffn[0].reshape(1, d),
        w_gate=w_ffn_gate[0].astype(BF16), w_up=w_ffn_up[0].astype(BF16),
        w_down=w_ffn_down[0].astype(BF16), g_ple=g_ple[0].reshape(1, d),
        w_ple=w_ple[0].astype(BF16), w_ple_gate=w_ple_gate[0].astype(BF16),
        g_final=g_final.reshape(1, d))
    gmix = g_mix[0].reshape(1, d)
    sinks = attn_sinks[0]

    tp = bsz * seq
    xp = x_prompt.reshape(tp, d)
    tm_in = _row_tile(seq, 1024)
    proj, dtraw = _inproj(xp, gmix, w_main, w_dt, _rope_tables(jnp.arange(seq)), tm_in)
    proj3 = proj.reshape(bsz, seq, PROJ_DIM)
    attn, nk_p, nv_p = _attn_prompt(proj3, sinks, ATTN_BLOCKS)
    yp, nssm_p, nconv_p = _ssd_prompt(proj3, dtraw.reshape(bsz, seq, HEADS_PAD), cw, cb,
                                      tileh(dt_bias[0]), tileh(a_log[0]), dsk_x, gn, SSD_CHUNKS)
    tm = _row_tile(tp, 512)
    y_prompt = _dense_tail(xp, attn.reshape(tp, Q_DIM), yp.reshape(tp, SSM_INNER),
                           p_prompt[0].reshape(tp, -1), wts, tm).reshape(bsz, seq, d)

    xs = x_sample.reshape(nb, d)
    tab_s = _rope_tables(jnp.full((nb,), PAST_LEN, jnp.int32))
    proj_s, dtraw_s = _inproj(xs, gmix, w_main, w_dt, tab_s, nb)
    window = cache_k.shape[2]
    sconv_t = jnp.transpose(state_conv[0], (1, 0, 2))
    expand = (jnp.arange(HEADS_PAD)[:, None] == (jnp.arange(SSM_INNER) // SSM_HEAD_DIM)[None, :]
              ).astype(BF16)
    q_s = proj_s[:, COL_Q:COL_Q + Q_DIM]
    k_s = proj_s[:, COL_K:COL_K + KV_DIM]
    v_s = proj_s[:, COL_V:COL_V + KV_DIM]
    z_s = proj_s[:, COL_Z:COL_Z + SSM_INNER]
    xbc_s = proj_s[:, COL_XBC:COL_XBC + CONV_DIM]
    xdt, bm, cm, decx, ypart, dech = _sample_pre(
        xbc_s, dtraw_s, sconv_t, cw, cb, dtb, alog, dsk_x, expand)
    bb = 8 if nb % 8 == 0 else nb
    attn_s, nk_s, nv_s = _attn_sample(
        sinks.reshape(N_HEADS, 1), q_s.reshape(nb, N_HEADS, HEAD_DIM), k_s, v_s,
        cache_k[0].reshape(nb, window, KV_DIM), cache_v[0].reshape(nb, window, KV_DIM), bb)
    ys, nssm_s = _ssd_sample(
        state_ssm[0].reshape(nb, SSM_INNER, SSM_STATE), xdt,
        bm.reshape(nb, SSM_GROUPS, SSM_STATE), cm.reshape(nb, SSM_GROUPS, SSM_STATE),
        decx, dech, ypart, z_s, gn, bb)
    y_sample = _dense_tail(xs, attn_s.reshape(nb, Q_DIM), ys, p_sample[0].reshape(nb, -1), wts,
                           nb).reshape(nb, 1, d)
    nconv_s = jnp.concatenate([state_conv[0][:, 1:], xbc_s[:, None, :]], axis=1)

    kv5 = lambda t: t.reshape(1, t.shape[0], window, N_KV_HEADS, HEAD_DIM)
    return (y_prompt, y_sample,
            kv5(nk_p), kv5(nv_p),
            nssm_p.reshape(1, bsz, SSM_HEADS, SSM_HEAD_DIM, SSM_STATE),
            nconv_p[None, :, 8 - (CONV_WIDTH - 1):, :],
            kv5(nk_s), kv5(nv_s),
            nssm_s.reshape(1, nb, SSM_HEADS, SSM_HEAD_DIM, SSM_STATE),
            nconv_s[None])
```

```python
import functools

import numpy as np
import jax
import jax.numpy as jnp
from jax import lax
from jax.experimental import pallas as pl
from jax.experimental.pallas import tpu as pltpu

F32 = jnp.float32
BF16 = jnp.bfloat16

HEAD_DIM = 64
N_HEADS = 16
N_KV_HEADS = 4
WINDOW = 128
ROPE_DIM = 16
ROPE_THETA = 500000.0
SSM_HEADS = 16
SSM_HEAD_DIM = 64
SSM_GROUPS = 4
SSM_STATE = 128
CONV_WIDTH = 4
SSD_CHUNK = 128
RMS_EPS = 1e-6
PAST_LEN = 16384
LOG2E = 1.4426950408889634

Q_DIM = N_HEADS * HEAD_DIM
KV_DIM = N_KV_HEADS * HEAD_DIM
SSM_INNER = SSM_HEADS * SSM_HEAD_DIM
BC_DIM = SSM_GROUPS * SSM_STATE
CONV_DIM = SSM_INNER + 2 * BC_DIM
LANES = 128
HEADS_PAD = LANES

COL_Q = 0
COL_K = Q_DIM
COL_V = COL_K + KV_DIM
COL_Z = COL_V + KV_DIM
COL_XBC = COL_Z + SSM_INNER
PROJ_DIM = COL_XBC + CONV_DIM
PIECE = 512

INPROJ_TN = Q_DIM + 2 * KV_DIM
Q_SCALE = HEAD_DIM ** -0.5 * LOG2E
FFN_TH = 512
ATTN_BLOCKS = 8
SSD_CHUNKS = 8
VMEM_LIMIT = 56 * 1024 * 1024


def _cparams(n_axes):
    return pltpu.CompilerParams(
        dimension_semantics=("arbitrary",) * n_axes, vmem_limit_bytes=VMEM_LIMIT)


def _rms(x):
    return x * lax.rsqrt(jnp.mean(x * x, axis=-1, keepdims=True) + RMS_EPS)


def _sigmoid(x):
    return 1.0 / (1.0 + jnp.exp(-x))


def _silu(x):
    return x * _sigmoid(x)


def _dot(a, b):
    return jnp.dot(a, b, preferred_element_type=F32)


def _dot_nt(a, b):
    return lax.dot_general(a, b, (((1,), (1,)), ((), ())), preferred_element_type=F32)


def _dot_tn(a, b):
    return lax.dot_general(a, b, (((0,), (0,)), ((), ())), preferred_element_type=F32)


def _split3(x):
    hi = x.astype(BF16)
    r = x - hi.astype(F32)
    mid = r.astype(BF16)
    lo = (r - mid.astype(F32)).astype(BF16)
    return hi, mid, lo


def _dot_exact_lhs01(m01, x):
    hi, mid, lo = _split3(x)
    return _dot(m01, hi) + _dot(m01, mid) + _dot(m01, lo)


def _dot_exact_rhs01(x, m01):
    hi, mid, lo = _split3(x)
    return _dot(hi, m01) + _dot(mid, m01) + _dot(lo, m01)


def _inproj_kernel(x_ref, g_ref, w_ref, wdt_ref, tab_ref, o_ref, odt_ref, u_ref):
    j = pl.program_id(1)

    @pl.when(j == 0)
    def _():
        u = (_rms(x_ref[...]) * g_ref[...]).astype(BF16)
        u_ref[...] = u
        odt_ref[...] = _dot(u, wdt_ref[...])
        res = _dot(u, w_ref[...])
        tab = tab_ref[...]
        for c in range((Q_DIM + KV_DIM) // LANES):
            cols = slice(c * LANES, (c + 1) * LANES)
            r = _rope(res[:, cols], tab)
            o_ref[:, cols] = r * Q_SCALE if c < Q_DIM // LANES else r
        o_ref[:, Q_DIM + KV_DIM:] = res[:, Q_DIM + KV_DIM:]

    @pl.when(j > 0)
    def _():
        o_ref[...] = _dot(u_ref[...], w_ref[...])


def _inproj(x, g, w, wdt, tab, tm):
    t, d = x.shape
    n = PROJ_DIM
    tn = INPROJ_TN
    nt = tab.shape[0] // tm
    return pl.pallas_call(
        _inproj_kernel,
        grid=(t // tm, n // tn),
        in_specs=[
            pl.BlockSpec((tm, d), lambda i, j: (i, 0)),
            pl.BlockSpec((1, d), lambda i, j: (0, 0)),
            pl.BlockSpec((d, tn), lambda i, j: (0, j)),
            pl.BlockSpec((d, HEADS_PAD), lambda i, j: (0, 0)),
            pl.BlockSpec((tm, 3 * LANES), lambda i, j: (i % nt, 0)),
        ],
        out_specs=[
            pl.BlockSpec((tm, tn), lambda i, j: (i, j)),
            pl.BlockSpec((tm, HEADS_PAD), lambda i, j: (i, 0)),
        ],
        out_shape=[
            jax.ShapeDtypeStruct((t, n), F32),
            jax.ShapeDtypeStruct((t, HEADS_PAD), F32),
        ],
        scratch_shapes=[pltpu.VMEM((tm, d), BF16)],
        compiler_params=_cparams(2),
        name="inproj",
    )(x, g, w, wdt, tab)


def _rope_tables(pos):
    half = ROPE_DIM // 2
    inv = ROPE_THETA ** (-jnp.arange(half, dtype=F32) * (2.0 / ROPE_DIM))
    ang = pos.astype(F32)[:, None] * inv[None, :]
    cs = jnp.concatenate([jnp.cos(ang), jnp.sin(ang)], axis=1)
    expand = np.zeros((2 * half, 3 * LANES), np.float32)
    base = np.zeros((1, 3 * LANES), np.float32)
    for lane in range(LANES):
        m = lane % HEAD_DIM
        if m >= ROPE_DIM:
            base[0, lane] = 1.0
            continue
        expand[m % half, lane] = 1.0
        if m < half:
            expand[half + m, LANES + lane] = -1.0
        else:
            expand[m, 2 * LANES + lane] = 1.0
    cs3 = jnp.concatenate(_split3(cs), axis=1)
    expand3 = jnp.asarray(np.concatenate([expand] * 3, axis=0), BF16)
    return jnp.dot(cs3, expand3, preferred_element_type=F32) + base


def _rope(x, tab):
    half = ROPE_DIM // 2
    c, sa, sb = tab[:, :LANES], tab[:, LANES:2 * LANES], tab[:, 2 * LANES:]
    return x * c + pltpu.roll(x, LANES - half, 1) * sa + pltpu.roll(x, half, 1) * sb


def _softmax_fold(s, band, prev_bias, sink2):
    sp = s[:, :WINDOW] if prev_bias is None else s[:, :WINDOW] + prev_bias
    t = jnp.where(band, sp, s[:, WINDOW:])
    m = jnp.maximum(jnp.max(t, axis=-1, keepdims=True), sink2)
    e = jnp.exp2(t - m)
    den = jnp.sum(e, axis=-1, keepdims=True) + jnp.exp2(sink2 - m)
    p = jnp.concatenate([jnp.where(band, e, 0.0), jnp.where(band, 0.0, e)], axis=1)
    return p.astype(BF16), den


def _attn_prompt_kernel(sink_ref, q_ref, kc_ref, kp_ref, vc_ref, vp_ref,
                        o_ref, nk_ref, nv_ref, *, nq):
    i = pl.program_id(1)
    nsteps = pl.num_programs(1)
    w = WINDOW
    kcr = kc_ref[...]
    kpr = kp_ref[...]
    vc = vc_ref[...]

    @pl.when(i == nsteps - 1)
    def _():
        nk_ref[...] = kcr[(nq - 1) * w:]
        nv_ref[...] = vc[(nq - 1) * w:]

    kall = jnp.concatenate([kpr, kcr], axis=0)
    vall = jnp.concatenate([vp_ref[...], vc], axis=0)
    lo = lax.broadcasted_iota(jnp.int32, ((nq + 1) * w, LANES), 1) < HEAD_DIM
    lo_q = lax.broadcasted_iota(jnp.int32, (w, LANES), 1) < HEAD_DIM
    band = (lax.broadcasted_iota(jnp.int32, (w, w), 1) > lax.broadcasted_iota(jnp.int32, (w, w), 0))
    first_bias = jnp.where(i == 0, -jnp.inf, 0.0)

    for g in range(N_KV_HEADS):
        col, odd = g // 2, g % 2
        kg = kall[:, col * LANES:(col + 1) * LANES]
        vg = vall[:, col * LANES:(col + 1) * LANES]
        kg_sw = pltpu.roll(kg, HEAD_DIM, 1)
        vg_sw = pltpu.roll(vg, HEAD_DIM, 1)
        k_lo = jnp.where(lo, kg_sw if odd else kg, 0.0).astype(BF16)
        k_hi = jnp.where(lo, 0.0, kg if odd else kg_sw).astype(BF16)
        v_lo = jnp.where(lo, vg_sw if odd else vg, 0.0).astype(BF16)
        v_hi = jnp.where(lo, 0.0, vg if odd else vg_sw).astype(BF16)
        sinks2 = [sink_ref[4 * g + r] * LOG2E for r in range(4)]
        for s in range(nq):
            rows = slice(s * w, (s + 1) * w)
            keys = slice(s * w, (s + 2) * w)
            pb = first_bias if s == 0 else None
            qst = jnp.concatenate([q_ref[rows, (2 * g) * LANES:(2 * g + 1) * LANES],
                                   q_ref[rows, (2 * g + 1) * LANES:(2 * g + 2) * LANES]],
                                  axis=0).astype(BF16)
            s_lo = _dot_nt(qst, k_lo[keys])
            s_hi = _dot_nt(qst, k_hi[keys])
            e0, d0 = _softmax_fold(s_lo[:w], band, pb, sinks2[0])
            e1, d1 = _softmax_fold(s_hi[:w], band, pb, sinks2[1])
            e2, d2 = _softmax_fold(s_lo[w:], band, pb, sinks2[2])
            e3, d3 = _softmax_fold(s_hi[w:], band, pb, sinks2[3])
            p = jnp.concatenate([jnp.concatenate([e0, e1], axis=1),
                                 jnp.concatenate([e2, e3], axis=1)], axis=0)
            vcat = jnp.concatenate([v_lo[keys], v_hi[keys]], axis=0)
            o = _dot(p, vcat)
            oa = o[:w] / jnp.where(lo_q, d0, d1)
            ob = o[w:] / jnp.where(lo_q, d2, d3)
            o_ref[rows, (2 * g) * LANES:(2 * g + 1) * LANES] = oa.astype(o_ref.dtype)
            o_ref[rows, (2 * g + 1) * LANES:(2 * g + 2) * LANES] = ob.astype(o_ref.dtype)


def _attn_prompt(proj, sinks, nq):
    b, l, _ = proj.shape
    w = WINDOW
    nsteps = l // (nq * w)
    kcol, vcol = COL_K // KV_DIM, COL_V // KV_DIM
    prev = lambda bi, i: jnp.maximum(nq * i - 1, 0)
    return pl.pallas_call(
        functools.partial(_attn_prompt_kernel, nq=nq),
        grid=(b, nsteps),
        in_specs=[
            pl.BlockSpec(memory_space=pltpu.SMEM),
            pl.BlockSpec((None, nq * w, Q_DIM), lambda bi, i: (bi, i, COL_Q // Q_DIM)),
            pl.BlockSpec((None, nq * w, KV_DIM), lambda bi, i: (bi, i, kcol)),
            pl.BlockSpec((None, w, KV_DIM), lambda bi, i: (bi, prev(bi, i), kcol)),
            pl.BlockSpec((None, nq * w, KV_DIM), lambda bi, i: (bi, i, vcol)),
            pl.BlockSpec((None, w, KV_DIM), lambda bi, i: (bi, prev(bi, i), vcol)),
        ],
        out_specs=[
            pl.BlockSpec((None, nq * w, Q_DIM), lambda bi, i: (bi, i, 0)),
            pl.BlockSpec((None, w, KV_DIM), lambda bi, i: (bi, 0, 0)),
            pl.BlockSpec((None, w, KV_DIM), lambda bi, i: (bi, 0, 0)),
        ],
        out_shape=[
            jax.ShapeDtypeStruct((b, l, Q_DIM), BF16),
            jax.ShapeDtypeStruct((b, w, KV_DIM), F32),
            jax.ShapeDtypeStruct((b, w, KV_DIM), F32),
        ],
        compiler_params=_cparams(2),
        name="attn_prompt",
    )(sinks, proj, proj, proj, proj, proj)


def _softplus(v):
    return jnp.maximum(v, 0.0) + jnp.log1p(jnp.exp(-jnp.abs(v)))


def _head_expand(vals, ex2_ref):
    hi = vals.astype(BF16)
    mid = (vals - hi.astype(F32)).astype(BF16)
    return _dot(jnp.concatenate([hi, mid], axis=1), ex2_ref[...])


def _ssd_prompt_kernel(*refs, nsub):
    nz, nx = SSM_INNER // PIECE, CONV_DIM // PIECE
    z_refs, x_refs = refs[:nz], refs[nz:nz + nx]
    (dt_ref, cw_ref, cb_ref, dtb_ref, alog_ref, dsk_ref, gn_ref, ex2_ref,
     y_ref, nssm_ref, nconv_ref, state_ref, carry_ref) = refs[nz + nx:]
    i = pl.program_id(1)
    nc = pl.num_programs(1)
    q = SSD_CHUNK

    @pl.when(i == 0)
    def _():
        state_ref[...] = jnp.zeros_like(state_ref)
        carry_ref[...] = jnp.zeros_like(carry_ref)

    dtp = dt_ref[:q, :]
    for s in range(1, nsub):
        dtp = dtp + pltpu.roll(dt_ref[s * q:(s + 1) * q, :], s * SSM_HEADS, 1)
    dt = _softplus(dtp + dtb_ref[...])
    da = dt * (-jnp.exp(alog_ref[...]))
    tri = (lax.broadcasted_iota(jnp.int32, (q, q), 0) >= lax.broadcasted_iota(jnp.int32, (q, q), 1))
    cs = _dot_exact_lhs01(tri.astype(BF16), da)
    cs_t = cs.T
    sc = dict(
        tri=tri, cs=cs, cs_t=cs_t, dt_t=dt.T,
        cdec_t=jnp.exp(cs_t[:, q - 1:q]),
        ecs_x=_head_expand(jnp.exp(cs), ex2_ref),
        wgt_x=_head_expand(dt * jnp.exp(cs[q - 1:q, :] - cs), ex2_ref))

    rows_of = lambda rs, sl: jnp.concatenate([r[sl, :] for r in rs], axis=1)
    for s in range(nsub):
        _ssd_chunk(s, sc, rows_of, z_refs, x_refs, cw_ref, cb_ref, dsk_ref,
                   gn_ref, y_ref, state_ref, carry_ref)
    tail = rows_of(x_refs, slice(nsub * q - 8, nsub * q))
    carry_ref[...] = tail

    @pl.when(i == nc - 1)
    def _():
        nconv_ref[...] = tail
        nssm_ref[...] = state_ref[...]


def _ssd_chunk(s, sc, rows_of, z_refs, x_refs, cw_ref, cb_ref, dsk_ref,
               gn_ref, y_ref, state_ref, carry_ref):
    q = SSD_CHUNK
    tri, cs, cs_t, dt_t, cdec_t = sc["tri"], sc["cs"], sc["cs_t"], sc["dt_t"], sc["cdec_t"]
    hoff = s * SSM_HEADS
    xoff = s * SSM_INNER
    trows = slice(s * q, (s + 1) * q)
    x = rows_of(x_refs, trows)
    z = rows_of(z_refs, trows)
    prev = carry_ref[...] if s == 0 else rows_of(x_refs, slice(s * q - 8, s * q))
    row8 = lax.broadcasted_iota(jnp.int32, (8, CONV_DIM), 0)

    def shifted(k):
        r = pltpu.roll(x, k, 0)
        head = jnp.where(row8 < k, pltpu.roll(prev, k, 0), r[:8])
        return jnp.concatenate([head, r[8:]], axis=0)

    conv = x * cw_ref[CONV_WIDTH - 1:CONV_WIDTH, :] + cb_ref[...]
    for k in range(1, CONV_WIDTH):
        conv = conv + shifted(k) * cw_ref[CONV_WIDTH - 1 - k:CONV_WIDTH - k, :]

    xc = _silu(conv)
    xs = xc[:, :SSM_INNER]
    bm = xc[:, SSM_INNER:SSM_INNER + BC_DIM]
    cm = xc[:, SSM_INNER + BC_DIM:]

    lo = lax.broadcasted_iota(jnp.int32, (q, LANES), 1) < SSM_HEAD_DIM

    hpg = SSM_HEADS // SSM_GROUPS
    gw = hpg * SSM_HEAD_DIM
    for g in range(SSM_GROUPS):
        bg = bm[:, g * SSM_STATE:(g + 1) * SSM_STATE].astype(BF16)
        cg = cm[:, g * SSM_STATE:(g + 1) * SSM_STATE].astype(BF16)
        cb = _dot_nt(cg, bg)
        st = state_ref[g * gw:(g + 1) * gw, :]
        yoff = _dot_nt(cg, st.astype(BF16))
        ys = []
        for pr in range(2):
            pair = 2 * g + pr
            ms = []
            for h in (hoff + 2 * pair, hoff + 2 * pair + 1):
                diff = cs[:, h:h + 1] - cs_t[h:h + 1, :]
                lm = jnp.exp(jnp.where(tri, diff, -jnp.inf))
                ms.append((cb * lm * dt_t[h:h + 1, :]).astype(BF16))
            xp = xs[:, pair * LANES:(pair + 1) * LANES]
            x2 = jnp.concatenate([jnp.where(lo, xp, 0.0), jnp.where(lo, 0.0, xp)],
                                 axis=0).astype(BF16)
            yd = _dot(jnp.concatenate(ms, axis=1), x2)
            yo = yoff[:, pr * LANES:(pr + 1) * LANES] * sc["ecs_x"][
                :, xoff + pair * LANES:xoff + (pair + 1) * LANES]
            ys.append(yd + yo + xp * dsk_ref[:, pair * LANES:(pair + 1) * LANES])
        yg = jnp.concatenate(ys, axis=1)
        wx = xs[:, g * gw:(g + 1) * gw] * sc["wgt_x"][:, xoff + g * gw:xoff + (g + 1) * gw]
        s_new = _dot_tn(wx.astype(BF16), bg)
        for r in range(hpg):
            h = hoff + hpg * g + r
            rows = slice(g * gw + r * SSM_HEAD_DIM, g * gw + (r + 1) * SSM_HEAD_DIM)
            state_ref[rows, :] = (st[r * SSM_HEAD_DIM:(r + 1) * SSM_HEAD_DIM, :] * cdec_t[h:h + 1, :]
                                  + s_new[r * SSM_HEAD_DIM:(r + 1) * SSM_HEAD_DIM, :])
        hg = yg * _silu(z[:, g * gw:(g + 1) * gw])
        y_ref[trows, g * gw:(g + 1) * gw] = (_rms(hg) * gn_ref[:, g * gw:(g + 1) * gw]
                                            ).astype(y_ref.dtype)


def _ssd_prompt(proj, dtraw, conv_w, conv_b, dt_bias_t, a_log_t, d_skip_x, norm_g, nsub):
    assert nsub * SSM_HEADS <= LANES
    src = np.arange(nsub * SSM_INNER) // SSM_HEAD_DIM
    ex = (np.arange(LANES)[:, None] == src[None, :]).astype(np.float32)
    ex2 = jnp.asarray(np.concatenate([ex, ex], axis=0), BF16)
    b, l, _ = proj.shape
    q = SSD_CHUNK * nsub
    nc = l // q
    const = lambda bi, i: (0, 0)
    piece = lambda c: pl.BlockSpec((None, q, PIECE), lambda bi, i: (bi, i, c))
    n_pieces = (SSM_INNER + CONV_DIM) // PIECE
    return pl.pallas_call(
        functools.partial(_ssd_prompt_kernel, nsub=nsub),
        grid=(b, nc),
        in_specs=[piece(COL_Z // PIECE + c) for c in range(SSM_INNER // PIECE)] + [
            piece(COL_XBC // PIECE + c) for c in range(CONV_DIM // PIECE)] + [
            pl.BlockSpec((None, q, HEADS_PAD), lambda bi, i: (bi, i, 0)),
            pl.BlockSpec((CONV_WIDTH, CONV_DIM), const),
            pl.BlockSpec((1, CONV_DIM), const),
            pl.BlockSpec((1, HEADS_PAD), const),
            pl.BlockSpec((1, HEADS_PAD), const),
            pl.BlockSpec((1, SSM_INNER), const),
            pl.BlockSpec((1, SSM_INNER), const),
            pl.BlockSpec((2 * LANES, nsub * SSM_INNER), const),
        ],
        out_specs=[
            pl.BlockSpec((None, q, SSM_INNER), lambda bi, i: (bi, i, 0)),
            pl.BlockSpec((None, SSM_INNER, SSM_STATE), lambda bi, i: (bi, 0, 0)),
            pl.BlockSpec((None, 8, CONV_DIM), lambda bi, i: (bi, 0, 0)),
        ],
        out_shape=[
            jax.ShapeDtypeStruct((b, l, SSM_INNER), BF16),
            jax.ShapeDtypeStruct((b, SSM_INNER, SSM_STATE), F32),
            jax.ShapeDtypeStruct((b, 8, CONV_DIM), F32),
        ],
        scratch_shapes=[pltpu.VMEM((SSM_INNER, SSM_STATE), F32),
                        pltpu.VMEM((8, CONV_DIM), F32)],
        compiler_params=_cparams(2),
        name="ssd_prompt",
    )(*([proj] * n_pieces), dtraw, conv_w, conv_b, dt_bias_t, a_log_t, d_skip_x, norm_g, ex2)


def _outproj_kernel(x_ref, a_ref, y_ref, wa_ref, wy_ref, o_ref):
    o_ref[...] = (x_ref[...] + _dot(a_ref[...].astype(BF16), wa_ref[...])
                  + _dot(y_ref[...].astype(BF16), wy_ref[...]))


def _outproj(x, attn, y, w_out, tm):
    t, d = x.shape
    half = w_out.shape[0] // 2
    return pl.pallas_call(
        _outproj_kernel,
        grid=(t // tm,),
        in_specs=[
            pl.BlockSpec((tm, d), lambda i: (i, 0)),
            pl.BlockSpec((tm, half), lambda i: (i, 0)),
            pl.BlockSpec((tm, half), lambda i: (i, 0)),
            pl.BlockSpec((half, d), lambda i: (0, 0)),
            pl.BlockSpec((half, d), lambda i: (1, 0)),
        ],
        out_specs=pl.BlockSpec((tm, d), lambda i: (i, 0)),
        out_shape=jax.ShapeDtypeStruct((t, d), F32),
        compiler_params=_cparams(1),
        name="outproj",
    )(x, attn, y, w_out, w_out)


def _ffn_kernel(h_ref, g_ref, wg_ref, wu_ref, wd_ref, o_ref, f_ref):
    def delta(f):
        hid = (_silu(_dot(f, wg_ref[...])) * _dot(f, wu_ref[...])).astype(BF16)
        return _dot(hid, wd_ref[...])

    @pl.when(pl.program_id(1) == 0)
    def _():
        h = h_ref[...]
        f = (_rms(h) * g_ref[...]).astype(BF16)
        f_ref[...] = f
        o_ref[...] = h + delta(f)

    @pl.when(pl.program_id(1) > 0)
    def _():
        o_ref[...] += delta(f_ref[...])


def _ffn(h, g, wg, wu, wd, tm, th):
    t, d = h.shape
    hidden = wd.shape[0]
    return pl.pallas_call(
        _ffn_kernel,
        grid=(t // tm, hidden // th),
        in_specs=[
            pl.BlockSpec((tm, d), lambda i, j: (i, 0)),
            pl.BlockSpec((1, d), lambda i, j: (0, 0)),
            pl.BlockSpec((d, th), lambda i, j: (0, j)),
            pl.BlockSpec((d, th), lambda i, j: (0, j)),
            pl.BlockSpec((th, d), lambda i, j: (j, 0)),
        ],
        out_specs=pl.BlockSpec((tm, d), lambda i, j: (i, 0)),
        out_shape=jax.ShapeDtypeStruct((t, d), F32),
        scratch_shapes=[pltpu.VMEM((tm, d), BF16)],
        compiler_params=_cparams(2),
        name="ffn",
    )(h, g, wg, wu, wd)


def _ple_kernel(h_ref, p_ref, gp_ref, wp_ref, wg_ref, gf_ref, o_ref, *, tn, row_parts):
    tm, d = h_ref.shape
    rp = tm // row_parts
    for r in range(row_parts):
        rows = slice(r * rp, (r + 1) * rp)
        n = (_rms(h_ref[rows, :]) * gp_ref[...]).astype(BF16)
        pb = p_ref[rows, :].astype(BF16)
        ss = jnp.zeros((rp, 1), F32)
        for c in range(d // tn):
            cols = slice(c * tn, (c + 1) * tn)
            gate = _dot(n, wg_ref[:, cols])
            h3 = h_ref[rows, cols] + _dot(pb, wp_ref[:, cols]) * _sigmoid(gate)
            o_ref[rows, cols] = h3
            ss = ss + jnp.sum(h3 * h3, axis=-1, keepdims=True)
        inv = lax.rsqrt(ss * (1.0 / d) + RMS_EPS)
        o_ref[rows, :] = o_ref[rows, :] * inv * gf_ref[...]


def _ple_final(h, p, g_ple, w_ple, w_gate, g_final, tm):
    t, d = h.shape
    pd = p.shape[1]
    const = lambda i: (0, 0)
    resident = dict(pipeline_mode=pl.Buffered(1))
    return pl.pallas_call(
        functools.partial(_ple_kernel, tn=512, row_parts=max(tm // 256, 1)),
        grid=(t // tm,),
        in_specs=[
            pl.BlockSpec((tm, d), lambda i: (i, 0)),
            pl.BlockSpec((tm, pd), lambda i: (i, 0)),
            pl.BlockSpec((1, d), const),
            pl.BlockSpec((pd, d), const, **resident),
            pl.BlockSpec((d, d), const, **resident),
            pl.BlockSpec((1, d), const),
        ],
        out_specs=pl.BlockSpec((tm, d), lambda i: (i, 0)),
        out_shape=jax.ShapeDtypeStruct((t, d), F32),
        compiler_params=_cparams(1),
        name="ple_final",
    )(h, p, g_ple, w_ple, w_gate, g_final)


def _sample_pre_kernel(x_ref, dt_ref, sc_ref, cw_ref, cb_ref, dtb_ref,
                       alog_ref, dsk_ref, exp_ref,
                       xdt_ref, b_ref, c_ref, dec_ref, yp_ref, dech_ref):
    conv = x_ref[...] * cw_ref[CONV_WIDTH - 1:CONV_WIDTH, :] + cb_ref[...]
    for k in range(CONV_WIDTH - 1):
        conv = conv + sc_ref[k] * cw_ref[k:k + 1, :]
    xc = _silu(conv)
    xs = xc[:, :SSM_INNER]
    bm = xc[:, SSM_INNER:SSM_INNER + BC_DIM]
    cm = xc[:, SSM_INNER + BC_DIM:]
    b_ref[...] = bm
    c_ref[...] = cm
    dt = _softplus(dt_ref[...] + dtb_ref[...])
    dec = jnp.exp(dt * (-jnp.exp(alog_ref[...])))
    ex = exp_ref[...]
    dtx = _dot_exact_rhs01(dt, ex)
    dec_ref[...] = _dot_exact_rhs01(dec, ex)
    dech_ref[...] = dec
    xdt = xs * dtx
    xdt_ref[...] = xdt
    gw = SSM_INNER // SSM_GROUPS
    cbs = []
    for g in range(SSM_GROUPS):
        prod = cm[:, g * SSM_STATE:(g + 1) * SSM_STATE] * bm[:, g * SSM_STATE:(g + 1) * SSM_STATE]
        cbs.append(jnp.broadcast_to(jnp.sum(prod, axis=-1, keepdims=True), (prod.shape[0], gw)))
    yp_ref[...] = xdt * jnp.concatenate(cbs, axis=1) + xs * dsk_ref[...]


def _sample_pre(xbc, dtraw, sconv_t, conv_w, conv_b, dt_bias, a_log, d_skip_x, expand):
    nb = xbc.shape[0]
    shapes = [(nb, SSM_INNER), (nb, BC_DIM), (nb, BC_DIM),
              (nb, SSM_INNER), (nb, SSM_INNER), (nb, HEADS_PAD)]
    return pl.pallas_call(
        _sample_pre_kernel,
        out_shape=[jax.ShapeDtypeStruct(s, F32) for s in shapes],
        compiler_params=pltpu.CompilerParams(vmem_limit_bytes=VMEM_LIMIT),
        name="sample_pre",
    )(xbc, dtraw, sconv_t, conv_w, conv_b, dt_bias, a_log, d_skip_x, expand)


def _attn_sample_kernel(sink_ref, q_ref, kn_ref, vn_ref, ck_ref, cv_ref, o_ref, nk_ref, nv_ref, *, bb):
    w = WINDOW
    row = lax.broadcasted_iota(jnp.int32, (w, KV_DIM), 0)
    hrow = lax.broadcasted_iota(jnp.int32, (N_HEADS, KV_DIM), 0) // (N_HEADS // N_KV_HEADS)
    hgrp = lax.broadcasted_iota(jnp.int32, (N_HEADS, KV_DIM), 1) // HEAD_DIM
    own = hrow == hgrp
    sink = sink_ref[...] * LOG2E
    for b in range(bb):
        kk = jnp.where(row == w - 1, kn_ref[b:b + 1, :], pltpu.roll(ck_ref[b], w - 1, 0))
        vv = jnp.where(row == w - 1, vn_ref[b:b + 1, :], pltpu.roll(cv_ref[b], w - 1, 0))
        nk_ref[b] = kk
        nv_ref[b] = vv
        qb = q_ref[b]
        qrow = jnp.where(own, jnp.concatenate([qb] * N_KV_HEADS, axis=1), 0.0)
        s = _dot_nt(qrow.astype(BF16), kk.astype(BF16))
        m = jnp.maximum(jnp.max(s, axis=-1, keepdims=True), sink)
        e = jnp.exp2(s - m)
        den = jnp.sum(e, axis=-1, keepdims=True) + jnp.exp2(sink - m)
        of = jnp.where(own, _dot(e.astype(BF16), vv.astype(BF16)), 0.0)
        o = of[:, :HEAD_DIM]
        for g in range(1, N_KV_HEADS):
            o = o + of[:, g * HEAD_DIM:(g + 1) * HEAD_DIM]
        o_ref[b] = o / den


def _attn_sample(sinks_col, q3, knew, vnew, cache_k, cache_v, bb):
    nb = q3.shape[0]
    w = WINDOW
    return pl.pallas_call(
        functools.partial(_attn_sample_kernel, bb=bb),
        grid=(nb // bb,),
        in_specs=[
            pl.BlockSpec((N_HEADS, 1), lambda i: (0, 0)),
            pl.BlockSpec((bb, N_HEADS, HEAD_DIM), lambda i: (i, 0, 0)),
            pl.BlockSpec((bb, KV_DIM), lambda i: (i, 0)),
            pl.BlockSpec((bb, KV_DIM), lambda i: (i, 0)),
            pl.BlockSpec((bb, w, KV_DIM), lambda i: (i, 0, 0)),
            pl.BlockSpec((bb, w, KV_DIM), lambda i: (i, 0, 0)),
        ],
        out_specs=[
            pl.BlockSpec((bb, N_HEADS, HEAD_DIM), lambda i: (i, 0, 0)),
            pl.BlockSpec((bb, w, KV_DIM), lambda i: (i, 0, 0)),
            pl.BlockSpec((bb, w, KV_DIM), lambda i: (i, 0, 0)),
        ],
        out_shape=[
            jax.ShapeDtypeStruct((nb, N_HEADS, HEAD_DIM), F32),
            jax.ShapeDtypeStruct((nb, w, KV_DIM), F32),
            jax.ShapeDtypeStruct((nb, w, KV_DIM), F32),
        ],
        compiler_params=_cparams(1),
        name="attn_sample",
    )(sinks_col, q3, knew, vnew, cache_k, cache_v)


def _ssd_sample_kernel(st_ref, xdt_ref, b_ref, c_ref, dec_ref, dech_ref, yp_ref, z_ref, gn_ref,
                       y_ref, ns_ref, *, bb):
    gw = SSM_INNER // SSM_GROUPS
    grow = lax.broadcasted_iota(jnp.int32, (8, SSM_INNER), 0)
    glane = lax.broadcasted_iota(jnp.int32, (8, SSM_INNER), 1) // gw
    own = grow == glane
    pad = jnp.zeros((8 - SSM_GROUPS, SSM_STATE), F32)
    yoffs = []
    for b in range(bb):
        st = st_ref[b]
        cmat = jnp.concatenate([c_ref[b], pad], axis=0).astype(BF16)
        bmat = jnp.concatenate([b_ref[b], pad], axis=0).astype(BF16)
        r = _dot_nt(cmat, st.astype(BF16))
        yoffs.append(jnp.sum(jnp.where(own, r, 0.0), axis=0, keepdims=True))
        amat = jnp.where(own, jnp.broadcast_to(xdt_ref[b:b + 1, :], (8, SSM_INNER)), 0.0)
        outer = _dot_tn(amat.astype(BF16), bmat)
        for h in range(SSM_HEADS):
            rows = slice(h * SSM_HEAD_DIM, (h + 1) * SSM_HEAD_DIM)
            ns_ref[b, rows, :] = st[rows, :] * dech_ref[b:b + 1, h:h + 1] + outer[rows, :]
    y = yp_ref[...] + jnp.concatenate(yoffs, axis=0) * dec_ref[...]
    hg = y * _silu(z_ref[...])
    outs = []
    for g in range(SSM_GROUPS):
        outs.append(_rms(hg[:, g * gw:(g + 1) * gw]))
    y_ref[...] = jnp.concatenate(outs, axis=1) * gn_ref[...]


def _ssd_sample(state, xdt, b3, c3, decx, dech, ypart, z, norm_g, bb):
    nb = state.shape[0]
    row = lambda i: (i, 0)
    return pl.pallas_call(
        functools.partial(_ssd_sample_kernel, bb=bb),
        grid=(nb // bb,),
        in_specs=[
            pl.BlockSpec((bb, SSM_INNER, SSM_STATE), lambda i: (i, 0, 0)),
            pl.BlockSpec((bb, SSM_INNER), row),
            pl.BlockSpec((bb, SSM_GROUPS, SSM_STATE), lambda i: (i, 0, 0)),
            pl.BlockSpec((bb, SSM_GROUPS, SSM_STATE), lambda i: (i, 0, 0)),
            pl.BlockSpec((bb, SSM_INNER), row),
            pl.BlockSpec((bb, HEADS_PAD), row),
            pl.BlockSpec((bb, SSM_INNER), row),
            pl.BlockSpec((bb, SSM_INNER), row),
            pl.BlockSpec((1, SSM_INNER), lambda i: (0, 0)),
        ],
        out_specs=[
            pl.BlockSpec((bb, SSM_INNER), row),
            pl.BlockSpec((bb, SSM_INNER, SSM_STATE), lambda i: (i, 0, 0)),
        ],
        out_shape=[
            jax.ShapeDtypeStruct((nb, SSM_INNER), F32),
            jax.ShapeDtypeStruct((nb, SSM_INNER, SSM_STATE), F32),
        ],
        compiler_params=_cparams(1),
        name="ssd_sample",
    )(state, xdt, b3, c3, decx, dech, ypart, z, norm_g)


def _row_tile(t, want):
    return want if t % want == 0 else t


def _dense_tail(x, attn, y, p, wts, tm):
    h1 = _outproj(x, attn, y, wts["w_out"], tm)
    h2 = _ffn(h1, wts["g_ffn"], wts["w_gate"], wts["w_up"], wts["w_down"], tm, FFN_TH)
    return _ple_final(h2, p, wts["g_ple"], wts["w_ple"], wts["w_ple_gate"], wts["g_final"],
                      _row_tile(x.shape[0], 1024))


def kernel(x_prompt, x_sample, cache_k, cache_v, state_ssm, state_conv, p_prompt, p_sample, w_in, conv_w, conv_b, dt_bias, a_log, d_skip, ssm_norm_g, attn_sinks, w_out, g_mix, g_ffn, w_ffn_gate, w_ffn_up, w_ffn_down, g_ple, w_ple, w_ple_gate, g_final):
    depth = w_in.shape[0]
    assert depth == 1, "single-layer step only"
    bsz, seq, d = x_prompt.shape
    nb, dseq, _ = x_sample.shape
    assert dseq == 1 and seq % SSD_CHUNK == 0 and seq % WINDOW == 0

    w = w_in[0]
    w_main = w.astype(BF16)
    w_dt = jnp.pad(w[:, PROJ_DIM:], ((0, 0), (0, HEADS_PAD - SSM_HEADS))).astype(BF16)
    padh = lambda v: jnp.pad(v, (0, HEADS_PAD - SSM_HEADS)).reshape(1, HEADS_PAD)
    tileh = lambda v: jnp.tile(v, HEADS_PAD // SSM_HEADS).reshape(1, HEADS_PAD)
    dtb, alog = padh(dt_bias[0]), padh(a_log[0])
    dsk_x = jnp.repeat(d_skip[0], SSM_HEAD_DIM).reshape(1, SSM_INNER)
    gn = ssm_norm_g[0].reshape(1, SSM_INNER)
    cw, cb = conv_w[0], conv_b[0].reshape(1, CONV_DIM)
    wts = dict(
        w_out=w_out[0].astype(BF16), g_ffn=g_ffn[0].reshape(1, d),
        w_gate=w_ffn_gate[0].astype(BF16), w_up=w_ffn_up[0].astype(BF16),
        w_down=w_ffn_down[0].astype(BF16), g_ple=g_ple[0].reshape(1, d),
        w_ple=w_ple[0].astype(BF16), w_ple_gate=w_ple_gate[0].astype(BF16),
        g_final=g_final.reshape(1, d))
    gmix = g_mix[0].reshape(1, d)
    sinks = attn_sinks[0]

    tp = bsz * seq
    xp = x_prompt.reshape(tp, d)
    tm_in = _row_tile(seq, 1024)
    proj, dtraw = _inproj(xp, gmix, w_main, w_dt, _rope_tables(jnp.arange(seq)), tm_in)
    proj3 = proj.reshape(bsz, seq, PROJ_DIM)
    attn, nk_p, nv_p = _attn_prompt(proj3, sinks, ATTN_BLOCKS)
    yp, nssm_p, nconv_p = _ssd_prompt(proj3, dtraw.reshape(bsz, seq, HEADS_PAD), cw, cb,
                                      tileh(dt_bias[0]), tileh(a_log[0]), dsk_x, gn, SSD_CHUNKS)
    tm = _row_tile(tp, 512)
    y_prompt = _dense_tail(xp, attn.reshape(tp, Q_DIM), yp.reshape(tp, SSM_INNER),
                           p_prompt[0].reshape(tp, -1), wts, tm).reshape(bsz, seq, d)

    xs = x_sample.reshape(nb, d)
    tab_s = _rope_tables(jnp.full((nb,), PAST_LEN, jnp.int32))
    proj_s, dtraw_s = _inproj(xs, gmix, w_main, w_dt, tab_s, nb)
    window = cache_k.shape[2]
    sconv_t = jnp.transpose(state_conv[0], (1, 0, 2))
    expand = (jnp.arange(HEADS_PAD)[:, None] == (jnp.arange(SSM_INNER) // SSM_HEAD_DIM)[None, :]
              ).astype(BF16)
    q_s = proj_s[:, COL_Q:COL_Q + Q_DIM]
    k_s = proj_s[:, COL_K:COL_K + KV_DIM]
    v_s = proj_s[:, COL_V:COL_V + KV_DIM]
    z_s = proj_s[:, COL_Z:COL_Z + SSM_INNER]
    xbc_s = proj_s[:, COL_XBC:COL_XBC + CONV_DIM]
    xdt, bm, cm, decx, ypart, dech = _sample_pre(
        xbc_s, dtraw_s, sconv_t, cw, cb, dtb, alog, dsk_x, expand)
    bb = 8 if nb % 8 == 0 else nb
    attn_s, nk_s, nv_s = _attn_sample(
        sinks.reshape(N_HEADS, 1), q_s.reshape(nb, N_HEADS, HEAD_DIM), k_s, v_s,
        cache_k[0].reshape(nb, window, KV_DIM), cache_v[0].reshape(nb, window, KV_DIM), bb)
    ys, nssm_s = _ssd_sample(
        state_ssm[0].reshape(nb, SSM_INNER, SSM_STATE), xdt,
        bm.reshape(nb, SSM_GROUPS, SSM_STATE), cm.reshape(nb, SSM_GROUPS, SSM_STATE),
        decx, dech, ypart, z_s, gn, bb)
    y_sample = _dense_tail(xs, attn_s.reshape(nb, Q_DIM), ys, p_sample[0].reshape(nb, -1), wts,
                           nb).reshape(nb, 1, d)
    nconv_s = jnp.concatenate([state_conv[0][:, 1:], xbc_s[:, None, :]], axis=1)

    kv5 = lambda t: t.reshape(1, t.shape[0], window, N_KV_HEADS, HEAD_DIM)
    return (y_prompt, y_sample,
            kv5(nk_p), kv5(nv_p),
            nssm_p.reshape(1, bsz, SSM_HEADS, SSM_HEAD_DIM, SSM_STATE),
            nconv_p[None, :, 8 - (CONV_WIDTH - 1):, :],
            kv5(nk_s), kv5(nv_s),
            nssm_s.reshape(1, nb, SSM_HEADS, SSM_HEAD_DIM, SSM_STATE),
            nconv_s[None])
```

```python
import functools

import numpy as np
import jax
import jax.numpy as jnp
from jax import lax
from jax.experimental import pallas as pl
from jax.experimental.pallas import tpu as pltpu

F32 = jnp.float32
BF16 = jnp.bfloat16

HEAD_DIM = 64
N_HEADS = 16
N_KV_HEADS = 4
WINDOW = 128
ROPE_DIM = 16
ROPE_THETA = 500000.0
SSM_HEADS = 16
SSM_HEAD_DIM = 64
SSM_GROUPS = 4
SSM_STATE = 128
CONV_WIDTH = 4
SSD_CHUNK = 128
RMS_EPS = 1e-6
PAST_LEN = 16384
LOG2E = 1.4426950408889634

Q_DIM = N_HEADS * HEAD_DIM
KV_DIM = N_KV_HEADS * HEAD_DIM
SSM_INNER = SSM_HEADS * SSM_HEAD_DIM
BC_DIM = SSM_GROUPS * SSM_STATE
CONV_DIM = SSM_INNER + 2 * BC_DIM
LANES = 128
HEADS_PAD = LANES

COL_Q = 0
COL_K = Q_DIM
COL_V = COL_K + KV_DIM
COL_Z = COL_V + KV_DIM
COL_XBC = COL_Z + SSM_INNER
PROJ_DIM = COL_XBC + CONV_DIM
PIECE = 512

INPROJ_TN = Q_DIM + 2 * KV_DIM
Q_SCALE = HEAD_DIM ** -0.5 * LOG2E
FFN_TM, FFN_TH = 1024, 256
ATTN_BLOCKS = 8
SSD_CHUNKS = 8
VMEM_LIMIT = 56 * 1024 * 1024


def _cparams(n_axes):
    return pltpu.CompilerParams(
        dimension_semantics=("arbitrary",) * n_axes, vmem_limit_bytes=VMEM_LIMIT)


def _rms(x):
    return x * lax.rsqrt(jnp.mean(x * x, axis=-1, keepdims=True) + RMS_EPS)


def _sigmoid(x):
    return 1.0 / (1.0 + jnp.exp(-x))


def _silu(x):
    return x * _sigmoid(x)


def _dot(a, b):
    return jnp.dot(a, b, preferred_element_type=F32)


def _dot_nt(a, b):
    return lax.dot_general(a, b, (((1,), (1,)), ((), ())), preferred_element_type=F32)


def _dot_tn(a, b):
    return lax.dot_general(a, b, (((0,), (0,)), ((), ())), preferred_element_type=F32)


def _split3(x):
    hi = x.astype(BF16)
    r = x - hi.astype(F32)
    mid = r.astype(BF16)
    lo = (r - mid.astype(F32)).astype(BF16)
    return hi, mid, lo


def _dot_exact_lhs01(m01, x):
    hi, mid, lo = _split3(x)
    return _dot(m01, hi) + _dot(m01, mid) + _dot(m01, lo)


def _dot_exact_rhs01(x, m01):
    hi, mid, lo = _split3(x)
    return _dot(hi, m01) + _dot(mid, m01) + _dot(lo, m01)


def _inproj_kernel(x_ref, g_ref, w_ref, wdt_ref, tab_ref, o_ref, odt_ref, u_ref):
    j = pl.program_id(1)

    @pl.when(j == 0)
    def _():
        u = (_rms(x_ref[...]) * g_ref[...]).astype(BF16)
        u_ref[...] = u
        odt_ref[...] = _dot(u, wdt_ref[...])
        res = _dot(u, w_ref[...])
        tab = tab_ref[...]
        for c in range((Q_DIM + KV_DIM) // LANES):
            cols = slice(c * LANES, (c + 1) * LANES)
            r = _rope(res[:, cols], tab)
            o_ref[:, cols] = r * Q_SCALE if c < Q_DIM // LANES else r
        o_ref[:, Q_DIM + KV_DIM:] = res[:, Q_DIM + KV_DIM:]

    @pl.when(j > 0)
    def _():
        o_ref[...] = _dot(u_ref[...], w_ref[...])


def _inproj(x, g, w, wdt, tab, tm):
    t, d = x.shape
    n = PROJ_DIM
    tn = INPROJ_TN
    nt = tab.shape[0] // tm
    return pl.pallas_call(
        _inproj_kernel,
        grid=(t // tm, n // tn),
        in_specs=[
            pl.BlockSpec((tm, d), lambda i, j: (i, 0)),
            pl.BlockSpec((1, d), lambda i, j: (0, 0)),
            pl.BlockSpec((d, tn), lambda i, j: (0, j)),
            pl.BlockSpec((d, HEADS_PAD), lambda i, j: (0, 0)),
            pl.BlockSpec((tm, 3 * LANES), lambda i, j: (i % nt, 0)),
        ],
        out_specs=[
            pl.BlockSpec((tm, tn), lambda i, j: (i, j)),
            pl.BlockSpec((tm, HEADS_PAD), lambda i, j: (i, 0)),
        ],
        out_shape=[
            jax.ShapeDtypeStruct((t, n), F32),
            jax.ShapeDtypeStruct((t, HEADS_PAD), F32),
        ],
        scratch_shapes=[pltpu.VMEM((tm, d), BF16)],
        compiler_params=_cparams(2),
        name="inproj",
    )(x, g, w, wdt, tab)


def _rope_tables(pos):
    half = ROPE_DIM // 2
    inv = ROPE_THETA ** (-jnp.arange(half, dtype=F32) * (2.0 / ROPE_DIM))
    ang = pos.astype(F32)[:, None] * inv[None, :]
    cs = jnp.concatenate([jnp.cos(ang), jnp.sin(ang)], axis=1)
    expand = np.zeros((2 * half, 3 * LANES), np.float32)
    base = np.zeros((1, 3 * LANES), np.float32)
    for lane in range(LANES):
        m = lane % HEAD_DIM
        if m >= ROPE_DIM:
            base[0, lane] = 1.0
            continue
        expand[m % half, lane] = 1.0
        if m < half:
            expand[half + m, LANES + lane] = -1.0
        else:
            expand[m, 2 * LANES + lane] = 1.0
    cs3 = jnp.concatenate(_split3(cs), axis=1)
    expand3 = jnp.asarray(np.concatenate([expand] * 3, axis=0), BF16)
    return jnp.dot(cs3, expand3, preferred_element_type=F32) + base


def _rope(x, tab):
    half = ROPE_DIM // 2
    c, sa, sb = tab[:, :LANES], tab[:, LANES:2 * LANES], tab[:, 2 * LANES:]
    return x * c + pltpu.roll(x, LANES - half, 1) * sa + pltpu.roll(x, half, 1) * sb


def _softmax_fold(s, band, prev_bias, sink2):
    sp = s[:, :WINDOW] if prev_bias is None else s[:, :WINDOW] + prev_bias
    t = jnp.where(band, sp, s[:, WINDOW:])
    m = jnp.maximum(jnp.max(t, axis=-1, keepdims=True), sink2)
    e = jnp.exp2(t - m)
    den = jnp.sum(e, axis=-1, keepdims=True) + jnp.exp2(sink2 - m)
    p = jnp.concatenate([jnp.where(band, e, 0.0), jnp.where(band, 0.0, e)], axis=1)
    return p.astype(BF16), den


def _attn_prompt_kernel(sink_ref, q_ref, kc_ref, kp_ref, vc_ref, vp_ref,
                        o_ref, nk_ref, nv_ref, *, nq):
    i = pl.program_id(1)
    nsteps = pl.num_programs(1)
    w = WINDOW
    kcr = kc_ref[...]
    kpr = kp_ref[...]
    vc = vc_ref[...]

    @pl.when(i == nsteps - 1)
    def _():
        nk_ref[...] = kcr[(nq - 1) * w:]
        nv_ref[...] = vc[(nq - 1) * w:]

    kall = jnp.concatenate([kpr, kcr], axis=0)
    vall = jnp.concatenate([vp_ref[...], vc], axis=0)
    lo = lax.broadcasted_iota(jnp.int32, ((nq + 1) * w, LANES), 1) < HEAD_DIM
    lo_q = lax.broadcasted_iota(jnp.int32, (w, LANES), 1) < HEAD_DIM
    band = (lax.broadcasted_iota(jnp.int32, (w, w), 1) > lax.broadcasted_iota(jnp.int32, (w, w), 0))
    first_bias = jnp.where(i == 0, -jnp.inf, 0.0)

    for g in range(N_KV_HEADS):
        col, odd = g // 2, g % 2
        kg = kall[:, col * LANES:(col + 1) * LANES]
        vg = vall[:, col * LANES:(col + 1) * LANES]
        kg_sw = pltpu.roll(kg, HEAD_DIM, 1)
        vg_sw = pltpu.roll(vg, HEAD_DIM, 1)
        k_lo = jnp.where(lo, kg_sw if odd else kg, 0.0).astype(BF16)
        k_hi = jnp.where(lo, 0.0, kg if odd else kg_sw).astype(BF16)
        v_lo = jnp.where(lo, vg_sw if odd else vg, 0.0).astype(BF16)
        v_hi = jnp.where(lo, 0.0, vg if odd else vg_sw).astype(BF16)
        sinks2 = [sink_ref[4 * g + r] * LOG2E for r in range(4)]
        for s in range(nq):
            rows = slice(s * w, (s + 1) * w)
            keys = slice(s * w, (s + 2) * w)
            pb = first_bias if s == 0 else None
            qst = jnp.concatenate([q_ref[rows, (2 * g) * LANES:(2 * g + 1) * LANES],
                                   q_ref[rows, (2 * g + 1) * LANES:(2 * g + 2) * LANES]],
                                  axis=0).astype(BF16)
            s_lo = _dot_nt(qst, k_lo[keys])
            s_hi = _dot_nt(qst, k_hi[keys])
            e0, d0 = _softmax_fold(s_lo[:w], band, pb, sinks2[0])
            e1, d1 = _softmax_fold(s_hi[:w], band, pb, sinks2[1])
            e2, d2 = _softmax_fold(s_lo[w:], band, pb, sinks2[2])
            e3, d3 = _softmax_fold(s_hi[w:], band, pb, sinks2[3])
            p = jnp.concatenate([jnp.concatenate([e0, e1], axis=1),
                                 jnp.concatenate([e2, e3], axis=1)], axis=0)
            vcat = jnp.concatenate([v_lo[keys], v_hi[keys]], axis=0)
            o = _dot(p, vcat)
            oa = o[:w] / jnp.where(lo_q, d0, d1)
            ob = o[w:] / jnp.where(lo_q, d2, d3)
            o_ref[rows, (2 * g) * LANES:(2 * g + 1) * LANES] = oa.astype(o_ref.dtype)
            o_ref[rows, (2 * g + 1) * LANES:(2 * g + 2) * LANES] = ob.astype(o_ref.dtype)


def _attn_prompt(proj, sinks, nq):
    b, l, _ = proj.shape
    w = WINDOW
    nsteps = l // (nq * w)
    kcol, vcol = COL_K // KV_DIM, COL_V // KV_DIM
    prev = lambda bi, i: jnp.maximum(nq * i - 1, 0)
    return pl.pallas_call(
        functools.partial(_attn_prompt_kernel, nq=nq),
        grid=(b, nsteps),
        in_specs=[
            pl.BlockSpec(memory_space=pltpu.SMEM),
            pl.BlockSpec((None, nq * w, Q_DIM), lambda bi, i: (bi, i, COL_Q // Q_DIM)),
            pl.BlockSpec((None, nq * w, KV_DIM), lambda bi, i: (bi, i, kcol)),
            pl.BlockSpec((None, w, KV_DIM), lambda bi, i: (bi, prev(bi, i), kcol)),
            pl.BlockSpec((None, nq * w, KV_DIM), lambda bi, i: (bi, i, vcol)),
            pl.BlockSpec((None, w, KV_DIM), lambda bi, i: (bi, prev(bi, i), vcol)),
        ],
        out_specs=[
            pl.BlockSpec((None, nq * w, Q_DIM), lambda bi, i: (bi, i, 0)),
            pl.BlockSpec((None, w, KV_DIM), lambda bi, i: (bi, 0, 0)),
            pl.BlockSpec((None, w, KV_DIM), lambda bi, i: (bi, 0, 0)),
        ],
        out_shape=[
            jax.ShapeDtypeStruct((b, l, Q_DIM), BF16),
            jax.ShapeDtypeStruct((b, w, KV_DIM), F32),
            jax.ShapeDtypeStruct((b, w, KV_DIM), F32),
        ],
        compiler_params=_cparams(2),
        name="attn_prompt",
    )(sinks, proj, proj, proj, proj, proj)


def _softplus(v):
    return jnp.maximum(v, 0.0) + jnp.log1p(jnp.exp(-jnp.abs(v)))


def _head_expand(vals, ex2_ref):
    hi = vals.astype(BF16)
    mid = (vals - hi.astype(F32)).astype(BF16)
    return _dot(jnp.concatenate([hi, mid], axis=1), ex2_ref[...])


def _ssd_prompt_kernel(*refs, nsub):
    nz, nx = SSM_INNER // PIECE, CONV_DIM // PIECE
    z_refs, x_refs = refs[:nz], refs[nz:nz + nx]
    (dt_ref, cw_ref, cb_ref, dtb_ref, alog_ref, dsk_ref, gn_ref, ex2_ref,
     y_ref, nssm_ref, nconv_ref, state_ref, carry_ref) = refs[nz + nx:]
    i = pl.program_id(1)
    nc = pl.num_programs(1)
    q = SSD_CHUNK

    @pl.when(i == 0)
    def _():
        state_ref[...] = jnp.zeros_like(state_ref)
        carry_ref[...] = jnp.zeros_like(carry_ref)

    dtp = dt_ref[:q, :]
    for s in range(1, nsub):
        dtp = dtp + pltpu.roll(dt_ref[s * q:(s + 1) * q, :], s * SSM_HEADS, 1)
    dt = _softplus(dtp + dtb_ref[...])
    da = dt * (-jnp.exp(alog_ref[...]))
    tri = (lax.broadcasted_iota(jnp.int32, (q, q), 0) >= lax.broadcasted_iota(jnp.int32, (q, q), 1))
    cs = _dot_exact_lhs01(tri.astype(BF16), da)
    cs_t = cs.T
    sc = dict(
        tri=tri, cs=cs, cs_t=cs_t, dt_t=dt.T,
        cdec_t=jnp.exp(cs_t[:, q - 1:q]),
        ecs_x=_head_expand(jnp.exp(cs), ex2_ref),
        wgt_x=_head_expand(dt * jnp.exp(cs[q - 1:q, :] - cs), ex2_ref))

    rows_of = lambda rs, sl: jnp.concatenate([r[sl, :] for r in rs], axis=1)
    for s in range(nsub):
        _ssd_chunk(s, sc, rows_of, z_refs, x_refs, cw_ref, cb_ref, dsk_ref,
                   gn_ref, y_ref, state_ref, carry_ref)
    tail = rows_of(x_refs, slice(nsub * q - 8, nsub * q))
    carry_ref[...] = tail

    @pl.when(i == nc - 1)
    def _():
        nconv_ref[...] = tail
        nssm_ref[...] = state_ref[...]


def _ssd_chunk(s, sc, rows_of, z_refs, x_refs, cw_ref, cb_ref, dsk_ref,
               gn_ref, y_ref, state_ref, carry_ref):
    q = SSD_CHUNK
    tri, cs, cs_t, dt_t, cdec_t = sc["tri"], sc["cs"], sc["cs_t"], sc["dt_t"], sc["cdec_t"]
    hoff = s * SSM_HEADS
    xoff = s * SSM_INNER
    trows = slice(s * q, (s + 1) * q)
    x = rows_of(x_refs, trows)
    z = rows_of(z_refs, trows)
    prev = carry_ref[...] if s == 0 else rows_of(x_refs, slice(s * q - 8, s * q))
    row8 = lax.broadcasted_iota(jnp.int32, (8, CONV_DIM), 0)

    def shifted(k):
        r = pltpu.roll(x, k, 0)
        head = jnp.where(row8 < k, pltpu.roll(prev, k, 0), r[:8])
        return jnp.concatenate([head, r[8:]], axis=0)

    conv = x * cw_ref[CONV_WIDTH - 1:CONV_WIDTH, :] + cb_ref[...]
    for k in range(1, CONV_WIDTH):
        conv = conv + shifted(k) * cw_ref[CONV_WIDTH - 1 - k:CONV_WIDTH - k, :]

    xc = _silu(conv)
    xs = xc[:, :SSM_INNER]
    bm = xc[:, SSM_INNER:SSM_INNER + BC_DIM]
    cm = xc[:, SSM_INNER + BC_DIM:]

    lo = lax.broadcasted_iota(jnp.int32, (q, LANES), 1) < SSM_HEAD_DIM

    hpg = SSM_HEADS // SSM_GROUPS
    gw = hpg * SSM_HEAD_DIM
    for g in range(SSM_GROUPS):
        bg = bm[:, g * SSM_STATE:(g + 1) * SSM_STATE].astype(BF16)
        cg = cm[:, g * SSM_STATE:(g + 1) * SSM_STATE].astype(BF16)
        cb = _dot_nt(cg, bg)
        st = state_ref[g * gw:(g + 1) * gw, :]
        yoff = _dot_nt(cg, st.astype(BF16))
        ys = []
        for pr in range(2):
            pair = 2 * g + pr
            ms = []
            for h in (hoff + 2 * pair, hoff + 2 * pair + 1):
                diff = cs[:, h:h + 1] - cs_t[h:h + 1, :]
                lm = jnp.exp(jnp.where(tri, diff, -jnp.inf))
                ms.append((cb * lm * dt_t[h:h + 1, :]).astype(BF16))
            xp = xs[:, pair * LANES:(pair + 1) * LANES]
            x2 = jnp.concatenate([jnp.where(lo, xp, 0.0), jnp.where(lo, 0.0, xp)],
                                 axis=0).astype(BF16)
            yd = _dot(jnp.concatenate(ms, axis=1), x2)
            yo = yoff[:, pr * LANES:(pr + 1) * LANES] * sc["ecs_x"][
                :, xoff + pair * LANES:xoff + (pair + 1) * LANES]
            ys.append(yd + yo + xp * dsk_ref[:, pair * LANES:(pair + 1) * LANES])
        yg = jnp.concatenate(ys, axis=1)
        wx = xs[:, g * gw:(g + 1) * gw] * sc["wgt_x"][:, xoff + g * gw:xoff + (g + 1) * gw]
        s_new = _dot_tn(wx.astype(BF16), bg)
        for r in range(hpg):
            h = hoff + hpg * g + r
            rows = slice(g * gw + r * SSM_HEAD_DIM, g * gw + (r + 1) * SSM_HEAD_DIM)
            state_ref[rows, :] = (st[r * SSM_HEAD_DIM:(r + 1) * SSM_HEAD_DIM, :] * cdec_t[h:h + 1, :]
                                  + s_new[r * SSM_HEAD_DIM:(r + 1) * SSM_HEAD_DIM, :])
        hg = yg * _silu(z[:, g * gw:(g + 1) * gw])
        y_ref[trows, g * gw:(g + 1) * gw] = (_rms(hg) * gn_ref[:, g * gw:(g + 1) * gw]
                                            ).astype(y_ref.dtype)


def _ssd_prompt(proj, dtraw, conv_w, conv_b, dt_bias_t, a_log_t, d_skip_x, norm_g, nsub):
    assert nsub * SSM_HEADS <= LANES
    src = np.arange(nsub * SSM_INNER) // SSM_HEAD_DIM
    ex = (np.arange(LANES)[:, None] == src[None, :]).astype(np.float32)
    ex2 = jnp.asarray(np.concatenate([ex, ex], axis=0), BF16)
    b, l, _ = proj.shape
    q = SSD_CHUNK * nsub
    nc = l // q
    const = lambda bi, i: (0, 0)
    piece = lambda c: pl.BlockSpec((None, q, PIECE), lambda bi, i: (bi, i, c))
    n_pieces = (SSM_INNER + CONV_DIM) // PIECE
    return pl.pallas_call(
        functools.partial(_ssd_prompt_kernel, nsub=nsub),
        grid=(b, nc),
        in_specs=[piece(COL_Z // PIECE + c) for c in range(SSM_INNER // PIECE)] + [
            piece(COL_XBC // PIECE + c) for c in range(CONV_DIM // PIECE)] + [
            pl.BlockSpec((None, q, HEADS_PAD), lambda bi, i: (bi, i, 0)),
            pl.BlockSpec((CONV_WIDTH, CONV_DIM), const),
            pl.BlockSpec((1, CONV_DIM), const),
            pl.BlockSpec((1, HEADS_PAD), const),
            pl.BlockSpec((1, HEADS_PAD), const),
            pl.BlockSpec((1, SSM_INNER), const),
            pl.BlockSpec((1, SSM_INNER), const),
            pl.BlockSpec((2 * LANES, nsub * SSM_INNER), const),
        ],
        out_specs=[
            pl.BlockSpec((None, q, SSM_INNER), lambda bi, i: (bi, i, 0)),
            pl.BlockSpec((None, SSM_INNER, SSM_STATE), lambda bi, i: (bi, 0, 0)),
            pl.BlockSpec((None, 8, CONV_DIM), lambda bi, i: (bi, 0, 0)),
        ],
        out_shape=[
            jax.ShapeDtypeStruct((b, l, SSM_INNER), BF16),
            jax.ShapeDtypeStruct((b, SSM_INNER, SSM_STATE), F32),
            jax.ShapeDtypeStruct((b, 8, CONV_DIM), F32),
        ],
        scratch_shapes=[pltpu.VMEM((SSM_INNER, SSM_STATE), F32),
                        pltpu.VMEM((8, CONV_DIM), F32)],
        compiler_params=_cparams(2),
        name="ssd_prompt",
    )(*([proj] * n_pieces), dtraw, conv_w, conv_b, dt_bias_t, a_log_t, d_skip_x, norm_g, ex2)


def _outproj_kernel(x_ref, a_ref, y_ref, wa_ref, wy_ref, o_ref):
    o_ref[...] = (x_ref[...] + _dot(a_ref[...].astype(BF16), wa_ref[...])
                  + _dot(y_ref[...].astype(BF16), wy_ref[...]))


def _outproj(x, attn, y, w_out, tm):
    t, d = x.shape
    half = w_out.shape[0] // 2
    return pl.pallas_call(
        _outproj_kernel,
        grid=(t // tm,),
        in_specs=[
            pl.BlockSpec((tm, d), lambda i: (i, 0)),
            pl.BlockSpec((tm, half), lambda i: (i, 0)),
            pl.BlockSpec((tm, half), lambda i: (i, 0)),
            pl.BlockSpec((half, d), lambda i: (0, 0)),
            pl.BlockSpec((half, d), lambda i: (1, 0)),
        ],
        out_specs=pl.BlockSpec((tm, d), lambda i: (i, 0)),
        out_shape=jax.ShapeDtypeStruct((t, d), F32),
        compiler_params=_cparams(1),
        name="outproj",
    )(x, attn, y, w_out, w_out)


def _ffn_kernel(h_ref, g_ref, wg_ref, wu_ref, wd_ref, o_ref, f_ref):
    def delta(f):
        hid = (_silu(_dot(f, wg_ref[...])) * _dot(f, wu_ref[...])).astype(BF16)
        return _dot(hid, wd_ref[...])

    @pl.when(pl.program_id(1) == 0)
    def _():
        h = h_ref[...]
        f = (_rms(h) * g_ref[...]).astype(BF16)
        f_ref[...] = f
        o_ref[...] = h + delta(f)

    @pl.when(pl.program_id(1) > 0)
    def _():
        o_ref[...] += delta(f_ref[...])


def _ffn(h, g, wg, wu, wd, tm, th):
    t, d = h.shape
    hidden = wd.shape[0]
    return pl.pallas_call(
        _ffn_kernel,
        grid=(t // tm, hidden // th),
        in_specs=[
            pl.BlockSpec((tm, d), lambda i, j: (i, 0)),
            pl.BlockSpec((1, d), lambda i, j: (0, 0)),
            pl.BlockSpec((d, th), lambda i, j: (0, j)),
            pl.BlockSpec((d, th), lambda i, j: (0, j)),
            pl.BlockSpec((th, d), lambda i, j: (j, 0)),
        ],
        out_specs=pl.BlockSpec((tm, d), lambda i, j: (i, 0)),
        out_shape=jax.ShapeDtypeStruct((t, d), F32),
        scratch_shapes=[pltpu.VMEM((tm, d), BF16)],
        compiler_params=_cparams(2),
        name="ffn",
    )(h, g, wg, wu, wd)


def _ple_kernel(h_ref, p_ref, gp_ref, wp_ref, wg_ref, gf_ref, o_ref, *, tn, row_parts):
    tm, d = h_ref.shape
    rp = tm // row_parts
    for r in range(row_parts):
        rows = slice(r * rp, (r + 1) * rp)
        n = (_rms(h_ref[rows, :]) * gp_ref[...]).astype(BF16)
        pb = p_ref[rows, :].astype(BF16)
        ss = jnp.zeros((rp, 1), F32)
        for c in range(d // tn):
            cols = slice(c * tn, (c + 1) * tn)
            gate = _dot(n, wg_ref[:, cols])
            h3 = h_ref[rows, cols] + _dot(pb, wp_ref[:, cols]) * _sigmoid(gate)
            o_ref[rows, cols] = h3
            ss = ss + jnp.sum(h3 * h3, axis=-1, keepdims=True)
        inv = lax.rsqrt(ss * (1.0 / d) + RMS_EPS)
        o_ref[rows, :] = o_ref[rows, :] * inv * gf_ref[...]


def _ple_final(h, p, g_ple, w_ple, w_gate, g_final, tm):
    t, d = h.shape
    pd = p.shape[1]
    const = lambda i: (0, 0)
    resident = dict(pipeline_mode=pl.Buffered(1))
    return pl.pallas_call(
        functools.partial(_ple_kernel, tn=512, row_parts=max(tm // 256, 1)),
        grid=(t // tm,),
        in_specs=[
            pl.BlockSpec((tm, d), lambda i: (i, 0)),
            pl.BlockSpec((tm, pd), lambda i: (i, 0)),
            pl.BlockSpec((1, d), const),
            pl.BlockSpec((pd, d), const, **resident),
            pl.BlockSpec((d, d), const, **resident),
            pl.BlockSpec((1, d), const),
        ],
        out_specs=pl.BlockSpec((tm, d), lambda i: (i, 0)),
        out_shape=jax.ShapeDtypeStruct((t, d), F32),
        compiler_params=_cparams(1),
        name="ple_final",
    )(h, p, g_ple, w_ple, w_gate, g_final)


def _sample_pre_kernel(x_ref, dt_ref, sc_ref, cw_ref, cb_ref, dtb_ref,
                       alog_ref, dsk_ref, exp_ref,
                       xdt_ref, b_ref, c_ref, dec_ref, yp_ref, dech_ref):
    conv = x_ref[...] * cw_ref[CONV_WIDTH - 1:CONV_WIDTH, :] + cb_ref[...]
    for k in range(CONV_WIDTH - 1):
        conv = conv + sc_ref[k] * cw_ref[k:k + 1, :]
    xc = _silu(conv)
    xs = xc[:, :SSM_INNER]
    bm = xc[:, SSM_INNER:SSM_INNER + BC_DIM]
    cm = xc[:, SSM_INNER + BC_DIM:]
    b_ref[...] = bm
    c_ref[...] = cm
    dt = _softplus(dt_ref[...] + dtb_ref[...])
    dec = jnp.exp(dt * (-jnp.exp(alog_ref[...])))
    ex = exp_ref[...]
    dtx = _dot_exact_rhs01(dt, ex)
    dec_ref[...] = _dot_exact_rhs01(dec, ex)
    dech_ref[...] = dec
    xdt = xs * dtx
    xdt_ref[...] = xdt
    gw = SSM_INNER // SSM_GROUPS
    cbs = []
    for g in range(SSM_GROUPS):
        prod = cm[:, g * SSM_STATE:(g + 1) * SSM_STATE] * bm[:, g * SSM_STATE:(g + 1) * SSM_STATE]
        cbs.append(jnp.broadcast_to(jnp.sum(prod, axis=-1, keepdims=True), (prod.shape[0], gw)))
    yp_ref[...] = xdt * jnp.concatenate(cbs, axis=1) + xs * dsk_ref[...]


def _sample_pre(xbc, dtraw, sconv_t, conv_w, conv_b, dt_bias, a_log, d_skip_x, expand):
    nb = xbc.shape[0]
    shapes = [(nb, SSM_INNER), (nb, BC_DIM), (nb, BC_DIM),
              (nb, SSM_INNER), (nb, SSM_INNER), (nb, HEADS_PAD)]
    return pl.pallas_call(
        _sample_pre_kernel,
        out_shape=[jax.ShapeDtypeStruct(s, F32) for s in shapes],
        compiler_params=pltpu.CompilerParams(vmem_limit_bytes=VMEM_LIMIT),
        name="sample_pre",
    )(xbc, dtraw, sconv_t, conv_w, conv_b, dt_bias, a_log, d_skip_x, expand)


def _attn_sample_kernel(sink_ref, q_ref, kn_ref, vn_ref, ck_ref, cv_ref, o_ref, nk_ref, nv_ref, *, bb):
    w = WINDOW
    row = lax.broadcasted_iota(jnp.int32, (w, KV_DIM), 0)
    hrow = lax.broadcasted_iota(jnp.int32, (N_HEADS, KV_DIM), 0) // (N_HEADS // N_KV_HEADS)
    hgrp = lax.broadcasted_iota(jnp.int32, (N_HEADS, KV_DIM), 1) // HEAD_DIM
    own = hrow == hgrp
    sink = sink_ref[...] * LOG2E
    for b in range(bb):
        kk = jnp.where(row == w - 1, kn_ref[b:b + 1, :], pltpu.roll(ck_ref[b], w - 1, 0))
        vv = jnp.where(row == w - 1, vn_ref[b:b + 1, :], pltpu.roll(cv_ref[b], w - 1, 0))
        nk_ref[b] = kk
        nv_ref[b] = vv
        qb = q_ref[b]
        qrow = jnp.where(own, jnp.concatenate([qb] * N_KV_HEADS, axis=1), 0.0)
        s = _dot_nt(qrow.astype(BF16), kk.astype(BF16))
        m = jnp.maximum(jnp.max(s, axis=-1, keepdims=True), sink)
        e = jnp.exp2(s - m)
        den = jnp.sum(e, axis=-1, keepdims=True) + jnp.exp2(sink - m)
        of = jnp.where(own, _dot(e.astype(BF16), vv.astype(BF16)), 0.0)
        o = of[:, :HEAD_DIM]
        for g in range(1, N_KV_HEADS):
            o = o + of[:, g * HEAD_DIM:(g + 1) * HEAD_DIM]
        o_ref[b] = o / den


def _attn_sample(sinks_col, q3, knew, vnew, cache_k, cache_v, bb):
    nb = q3.shape[0]
    w = WINDOW
    return pl.pallas_call(
        functools.partial(_attn_sample_kernel, bb=bb),
        grid=(nb // bb,),
        in_specs=[
            pl.BlockSpec((N_HEADS, 1), lambda i: (0, 0)),
            pl.BlockSpec((bb, N_HEADS, HEAD_DIM), lambda i: (i, 0, 0)),
            pl.BlockSpec((bb, KV_DIM), lambda i: (i, 0)),
            pl.BlockSpec((bb, KV_DIM), lambda i: (i, 0)),
            pl.BlockSpec((bb, w, KV_DIM), lambda i: (i, 0, 0)),
            pl.BlockSpec((bb, w, KV_DIM), lambda i: (i, 0, 0)),
        ],
        out_specs=[
            pl.BlockSpec((bb, N_HEADS, HEAD_DIM), lambda i: (i, 0, 0)),
            pl.BlockSpec((bb, w, KV_DIM), lambda i: (i, 0, 0)),
            pl.BlockSpec((bb, w, KV_DIM), lambda i: (i, 0, 0)),
        ],
        out_shape=[
            jax.ShapeDtypeStruct((nb, N_HEADS, HEAD_DIM), F32),
            jax.ShapeDtypeStruct((nb, w, KV_DIM), F32),
            jax.ShapeDtypeStruct((nb, w, KV_DIM), F32),
        ],
        compiler_params=_cparams(1),
        name="attn_sample",
    )(sinks_col, q3, knew, vnew, cache_k, cache_v)


def _ssd_sample_kernel(st_ref, xdt_ref, b_ref, c_ref, dec_ref, dech_ref, yp_ref, z_ref, gn_ref,
                       y_ref, ns_ref, *, bb):
    gw = SSM_INNER // SSM_GROUPS
    grow = lax.broadcasted_iota(jnp.int32, (8, SSM_INNER), 0)
    glane = lax.broadcasted_iota(jnp.int32, (8, SSM_INNER), 1) // gw
    own = grow == glane
    pad = jnp.zeros((8 - SSM_GROUPS, SSM_STATE), F32)
    yoffs = []
    for b in range(bb):
        st = st_ref[b]
        cmat = jnp.concatenate([c_ref[b], pad], axis=0).astype(BF16)
        bmat = jnp.concatenate([b_ref[b], pad], axis=0).astype(BF16)
        r = _dot_nt(cmat, st.astype(BF16))
        yoffs.append(jnp.sum(jnp.where(own, r, 0.0), axis=0, keepdims=True))
        amat = jnp.where(own, jnp.broadcast_to(xdt_ref[b:b + 1, :], (8, SSM_INNER)), 0.0)
        outer = _dot_tn(amat.astype(BF16), bmat)
        for h in range(SSM_HEADS):
            rows = slice(h * SSM_HEAD_DIM, (h + 1) * SSM_HEAD_DIM)
            ns_ref[b, rows, :] = st[rows, :] * dech_ref[b:b + 1, h:h + 1] + outer[rows, :]
    y = yp_ref[...] + jnp.concatenate(yoffs, axis=0) * dec_ref[...]
    hg = y * _silu(z_ref[...])
    outs = []
    for g in range(SSM_GROUPS):
        outs.append(_rms(hg[:, g * gw:(g + 1) * gw]))
    y_ref[...] = jnp.concatenate(outs, axis=1) * gn_ref[...]


def _ssd_sample(state, xdt, b3, c3, decx, dech, ypart, z, norm_g, bb):
    nb = state.shape[0]
    row = lambda i: (i, 0)
    return pl.pallas_call(
        functools.partial(_ssd_sample_kernel, bb=bb),
        grid=(nb // bb,),
        in_specs=[
            pl.BlockSpec((bb, SSM_INNER, SSM_STATE), lambda i: (i, 0, 0)),
            pl.BlockSpec((bb, SSM_INNER), row),
            pl.BlockSpec((bb, SSM_GROUPS, SSM_STATE), lambda i: (i, 0, 0)),
            pl.BlockSpec((bb, SSM_GROUPS, SSM_STATE), lambda i: (i, 0, 0)),
            pl.BlockSpec((bb, SSM_INNER), row),
            pl.BlockSpec((bb, HEADS_PAD), row),
            pl.BlockSpec((bb, SSM_INNER), row),
            pl.BlockSpec((bb, SSM_INNER), row),
            pl.BlockSpec((1, SSM_INNER), lambda i: (0, 0)),
        ],
        out_specs=[
            pl.BlockSpec((bb, SSM_INNER), row),
            pl.BlockSpec((bb, SSM_INNER, SSM_STATE), lambda i: (i, 0, 0)),
        ],
        out_shape=[
            jax.ShapeDtypeStruct((nb, SSM_INNER), F32),
            jax.ShapeDtypeStruct((nb, SSM_INNER, SSM_STATE), F32),
        ],
        compiler_params=_cparams(1),
        name="ssd_sample",
    )(state, xdt, b3, c3, decx, dech, ypart, z, norm_g)


def _row_tile(t, want):
    return want if t % want == 0 else t


def _dense_tail(x, attn, y, p, wts, tm):
    h1 = _outproj(x, attn, y, wts["w_out"], tm)
    h2 = _ffn(h1, wts["g_ffn"], wts["w_gate"], wts["w_up"], wts["w_down"],
              _row_tile(x.shape[0], FFN_TM), FFN_TH)
    return _ple_final(h2, p, wts["g_ple"], wts["w_ple"], wts["w_ple_gate"], wts["g_final"],
                      _row_tile(x.shape[0], 1024))


def kernel(x_prompt, x_sample, cache_k, cache_v, state_ssm, state_conv, p_prompt, p_sample, w_in, conv_w, conv_b, dt_bias, a_log, d_skip, ssm_norm_g, attn_sinks, w_out, g_mix, g_ffn, w_ffn_gate, w_ffn_up, w_ffn_down, g_ple, w_ple, w_ple_gate, g_final):
    depth = w_in.shape[0]
    assert depth == 1, "single-layer step only"
    bsz, seq, d = x_prompt.shape
    nb, dseq, _ = x_sample.shape
    assert dseq == 1 and seq % SSD_CHUNK == 0 and seq % WINDOW == 0

    w = w_in[0]
    w_main = w.astype(BF16)
    w_dt = jnp.pad(w[:, PROJ_DIM:], ((0, 0), (0, HEADS_PAD - SSM_HEADS))).astype(BF16)
    padh = lambda v: jnp.pad(v, (0, HEADS_PAD - SSM_HEADS)).reshape(1, HEADS_PAD)
    tileh = lambda v: jnp.tile(v, HEADS_PAD // SSM_HEADS).reshape(1, HEADS_PAD)
    dtb, alog = padh(dt_bias[0]), padh(a_log[0])
    dsk_x = jnp.repeat(d_skip[0], SSM_HEAD_DIM).reshape(1, SSM_INNER)
    gn = ssm_norm_g[0].reshape(1, SSM_INNER)
    cw, cb = conv_w[0], conv_b[0].reshape(1, CONV_DIM)
    wts = dict(
        w_out=w_out[0].astype(BF16), g_ffn=g_ffn[0].reshape(1, d),
        w_gate=w_ffn_gate[0].astype(BF16), w_up=w_ffn_up[0].astype(BF16),
        w_down=w_ffn_down[0].astype(BF16), g_ple=g_ple[0].reshape(1, d),
        w_ple=w_ple[0].astype(BF16), w_ple_gate=w_ple_gate[0].astype(BF16),
        g_final=g_final.reshape(1, d))
    gmix = g_mix[0].reshape(1, d)
    sinks = attn_sinks[0]

    tp = bsz * seq
    xp = x_prompt.reshape(tp, d)
    tm_in = _row_tile(seq, 1024)
    proj, dtraw = _inproj(xp, gmix, w_main, w_dt, _rope_tables(jnp.arange(seq)), tm_in)
    proj3 = proj.reshape(bsz, seq, PROJ_DIM)
    attn, nk_p, nv_p = _attn_prompt(proj3, sinks, ATTN_BLOCKS)
    yp, nssm_p, nconv_p = _ssd_prompt(proj3, dtraw.reshape(bsz, seq, HEADS_PAD), cw, cb,
                                      tileh(dt_bias[0]), tileh(a_log[0]), dsk_x, gn, SSD_CHUNKS)
    tm = _row_tile(tp, 512)
    y_prompt = _dense_tail(xp, attn.reshape(tp, Q_DIM), yp.reshape(tp, SSM_INNER),
                           p_prompt[0].reshape(tp, -1), wts, tm).reshape(bsz, seq, d)

    xs = x_sample.reshape(nb, d)
    tab_s = _rope_tables(jnp.full((nb,), PAST_LEN, jnp.int32))
    proj_s, dtraw_s = _inproj(xs, gmix, w_main, w_dt, tab_s, nb)
    window = cache_k.shape[2]
    sconv_t = jnp.transpose(state_conv[0], (1, 0, 2))
    expand = (jnp.arange(HEADS_PAD)[:, None] == (jnp.arange(SSM_INNER) // SSM_HEAD_DIM)[None, :]
              ).astype(BF16)
    q_s = proj_s[:, COL_Q:COL_Q + Q_DIM]
    k_s = proj_s[:, COL_K:COL_K + KV_DIM]
    v_s = proj_s[:, COL_V:COL_V + KV_DIM]
    z_s = proj_s[:, COL_Z:COL_Z + SSM_INNER]
    xbc_s = proj_s[:, COL_XBC:COL_XBC + CONV_DIM]
    xdt, bm, cm, decx, ypart, dech = _sample_pre(
        xbc_s, dtraw_s, sconv_t, cw, cb, dtb, alog, dsk_x, expand)
    bb = 8 if nb % 8 == 0 else nb
    attn_s, nk_s, nv_s = _attn_sample(
        sinks.reshape(N_HEADS, 1), q_s.reshape(nb, N_HEADS, HEAD_DIM), k_s, v_s,
        cache_k[0].reshape(nb, window, KV_DIM), cache_v[0].reshape(nb, window, KV_DIM), bb)
    ys, nssm_s = _ssd_sample(
        state_ssm[0].reshape(nb, SSM_INNER, SSM_STATE), xdt,
        bm.reshape(nb, SSM_GROUPS, SSM_STATE), cm.reshape(nb, SSM_GROUPS, SSM_STATE),
        decx, dech, ypart, z_s, gn, bb)
    y_sample = _dense_tail(xs, attn_s.reshape(nb, Q_DIM), ys, p_sample[0].reshape(nb, -1), wts,
                           nb).reshape(nb, 1, d)
    nconv_s = jnp.concatenate([state_conv[0][:, 1:], xbc_s[:, None, :]], axis=1)

    kv5 = lambda t: t.reshape(1, t.shape[0], window, N_KV_HEADS, HEAD_DIM)
    return (y_prompt, y_sample,
            kv5(nk_p), kv5(nv_p),
            nssm_p.reshape(1, bsz, SSM_HEADS, SSM_HEAD_DIM, SSM_STATE),
            nconv_p[None, :, 8 - (CONV_WIDTH - 1):, :],
            kv5(nk_s), kv5(nv_s),
            nssm_s.reshape(1, nb, SSM_HEADS, SSM_HEAD_DIM, SSM_STATE),
            nconv_s[None])
```

```python
import functools

import numpy as np
import jax
import jax.numpy as jnp
from jax import lax
from jax.experimental import pallas as pl
from jax.experimental.pallas import tpu as pltpu

F32 = jnp.float32
BF16 = jnp.bfloat16

HEAD_DIM = 64
N_HEADS = 16
N_KV_HEADS = 4
WINDOW = 128
ROPE_DIM = 16
ROPE_THETA = 500000.0
SSM_HEADS = 16
SSM_HEAD_DIM = 64
SSM_GROUPS = 4
SSM_STATE = 128
CONV_WIDTH = 4
SSD_CHUNK = 128
RMS_EPS = 1e-6
PAST_LEN = 16384
LOG2E = 1.4426950408889634

Q_DIM = N_HEADS * HEAD_DIM
KV_DIM = N_KV_HEADS * HEAD_DIM
SSM_INNER = SSM_HEADS * SSM_HEAD_DIM
BC_DIM = SSM_GROUPS * SSM_STATE
CONV_DIM = SSM_INNER + 2 * BC_DIM
LANES = 128
HEADS_PAD = LANES

COL_Q = 0
COL_K = Q_DIM
COL_V = COL_K + KV_DIM
COL_Z = COL_V + KV_DIM
COL_XBC = COL_Z + SSM_INNER
PROJ_DIM = COL_XBC + CONV_DIM
PIECE = 512

INPROJ_TN = Q_DIM + 2 * KV_DIM
Q_SCALE = HEAD_DIM ** -0.5 * LOG2E
FFN_TM, FFN_TH = 1024, 512
ATTN_BLOCKS = 8
SSD_CHUNKS = 8
VMEM_LIMIT = 56 * 1024 * 1024


def _cparams(n_axes):
    return pltpu.CompilerParams(
        dimension_semantics=("arbitrary",) * n_axes, vmem_limit_bytes=VMEM_LIMIT)


def _rms(x):
    return x * lax.rsqrt(jnp.mean(x * x, axis=-1, keepdims=True) + RMS_EPS)


def _sigmoid(x):
    return 1.0 / (1.0 + jnp.exp(-x))


def _silu(x):
    return x * _sigmoid(x)


def _dot(a, b):
    return jnp.dot(a, b, preferred_element_type=F32)


def _dot_nt(a, b):
    return lax.dot_general(a, b, (((1,), (1,)), ((), ())), preferred_element_type=F32)


def _dot_tn(a, b):
    return lax.dot_general(a, b, (((0,), (0,)), ((), ())), preferred_element_type=F32)


def _split3(x):
    hi = x.astype(BF16)
    r = x - hi.astype(F32)
    mid = r.astype(BF16)
    lo = (r - mid.astype(F32)).astype(BF16)
    return hi, mid, lo


def _dot_exact_lhs01(m01, x):
    hi, mid, lo = _split3(x)
    return _dot(m01, hi) + _dot(m01, mid) + _dot(m01, lo)


def _dot_exact_rhs01(x, m01):
    hi, mid, lo = _split3(x)
    return _dot(hi, m01) + _dot(mid, m01) + _dot(lo, m01)


def _inproj_kernel(x_ref, g_ref, w_ref, wdt_ref, tab_ref, o_ref, odt_ref, u_ref):
    j = pl.program_id(1)

    @pl.when(j == 0)
    def _():
        u = (_rms(x_ref[...]) * g_ref[...]).astype(BF16)
        u_ref[...] = u
        odt_ref[...] = _dot(u, wdt_ref[...])
        res = _dot(u, w_ref[...])
        tab = tab_ref[...]
        for c in range((Q_DIM + KV_DIM) // LANES):
            cols = slice(c * LANES, (c + 1) * LANES)
            r = _rope(res[:, cols], tab)
            o_ref[:, cols] = r * Q_SCALE if c < Q_DIM // LANES else r
        o_ref[:, Q_DIM + KV_DIM:] = res[:, Q_DIM + KV_DIM:]

    @pl.when(j > 0)
    def _():
        o_ref[...] = _dot(u_ref[...], w_ref[...])


def _inproj(x, g, w, wdt, tab, tm):
    t, d = x.shape
    n = PROJ_DIM
    tn = INPROJ_TN
    nt = tab.shape[0] // tm
    return pl.pallas_call(
        _inproj_kernel,
        grid=(t // tm, n // tn),
        in_specs=[
            pl.BlockSpec((tm, d), lambda i, j: (i, 0)),
            pl.BlockSpec((1, d), lambda i, j: (0, 0)),
            pl.BlockSpec((d, tn), lambda i, j: (0, j)),
            pl.BlockSpec((d, HEADS_PAD), lambda i, j: (0, 0)),
            pl.BlockSpec((tm, 3 * LANES), lambda i, j: (i % nt, 0)),
        ],
        out_specs=[
            pl.BlockSpec((tm, tn), lambda i, j: (i, j)),
            pl.BlockSpec((tm, HEADS_PAD), lambda i, j: (i, 0)),
        ],
        out_shape=[
            jax.ShapeDtypeStruct((t, n), F32),
            jax.ShapeDtypeStruct((t, HEADS_PAD), F32),
        ],
        scratch_shapes=[pltpu.VMEM((tm, d), BF16)],
        compiler_params=_cparams(2),
        name="inproj",
    )(x, g, w, wdt, tab)


def _rope_tables(pos):
    half = ROPE_DIM // 2
    inv = ROPE_THETA ** (-jnp.arange(half, dtype=F32) * (2.0 / ROPE_DIM))
    ang = pos.astype(F32)[:, None] * inv[None, :]
    cs = jnp.concatenate([jnp.cos(ang), jnp.sin(ang)], axis=1)
    expand = np.zeros((2 * half, 3 * LANES), np.float32)
    base = np.zeros((1, 3 * LANES), np.float32)
    for lane in range(LANES):
        m = lane % HEAD_DIM
        if m >= ROPE_DIM:
            base[0, lane] = 1.0
            continue
        expand[m % half, lane] = 1.0
        if m < half:
            expand[half + m, LANES + lane] = -1.0
        else:
            expand[m, 2 * LANES + lane] = 1.0
    cs3 = jnp.concatenate(_split3(cs), axis=1)
    expand3 = jnp.asarray(np.concatenate([expand] * 3, axis=0), BF16)
    return jnp.dot(cs3, expand3, preferred_element_type=F32) + base


def _rope(x, tab):
    half = ROPE_DIM // 2
    c, sa, sb = tab[:, :LANES], tab[:, LANES:2 * LANES], tab[:, 2 * LANES:]
    return x * c + pltpu.roll(x, LANES - half, 1) * sa + pltpu.roll(x, half, 1) * sb


def _softmax_fold(s, band, prev_bias, sink2):
    sp = s[:, :WINDOW] if prev_bias is None else s[:, :WINDOW] + prev_bias
    t = jnp.where(band, sp, s[:, WINDOW:])
    m = jnp.maximum(jnp.max(t, axis=-1, keepdims=True), sink2)
    e = jnp.exp2(t - m)
    den = jnp.sum(e, axis=-1, keepdims=True) + jnp.exp2(sink2 - m)
    p = jnp.concatenate([jnp.where(band, e, 0.0), jnp.where(band, 0.0, e)], axis=1)
    return p.astype(BF16), den


def _attn_prompt_kernel(sink_ref, q_ref, kc_ref, kp_ref, vc_ref, vp_ref,
                        o_ref, nk_ref, nv_ref, *, nq):
    i = pl.program_id(1)
    nsteps = pl.num_programs(1)
    w = WINDOW
    kcr = kc_ref[...]
    kpr = kp_ref[...]
    vc = vc_ref[...]

    @pl.when(i == nsteps - 1)
    def _():
        nk_ref[...] = kcr[(nq - 1) * w:]
        nv_ref[...] = vc[(nq - 1) * w:]

    kall = jnp.concatenate([kpr, kcr], axis=0)
    vall = jnp.concatenate([vp_ref[...], vc], axis=0)
    lo = lax.broadcasted_iota(jnp.int32, ((nq + 1) * w, LANES), 1) < HEAD_DIM
    lo_q = lax.broadcasted_iota(jnp.int32, (w, LANES), 1) < HEAD_DIM
    band = (lax.broadcasted_iota(jnp.int32, (w, w), 1) > lax.broadcasted_iota(jnp.int32, (w, w), 0))
    first_bias = jnp.where(i == 0, -jnp.inf, 0.0)

    for g in range(N_KV_HEADS):
        col, odd = g // 2, g % 2
        kg = kall[:, col * LANES:(col + 1) * LANES]
        vg = vall[:, col * LANES:(col + 1) * LANES]
        kg_sw = pltpu.roll(kg, HEAD_DIM, 1)
        vg_sw = pltpu.roll(vg, HEAD_DIM, 1)
        k_lo = jnp.where(lo, kg_sw if odd else kg, 0.0).astype(BF16)
        k_hi = jnp.where(lo, 0.0, kg if odd else kg_sw).astype(BF16)
        v_lo = jnp.where(lo, vg_sw if odd else vg, 0.0).astype(BF16)
        v_hi = jnp.where(lo, 0.0, vg if odd else vg_sw).astype(BF16)
        sinks2 = [sink_ref[4 * g + r] * LOG2E for r in range(4)]
        for s in range(nq):
            rows = slice(s * w, (s + 1) * w)
            keys = slice(s * w, (s + 2) * w)
            pb = first_bias if s == 0 else None
            qst = jnp.concatenate([q_ref[rows, (2 * g) * LANES:(2 * g + 1) * LANES],
                                   q_ref[rows, (2 * g + 1) * LANES:(2 * g + 2) * LANES]],
                                  axis=0).astype(BF16)
            s_lo = _dot_nt(qst, k_lo[keys])
            s_hi = _dot_nt(qst, k_hi[keys])
            e0, d0 = _softmax_fold(s_lo[:w], band, pb, sinks2[0])
            e1, d1 = _softmax_fold(s_hi[:w], band, pb, sinks2[1])
            e2, d2 = _softmax_fold(s_lo[w:], band, pb, sinks2[2])
            e3, d3 = _softmax_fold(s_hi[w:], band, pb, sinks2[3])
            p = jnp.concatenate([jnp.concatenate([e0, e1], axis=1),
                                 jnp.concatenate([e2, e3], axis=1)], axis=0)
            vcat = jnp.concatenate([v_lo[keys], v_hi[keys]], axis=0)
            o = _dot(p, vcat)
            oa = o[:w] / jnp.where(lo_q, d0, d1)
            ob = o[w:] / jnp.where(lo_q, d2, d3)
            o_ref[rows, (2 * g) * LANES:(2 * g + 1) * LANES] = oa.astype(o_ref.dtype)
            o_ref[rows, (2 * g + 1) * LANES:(2 * g + 2) * LANES] = ob.astype(o_ref.dtype)


def _attn_prompt(proj, sinks, nq):
    b, l, _ = proj.shape
    w = WINDOW
    nsteps = l // (nq * w)
    kcol, vcol = COL_K // KV_DIM, COL_V // KV_DIM
    prev = lambda bi, i: jnp.maximum(nq * i - 1, 0)
    return pl.pallas_call(
        functools.partial(_attn_prompt_kernel, nq=nq),
        grid=(b, nsteps),
        in_specs=[
            pl.BlockSpec(memory_space=pltpu.SMEM),
            pl.BlockSpec((None, nq * w, Q_DIM), lambda bi, i: (bi, i, COL_Q // Q_DIM)),
            pl.BlockSpec((None, nq * w, KV_DIM), lambda bi, i: (bi, i, kcol)),
            pl.BlockSpec((None, w, KV_DIM), lambda bi, i: (bi, prev(bi, i), kcol)),
            pl.BlockSpec((None, nq * w, KV_DIM), lambda bi, i: (bi, i, vcol)),
            pl.BlockSpec((None, w, KV_DIM), lambda bi, i: (bi, prev(bi, i), vcol)),
        ],
        out_specs=[
            pl.BlockSpec((None, nq * w, Q_DIM), lambda bi, i: (bi, i, 0)),
            pl.BlockSpec((None, w, KV_DIM), lambda bi, i: (bi, 0, 0)),
            pl.BlockSpec((None, w, KV_DIM), lambda bi, i: (bi, 0, 0)),
        ],
        out_shape=[
            jax.ShapeDtypeStruct((b, l, Q_DIM), BF16),
            jax.ShapeDtypeStruct((b, w, KV_DIM), F32),
            jax.ShapeDtypeStruct((b, w, KV_DIM), F32),
        ],
        compiler_params=_cparams(2),
        name="attn_prompt",
    )(sinks, proj, proj, proj, proj, proj)


def _softplus(v):
    return jnp.maximum(v, 0.0) + jnp.log1p(jnp.exp(-jnp.abs(v)))


def _head_expand(vals, ex2_ref):
    hi = vals.astype(BF16)
    mid = (vals - hi.astype(F32)).astype(BF16)
    return _dot(jnp.concatenate([hi, mid], axis=1), ex2_ref[...])


def _ssd_prompt_kernel(*refs, nsub):
    nz, nx = SSM_INNER // PIECE, CONV_DIM // PIECE
    z_refs, x_refs = refs[:nz], refs[nz:nz + nx]
    (dt_ref, cw_ref, cb_ref, dtb_ref, alog_ref, dsk_ref, gn_ref, ex2_ref,
     y_ref, nssm_ref, nconv_ref, state_ref, carry_ref) = refs[nz + nx:]
    i = pl.program_id(1)
    nc = pl.num_programs(1)
    q = SSD_CHUNK

    @pl.when(i == 0)
    def _():
        state_ref[...] = jnp.zeros_like(state_ref)
        carry_ref[...] = jnp.zeros_like(carry_ref)

    dtp = dt_ref[:q, :]
    for s in range(1, nsub):
        dtp = dtp + pltpu.roll(dt_ref[s * q:(s + 1) * q, :], s * SSM_HEADS, 1)
    dt = _softplus(dtp + dtb_ref[...])
    da = dt * (-jnp.exp(alog_ref[...]))
    tri = (lax.broadcasted_iota(jnp.int32, (q, q), 0) >= lax.broadcasted_iota(jnp.int32, (q, q), 1))
    cs = _dot_exact_lhs01(tri.astype(BF16), da)
    cs_t = cs.T
    sc = dict(
        tri=tri, cs=cs, cs_t=cs_t, dt_t=dt.T,
        cdec_t=jnp.exp(cs_t[:, q - 1:q]),
        ecs_x=_head_expand(jnp.exp(cs), ex2_ref),
        wgt_x=_head_expand(dt * jnp.exp(cs[q - 1:q, :] - cs), ex2_ref))

    rows_of = lambda rs, sl: jnp.concatenate([r[sl, :] for r in rs], axis=1)
    for s in range(nsub):
        _ssd_chunk(s, sc, rows_of, z_refs, x_refs, cw_ref, cb_ref, dsk_ref,
                   gn_ref, y_ref, state_ref, carry_ref)
    tail = rows_of(x_refs, slice(nsub * q - 8, nsub * q))
    carry_ref[...] = tail

    @pl.when(i == nc - 1)
    def _():
        nconv_ref[...] = tail
        nssm_ref[...] = state_ref[...]


def _ssd_chunk(s, sc, rows_of, z_refs, x_refs, cw_ref, cb_ref, dsk_ref,
               gn_ref, y_ref, state_ref, carry_ref):
    q = SSD_CHUNK
    tri, cs, cs_t, dt_t, cdec_t = sc["tri"], sc["cs"], sc["cs_t"], sc["dt_t"], sc["cdec_t"]
    hoff = s * SSM_HEADS
    xoff = s * SSM_INNER
    trows = slice(s * q, (s + 1) * q)
    x = rows_of(x_refs, trows)
    z = rows_of(z_refs, trows)
    prev = carry_ref[...] if s == 0 else rows_of(x_refs, slice(s * q - 8, s * q))
    row8 = lax.broadcasted_iota(jnp.int32, (8, CONV_DIM), 0)

    def shifted(k):
        r = pltpu.roll(x, k, 0)
        head = jnp.where(row8 < k, pltpu.roll(prev, k, 0), r[:8])
        return jnp.concatenate([head, r[8:]], axis=0)

    conv = x * cw_ref[CONV_WIDTH - 1:CONV_WIDTH, :] + cb_ref[...]
    for k in range(1, CONV_WIDTH):
        conv = conv + shifted(k) * cw_ref[CONV_WIDTH - 1 - k:CONV_WIDTH - k, :]

    xc = _silu(conv)
    xs = xc[:, :SSM_INNER]
    bm = xc[:, SSM_INNER:SSM_INNER + BC_DIM]
    cm = xc[:, SSM_INNER + BC_DIM:]

    lo = lax.broadcasted_iota(jnp.int32, (q, LANES), 1) < SSM_HEAD_DIM

    hpg = SSM_HEADS // SSM_GROUPS
    gw = hpg * SSM_HEAD_DIM
    for g in range(SSM_GROUPS):
        bg = bm[:, g * SSM_STATE:(g + 1) * SSM_STATE].astype(BF16)
        cg = cm[:, g * SSM_STATE:(g + 1) * SSM_STATE].astype(BF16)
        cb = _dot_nt(cg, bg)
        st = state_ref[g * gw:(g + 1) * gw, :]
        yoff = _dot_nt(cg, st.astype(BF16))
        ys = []
        for pr in range(2):
            pair = 2 * g + pr
            ms = []
            for h in (hoff + 2 * pair, hoff + 2 * pair + 1):
                diff = cs[:, h:h + 1] - cs_t[h:h + 1, :]
                lm = jnp.exp(jnp.where(tri, diff, -jnp.inf))
                ms.append((cb * lm * dt_t[h:h + 1, :]).astype(BF16))
            xp = xs[:, pair * LANES:(pair + 1) * LANES]
            x2 = jnp.concatenate([jnp.where(lo, xp, 0.0), jnp.where(lo, 0.0, xp)],
                                 axis=0).astype(BF16)
            yd = _dot(jnp.concatenate(ms, axis=1), x2)
            yo = yoff[:, pr * LANES:(pr + 1) * LANES] * sc["ecs_x"][
                :, xoff + pair * LANES:xoff + (pair + 1) * LANES]
            ys.append(yd + yo + xp * dsk_ref[:, pair * LANES:(pair + 1) * LANES])
        yg = jnp.concatenate(ys, axis=1)
        wx = xs[:, g * gw:(g + 1) * gw] * sc["wgt_x"][:, xoff + g * gw:xoff + (g + 1) * gw]
        s_new = _dot_tn(wx.astype(BF16), bg)
        for r in range(hpg):
            h = hoff + hpg * g + r
            rows = slice(g * gw + r * SSM_HEAD_DIM, g * gw + (r + 1) * SSM_HEAD_DIM)
            state_ref[rows, :] = (st[r * SSM_HEAD_DIM:(r + 1) * SSM_HEAD_DIM, :] * cdec_t[h:h + 1, :]
                                  + s_new[r * SSM_HEAD_DIM:(r + 1) * SSM_HEAD_DIM, :])
        hg = yg * _silu(z[:, g * gw:(g + 1) * gw])
        y_ref[trows, g * gw:(g + 1) * gw] = (_rms(hg) * gn_ref[:, g * gw:(g + 1) * gw]
                                            ).astype(y_ref.dtype)


def _ssd_prompt(proj, dtraw, conv_w, conv_b, dt_bias_t, a_log_t, d_skip_x, norm_g, nsub):
    assert nsub * SSM_HEADS <= LANES
    src = np.arange(nsub * SSM_INNER) // SSM_HEAD_DIM
    ex = (np.arange(LANES)[:, None] == src[None, :]).astype(np.float32)
    ex2 = jnp.asarray(np.concatenate([ex, ex], axis=0), BF16)
    b, l, _ = proj.shape
    q = SSD_CHUNK * nsub
    nc = l // q
    const = lambda bi, i: (0, 0)
    piece = lambda c: pl.BlockSpec((None, q, PIECE), lambda bi, i: (bi, i, c))
    n_pieces = (SSM_INNER + CONV_DIM) // PIECE
    return pl.pallas_call(
        functools.partial(_ssd_prompt_kernel, nsub=nsub),
        grid=(b, nc),
        in_specs=[piece(COL_Z // PIECE + c) for c in range(SSM_INNER // PIECE)] + [
            piece(COL_XBC // PIECE + c) for c in range(CONV_DIM // PIECE)] + [
            pl.BlockSpec((None, q, HEADS_PAD), lambda bi, i: (bi, i, 0)),
            pl.BlockSpec((CONV_WIDTH, CONV_DIM), const),
            pl.BlockSpec((1, CONV_DIM), const),
            pl.BlockSpec((1, HEADS_PAD), const),
            pl.BlockSpec((1, HEADS_PAD), const),
            pl.BlockSpec((1, SSM_INNER), const),
            pl.BlockSpec((1, SSM_INNER), const),
            pl.BlockSpec((2 * LANES, nsub * SSM_INNER), const),
        ],
        out_specs=[
            pl.BlockSpec((None, q, SSM_INNER), lambda bi, i: (bi, i, 0)),
            pl.BlockSpec((None, SSM_INNER, SSM_STATE), lambda bi, i: (bi, 0, 0)),
            pl.BlockSpec((None, 8, CONV_DIM), lambda bi, i: (bi, 0, 0)),
        ],
        out_shape=[
            jax.ShapeDtypeStruct((b, l, SSM_INNER), BF16),
            jax.ShapeDtypeStruct((b, SSM_INNER, SSM_STATE), F32),
            jax.ShapeDtypeStruct((b, 8, CONV_DIM), F32),
        ],
        scratch_shapes=[pltpu.VMEM((SSM_INNER, SSM_STATE), F32),
                        pltpu.VMEM((8, CONV_DIM), F32)],
        compiler_params=_cparams(2),
        name="ssd_prompt",
    )(*([proj] * n_pieces), dtraw, conv_w, conv_b, dt_bias_t, a_log_t, d_skip_x, norm_g, ex2)


def _outproj_kernel(x_ref, a_ref, y_ref, wa_ref, wy_ref, o_ref):
    o_ref[...] = (x_ref[...] + _dot(a_ref[...].astype(BF16), wa_ref[...])
                  + _dot(y_ref[...].astype(BF16), wy_ref[...]))


def _outproj(x, attn, y, w_out, tm):
    t, d = x.shape
    half = w_out.shape[0] // 2
    return pl.pallas_call(
        _outproj_kernel,
        grid=(t // tm,),
        in_specs=[
            pl.BlockSpec((tm, d), lambda i: (i, 0)),
            pl.BlockSpec((tm, half), lambda i: (i, 0)),
            pl.BlockSpec((tm, half), lambda i: (i, 0)),
            pl.BlockSpec((half, d), lambda i: (0, 0)),
            pl.BlockSpec((half, d), lambda i: (1, 0)),
        ],
        out_specs=pl.BlockSpec((tm, d), lambda i: (i, 0)),
        out_shape=jax.ShapeDtypeStruct((t, d), F32),
        compiler_params=_cparams(1),
        name="outproj",
    )(x, attn, y, w_out, w_out)


def _ffn_kernel(h_ref, g_ref, wg_ref, wu_ref, wd_ref, o_ref, f_ref):
    d = o_ref.shape[1]
    tn = min(d, 512)

    def add_delta(f, base_ref):
        hid = (_silu(_dot(f, wg_ref[...])) * _dot(f, wu_ref[...])).astype(BF16)
        for c in range(d // tn):
            cols = slice(c * tn, (c + 1) * tn)
            o_ref[:, cols] = base_ref[:, cols] + _dot(hid, wd_ref[:, cols])

    @pl.when(pl.program_id(1) == 0)
    def _():
        f = (_rms(h_ref[...]) * g_ref[...]).astype(BF16)
        f_ref[...] = f
        add_delta(f, h_ref)

    @pl.when(pl.program_id(1) > 0)
    def _():
        add_delta(f_ref[...], o_ref)


def _ffn(h, g, wg, wu, wd, tm, th):
    t, d = h.shape
    hidden = wd.shape[0]
    return pl.pallas_call(
        _ffn_kernel,
        grid=(t // tm, hidden // th),
        in_specs=[
            pl.BlockSpec((tm, d), lambda i, j: (i, 0)),
            pl.BlockSpec((1, d), lambda i, j: (0, 0)),
            pl.BlockSpec((d, th), lambda i, j: (0, j)),
            pl.BlockSpec((d, th), lambda i, j: (0, j)),
            pl.BlockSpec((th, d), lambda i, j: (j, 0)),
        ],
        out_specs=pl.BlockSpec((tm, d), lambda i, j: (i, 0)),
        out_shape=jax.ShapeDtypeStruct((t, d), F32),
        scratch_shapes=[pltpu.VMEM((tm, d), BF16)],
        compiler_params=_cparams(2),
        name="ffn",
    )(h, g, wg, wu, wd)


def _ple_kernel(h_ref, p_ref, gp_ref, wp_ref, wg_ref, gf_ref, o_ref, *, tn, row_parts):
    tm, d = h_ref.shape
    rp = tm // row_parts
    for r in range(row_parts):
        rows = slice(r * rp, (r + 1) * rp)
        n = (_rms(h_ref[rows, :]) * gp_ref[...]).astype(BF16)
        pb = p_ref[rows, :].astype(BF16)
        ss = jnp.zeros((rp, 1), F32)
        for c in range(d // tn):
            cols = slice(c * tn, (c + 1) * tn)
            gate = _dot(n, wg_ref[:, cols])
            h3 = h_ref[rows, cols] + _dot(pb, wp_ref[:, cols]) * _sigmoid(gate)
            o_ref[rows, cols] = h3
            ss = ss + jnp.sum(h3 * h3, axis=-1, keepdims=True)
        inv = lax.rsqrt(ss * (1.0 / d) + RMS_EPS)
        o_ref[rows, :] = o_ref[rows, :] * inv * gf_ref[...]


def _ple_final(h, p, g_ple, w_ple, w_gate, g_final, tm):
    t, d = h.shape
    pd = p.shape[1]
    const = lambda i: (0, 0)
    resident = dict(pipeline_mode=pl.Buffered(1))
    return pl.pallas_call(
        functools.partial(_ple_kernel, tn=512, row_parts=max(tm // 256, 1)),
        grid=(t // tm,),
        in_specs=[
            pl.BlockSpec((tm, d), lambda i: (i, 0)),
            pl.BlockSpec((tm, pd), lambda i: (i, 0)),
            pl.BlockSpec((1, d), const),
            pl.BlockSpec((pd, d), const, **resident),
            pl.BlockSpec((d, d), const, **resident),
            pl.BlockSpec((1, d), const),
        ],
        out_specs=pl.BlockSpec((tm, d), lambda i: (i, 0)),
        out_shape=jax.ShapeDtypeStruct((t, d), F32),
        compiler_params=_cparams(1),
        name="ple_final",
    )(h, p, g_ple, w_ple, w_gate, g_final)


def _sample_pre_kernel(x_ref, dt_ref, sc_ref, cw_ref, cb_ref, dtb_ref,
                       alog_ref, dsk_ref, exp_ref,
                       xdt_ref, b_ref, c_ref, dec_ref, yp_ref, dech_ref):
    conv = x_ref[...] * cw_ref[CONV_WIDTH - 1:CONV_WIDTH, :] + cb_ref[...]
    for k in range(CONV_WIDTH - 1):
        conv = conv + sc_ref[k] * cw_ref[k:k + 1, :]
    xc = _silu(conv)
    xs = xc[:, :SSM_INNER]
    bm = xc[:, SSM_INNER:SSM_INNER + BC_DIM]
    cm = xc[:, SSM_INNER + BC_DIM:]
    b_ref[...] = bm
    c_ref[...] = cm
    dt = _softplus(dt_ref[...] + dtb_ref[...])
    dec = jnp.exp(dt * (-jnp.exp(alog_ref[...])))
    ex = exp_ref[...]
    dtx = _dot_exact_rhs01(dt, ex)
    dec_ref[...] = _dot_exact_rhs01(dec, ex)
    dech_ref[...] = dec
    xdt = xs * dtx
    xdt_ref[...] = xdt
    gw = SSM_INNER // SSM_GROUPS
    cbs = []
    for g in range(SSM_GROUPS):
        prod = cm[:, g * SSM_STATE:(g + 1) * SSM_STATE] * bm[:, g * SSM_STATE:(g + 1) * SSM_STATE]
        cbs.append(jnp.broadcast_to(jnp.sum(prod, axis=-1, keepdims=True), (prod.shape[0], gw)))
    yp_ref[...] = xdt * jnp.concatenate(cbs, axis=1) + xs * dsk_ref[...]


def _sample_pre(xbc, dtraw, sconv_t, conv_w, conv_b, dt_bias, a_log, d_skip_x, expand):
    nb = xbc.shape[0]
    shapes = [(nb, SSM_INNER), (nb, BC_DIM), (nb, BC_DIM),
              (nb, SSM_INNER), (nb, SSM_INNER), (nb, HEADS_PAD)]
    return pl.pallas_call(
        _sample_pre_kernel,
        out_shape=[jax.ShapeDtypeStruct(s, F32) for s in shapes],
        compiler_params=pltpu.CompilerParams(vmem_limit_bytes=VMEM_LIMIT),
        name="sample_pre",
    )(xbc, dtraw, sconv_t, conv_w, conv_b, dt_bias, a_log, d_skip_x, expand)


def _attn_sample_kernel(sink_ref, q_ref, kn_ref, vn_ref, ck_ref, cv_ref, o_ref, nk_ref, nv_ref, *, bb):
    w = WINDOW
    row = lax.broadcasted_iota(jnp.int32, (w, KV_DIM), 0)
    hrow = lax.broadcasted_iota(jnp.int32, (N_HEADS, KV_DIM), 0) // (N_HEADS // N_KV_HEADS)
    hgrp = lax.broadcasted_iota(jnp.int32, (N_HEADS, KV_DIM), 1) // HEAD_DIM
    own = hrow == hgrp
    sink = sink_ref[...] * LOG2E
    for b in range(bb):
        kk = jnp.where(row == w - 1, kn_ref[b:b + 1, :], pltpu.roll(ck_ref[b], w - 1, 0))
        vv = jnp.where(row == w - 1, vn_ref[b:b + 1, :], pltpu.roll(cv_ref[b], w - 1, 0))
        nk_ref[b] = kk
        nv_ref[b] = vv
        qb = q_ref[b]
        qrow = jnp.where(own, jnp.concatenate([qb] * N_KV_HEADS, axis=1), 0.0)
        s = _dot_nt(qrow.astype(BF16), kk.astype(BF16))
        m = jnp.maximum(jnp.max(s, axis=-1, keepdims=True), sink)
        e = jnp.exp2(s - m)
        den = jnp.sum(e, axis=-1, keepdims=True) + jnp.exp2(sink - m)
        of = jnp.where(own, _dot(e.astype(BF16), vv.astype(BF16)), 0.0)
        o = of[:, :HEAD_DIM]
        for g in range(1, N_KV_HEADS):
            o = o + of[:, g * HEAD_DIM:(g + 1) * HEAD_DIM]
        o_ref[b] = o / den


def _attn_sample(sinks_col, q3, knew, vnew, cache_k, cache_v, bb):
    nb = q3.shape[0]
    w = WINDOW
    return pl.pallas_call(
        functools.partial(_attn_sample_kernel, bb=bb),
        grid=(nb // bb,),
        in_specs=[
            pl.BlockSpec((N_HEADS, 1), lambda i: (0, 0)),
            pl.BlockSpec((bb, N_HEADS, HEAD_DIM), lambda i: (i, 0, 0)),
            pl.BlockSpec((bb, KV_DIM), lambda i: (i, 0)),
            pl.BlockSpec((bb, KV_DIM), lambda i: (i, 0)),
            pl.BlockSpec((bb, w, KV_DIM), lambda i: (i, 0, 0)),
            pl.BlockSpec((bb, w, KV_DIM), lambda i: (i, 0, 0)),
        ],
        out_specs=[
            pl.BlockSpec((bb, N_HEADS, HEAD_DIM), lambda i: (i, 0, 0)),
            pl.BlockSpec((bb, w, KV_DIM), lambda i: (i, 0, 0)),
            pl.BlockSpec((bb, w, KV_DIM), lambda i: (i, 0, 0)),
        ],
        out_shape=[
            jax.ShapeDtypeStruct((nb, N_HEADS, HEAD_DIM), F32),
            jax.ShapeDtypeStruct((nb, w, KV_DIM), F32),
            jax.ShapeDtypeStruct((nb, w, KV_DIM), F32),
        ],
        compiler_params=_cparams(1),
        name="attn_sample",
    )(sinks_col, q3, knew, vnew, cache_k, cache_v)


def _ssd_sample_kernel(st_ref, xdt_ref, b_ref, c_ref, dec_ref, dech_ref, yp_ref, z_ref, gn_ref,
                       y_ref, ns_ref, *, bb):
    gw = SSM_INNER // SSM_GROUPS
    grow = lax.broadcasted_iota(jnp.int32, (8, SSM_INNER), 0)
    glane = lax.broadcasted_iota(jnp.int32, (8, SSM_INNER), 1) // gw
    own = grow == glane
    pad = jnp.zeros((8 - SSM_GROUPS, SSM_STATE), F32)
    yoffs = []
    for b in range(bb):
        st = st_ref[b]
        cmat = jnp.concatenate([c_ref[b], pad], axis=0).astype(BF16)
        bmat = jnp.concatenate([b_ref[b], pad], axis=0).astype(BF16)
        r = _dot_nt(cmat, st.astype(BF16))
        yoffs.append(jnp.sum(jnp.where(own, r, 0.0), axis=0, keepdims=True))
        amat = jnp.where(own, jnp.broadcast_to(xdt_ref[b:b + 1, :], (8, SSM_INNER)), 0.0)
        outer = _dot_tn(amat.astype(BF16), bmat)
        for h in range(SSM_HEADS):
            rows = slice(h * SSM_HEAD_DIM, (h + 1) * SSM_HEAD_DIM)
            ns_ref[b, rows, :] = st[rows, :] * dech_ref[b:b + 1, h:h + 1] + outer[rows, :]
    y = yp_ref[...] + jnp.concatenate(yoffs, axis=0) * dec_ref[...]
    hg = y * _silu(z_ref[...])
    outs = []
    for g in range(SSM_GROUPS):
        outs.append(_rms(hg[:, g * gw:(g + 1) * gw]))
    y_ref[...] = jnp.concatenate(outs, axis=1) * gn_ref[...]


def _ssd_sample(state, xdt, b3, c3, decx, dech, ypart, z, norm_g, bb):
    nb = state.shape[0]
    row = lambda i: (i, 0)
    return pl.pallas_call(
        functools.partial(_ssd_sample_kernel, bb=bb),
        grid=(nb // bb,),
        in_specs=[
            pl.BlockSpec((bb, SSM_INNER, SSM_STATE), lambda i: (i, 0, 0)),
            pl.BlockSpec((bb, SSM_INNER), row),
            pl.BlockSpec((bb, SSM_GROUPS, SSM_STATE), lambda i: (i, 0, 0)),
            pl.BlockSpec((bb, SSM_GROUPS, SSM_STATE), lambda i: (i, 0, 0)),
            pl.BlockSpec((bb, SSM_INNER), row),
            pl.BlockSpec((bb, HEADS_PAD), row),
            pl.BlockSpec((bb, SSM_INNER), row),
            pl.BlockSpec((bb, SSM_INNER), row),
            pl.BlockSpec((1, SSM_INNER), lambda i: (0, 0)),
        ],
        out_specs=[
            pl.BlockSpec((bb, SSM_INNER), row),
            pl.BlockSpec((bb, SSM_INNER, SSM_STATE), lambda i: (i, 0, 0)),
        ],
        out_shape=[
            jax.ShapeDtypeStruct((nb, SSM_INNER), F32),
            jax.ShapeDtypeStruct((nb, SSM_INNER, SSM_STATE), F32),
        ],
        compiler_params=_cparams(1),
        name="ssd_sample",
    )(state, xdt, b3, c3, decx, dech, ypart, z, norm_g)


def _row_tile(t, want):
    return want if t % want == 0 else t


def _dense_tail(x, attn, y, p, wts, tm):
    h1 = _outproj(x, attn, y, wts["w_out"], tm)
    h2 = _ffn(h1, wts["g_ffn"], wts["w_gate"], wts["w_up"], wts["w_down"],
              _row_tile(x.shape[0], FFN_TM), FFN_TH)
    return _ple_final(h2, p, wts["g_ple"], wts["w_ple"], wts["w_ple_gate"], wts["g_final"],
                      _row_tile(x.shape[0], 1024))


def kernel(x_prompt, x_sample, cache_k, cache_v, state_ssm, state_conv, p_prompt, p_sample, w_in, conv_w, conv_b, dt_bias, a_log, d_skip, ssm_norm_g, attn_sinks, w_out, g_mix, g_ffn, w_ffn_gate, w_ffn_up, w_ffn_down, g_ple, w_ple, w_ple_gate, g_final):
    depth = w_in.shape[0]
    assert depth == 1, "single-layer step only"
    bsz, seq, d = x_prompt.shape
    nb, dseq, _ = x_sample.shape
    assert dseq == 1 and seq % SSD_CHUNK == 0 and seq % WINDOW == 0

    w = w_in[0]
    w_main = w.astype(BF16)
    w_dt = jnp.pad(w[:, PROJ_DIM:], ((0, 0), (0, HEADS_PAD - SSM_HEADS))).astype(BF16)
    padh = lambda v: jnp.pad(v, (0, HEADS_PAD - SSM_HEADS)).reshape(1, HEADS_PAD)
    tileh = lambda v: jnp.tile(v, HEADS_PAD // SSM_HEADS).reshape(1, HEADS_PAD)
    dtb, alog = padh(dt_bias[0]), padh(a_log[0])
    dsk_x = jnp.repeat(d_skip[0], SSM_HEAD_DIM).reshape(1, SSM_INNER)
    gn = ssm_norm_g[0].reshape(1, SSM_INNER)
    cw, cb = conv_w[0], conv_b[0].reshape(1, CONV_DIM)
    wts = dict(
        w_out=w_out[0].astype(BF16), g_ffn=g_ffn[0].reshape(1, d),
        w_gate=w_ffn_gate[0].astype(BF16), w_up=w_ffn_up[0].astype(BF16),
        w_down=w_ffn_down[0].astype(BF16), g_ple=g_ple[0].reshape(1, d),
        w_ple=w_ple[0].astype(BF16), w_ple_gate=w_ple_gate[0].astype(BF16),
        g_final=g_final.reshape(1, d))
    gmix = g_mix[0].reshape(1, d)
    sinks = attn_sinks[0]

    tp = bsz * seq
    xp = x_prompt.reshape(tp, d)
    tm_in = _row_tile(seq, 1024)
    proj, dtraw = _inproj(xp, gmix, w_main, w_dt, _rope_tables(jnp.arange(seq)), tm_in)
    proj3 = proj.reshape(bsz, seq, PROJ_DIM)
    attn, nk_p, nv_p = _attn_prompt(proj3, sinks, ATTN_BLOCKS)
    yp, nssm_p, nconv_p = _ssd_prompt(proj3, dtraw.reshape(bsz, seq, HEADS_PAD), cw, cb,
                                      tileh(dt_bias[0]), tileh(a_log[0]), dsk_x, gn, SSD_CHUNKS)
    tm = _row_tile(tp, 512)
    y_prompt = _dense_tail(xp, attn.reshape(tp, Q_DIM), yp.reshape(tp, SSM_INNER),
                           p_prompt[0].reshape(tp, -1), wts, tm).reshape(bsz, seq, d)

    xs = x_sample.reshape(nb, d)
    tab_s = _rope_tables(jnp.full((nb,), PAST_LEN, jnp.int32))
    proj_s, dtraw_s = _inproj(xs, gmix, w_main, w_dt, tab_s, nb)
    window = cache_k.shape[2]
    sconv_t = jnp.transpose(state_conv[0], (1, 0, 2))
    expand = (jnp.arange(HEADS_PAD)[:, None] == (jnp.arange(SSM_INNER) // SSM_HEAD_DIM)[None, :]
              ).astype(BF16)
    q_s = proj_s[:, COL_Q:COL_Q + Q_DIM]
    k_s = proj_s[:, COL_K:COL_K + KV_DIM]
    v_s = proj_s[:, COL_V:COL_V + KV_DIM]
    z_s = proj_s[:, COL_Z:COL_Z + SSM_INNER]
    xbc_s = proj_s[:, COL_XBC:COL_XBC + CONV_DIM]
    xdt, bm, cm, decx, ypart, dech = _sample_pre(
        xbc_s, dtraw_s, sconv_t, cw, cb, dtb, alog, dsk_x, expand)
    bb = 8 if nb % 8 == 0 else nb
    attn_s, nk_s, nv_s = _attn_sample(
        sinks.reshape(N_HEADS, 1), q_s.reshape(nb, N_HEADS, HEAD_DIM), k_s, v_s,
        cache_k[0].reshape(nb, window, KV_DIM), cache_v[0].reshape(nb, window, KV_DIM), bb)
    ys, nssm_s = _ssd_sample(
        state_ssm[0].reshape(nb, SSM_INNER, SSM_STATE), xdt,
        bm.reshape(nb, SSM_GROUPS, SSM_STATE), cm.reshape(nb, SSM_GROUPS, SSM_STATE),
        decx, dech, ypart, z_s, gn, bb)
    y_sample = _dense_tail(xs, attn_s.reshape(nb, Q_DIM), ys, p_sample[0].reshape(nb, -1), wts,
                           nb).reshape(nb, 1, d)
    nconv_s = jnp.concatenate([state_conv[0][:, 1:], xbc_s[:, None, :]], axis=1)

    kv5 = lambda t: t.reshape(1, t.shape[0], window, N_KV_HEADS, HEAD_DIM)
    return (y_prompt, y_sample,
            kv5(nk_p), kv5(nv_p),
            nssm_p.reshape(1, bsz, SSM_HEADS, SSM_HEAD_DIM, SSM_STATE),
            nconv_p[None, :, 8 - (CONV_WIDTH - 1):, :],
            kv5(nk_s), kv5(nv_s),
            nssm_s.reshape(1, nb, SSM_HEADS, SSM_HEAD_DIM, SSM_STATE),
            nconv_s[None])
```

```python
import functools

import numpy as np
import jax
import jax.numpy as jnp
from jax import lax
from jax.experimental import pallas as pl
from jax.experimental.pallas import tpu as pltpu

F32 = jnp.float32
BF16 = jnp.bfloat16

HEAD_DIM = 64
N_HEADS = 16
N_KV_HEADS = 4
WINDOW = 128
ROPE_DIM = 16
ROPE_THETA = 500000.0
SSM_HEADS = 16
SSM_HEAD_DIM = 64
SSM_GROUPS = 4
SSM_STATE = 128
CONV_WIDTH = 4
SSD_CHUNK = 128
RMS_EPS = 1e-6
PAST_LEN = 16384
LOG2E = 1.4426950408889634

Q_DIM = N_HEADS * HEAD_DIM
KV_DIM = N_KV_HEADS * HEAD_DIM
SSM_INNER = SSM_HEADS * SSM_HEAD_DIM
BC_DIM = SSM_GROUPS * SSM_STATE
CONV_DIM = SSM_INNER + 2 * BC_DIM
LANES = 128
HEADS_PAD = LANES

COL_Q = 0
COL_K = Q_DIM
COL_V = COL_K + KV_DIM
COL_Z = COL_V + KV_DIM
COL_XBC = COL_Z + SSM_INNER
PROJ_DIM = COL_XBC + CONV_DIM
PIECE = 512

INPROJ_TN = Q_DIM + 2 * KV_DIM
Q_SCALE = HEAD_DIM ** -0.5 * LOG2E
FFN_TM, FFN_TH = 1024, 512
ATTN_BLOCKS = 8
SSD_CHUNKS = 8
VMEM_LIMIT = 56 * 1024 * 1024


def _cparams(n_axes):
    return pltpu.CompilerParams(
        dimension_semantics=("arbitrary",) * n_axes, vmem_limit_bytes=VMEM_LIMIT)


def _rms(x):
    return x * lax.rsqrt(jnp.mean(x * x, axis=-1, keepdims=True) + RMS_EPS)


def _sigmoid(x):
    return 1.0 / (1.0 + jnp.exp(-x))


def _silu(x):
    return x * _sigmoid(x)


def _dot(a, b):
    return jnp.dot(a, b, preferred_element_type=F32)


def _dot_nt(a, b):
    return lax.dot_general(a, b, (((1,), (1,)), ((), ())), preferred_element_type=F32)


def _dot_tn(a, b):
    return lax.dot_general(a, b, (((0,), (0,)), ((), ())), preferred_element_type=F32)


def _split3(x):
    hi = x.astype(BF16)
    r = x - hi.astype(F32)
    mid = r.astype(BF16)
    lo = (r - mid.astype(F32)).astype(BF16)
    return hi, mid, lo


def _dot_exact_lhs01(m01, x):
    hi, mid, lo = _split3(x)
    return _dot(m01, hi) + _dot(m01, mid) + _dot(m01, lo)


def _dot_exact_rhs01(x, m01):
    hi, mid, lo = _split3(x)
    return _dot(hi, m01) + _dot(mid, m01) + _dot(lo, m01)


def _inproj_kernel(x_ref, g_ref, w_ref, wdt_ref, tab_ref, o_ref, odt_ref, u_ref):
    j = pl.program_id(1)

    @pl.when(j == 0)
    def _():
        u = (_rms(x_ref[...]) * g_ref[...]).astype(BF16)
        u_ref[...] = u
        odt_ref[...] = _dot(u, wdt_ref[...])
        res = _dot(u, w_ref[...])
        tab = tab_ref[...]
        for c in range((Q_DIM + KV_DIM) // LANES):
            cols = slice(c * LANES, (c + 1) * LANES)
            r = _rope(res[:, cols], tab)
            o_ref[:, cols] = r * Q_SCALE if c < Q_DIM // LANES else r
        o_ref[:, Q_DIM + KV_DIM:] = res[:, Q_DIM + KV_DIM:]

    @pl.when(j > 0)
    def _():
        o_ref[...] = _dot(u_ref[...], w_ref[...])


def _inproj(x, g, w, wdt, tab, tm):
    t, d = x.shape
    n = PROJ_DIM
    tn = INPROJ_TN
    nt = tab.shape[0] // tm
    return pl.pallas_call(
        _inproj_kernel,
        grid=(t // tm, n // tn),
        in_specs=[
            pl.BlockSpec((tm, d), lambda i, j: (i, 0)),
            pl.BlockSpec((1, d), lambda i, j: (0, 0)),
            pl.BlockSpec((d, tn), lambda i, j: (0, j)),
            pl.BlockSpec((d, HEADS_PAD), lambda i, j: (0, 0)),
            pl.BlockSpec((tm, 3 * LANES), lambda i, j: (i % nt, 0)),
        ],
        out_specs=[
            pl.BlockSpec((tm, tn), lambda i, j: (i, j)),
            pl.BlockSpec((tm, HEADS_PAD), lambda i, j: (i, 0)),
        ],
        out_shape=[
            jax.ShapeDtypeStruct((t, n), F32),
            jax.ShapeDtypeStruct((t, HEADS_PAD), F32),
        ],
        scratch_shapes=[pltpu.VMEM((tm, d), BF16)],
        compiler_params=_cparams(2),
        name="inproj",
    )(x, g, w, wdt, tab)


def _rope_tables(pos):
    half = ROPE_DIM // 2
    inv = ROPE_THETA ** (-jnp.arange(half, dtype=F32) * (2.0 / ROPE_DIM))
    ang = pos.astype(F32)[:, None] * inv[None, :]
    cs = jnp.concatenate([jnp.cos(ang), jnp.sin(ang)], axis=1)
    expand = np.zeros((2 * half, 3 * LANES), np.float32)
    base = np.zeros((1, 3 * LANES), np.float32)
    for lane in range(LANES):
        m = lane % HEAD_DIM
        if m >= ROPE_DIM:
            base[0, lane] = 1.0
            continue
        expand[m % half, lane] = 1.0
        if m < half:
            expand[half + m, LANES + lane] = -1.0
        else:
            expand[m, 2 * LANES + lane] = 1.0
    cs3 = jnp.concatenate(_split3(cs), axis=1)
    expand3 = jnp.asarray(np.concatenate([expand] * 3, axis=0), BF16)
    return jnp.dot(cs3, expand3, preferred_element_type=F32) + base


def _rope(x, tab):
    half = ROPE_DIM // 2
    c, sa, sb = tab[:, :LANES], tab[:, LANES:2 * LANES], tab[:, 2 * LANES:]
    return x * c + pltpu.roll(x, LANES - half, 1) * sa + pltpu.roll(x, half, 1) * sb


def _softmax_fold(s, band, prev_bias, sink2):
    sp = s[:, :WINDOW] if prev_bias is None else s[:, :WINDOW] + prev_bias
    t = jnp.where(band, sp, s[:, WINDOW:])
    m = jnp.maximum(jnp.max(t, axis=-1, keepdims=True), sink2)
    e = jnp.exp2(t - m)
    den = jnp.sum(e, axis=-1, keepdims=True) + jnp.exp2(sink2 - m)
    p = jnp.concatenate([jnp.where(band, e, 0.0), jnp.where(band, 0.0, e)], axis=1)
    return p.astype(BF16), den


def _attn_prompt_kernel(sink_ref, q_ref, kc_ref, kp_ref, vc_ref, vp_ref,
                        o_ref, nk_ref, nv_ref, *, nq):
    i = pl.program_id(1)
    nsteps = pl.num_programs(1)
    w = WINDOW
    kcr = kc_ref[...]
    kpr = kp_ref[...]
    vc = vc_ref[...]

    @pl.when(i == nsteps - 1)
    def _():
        nk_ref[...] = kcr[(nq - 1) * w:]
        nv_ref[...] = vc[(nq - 1) * w:]

    kall = jnp.concatenate([kpr, kcr], axis=0)
    vall = jnp.concatenate([vp_ref[...], vc], axis=0)
    lo = lax.broadcasted_iota(jnp.int32, ((nq + 1) * w, LANES), 1) < HEAD_DIM
    lo_q = lax.broadcasted_iota(jnp.int32, (w, LANES), 1) < HEAD_DIM
    band = (lax.broadcasted_iota(jnp.int32, (w, w), 1) > lax.broadcasted_iota(jnp.int32, (w, w), 0))
    first_bias = jnp.where(i == 0, -jnp.inf, 0.0)

    for g in range(N_KV_HEADS):
        col, odd = g // 2, g % 2
        kg = kall[:, col * LANES:(col + 1) * LANES]
        vg = vall[:, col * LANES:(col + 1) * LANES]
        kg_sw = pltpu.roll(kg, HEAD_DIM, 1)
        vg_sw = pltpu.roll(vg, HEAD_DIM, 1)
        k_lo = jnp.where(lo, kg_sw if odd else kg, 0.0).astype(BF16)
        k_hi = jnp.where(lo, 0.0, kg if odd else kg_sw).astype(BF16)
        v_lo = jnp.where(lo, vg_sw if odd else vg, 0.0).astype(BF16)
        v_hi = jnp.where(lo, 0.0, vg if odd else vg_sw).astype(BF16)
        sinks2 = [sink_ref[4 * g + r] * LOG2E for r in range(4)]
        def scores(s, g=g, k_lo=k_lo, k_hi=k_hi):
            rows = slice(s * w, (s + 1) * w)
            keys = slice(s * w, (s + 2) * w)
            qst = jnp.concatenate([q_ref[rows, (2 * g) * LANES:(2 * g + 1) * LANES],
                                   q_ref[rows, (2 * g + 1) * LANES:(2 * g + 2) * LANES]],
                                  axis=0).astype(BF16)
            return _dot_nt(qst, k_lo[keys]), _dot_nt(qst, k_hi[keys])

        ahead = scores(0)
        for s in range(nq):
            rows = slice(s * w, (s + 1) * w)
            keys = slice(s * w, (s + 2) * w)
            pb = first_bias if s == 0 else None
            s_lo, s_hi = ahead
            if s + 1 < nq:
                ahead = scores(s + 1)
            e0, d0 = _softmax_fold(s_lo[:w], band, pb, sinks2[0])
            e1, d1 = _softmax_fold(s_hi[:w], band, pb, sinks2[1])
            e2, d2 = _softmax_fold(s_lo[w:], band, pb, sinks2[2])
            e3, d3 = _softmax_fold(s_hi[w:], band, pb, sinks2[3])
            p = jnp.concatenate([jnp.concatenate([e0, e1], axis=1),
                                 jnp.concatenate([e2, e3], axis=1)], axis=0)
            vcat = jnp.concatenate([v_lo[keys], v_hi[keys]], axis=0)
            o = _dot(p, vcat)
            oa = o[:w] / jnp.where(lo_q, d0, d1)
            ob = o[w:] / jnp.where(lo_q, d2, d3)
            o_ref[rows, (2 * g) * LANES:(2 * g + 1) * LANES] = oa.astype(o_ref.dtype)
            o_ref[rows, (2 * g + 1) * LANES:(2 * g + 2) * LANES] = ob.astype(o_ref.dtype)


def _attn_prompt(proj, sinks, nq):
    b, l, _ = proj.shape
    w = WINDOW
    nsteps = l // (nq * w)
    kcol, vcol = COL_K // KV_DIM, COL_V // KV_DIM
    prev = lambda bi, i: jnp.maximum(nq * i - 1, 0)
    return pl.pallas_call(
        functools.partial(_attn_prompt_kernel, nq=nq),
        grid=(b, nsteps),
        in_specs=[
            pl.BlockSpec(memory_space=pltpu.SMEM),
            pl.BlockSpec((None, nq * w, Q_DIM), lambda bi, i: (bi, i, COL_Q // Q_DIM)),
            pl.BlockSpec((None, nq * w, KV_DIM), lambda bi, i: (bi, i, kcol)),
            pl.BlockSpec((None, w, KV_DIM), lambda bi, i: (bi, prev(bi, i), kcol)),
            pl.BlockSpec((None, nq * w, KV_DIM), lambda bi, i: (bi, i, vcol)),
            pl.BlockSpec((None, w, KV_DIM), lambda bi, i: (bi, prev(bi, i), vcol)),
        ],
        out_specs=[
            pl.BlockSpec((None, nq * w, Q_DIM), lambda bi, i: (bi, i, 0)),
            pl.BlockSpec((None, w, KV_DIM), lambda bi, i: (bi, 0, 0)),
            pl.BlockSpec((None, w, KV_DIM), lambda bi, i: (bi, 0, 0)),
        ],
        out_shape=[
            jax.ShapeDtypeStruct((b, l, Q_DIM), BF16),
            jax.ShapeDtypeStruct((b, w, KV_DIM), F32),
            jax.ShapeDtypeStruct((b, w, KV_DIM), F32),
        ],
        compiler_params=_cparams(2),
        name="attn_prompt",
    )(sinks, proj, proj, proj, proj, proj)


def _softplus(v):
    return jnp.maximum(v, 0.0) + jnp.log1p(jnp.exp(-jnp.abs(v)))


def _head_expand(vals, ex2_ref):
    hi = vals.astype(BF16)
    mid = (vals - hi.astype(F32)).astype(BF16)
    return _dot(jnp.concatenate([hi, mid], axis=1), ex2_ref[...])


def _ssd_prompt_kernel(*refs, nsub):
    nz, nx = SSM_INNER // PIECE, CONV_DIM // PIECE
    z_refs, x_refs = refs[:nz], refs[nz:nz + nx]
    (dt_ref, cw_ref, cb_ref, dtb_ref, alog_ref, dsk_ref, gn_ref, ex2_ref,
     y_ref, nssm_ref, nconv_ref, state_ref, carry_ref) = refs[nz + nx:]
    i = pl.program_id(1)
    nc = pl.num_programs(1)
    q = SSD_CHUNK

    @pl.when(i == 0)
    def _():
        state_ref[...] = jnp.zeros_like(state_ref)
        carry_ref[...] = jnp.zeros_like(carry_ref)

    dtp = dt_ref[:q, :]
    for s in range(1, nsub):
        dtp = dtp + pltpu.roll(dt_ref[s * q:(s + 1) * q, :], s * SSM_HEADS, 1)
    dt = _softplus(dtp + dtb_ref[...])
    da = dt * (-jnp.exp(alog_ref[...]))
    tri = (lax.broadcasted_iota(jnp.int32, (q, q), 0) >= lax.broadcasted_iota(jnp.int32, (q, q), 1))
    cs = _dot_exact_lhs01(tri.astype(BF16), da)
    cs_t = cs.T
    sc = dict(
        tri=tri, cs=cs, cs_t=cs_t, dt_t=dt.T,
        cdec_t=jnp.exp(cs_t[:, q - 1:q]),
        ecs_x=_head_expand(jnp.exp(cs), ex2_ref),
        wgt_x=_head_expand(dt * jnp.exp(cs[q - 1:q, :] - cs), ex2_ref))

    rows_of = lambda rs, sl: jnp.concatenate([r[sl, :] for r in rs], axis=1)
    ahead = _ssd_conv(0, rows_of, x_refs, cw_ref, cb_ref, carry_ref)
    for s in range(nsub):
        xc = ahead
        if s + 1 < nsub:
            ahead = _ssd_conv(s + 1, rows_of, x_refs, cw_ref, cb_ref, carry_ref)
        _ssd_chunk(s, sc, xc, rows_of, z_refs, dsk_ref, gn_ref, y_ref, state_ref)
    tail = rows_of(x_refs, slice(nsub * q - 8, nsub * q))
    carry_ref[...] = tail

    @pl.when(i == nc - 1)
    def _():
        nconv_ref[...] = tail
        nssm_ref[...] = state_ref[...]


def _ssd_conv(s, rows_of, x_refs, cw_ref, cb_ref, carry_ref):
    q = SSD_CHUNK
    x = rows_of(x_refs, slice(s * q, (s + 1) * q))
    prev = carry_ref[...] if s == 0 else rows_of(x_refs, slice(s * q - 8, s * q))
    row8 = lax.broadcasted_iota(jnp.int32, (8, CONV_DIM), 0)

    def shifted(k):
        r = pltpu.roll(x, k, 0)
        head = jnp.where(row8 < k, pltpu.roll(prev, k, 0), r[:8])
        return jnp.concatenate([head, r[8:]], axis=0)

    conv = x * cw_ref[CONV_WIDTH - 1:CONV_WIDTH, :] + cb_ref[...]
    for k in range(1, CONV_WIDTH):
        conv = conv + shifted(k) * cw_ref[CONV_WIDTH - 1 - k:CONV_WIDTH - k, :]

    return _silu(conv)


def _ssd_chunk(s, sc, xc, rows_of, z_refs, dsk_ref, gn_ref, y_ref, state_ref):
    q = SSD_CHUNK
    tri, cs, cs_t, dt_t, cdec_t = sc["tri"], sc["cs"], sc["cs_t"], sc["dt_t"], sc["cdec_t"]
    hoff = s * SSM_HEADS
    xoff = s * SSM_INNER
    trows = slice(s * q, (s + 1) * q)
    z = rows_of(z_refs, trows)
    xs = xc[:, :SSM_INNER]
    bm = xc[:, SSM_INNER:SSM_INNER + BC_DIM]
    cm = xc[:, SSM_INNER + BC_DIM:]

    lo = lax.broadcasted_iota(jnp.int32, (q, LANES), 1) < SSM_HEAD_DIM

    hpg = SSM_HEADS // SSM_GROUPS
    gw = hpg * SSM_HEAD_DIM
    for g in range(SSM_GROUPS):
        bg = bm[:, g * SSM_STATE:(g + 1) * SSM_STATE].astype(BF16)
        cg = cm[:, g * SSM_STATE:(g + 1) * SSM_STATE].astype(BF16)
        cb = _dot_nt(cg, bg)
        st = state_ref[g * gw:(g + 1) * gw, :]
        yoff = _dot_nt(cg, st.astype(BF16))
        ys = []
        for pr in range(2):
            pair = 2 * g + pr
            ms = []
            for h in (hoff + 2 * pair, hoff + 2 * pair + 1):
                diff = cs[:, h:h + 1] - cs_t[h:h + 1, :]
                lm = jnp.exp(jnp.where(tri, diff, -jnp.inf))
                ms.append((cb * lm * dt_t[h:h + 1, :]).astype(BF16))
            xp = xs[:, pair * LANES:(pair + 1) * LANES]
            x2 = jnp.concatenate([jnp.where(lo, xp, 0.0), jnp.where(lo, 0.0, xp)],
                                 axis=0).astype(BF16)
            yd = _dot(jnp.concatenate(ms, axis=1), x2)
            yo = yoff[:, pr * LANES:(pr + 1) * LANES] * sc["ecs_x"][
                :, xoff + pair * LANES:xoff + (pair + 1) * LANES]
            ys.append(yd + yo + xp * dsk_ref[:, pair * LANES:(pair + 1) * LANES])
        yg = jnp.concatenate(ys, axis=1)
        wx = xs[:, g * gw:(g + 1) * gw] * sc["wgt_x"][:, xoff + g * gw:xoff + (g + 1) * gw]
        s_new = _dot_tn(wx.astype(BF16), bg)
        for r in range(hpg):
            h = hoff + hpg * g + r
            rows = slice(g * gw + r * SSM_HEAD_DIM, g * gw + (r + 1) * SSM_HEAD_DIM)
            state_ref[rows, :] = (st[r * SSM_HEAD_DIM:(r + 1) * SSM_HEAD_DIM, :] * cdec_t[h:h + 1, :]
                                  + s_new[r * SSM_HEAD_DIM:(r + 1) * SSM_HEAD_DIM, :])
        hg = yg * _silu(z[:, g * gw:(g + 1) * gw])
        y_ref[trows, g * gw:(g + 1) * gw] = (_rms(hg) * gn_ref[:, g * gw:(g + 1) * gw]
                                            ).astype(y_ref.dtype)


def _ssd_prompt(proj, dtraw, conv_w, conv_b, dt_bias_t, a_log_t, d_skip_x, norm_g, nsub):
    assert nsub * SSM_HEADS <= LANES
    src = np.arange(nsub * SSM_INNER) // SSM_HEAD_DIM
    ex = (np.arange(LANES)[:, None] == src[None, :]).astype(np.float32)
    ex2 = jnp.asarray(np.concatenate([ex, ex], axis=0), BF16)
    b, l, _ = proj.shape
    q = SSD_CHUNK * nsub
    nc = l // q
    const = lambda bi, i: (0, 0)
    piece = lambda c: pl.BlockSpec((None, q, PIECE), lambda bi, i: (bi, i, c))
    n_pieces = (SSM_INNER + CONV_DIM) // PIECE
    return pl.pallas_call(
        functools.partial(_ssd_prompt_kernel, nsub=nsub),
        grid=(b, nc),
        in_specs=[piece(COL_Z // PIECE + c) for c in range(SSM_INNER // PIECE)] + [
            piece(COL_XBC // PIECE + c) for c in range(CONV_DIM // PIECE)] + [
            pl.BlockSpec((None, q, HEADS_PAD), lambda bi, i: (bi, i, 0)),
            pl.BlockSpec((CONV_WIDTH, CONV_DIM), const),
            pl.BlockSpec((1, CONV_DIM), const),
            pl.BlockSpec((1, HEADS_PAD), const),
            pl.BlockSpec((1, HEADS_PAD), const),
            pl.BlockSpec((1, SSM_INNER), const),
            pl.BlockSpec((1, SSM_INNER), const),
            pl.BlockSpec((2 * LANES, nsub * SSM_INNER), const),
        ],
        out_specs=[
            pl.BlockSpec((None, q, SSM_INNER), lambda bi, i: (bi, i, 0)),
            pl.BlockSpec((None, SSM_INNER, SSM_STATE), lambda bi, i: (bi, 0, 0)),
            pl.BlockSpec((None, 8, CONV_DIM), lambda bi, i: (bi, 0, 0)),
        ],
        out_shape=[
            jax.ShapeDtypeStruct((b, l, SSM_INNER), BF16),
            jax.ShapeDtypeStruct((b, SSM_INNER, SSM_STATE), F32),
            jax.ShapeDtypeStruct((b, 8, CONV_DIM), F32),
        ],
        scratch_shapes=[pltpu.VMEM((SSM_INNER, SSM_STATE), F32),
                        pltpu.VMEM((8, CONV_DIM), F32)],
        compiler_params=_cparams(2),
        name="ssd_prompt",
    )(*([proj] * n_pieces), dtraw, conv_w, conv_b, dt_bias_t, a_log_t, d_skip_x, norm_g, ex2)


def _outproj_kernel(x_ref, a_ref, y_ref, wa_ref, wy_ref, o_ref):
    o_ref[...] = (x_ref[...] + _dot(a_ref[...].astype(BF16), wa_ref[...])
                  + _dot(y_ref[...].astype(BF16), wy_ref[...]))


def _outproj(x, attn, y, w_out, tm):
    t, d = x.shape
    half = w_out.shape[0] // 2
    return pl.pallas_call(
        _outproj_kernel,
        grid=(t // tm,),
        in_specs=[
            pl.BlockSpec((tm, d), lambda i: (i, 0)),
            pl.BlockSpec((tm, half), lambda i: (i, 0)),
            pl.BlockSpec((tm, half), lambda i: (i, 0)),
            pl.BlockSpec((half, d), lambda i: (0, 0)),
            pl.BlockSpec((half, d), lambda i: (1, 0)),
        ],
        out_specs=pl.BlockSpec((tm, d), lambda i: (i, 0)),
        out_shape=jax.ShapeDtypeStruct((t, d), F32),
        compiler_params=_cparams(1),
        name="outproj",
    )(x, attn, y, w_out, w_out)


def _ffn_kernel(h_ref, g_ref, wg_ref, wu_ref, wd_ref, o_ref, f_ref):
    d = o_ref.shape[1]
    tn = min(d, 512)

    def add_delta(f, base_ref):
        hid = (_silu(_dot(f, wg_ref[...])) * _dot(f, wu_ref[...])).astype(BF16)
        for c in range(d // tn):
            cols = slice(c * tn, (c + 1) * tn)
            o_ref[:, cols] = base_ref[:, cols] + _dot(hid, wd_ref[:, cols])

    @pl.when(pl.program_id(1) == 0)
    def _():
        f = (_rms(h_ref[...]) * g_ref[...]).astype(BF16)
        f_ref[...] = f
        add_delta(f, h_ref)

    @pl.when(pl.program_id(1) > 0)
    def _():
        add_delta(f_ref[...], o_ref)


def _ffn(h, g, wg, wu, wd, tm, th):
    t, d = h.shape
    hidden = wd.shape[0]
    return pl.pallas_call(
        _ffn_kernel,
        grid=(t // tm, hidden // th),
        in_specs=[
            pl.BlockSpec((tm, d), lambda i, j: (i, 0)),
            pl.BlockSpec((1, d), lambda i, j: (0, 0)),
            pl.BlockSpec((d, th), lambda i, j: (0, j)),
            pl.BlockSpec((d, th), lambda i, j: (0, j)),
            pl.BlockSpec((th, d), lambda i, j: (j, 0)),
        ],
        out_specs=pl.BlockSpec((tm, d), lambda i, j: (i, 0)),
        out_shape=jax.ShapeDtypeStruct((t, d), F32),
        scratch_shapes=[pltpu.VMEM((tm, d), BF16)],
        compiler_params=_cparams(2),
        name="ffn",
    )(h, g, wg, wu, wd)


def _ple_kernel(h_ref, p_ref, gp_ref, wp_ref, wg_ref, gf_ref, o_ref, *, tn, row_parts):
    tm, d = h_ref.shape
    rp = tm // row_parts
    for r in range(row_parts):
        rows = slice(r * rp, (r + 1) * rp)
        n = (_rms(h_ref[rows, :]) * gp_ref[...]).astype(BF16)
        pb = p_ref[rows, :].astype(BF16)
        ss = jnp.zeros((rp, 1), F32)
        for c in range(d // tn):
            cols = slice(c * tn, (c + 1) * tn)
            gate = _dot(n, wg_ref[:, cols])
            h3 = h_ref[rows, cols] + _dot(pb, wp_ref[:, cols]) * _sigmoid(gate)
            o_ref[rows, cols] = h3
            ss = ss + jnp.sum(h3 * h3, axis=-1, keepdims=True)
        inv = lax.rsqrt(ss * (1.0 / d) + RMS_EPS)
        o_ref[rows, :] = o_ref[rows, :] * inv * gf_ref[...]


def _ple_final(h, p, g_ple, w_ple, w_gate, g_final, tm):
    t, d = h.shape
    pd = p.shape[1]
    const = lambda i: (0, 0)
    resident = dict(pipeline_mode=pl.Buffered(1))
    return pl.pallas_call(
        functools.partial(_ple_kernel, tn=512, row_parts=max(tm // 256, 1)),
        grid=(t // tm,),
        in_specs=[
            pl.BlockSpec((tm, d), lambda i: (i, 0)),
            pl.BlockSpec((tm, pd), lambda i: (i, 0)),
            pl.BlockSpec((1, d), const),
            pl.BlockSpec((pd, d), const, **resident),
            pl.BlockSpec((d, d), const, **resident),
            pl.BlockSpec((1, d), const),
        ],
        out_specs=pl.BlockSpec((tm, d), lambda i: (i, 0)),
        out_shape=jax.ShapeDtypeStruct((t, d), F32),
        compiler_params=_cparams(1),
        name="ple_final",
    )(h, p, g_ple, w_ple, w_gate, g_final)


def _sample_pre_kernel(x_ref, dt_ref, sc_ref, cw_ref, cb_ref, dtb_ref,
                       alog_ref, dsk_ref, exp_ref,
                       xdt_ref, b_ref, c_ref, dec_ref, yp_ref, dech_ref):
    conv = x_ref[...] * cw_ref[CONV_WIDTH - 1:CONV_WIDTH, :] + cb_ref[...]
    for k in range(CONV_WIDTH - 1):
        conv = conv + sc_ref[k] * cw_ref[k:k + 1, :]
    xc = _silu(conv)
    xs = xc[:, :SSM_INNER]
    bm = xc[:, SSM_INNER:SSM_INNER + BC_DIM]
    cm = xc[:, SSM_INNER + BC_DIM:]
    b_ref[...] = bm
    c_ref[...] = cm
    dt = _softplus(dt_ref[...] + dtb_ref[...])
    dec = jnp.exp(dt * (-jnp.exp(alog_ref[...])))
    ex = exp_ref[...]
    dtx = _dot_exact_rhs01(dt, ex)
    dec_ref[...] = _dot_exact_rhs01(dec, ex)
    dech_ref[...] = dec
    xdt = xs * dtx
    xdt_ref[...] = xdt
    gw = SSM_INNER // SSM_GROUPS
    cbs = []
    for g in range(SSM_GROUPS):
        prod = cm[:, g * SSM_STATE:(g + 1) * SSM_STATE] * bm[:, g * SSM_STATE:(g + 1) * SSM_STATE]
        cbs.append(jnp.broadcast_to(jnp.sum(prod, axis=-1, keepdims=True), (prod.shape[0], gw)))
    yp_ref[...] = xdt * jnp.concatenate(cbs, axis=1) + xs * dsk_ref[...]


def _sample_pre(xbc, dtraw, sconv_t, conv_w, conv_b, dt_bias, a_log, d_skip_x, expand):
    nb = xbc.shape[0]
    shapes = [(nb, SSM_INNER), (nb, BC_DIM), (nb, BC_DIM),
              (nb, SSM_INNER), (nb, SSM_INNER), (nb, HEADS_PAD)]
    return pl.pallas_call(
        _sample_pre_kernel,
        out_shape=[jax.ShapeDtypeStruct(s, F32) for s in shapes],
        compiler_params=pltpu.CompilerParams(vmem_limit_bytes=VMEM_LIMIT),
        name="sample_pre",
    )(xbc, dtraw, sconv_t, conv_w, conv_b, dt_bias, a_log, d_skip_x, expand)


def _attn_sample_kernel(sink_ref, q_ref, kn_ref, vn_ref, ck_ref, cv_ref, o_ref, nk_ref, nv_ref, *, bb):
    w = WINDOW
    row = lax.broadcasted_iota(jnp.int32, (w, KV_DIM), 0)
    hrow = lax.broadcasted_iota(jnp.int32, (N_HEADS, KV_DIM), 0) // (N_HEADS // N_KV_HEADS)
    hgrp = lax.broadcasted_iota(jnp.int32, (N_HEADS, KV_DIM), 1) // HEAD_DIM
    own = hrow == hgrp
    sink = sink_ref[...] * LOG2E
    kks, vvs, scores, probs, dens = [], [], [], [], []
    for b in range(bb):
        kk = jnp.where(row == w - 1, kn_ref[b:b + 1, :], pltpu.roll(ck_ref[b], w - 1, 0))
        vv = jnp.where(row == w - 1, vn_ref[b:b + 1, :], pltpu.roll(cv_ref[b], w - 1, 0))
        nk_ref[b] = kk
        nv_ref[b] = vv
        kks.append(kk.astype(BF16))
        vvs.append(vv.astype(BF16))
    for b in range(bb):
        qb = q_ref[b]
        qrow = jnp.where(own, jnp.concatenate([qb] * N_KV_HEADS, axis=1), 0.0)
        scores.append(_dot_nt(qrow.astype(BF16), kks[b]))
    for b in range(bb):
        m = jnp.maximum(jnp.max(scores[b], axis=-1, keepdims=True), sink)
        e = jnp.exp2(scores[b] - m)
        dens.append(jnp.sum(e, axis=-1, keepdims=True) + jnp.exp2(sink - m))
        probs.append(e.astype(BF16))
    outs = [jnp.where(own, _dot(probs[b], vvs[b]), 0.0) for b in range(bb)]
    for b in range(bb):
        o = outs[b][:, :HEAD_DIM]
        for g in range(1, N_KV_HEADS):
            o = o + outs[b][:, g * HEAD_DIM:(g + 1) * HEAD_DIM]
        o_ref[b] = o / dens[b]


def _attn_sample(sinks_col, q3, knew, vnew, cache_k, cache_v, bb):
    nb = q3.shape[0]
    w = WINDOW
    return pl.pallas_call(
        functools.partial(_attn_sample_kernel, bb=bb),
        grid=(nb // bb,),
        in_specs=[
            pl.BlockSpec((N_HEADS, 1), lambda i: (0, 0)),
            pl.BlockSpec((bb, N_HEADS, HEAD_DIM), lambda i: (i, 0, 0)),
            pl.BlockSpec((bb, KV_DIM), lambda i: (i, 0)),
            pl.BlockSpec((bb, KV_DIM), lambda i: (i, 0)),
            pl.BlockSpec((bb, w, KV_DIM), lambda i: (i, 0, 0)),
            pl.BlockSpec((bb, w, KV_DIM), lambda i: (i, 0, 0)),
        ],
        out_specs=[
            pl.BlockSpec((bb, N_HEADS, HEAD_DIM), lambda i: (i, 0, 0)),
            pl.BlockSpec((bb, w, KV_DIM), lambda i: (i, 0, 0)),
            pl.BlockSpec((bb, w, KV_DIM), lambda i: (i, 0, 0)),
        ],
        out_shape=[
            jax.ShapeDtypeStruct((nb, N_HEADS, HEAD_DIM), F32),
            jax.ShapeDtypeStruct((nb, w, KV_DIM), F32),
            jax.ShapeDtypeStruct((nb, w, KV_DIM), F32),
        ],
        compiler_params=_cparams(1),
        name="attn_sample",
    )(sinks_col, q3, knew, vnew, cache_k, cache_v)


def _ssd_sample_kernel(st_ref, xdt_ref, b_ref, c_ref, dec_ref, dech_ref, yp_ref, z_ref, gn_ref,
                       y_ref, ns_ref, *, bb):
    gw = SSM_INNER // SSM_GROUPS
    grow = lax.broadcasted_iota(jnp.int32, (8, SSM_INNER), 0)
    glane = lax.broadcasted_iota(jnp.int32, (8, SSM_INNER), 1) // gw
    own = grow == glane
    pad = jnp.zeros((8 - SSM_GROUPS, SSM_STATE), F32)
    yoffs = []
    for b in range(bb):
        st = st_ref[b]
        cmat = jnp.concatenate([c_ref[b], pad], axis=0).astype(BF16)
        bmat = jnp.concatenate([b_ref[b], pad], axis=0).astype(BF16)
        r = _dot_nt(cmat, st.astype(BF16))
        yoffs.append(jnp.sum(jnp.where(own, r, 0.0), axis=0, keepdims=True))
        amat = jnp.where(own, jnp.broadcast_to(xdt_ref[b:b + 1, :], (8, SSM_INNER)), 0.0)
        outer = _dot_tn(amat.astype(BF16), bmat)
        for h in range(SSM_HEADS):
            rows = slice(h * SSM_HEAD_DIM, (h + 1) * SSM_HEAD_DIM)
            ns_ref[b, rows, :] = st[rows, :] * dech_ref[b:b + 1, h:h + 1] + outer[rows, :]
    y = yp_ref[...] + jnp.concatenate(yoffs, axis=0) * dec_ref[...]
    hg = y * _silu(z_ref[...])
    outs = []
    for g in range(SSM_GROUPS):
        outs.append(_rms(hg[:, g * gw:(g + 1) * gw]))
    y_ref[...] = jnp.concatenate(outs, axis=1) * gn_ref[...]


def _ssd_sample(state, xdt, b3, c3, decx, dech, ypart, z, norm_g, bb):
    nb = state.shape[0]
    row = lambda i: (i, 0)
    return pl.pallas_call(
        functools.partial(_ssd_sample_kernel, bb=bb),
        grid=(nb // bb,),
        in_specs=[
            pl.BlockSpec((bb, SSM_INNER, SSM_STATE), lambda i: (i, 0, 0)),
            pl.BlockSpec((bb, SSM_INNER), row),
            pl.BlockSpec((bb, SSM_GROUPS, SSM_STATE), lambda i: (i, 0, 0)),
            pl.BlockSpec((bb, SSM_GROUPS, SSM_STATE), lambda i: (i, 0, 0)),
            pl.BlockSpec((bb, SSM_INNER), row),
            pl.BlockSpec((bb, HEADS_PAD), row),
            pl.BlockSpec((bb, SSM_INNER), row),
            pl.BlockSpec((bb, SSM_INNER), row),
            pl.BlockSpec((1, SSM_INNER), lambda i: (0, 0)),
        ],
        out_specs=[
            pl.BlockSpec((bb, SSM_INNER), row),
            pl.BlockSpec((bb, SSM_INNER, SSM_STATE), lambda i: (i, 0, 0)),
        ],
        out_shape=[
            jax.ShapeDtypeStruct((nb, SSM_INNER), F32),
            jax.ShapeDtypeStruct((nb, SSM_INNER, SSM_STATE), F32),
        ],
        compiler_params=_cparams(1),
        name="ssd_sample",
    )(state, xdt, b3, c3, decx, dech, ypart, z, norm_g)


def _row_tile(t, want):
    return want if t % want == 0 else t


def _dense_tail(x, attn, y, p, wts, tm):
    h1 = _outproj(x, attn, y, wts["w_out"], tm)
    h2 = _ffn(h1, wts["g_ffn"], wts["w_gate"], wts["w_up"], wts["w_down"],
              _row_tile(x.shape[0], FFN_TM), FFN_TH)
    return _ple_final(h2, p, wts["g_ple"], wts["w_ple"], wts["w_ple_gate"], wts["g_final"],
                      _row_tile(x.shape[0], 1024))


def kernel(x_prompt, x_sample, cache_k, cache_v, state_ssm, state_conv, p_prompt, p_sample, w_in, conv_w, conv_b, dt_bias, a_log, d_skip, ssm_norm_g, attn_sinks, w_out, g_mix, g_ffn, w_ffn_gate, w_ffn_up, w_ffn_down, g_ple, w_ple, w_ple_gate, g_final):
    depth = w_in.shape[0]
    assert depth == 1, "single-layer step only"
    bsz, seq, d = x_prompt.shape
    nb, dseq, _ = x_sample.shape
    assert dseq == 1 and seq % SSD_CHUNK == 0 and seq % WINDOW == 0

    w = w_in[0]
    w_main = w.astype(BF16)
    w_dt = jnp.pad(w[:, PROJ_DIM:], ((0, 0), (0, HEADS_PAD - SSM_HEADS))).astype(BF16)
    padh = lambda v: jnp.pad(v, (0, HEADS_PAD - SSM_HEADS)).reshape(1, HEADS_PAD)
    tileh = lambda v: jnp.tile(v, HEADS_PAD // SSM_HEADS).reshape(1, HEADS_PAD)
    dtb, alog = padh(dt_bias[0]), padh(a_log[0])
    dsk_x = jnp.repeat(d_skip[0], SSM_HEAD_DIM).reshape(1, SSM_INNER)
    gn = ssm_norm_g[0].reshape(1, SSM_INNER)
    cw, cb = conv_w[0], conv_b[0].reshape(1, CONV_DIM)
    wts = dict(
        w_out=w_out[0].astype(BF16), g_ffn=g_ffn[0].reshape(1, d),
        w_gate=w_ffn_gate[0].astype(BF16), w_up=w_ffn_up[0].astype(BF16),
        w_down=w_ffn_down[0].astype(BF16), g_ple=g_ple[0].reshape(1, d),
        w_ple=w_ple[0].astype(BF16), w_ple_gate=w_ple_gate[0].astype(BF16),
        g_final=g_final.reshape(1, d))
    gmix = g_mix[0].reshape(1, d)
    sinks = attn_sinks[0]

    tp = bsz * seq
    xp = x_prompt.reshape(tp, d)
    tm_in = _row_tile(seq, 1024)
    proj, dtraw = _inproj(xp, gmix, w_main, w_dt, _rope_tables(jnp.arange(seq)), tm_in)
    proj3 = proj.reshape(bsz, seq, PROJ_DIM)
    attn, nk_p, nv_p = _attn_prompt(proj3, sinks, ATTN_BLOCKS)
    yp, nssm_p, nconv_p = _ssd_prompt(proj3, dtraw.reshape(bsz, seq, HEADS_PAD), cw, cb,
                                      tileh(dt_bias[0]), tileh(a_log[0]), dsk_x, gn, SSD_CHUNKS)
    tm = _row_tile(tp, 512)
    y_prompt = _dense_tail(xp, attn.reshape(tp, Q_DIM), yp.reshape(tp, SSM_INNER),
                           p_prompt[0].reshape(tp, -1), wts, tm).reshape(bsz, seq, d)

    xs = x_sample.reshape(nb, d)
    tab_s = _rope_tables(jnp.full((nb,), PAST_LEN, jnp.int32))
    proj_s, dtraw_s = _inproj(xs, gmix, w_main, w_dt, tab_s, nb)
    window = cache_k.shape[2]
    sconv_t = jnp.transpose(state_conv[0], (1, 0, 2))
    expand = (jnp.arange(HEADS_PAD)[:, None] == (jnp.arange(SSM_INNER) // SSM_HEAD_DIM)[None, :]
              ).astype(BF16)
    q_s = proj_s[:, COL_Q:COL_Q + Q_DIM]
    k_s = proj_s[:, COL_K:COL_K + KV_DIM]
    v_s = proj_s[:, COL_V:COL_V + KV_DIM]
    z_s = proj_s[:, COL_Z:COL_Z + SSM_INNER]
    xbc_s = proj_s[:, COL_XBC:COL_XBC + CONV_DIM]
    xdt, bm, cm, decx, ypart, dech = _sample_pre(
        xbc_s, dtraw_s, sconv_t, cw, cb, dtb, alog, dsk_x, expand)
    bb = 8 if nb % 8 == 0 else nb
    attn_s, nk_s, nv_s = _attn_sample(
        sinks.reshape(N_HEADS, 1), q_s.reshape(nb, N_HEADS, HEAD_DIM), k_s, v_s,
        cache_k[0].reshape(nb, window, KV_DIM), cache_v[0].reshape(nb, window, KV_DIM), bb)
    ys, nssm_s = _ssd_sample(
        state_ssm[0].reshape(nb, SSM_INNER, SSM_STATE), xdt,
        bm.reshape(nb, SSM_GROUPS, SSM_STATE), cm.reshape(nb, SSM_GROUPS, SSM_STATE),
        decx, dech, ypart, z_s, gn, bb)
    y_sample = _dense_tail(xs, attn_s.reshape(nb, Q_DIM), ys, p_sample[0].reshape(nb, -1), wts,
                           nb).reshape(nb, 1, d)
    nconv_s = jnp.concatenate([state_conv[0][:, 1:], xbc_s[:, None, :]], axis=1)

    kv5 = lambda t: t.reshape(1, t.shape[0], window, N_KV_HEADS, HEAD_DIM)
    return (y_prompt, y_sample,
            kv5(nk_p), kv5(nv_p),
            nssm_p.reshape(1, bsz, SSM_HEADS, SSM_HEAD_DIM, SSM_STATE),
            nconv_p[None, :, 8 - (CONV_WIDTH - 1):, :],
            kv5(nk_s), kv5(nv_s),
            nssm_s.reshape(1, nb, SSM_HEADS, SSM_HEAD_DIM, SSM_STATE),
            nconv_s[None])
```

```python
import functools

import numpy as np
import jax
import jax.numpy as jnp
from jax import lax
from jax.experimental import pallas as pl
from jax.experimental.pallas import tpu as pltpu

F32 = jnp.float32
BF16 = jnp.bfloat16

HEAD_DIM = 64
N_HEADS = 16
N_KV_HEADS = 4
WINDOW = 128
ROPE_DIM = 16
ROPE_THETA = 500000.0
SSM_HEADS = 16
SSM_HEAD_DIM = 64
SSM_GROUPS = 4
SSM_STATE = 128
CONV_WIDTH = 4
SSD_CHUNK = 128
RMS_EPS = 1e-6
PAST_LEN = 16384
LOG2E = 1.4426950408889634

Q_DIM = N_HEADS * HEAD_DIM
KV_DIM = N_KV_HEADS * HEAD_DIM
SSM_INNER = SSM_HEADS * SSM_HEAD_DIM
BC_DIM = SSM_GROUPS * SSM_STATE
CONV_DIM = SSM_INNER + 2 * BC_DIM
LANES = 128
HEADS_PAD = LANES

COL_Q = 0
COL_K = Q_DIM
COL_V = COL_K + KV_DIM
COL_Z = COL_V + KV_DIM
COL_XBC = COL_Z + SSM_INNER
PROJ_DIM = COL_XBC + CONV_DIM
PIECE = 512

INPROJ_TN = Q_DIM + 2 * KV_DIM
Q_SCALE = HEAD_DIM ** -0.5 * LOG2E
FFN_TM, FFN_TH = 1024, 512
ATTN_BLOCKS = 8
SSD_CHUNKS = 8
VMEM_LIMIT = 56 * 1024 * 1024


def _cparams(n_axes):
    return pltpu.CompilerParams(
        dimension_semantics=("arbitrary",) * n_axes, vmem_limit_bytes=VMEM_LIMIT)


def _rms(x):
    return x * lax.rsqrt(jnp.mean(x * x, axis=-1, keepdims=True) + RMS_EPS)


def _sigmoid(x):
    return 1.0 / (1.0 + jnp.exp2(x * -LOG2E))


def _silu(x):
    return x * _sigmoid(x)


def _dot(a, b):
    return jnp.dot(a, b, preferred_element_type=F32)


def _dot_nt(a, b):
    return lax.dot_general(a, b, (((1,), (1,)), ((), ())), preferred_element_type=F32)


def _dot_tn(a, b):
    return lax.dot_general(a, b, (((0,), (0,)), ((), ())), preferred_element_type=F32)


def _split3(x):
    hi = x.astype(BF16)
    r = x - hi.astype(F32)
    mid = r.astype(BF16)
    lo = (r - mid.astype(F32)).astype(BF16)
    return hi, mid, lo


def _dot_exact_lhs01(m01, x):
    hi, mid, lo = _split3(x)
    return _dot(m01, hi) + _dot(m01, mid) + _dot(m01, lo)


def _dot_exact_rhs01(x, m01):
    hi, mid, lo = _split3(x)
    return _dot(hi, m01) + _dot(mid, m01) + _dot(lo, m01)


def _inproj_kernel(xa_ref, xb_ref, g_ref, w_ref, wdt_ref, tab_ref, o_ref, odt_ref, u_ref):
    j = pl.program_id(1)

    @pl.when(j == 0)
    def _():
        rp = xa_ref.shape[0]
        for p, x_ref in enumerate((xa_ref, xb_ref)):
            rows = slice(p * rp, (p + 1) * rp)
            u = (_rms(x_ref[...]) * g_ref[...]).astype(BF16)
            u_ref[rows, :] = u
            odt_ref[rows, :] = _dot(u, wdt_ref[...])
            res = _dot(u, w_ref[...])
            tab = tab_ref[rows, :]
            for c in range((Q_DIM + KV_DIM) // LANES):
                cols = slice(c * LANES, (c + 1) * LANES)
                r = _rope(res[:, cols], tab)
                o_ref[rows, cols] = r * Q_SCALE if c < Q_DIM // LANES else r
            o_ref[rows, Q_DIM + KV_DIM:] = res[:, Q_DIM + KV_DIM:]

    @pl.when(j > 0)
    def _():
        o_ref[...] = _dot(u_ref[...], w_ref[...])


def _inproj(x, g, w, wdt, tab, tm):
    t, d = x.shape
    n = PROJ_DIM
    tn = INPROJ_TN
    nt = tab.shape[0] // tm
    ni, nj = t // tm, n // tn
    early = lambda i, j: jnp.minimum(i + (j == nj - 1).astype(jnp.int32), ni - 1)
    return pl.pallas_call(
        _inproj_kernel,
        grid=(ni, nj),
        in_specs=[
            pl.BlockSpec((tm // 2, d), lambda i, j: (2 * i, 0)),
            pl.BlockSpec((tm // 2, d), lambda i, j: (2 * early(i, j) + 1, 0)),
            pl.BlockSpec((1, d), lambda i, j: (0, 0)),
            pl.BlockSpec((d, tn), lambda i, j: (0, j)),
            pl.BlockSpec((d, HEADS_PAD), lambda i, j: (0, 0)),
            pl.BlockSpec((tm, 3 * LANES), lambda i, j: (i % nt, 0)),
        ],
        out_specs=[
            pl.BlockSpec((tm, tn), lambda i, j: (i, j)),
            pl.BlockSpec((tm, HEADS_PAD), lambda i, j: (i, 0)),
        ],
        out_shape=[
            jax.ShapeDtypeStruct((t, n), F32),
            jax.ShapeDtypeStruct((t, HEADS_PAD), F32),
        ],
        scratch_shapes=[pltpu.VMEM((tm, d), BF16)],
        compiler_params=_cparams(2),
        name="inproj",
    )(x, x, g, w, wdt, tab)


def _rope_tables(pos):
    half = ROPE_DIM // 2
    inv = ROPE_THETA ** (-jnp.arange(half, dtype=F32) * (2.0 / ROPE_DIM))
    ang = pos.astype(F32)[:, None] * inv[None, :]
    cs = jnp.concatenate([jnp.cos(ang), jnp.sin(ang)], axis=1)
    expand = np.zeros((2 * half, 3 * LANES), np.float32)
    base = np.zeros((1, 3 * LANES), np.float32)
    for lane in range(LANES):
        m = lane % HEAD_DIM
        if m >= ROPE_DIM:
            base[0, lane] = 1.0
            continue
        expand[m % half, lane] = 1.0
        if m < half:
            expand[half + m, LANES + lane] = -1.0
        else:
            expand[m, 2 * LANES + lane] = 1.0
    cs3 = jnp.concatenate(_split3(cs), axis=1)
    expand3 = jnp.asarray(np.concatenate([expand] * 3, axis=0), BF16)
    return jnp.dot(cs3, expand3, preferred_element_type=F32) + base


def _rope(x, tab):
    half = ROPE_DIM // 2
    c, sa, sb = tab[:, :LANES], tab[:, LANES:2 * LANES], tab[:, 2 * LANES:]
    return x * c + pltpu.roll(x, LANES - half, 1) * sa + pltpu.roll(x, half, 1) * sb


def _softmax_fold(s, band, prev_bias, sink2):
    sp = s[:, :WINDOW] if prev_bias is None else s[:, :WINDOW] + prev_bias
    t = jnp.where(band, sp, s[:, WINDOW:])
    m = jnp.maximum(jnp.max(t, axis=-1, keepdims=True), sink2)
    e = jnp.exp2(t - m)
    den = jnp.sum(e, axis=-1, keepdims=True) + jnp.exp2(sink2 - m)
    p = jnp.concatenate([jnp.where(band, e, 0.0), jnp.where(band, 0.0, e)], axis=1)
    return p.astype(BF16), den


def _attn_prompt_kernel(sink_ref, q_ref, kc_ref, kp_ref, vc_ref, vp_ref,
                        o_ref, nk_ref, nv_ref, *, nq):
    i = pl.program_id(1)
    nsteps = pl.num_programs(1)
    w = WINDOW
    kcr = kc_ref[...]
    kpr = kp_ref[...]
    vc = vc_ref[...]

    @pl.when(i == nsteps - 1)
    def _():
        nk_ref[...] = kcr[(nq - 1) * w:]
        nv_ref[...] = vc[(nq - 1) * w:]

    kall = jnp.concatenate([kpr, kcr], axis=0)
    vall = jnp.concatenate([vp_ref[...], vc], axis=0)
    lo = lax.broadcasted_iota(jnp.int32, ((nq + 1) * w, LANES), 1) < HEAD_DIM
    lo_q = lax.broadcasted_iota(jnp.int32, (w, LANES), 1) < HEAD_DIM
    band = (lax.broadcasted_iota(jnp.int32, (w, w), 1) > lax.broadcasted_iota(jnp.int32, (w, w), 0))
    first_bias = jnp.where(i == 0, -jnp.inf, 0.0)

    for g in range(N_KV_HEADS):
        col, odd = g // 2, g % 2
        kg = kall[:, col * LANES:(col + 1) * LANES]
        vg = vall[:, col * LANES:(col + 1) * LANES]
        kg_sw = pltpu.roll(kg, HEAD_DIM, 1)
        vg_sw = pltpu.roll(vg, HEAD_DIM, 1)
        k_lo = jnp.where(lo, kg_sw if odd else kg, 0.0).astype(BF16)
        k_hi = jnp.where(lo, 0.0, kg if odd else kg_sw).astype(BF16)
        v_lo = jnp.where(lo, vg_sw if odd else vg, 0.0).astype(BF16)
        v_hi = jnp.where(lo, 0.0, vg if odd else vg_sw).astype(BF16)
        sinks2 = [sink_ref[4 * g + r] * LOG2E for r in range(4)]
        def scores(s, g=g, k_lo=k_lo, k_hi=k_hi):
            rows = slice(s * w, (s + 1) * w)
            keys = slice(s * w, (s + 2) * w)
            qst = jnp.concatenate([q_ref[rows, (2 * g) * LANES:(2 * g + 1) * LANES],
                                   q_ref[rows, (2 * g + 1) * LANES:(2 * g + 2) * LANES]],
                                  axis=0).astype(BF16)
            return _dot_nt(qst, k_lo[keys]), _dot_nt(qst, k_hi[keys])

        ahead = scores(0)
        for s in range(nq):
            rows = slice(s * w, (s + 1) * w)
            keys = slice(s * w, (s + 2) * w)
            pb = first_bias if s == 0 else None
            s_lo, s_hi = ahead
            if s + 1 < nq:
                ahead = scores(s + 1)
            e0, d0 = _softmax_fold(s_lo[:w], band, pb, sinks2[0])
            e1, d1 = _softmax_fold(s_hi[:w], band, pb, sinks2[1])
            e2, d2 = _softmax_fold(s_lo[w:], band, pb, sinks2[2])
            e3, d3 = _softmax_fold(s_hi[w:], band, pb, sinks2[3])
            p = jnp.concatenate([jnp.concatenate([e0, e1], axis=1),
                                 jnp.concatenate([e2, e3], axis=1)], axis=0)
            vcat = jnp.concatenate([v_lo[keys], v_hi[keys]], axis=0)
            o = _dot(p, vcat)
            oa = o[:w] / jnp.where(lo_q, d0, d1)
            ob = o[w:] / jnp.where(lo_q, d2, d3)
            o_ref[rows, (2 * g) * LANES:(2 * g + 1) * LANES] = oa.astype(o_ref.dtype)
            o_ref[rows, (2 * g + 1) * LANES:(2 * g + 2) * LANES] = ob.astype(o_ref.dtype)


def _attn_prompt(proj, sinks, nq):
    b, l, _ = proj.shape
    w = WINDOW
    nsteps = l // (nq * w)
    kcol, vcol = COL_K // KV_DIM, COL_V // KV_DIM
    prev = lambda bi, i: jnp.maximum(nq * i - 1, 0)
    return pl.pallas_call(
        functools.partial(_attn_prompt_kernel, nq=nq),
        grid=(b, nsteps),
        in_specs=[
            pl.BlockSpec(memory_space=pltpu.SMEM),
            pl.BlockSpec((None, nq * w, Q_DIM), lambda bi, i: (bi, i, COL_Q // Q_DIM)),
            pl.BlockSpec((None, nq * w, KV_DIM), lambda bi, i: (bi, i, kcol)),
            pl.BlockSpec((None, w, KV_DIM), lambda bi, i: (bi, prev(bi, i), kcol)),
            pl.BlockSpec((None, nq * w, KV_DIM), lambda bi, i: (bi, i, vcol)),
            pl.BlockSpec((None, w, KV_DIM), lambda bi, i: (bi, prev(bi, i), vcol)),
        ],
        out_specs=[
            pl.BlockSpec((None, nq * w, Q_DIM), lambda bi, i: (bi, i, 0)),
            pl.BlockSpec((None, w, KV_DIM), lambda bi, i: (bi, 0, 0)),
            pl.BlockSpec((None, w, KV_DIM), lambda bi, i: (bi, 0, 0)),
        ],
        out_shape=[
            jax.ShapeDtypeStruct((b, l, Q_DIM), BF16),
            jax.ShapeDtypeStruct((b, w, KV_DIM), F32),
            jax.ShapeDtypeStruct((b, w, KV_DIM), F32),
        ],
        compiler_params=_cparams(2),
        name="attn_prompt",
    )(sinks, proj, proj, proj, proj, proj)


def _softplus(v):
    return jnp.maximum(v, 0.0) + jnp.log1p(jnp.exp(-jnp.abs(v)))


def _head_expand(vals, ex2_ref):
    hi = vals.astype(BF16)
    mid = (vals - hi.astype(F32)).astype(BF16)
    return _dot(jnp.concatenate([hi, mid], axis=1), ex2_ref[...])


def _ssd_prompt_kernel(*refs, nsub):
    nz, nx = SSM_INNER // PIECE, CONV_DIM // PIECE
    z_refs, x_refs = refs[:nz], refs[nz:nz + nx]
    (dt_ref, cw_ref, cb_ref, dtb_ref, alog_ref, dsk_ref, gn_ref, ex2_ref,
     y_ref, nssm_ref, nconv_ref, state_ref, carry_ref) = refs[nz + nx:]
    i = pl.program_id(1)
    nc = pl.num_programs(1)
    q = SSD_CHUNK

    @pl.when(i == 0)
    def _():
        state_ref[...] = jnp.zeros_like(state_ref)
        carry_ref[...] = jnp.zeros_like(carry_ref)

    dtp = dt_ref[:q, :]
    for s in range(1, nsub):
        dtp = dtp + pltpu.roll(dt_ref[s * q:(s + 1) * q, :], s * SSM_HEADS, 1)
    dt = _softplus(dtp + dtb_ref[...])
    da = dt * (-jnp.exp(alog_ref[...]))
    tri = (lax.broadcasted_iota(jnp.int32, (q, q), 0) >= lax.broadcasted_iota(jnp.int32, (q, q), 1))
    cs = _dot_exact_lhs01(tri.astype(BF16), da)
    cs2 = cs * LOG2E
    cs2_t = cs2.T
    sc = dict(
        tri=tri, cs2=cs2, cs2_t=cs2_t, dt_t=dt.T,
        cdec_t=jnp.exp2(cs2_t[:, q - 1:q]),
        ecs_x=_head_expand(jnp.exp(cs), ex2_ref),
        wgt_x=_head_expand(dt * jnp.exp(cs[q - 1:q, :] - cs), ex2_ref))

    rows_of = lambda rs, sl: jnp.concatenate([r[sl, :] for r in rs], axis=1)
    ahead = _ssd_conv(0, rows_of, x_refs, cw_ref, cb_ref, carry_ref)
    for s in range(nsub):
        xc = ahead
        if s + 1 < nsub:
            ahead = _ssd_conv(s + 1, rows_of, x_refs, cw_ref, cb_ref, carry_ref)
        _ssd_chunk(s, sc, xc, rows_of, z_refs, dsk_ref, gn_ref, y_ref, state_ref)
    tail = rows_of(x_refs, slice(nsub * q - 8, nsub * q))
    carry_ref[...] = tail

    @pl.when(i == nc - 1)
    def _():
        nconv_ref[...] = tail
        nssm_ref[...] = state_ref[...]


def _ssd_conv(s, rows_of, x_refs, cw_ref, cb_ref, carry_ref):
    q = SSD_CHUNK
    x = rows_of(x_refs, slice(s * q, (s + 1) * q))
    prev = carry_ref[...] if s == 0 else rows_of(x_refs, slice(s * q - 8, s * q))
    row8 = lax.broadcasted_iota(jnp.int32, (8, CONV_DIM), 0)

    def shifted(v, k, before):
        r = pltpu.roll(v, k, 0)
        head = jnp.where(row8 < k, pltpu.roll(before, k, 0), r[:8])
        return jnp.concatenate([head, r[8:]], axis=0)

    assert CONV_WIDTH == 4
    w0, w1, w2, w3 = (cw_ref[k:k + 1, :] for k in range(CONV_WIDTH))
    x1 = shifted(x, 1, prev)
    p = x * w1 + x1 * w0
    p_prev = prev * w1 + pltpu.roll(prev, 1, 0) * w0
    conv = x * w3 + cb_ref[...] + x1 * w2 + shifted(p, 2, p_prev)
    return _silu(conv)


def _ssd_chunk(s, sc, xc, rows_of, z_refs, dsk_ref, gn_ref, y_ref, state_ref):
    q = SSD_CHUNK
    tri, cs2, cs2_t, dt_t, cdec_t = sc["tri"], sc["cs2"], sc["cs2_t"], sc["dt_t"], sc["cdec_t"]
    hoff = s * SSM_HEADS
    xoff = s * SSM_INNER
    trows = slice(s * q, (s + 1) * q)
    z = rows_of(z_refs, trows)
    xs = xc[:, :SSM_INNER]
    bm = xc[:, SSM_INNER:SSM_INNER + BC_DIM]
    cm = xc[:, SSM_INNER + BC_DIM:]

    lo = lax.broadcasted_iota(jnp.int32, (q, LANES), 1) < SSM_HEAD_DIM

    hpg = SSM_HEADS // SSM_GROUPS
    gw = hpg * SSM_HEAD_DIM
    for g in range(SSM_GROUPS):
        bg = bm[:, g * SSM_STATE:(g + 1) * SSM_STATE].astype(BF16)
        cg = cm[:, g * SSM_STATE:(g + 1) * SSM_STATE].astype(BF16)
        cb = _dot_nt(cg, bg)
        st = state_ref[g * gw:(g + 1) * gw, :]
        yoff = _dot_nt(cg, st.astype(BF16))
        ys = []
        for pr in range(2):
            pair = 2 * g + pr
            ms = []
            for h in (hoff + 2 * pair, hoff + 2 * pair + 1):
                diff = cs2[:, h:h + 1] - cs2_t[h:h + 1, :]
                lm = jnp.exp2(jnp.where(tri, diff, -jnp.inf))
                ms.append((cb * lm * dt_t[h:h + 1, :]).astype(BF16))
            xp = xs[:, pair * LANES:(pair + 1) * LANES]
            x2 = jnp.concatenate([jnp.where(lo, xp, 0.0), jnp.where(lo, 0.0, xp)],
                                 axis=0).astype(BF16)
            yd = _dot(jnp.concatenate(ms, axis=1), x2)
            yo = yoff[:, pr * LANES:(pr + 1) * LANES] * sc["ecs_x"][
                :, xoff + pair * LANES:xoff + (pair + 1) * LANES]
            ys.append(yd + yo + xp * dsk_ref[:, pair * LANES:(pair + 1) * LANES])
        yg = jnp.concatenate(ys, axis=1)
        wx = xs[:, g * gw:(g + 1) * gw] * sc["wgt_x"][:, xoff + g * gw:xoff + (g + 1) * gw]
        s_new = _dot_tn(wx.astype(BF16), bg)
        for r in range(hpg):
            h = hoff + hpg * g + r
            rows = slice(g * gw + r * SSM_HEAD_DIM, g * gw + (r + 1) * SSM_HEAD_DIM)
            state_ref[rows, :] = (st[r * SSM_HEAD_DIM:(r + 1) * SSM_HEAD_DIM, :] * cdec_t[h:h + 1, :]
                                  + s_new[r * SSM_HEAD_DIM:(r + 1) * SSM_HEAD_DIM, :])
        hg = yg * _silu(z[:, g * gw:(g + 1) * gw])
        y_ref[trows, g * gw:(g + 1) * gw] = (_rms(hg) * gn_ref[:, g * gw:(g + 1) * gw]
                                            ).astype(y_ref.dtype)


def _ssd_prompt(proj, dtraw, conv_w, conv_b, dt_bias_t, a_log_t, d_skip_x, norm_g, nsub):
    assert nsub * SSM_HEADS <= LANES
    src = np.arange(nsub * SSM_INNER) // SSM_HEAD_DIM
    ex = (np.arange(LANES)[:, None] == src[None, :]).astype(np.float32)
    ex2 = jnp.asarray(np.concatenate([ex, ex], axis=0), BF16)
    b, l, _ = proj.shape
    q = SSD_CHUNK * nsub
    nc = l // q
    const = lambda bi, i: (0, 0)
    piece = lambda c: pl.BlockSpec((None, q, PIECE), lambda bi, i: (bi, i, c))
    n_pieces = (SSM_INNER + CONV_DIM) // PIECE
    return pl.pallas_call(
        functools.partial(_ssd_prompt_kernel, nsub=nsub),
        grid=(b, nc),
        in_specs=[piece(COL_Z // PIECE + c) for c in range(SSM_INNER // PIECE)] + [
            piece(COL_XBC // PIECE + c) for c in range(CONV_DIM // PIECE)] + [
            pl.BlockSpec((None, q, HEADS_PAD), lambda bi, i: (bi, i, 0)),
            pl.BlockSpec((CONV_WIDTH, CONV_DIM), const),
            pl.BlockSpec((1, CONV_DIM), const),
            pl.BlockSpec((1, HEADS_PAD), const),
            pl.BlockSpec((1, HEADS_PAD), const),
            pl.BlockSpec((1, SSM_INNER), const),
            pl.BlockSpec((1, SSM_INNER), const),
            pl.BlockSpec((2 * LANES, nsub * SSM_INNER), const),
        ],
        out_specs=[
            pl.BlockSpec((None, q, SSM_INNER), lambda bi, i: (bi, i, 0)),
            pl.BlockSpec((None, SSM_INNER, SSM_STATE), lambda bi, i: (bi, 0, 0)),
            pl.BlockSpec((None, 8, CONV_DIM), lambda bi, i: (bi, 0, 0)),
        ],
        out_shape=[
            jax.ShapeDtypeStruct((b, l, SSM_INNER), BF16),
            jax.ShapeDtypeStruct((b, SSM_INNER, SSM_STATE), F32),
            jax.ShapeDtypeStruct((b, 8, CONV_DIM), F32),
        ],
        scratch_shapes=[pltpu.VMEM((SSM_INNER, SSM_STATE), F32),
                        pltpu.VMEM((8, CONV_DIM), F32)],
        compiler_params=_cparams(2),
        name="ssd_prompt",
    )(*([proj] * n_pieces), dtraw, conv_w, conv_b, dt_bias_t, a_log_t, d_skip_x, norm_g, ex2)


def _outproj_kernel(x_ref, a_ref, y_ref, wa_ref, wy_ref, o_ref):
    o_ref[...] = (x_ref[...] + _dot(a_ref[...].astype(BF16), wa_ref[...])
                  + _dot(y_ref[...].astype(BF16), wy_ref[...]))


def _outproj(x, attn, y, w_out, tm):
    t, d = x.shape
    half = w_out.shape[0] // 2
    return pl.pallas_call(
        _outproj_kernel,
        grid=(t // tm,),
        in_specs=[
            pl.BlockSpec((tm, d), lambda i: (i, 0)),
            pl.BlockSpec((tm, half), lambda i: (i, 0)),
            pl.BlockSpec((tm, half), lambda i: (i, 0)),
            pl.BlockSpec((half, d), lambda i: (0, 0)),
            pl.BlockSpec((half, d), lambda i: (1, 0)),
        ],
        out_specs=pl.BlockSpec((tm, d), lambda i: (i, 0)),
        out_shape=jax.ShapeDtypeStruct((t, d), F32),
        compiler_params=_cparams(1),
        name="outproj",
    )(x, attn, y, w_out, w_out)


def _ffn_kernel(h_ref, g_ref, wg_ref, wu_ref, wd_ref, o_ref, f_ref):
    d = o_ref.shape[1]
    tn = min(d, 512)

    def add_delta(f, base_ref):
        hid = (_silu(_dot(f, wg_ref[...])) * _dot(f, wu_ref[...])).astype(BF16)
        for c in range(d // tn):
            cols = slice(c * tn, (c + 1) * tn)
            o_ref[:, cols] = base_ref[:, cols] + _dot(hid, wd_ref[:, cols])

    @pl.when(pl.program_id(1) == 0)
    def _():
        f = (_rms(h_ref[...]) * g_ref[...]).astype(BF16)
        f_ref[...] = f
        add_delta(f, h_ref)

    @pl.when(pl.program_id(1) > 0)
    def _():
        add_delta(f_ref[...], o_ref)


def _ffn(h, g, wg, wu, wd, tm, th):
    t, d = h.shape
    hidden = wd.shape[0]
    return pl.pallas_call(
        _ffn_kernel,
        grid=(t // tm, hidden // th),
        in_specs=[
            pl.BlockSpec((tm, d), lambda i, j: (i, 0)),
            pl.BlockSpec((1, d), lambda i, j: (0, 0)),
            pl.BlockSpec((d, th), lambda i, j: (0, j)),
            pl.BlockSpec((d, th), lambda i, j: (0, j)),
            pl.BlockSpec((th, d), lambda i, j: (j, 0)),
        ],
        out_specs=pl.BlockSpec((tm, d), lambda i, j: (i, 0)),
        out_shape=jax.ShapeDtypeStruct((t, d), F32),
        scratch_shapes=[pltpu.VMEM((tm, d), BF16)],
        compiler_params=_cparams(2),
        name="ffn",
    )(h, g, wg, wu, wd)


def _ple_kernel(h_ref, p_ref, gp_ref, wp_ref, wg_ref, gf_ref, o_ref, *, tn, row_parts):
    tm, d = h_ref.shape
    rp = tm // row_parts
    for r in range(row_parts):
        rows = slice(r * rp, (r + 1) * rp)
        n = (_rms(h_ref[rows, :]) * gp_ref[...]).astype(BF16)
        pb = p_ref[rows, :].astype(BF16)
        ss = jnp.zeros((rp, 1), F32)
        for c in range(d // tn):
            cols = slice(c * tn, (c + 1) * tn)
            gate = _dot(n, wg_ref[:, cols])
            h3 = h_ref[rows, cols] + _dot(pb, wp_ref[:, cols]) * _sigmoid(gate)
            o_ref[rows, cols] = h3
            ss = ss + jnp.sum(h3 * h3, axis=-1, keepdims=True)
        inv = lax.rsqrt(ss * (1.0 / d) + RMS_EPS)
        o_ref[rows, :] = o_ref[rows, :] * inv * gf_ref[...]


def _ple_final(h, p, g_ple, w_ple, w_gate, g_final, tm):
    t, d = h.shape
    pd = p.shape[1]
    const = lambda i: (0, 0)
    resident = dict(pipeline_mode=pl.Buffered(1))
    return pl.pallas_call(
        functools.partial(_ple_kernel, tn=512, row_parts=max(tm // 256, 1)),
        grid=(t // tm,),
        in_specs=[
            pl.BlockSpec((tm, d), lambda i: (i, 0)),
            pl.BlockSpec((tm, pd), lambda i: (i, 0)),
            pl.BlockSpec((1, d), const),
            pl.BlockSpec((pd, d), const, **resident),
            pl.BlockSpec((d, d), const, **resident),
            pl.BlockSpec((1, d), const),
        ],
        out_specs=pl.BlockSpec((tm, d), lambda i: (i, 0)),
        out_shape=jax.ShapeDtypeStruct((t, d), F32),
        compiler_params=_cparams(1),
        name="ple_final",
    )(h, p, g_ple, w_ple, w_gate, g_final)


def _sample_pre_kernel(x_ref, dt_ref, sc_ref, cw_ref, cb_ref, dtb_ref,
                       alog_ref, dsk_ref, exp_ref,
                       xdt_ref, b_ref, c_ref, dec_ref, yp_ref, dech_ref):
    conv = x_ref[...] * cw_ref[CONV_WIDTH - 1:CONV_WIDTH, :] + cb_ref[...]
    for k in range(CONV_WIDTH - 1):
        conv = conv + sc_ref[k] * cw_ref[k:k + 1, :]
    xc = _silu(conv)
    xs = xc[:, :SSM_INNER]
    bm = xc[:, SSM_INNER:SSM_INNER + BC_DIM]
    cm = xc[:, SSM_INNER + BC_DIM:]
    b_ref[...] = bm
    c_ref[...] = cm
    dt = _softplus(dt_ref[...] + dtb_ref[...])
    dec = jnp.exp(dt * (-jnp.exp(alog_ref[...])))
    ex = exp_ref[...]
    dtx = _dot_exact_rhs01(dt, ex)
    dec_ref[...] = _dot_exact_rhs01(dec, ex)
    dech_ref[...] = dec
    xdt = xs * dtx
    xdt_ref[...] = xdt
    gw = SSM_INNER // SSM_GROUPS
    cbs = []
    for g in range(SSM_GROUPS):
        prod = cm[:, g * SSM_STATE:(g + 1) * SSM_STATE] * bm[:, g * SSM_STATE:(g + 1) * SSM_STATE]
        cbs.append(jnp.broadcast_to(jnp.sum(prod, axis=-1, keepdims=True), (prod.shape[0], gw)))
    yp_ref[...] = xdt * jnp.concatenate(cbs, axis=1) + xs * dsk_ref[...]


def _sample_pre(xbc, dtraw, sconv_t, conv_w, conv_b, dt_bias, a_log, d_skip_x, expand):
    nb = xbc.shape[0]
    shapes = [(nb, SSM_INNER), (nb, BC_DIM), (nb, BC_DIM),
              (nb, SSM_INNER), (nb, SSM_INNER), (nb, HEADS_PAD)]
    return pl.pallas_call(
        _sample_pre_kernel,
        out_shape=[jax.ShapeDtypeStruct(s, F32) for s in shapes],
        compiler_params=pltpu.CompilerParams(vmem_limit_bytes=VMEM_LIMIT),
        name="sample_pre",
    )(xbc, dtraw, sconv_t, conv_w, conv_b, dt_bias, a_log, d_skip_x, expand)


def _attn_sample_kernel(sink_ref, q_ref, kn_ref, vn_ref, ck_ref, cv_ref, o_ref, nk_ref, nv_ref, *, bb):
    w = WINDOW
    row = lax.broadcasted_iota(jnp.int32, (w, KV_DIM), 0)
    hrow = lax.broadcasted_iota(jnp.int32, (N_HEADS, KV_DIM), 0) // (N_HEADS // N_KV_HEADS)
    hgrp = lax.broadcasted_iota(jnp.int32, (N_HEADS, KV_DIM), 1) // HEAD_DIM
    own = hrow == hgrp
    sink = sink_ref[...] * LOG2E
    kks, vvs, scores, probs, dens = [], [], [], [], []
    for b in range(bb):
        kk = jnp.where(row == w - 1, kn_ref[b:b + 1, :], pltpu.roll(ck_ref[b], w - 1, 0))
        vv = jnp.where(row == w - 1, vn_ref[b:b + 1, :], pltpu.roll(cv_ref[b], w - 1, 0))
        nk_ref[b] = kk
        nv_ref[b] = vv
        kks.append(kk.astype(BF16))
        vvs.append(vv.astype(BF16))
    for b in range(bb):
        qb = q_ref[b]
        qrow = jnp.where(own, jnp.concatenate([qb] * N_KV_HEADS, axis=1), 0.0)
        scores.append(_dot_nt(qrow.astype(BF16), kks[b]))
    for b in range(bb):
        m = jnp.maximum(jnp.max(scores[b], axis=-1, keepdims=True), sink)
        e = jnp.exp2(scores[b] - m)
        dens.append(jnp.sum(e, axis=-1, keepdims=True) + jnp.exp2(sink - m))
        probs.append(e.astype(BF16))
    outs = [jnp.where(own, _dot(probs[b], vvs[b]), 0.0) for b in range(bb)]
    for b in range(bb):
        o = outs[b][:, :HEAD_DIM]
        for g in range(1, N_KV_HEADS):
            o = o + outs[b][:, g * HEAD_DIM:(g + 1) * HEAD_DIM]
        o_ref[b] = o / dens[b]


def _attn_sample(sinks_col, q3, knew, vnew, cache_k, cache_v, bb):
    nb = q3.shape[0]
    w = WINDOW
    return pl.pallas_call(
        functools.partial(_attn_sample_kernel, bb=bb),
        grid=(nb // bb,),
        in_specs=[
            pl.BlockSpec((N_HEADS, 1), lambda i: (0, 0)),
            pl.BlockSpec((bb, N_HEADS, HEAD_DIM), lambda i: (i, 0, 0)),
            pl.BlockSpec((bb, KV_DIM), lambda i: (i, 0)),
            pl.BlockSpec((bb, KV_DIM), lambda i: (i, 0)),
            pl.BlockSpec((bb, w, KV_DIM), lambda i: (i, 0, 0)),
            pl.BlockSpec((bb, w, KV_DIM), lambda i: (i, 0, 0)),
        ],
        out_specs=[
            pl.BlockSpec((bb, N_HEADS, HEAD_DIM), lambda i: (i, 0, 0)),
            pl.BlockSpec((bb, w, KV_DIM), lambda i: (i, 0, 0)),
            pl.BlockSpec((bb, w, KV_DIM), lambda i: (i, 0, 0)),
        ],
        out_shape=[
            jax.ShapeDtypeStruct((nb, N_HEADS, HEAD_DIM), F32),
            jax.ShapeDtypeStruct((nb, w, KV_DIM), F32),
            jax.ShapeDtypeStruct((nb, w, KV_DIM), F32),
        ],
        compiler_params=_cparams(1),
        name="attn_sample",
    )(sinks_col, q3, knew, vnew, cache_k, cache_v)


def _ssd_sample_kernel(st_ref, xdt_ref, b_ref, c_ref, dec_ref, dech_ref, yp_ref, z_ref, gn_ref,
                       y_ref, ns_ref, *, bb):
    gw = SSM_INNER // SSM_GROUPS
    grow = lax.broadcasted_iota(jnp.int32, (8, SSM_INNER), 0)
    glane = lax.broadcasted_iota(jnp.int32, (8, SSM_INNER), 1) // gw
    own = grow == glane
    pad = jnp.zeros((8 - SSM_GROUPS, SSM_STATE), F32)
    yoffs = []
    for b in range(bb):
        st = st_ref[b]
        cmat = jnp.concatenate([c_ref[b], pad], axis=0).astype(BF16)
        bmat = jnp.concatenate([b_ref[b], pad], axis=0).astype(BF16)
        r = _dot_nt(cmat, st.astype(BF16))
        yoffs.append(jnp.sum(jnp.where(own, r, 0.0), axis=0, keepdims=True))
        amat = jnp.where(own, jnp.broadcast_to(xdt_ref[b:b + 1, :], (8, SSM_INNER)), 0.0)
        outer = _dot_tn(amat.astype(BF16), bmat)
        for h in range(SSM_HEADS):
            rows = slice(h * SSM_HEAD_DIM, (h + 1) * SSM_HEAD_DIM)
            ns_ref[b, rows, :] = st[rows, :] * dech_ref[b:b + 1, h:h + 1] + outer[rows, :]
    y = yp_ref[...] + jnp.concatenate(yoffs, axis=0) * dec_ref[...]
    hg = y * _silu(z_ref[...])
    outs = []
    for g in range(SSM_GROUPS):
        outs.append(_rms(hg[:, g * gw:(g + 1) * gw]))
    y_ref[...] = jnp.concatenate(outs, axis=1) * gn_ref[...]


def _ssd_sample(state, xdt, b3, c3, decx, dech, ypart, z, norm_g, bb):
    nb = state.shape[0]
    row = lambda i: (i, 0)
    return pl.pallas_call(
        functools.partial(_ssd_sample_kernel, bb=bb),
        grid=(nb // bb,),
        in_specs=[
            pl.BlockSpec((bb, SSM_INNER, SSM_STATE), lambda i: (i, 0, 0)),
            pl.BlockSpec((bb, SSM_INNER), row),
            pl.BlockSpec((bb, SSM_GROUPS, SSM_STATE), lambda i: (i, 0, 0)),
            pl.BlockSpec((bb, SSM_GROUPS, SSM_STATE), lambda i: (i, 0, 0)),
            pl.BlockSpec((bb, SSM_INNER), row),
            pl.BlockSpec((bb, HEADS_PAD), row),
            pl.BlockSpec((bb, SSM_INNER), row),
            pl.BlockSpec((bb, SSM_INNER), row),
            pl.BlockSpec((1, SSM_INNER), lambda i: (0, 0)),
        ],
        out_specs=[
            pl.BlockSpec((bb, SSM_INNER), row),
            pl.BlockSpec((bb, SSM_INNER, SSM_STATE), lambda i: (i, 0, 0)),
        ],
        out_shape=[
            jax.ShapeDtypeStruct((nb, SSM_INNER), F32),
            jax.ShapeDtypeStruct((nb, SSM_INNER, SSM_STATE), F32),
        ],
        compiler_params=_cparams(1),
        name="ssd_sample",
    )(state, xdt, b3, c3, decx, dech, ypart, z, norm_g)


def _row_tile(t, want):
    return want if t % want == 0 else t


def _dense_tail(x, attn, y, p, wts, tm):
    h1 = _outproj(x, attn, y, wts["w_out"], tm)
    h2 = _ffn(h1, wts["g_ffn"], wts["w_gate"], wts["w_up"], wts["w_down"],
              _row_tile(x.shape[0], FFN_TM), FFN_TH)
    return _ple_final(h2, p, wts["g_ple"], wts["w_ple"], wts["w_ple_gate"], wts["g_final"],
                      _row_tile(x.shape[0], 1024))


def kernel(x_prompt, x_sample, cache_k, cache_v, state_ssm, state_conv, p_prompt, p_sample, w_in, conv_w, conv_b, dt_bias, a_log, d_skip, ssm_norm_g, attn_sinks, w_out, g_mix, g_ffn, w_ffn_gate, w_ffn_up, w_ffn_down, g_ple, w_ple, w_ple_gate, g_final):
    depth = w_in.shape[0]
    assert depth == 1, "single-layer step only"
    bsz, seq, d = x_prompt.shape
    nb, dseq, _ = x_sample.shape
    assert dseq == 1 and seq % SSD_CHUNK == 0 and seq % WINDOW == 0

    w = w_in[0]
    w_main = w.astype(BF16)
    w_dt = jnp.pad(w[:, PROJ_DIM:], ((0, 0), (0, HEADS_PAD - SSM_HEADS))).astype(BF16)
    padh = lambda v: jnp.pad(v, (0, HEADS_PAD - SSM_HEADS)).reshape(1, HEADS_PAD)
    tileh = lambda v: jnp.tile(v, HEADS_PAD // SSM_HEADS).reshape(1, HEADS_PAD)
    dtb, alog = padh(dt_bias[0]), padh(a_log[0])
    dsk_x = jnp.repeat(d_skip[0], SSM_HEAD_DIM).reshape(1, SSM_INNER)
    gn = ssm_norm_g[0].reshape(1, SSM_INNER)
    cw, cb = conv_w[0], conv_b[0].reshape(1, CONV_DIM)
    wts = dict(
        w_out=w_out[0].astype(BF16), g_ffn=g_ffn[0].reshape(1, d),
        w_gate=w_ffn_gate[0].astype(BF16), w_up=w_ffn_up[0].astype(BF16),
        w_down=w_ffn_down[0].astype(BF16), g_ple=g_ple[0].reshape(1, d),
        w_ple=w_ple[0].astype(BF16), w_ple_gate=w_ple_gate[0].astype(BF16),
        g_final=g_final.reshape(1, d))
    gmix = g_mix[0].reshape(1, d)
    sinks = attn_sinks[0]

    tp = bsz * seq
    xp = x_prompt.reshape(tp, d)
    tm_in = _row_tile(seq, 1024)
    proj, dtraw = _inproj(xp, gmix, w_main, w_dt, _rope_tables(jnp.arange(seq)), tm_in)
    proj3 = proj.reshape(bsz, seq, PROJ_DIM)
    attn, nk_p, nv_p = _attn_prompt(proj3, sinks, ATTN_BLOCKS)
    yp, nssm_p, nconv_p = _ssd_prompt(proj3, dtraw.reshape(bsz, seq, HEADS_PAD), cw, cb,
                                      tileh(dt_bias[0]), tileh(a_log[0]), dsk_x, gn, SSD_CHUNKS)
    tm = _row_tile(tp, 512)
    y_prompt = _dense_tail(xp, attn.reshape(tp, Q_DIM), yp.reshape(tp, SSM_INNER),
                           p_prompt[0].reshape(tp, -1), wts, tm).reshape(bsz, seq, d)

    xs = x_sample.reshape(nb, d)
    tab_s = _rope_tables(jnp.full((nb,), PAST_LEN, jnp.int32))
    proj_s, dtraw_s = _inproj(xs, gmix, w_main, w_dt, tab_s, nb)
    window = cache_k.shape[2]
    sconv_t = jnp.transpose(state_conv[0], (1, 0, 2))
    expand = (jnp.arange(HEADS_PAD)[:, None] == (jnp.arange(SSM_INNER) // SSM_HEAD_DIM)[None, :]
              ).astype(BF16)
    q_s = proj_s[:, COL_Q:COL_Q + Q_DIM]
    k_s = proj_s[:, COL_K:COL_K + KV_DIM]
    v_s = proj_s[:, COL_V:COL_V + KV_DIM]
    z_s = proj_s[:, COL_Z:COL_Z + SSM_INNER]
    xbc_s = proj_s[:, COL_XBC:COL_XBC + CONV_DIM]
    xdt, bm, cm, decx, ypart, dech = _sample_pre(
        xbc_s, dtraw_s, sconv_t, cw, cb, dtb, alog, dsk_x, expand)
    bb = 8 if nb % 8 == 0 else nb
    attn_s, nk_s, nv_s = _attn_sample(
        sinks.reshape(N_HEADS, 1), q_s.reshape(nb, N_HEADS, HEAD_DIM), k_s, v_s,
        cache_k[0].reshape(nb, window, KV_DIM), cache_v[0].reshape(nb, window, KV_DIM), bb)
    ys, nssm_s = _ssd_sample(
        state_ssm[0].reshape(nb, SSM_INNER, SSM_STATE), xdt,
        bm.reshape(nb, SSM_GROUPS, SSM_STATE), cm.reshape(nb, SSM_GROUPS, SSM_STATE),
        decx, dech, ypart, z_s, gn, bb)
    y_sample = _dense_tail(xs, attn_s.reshape(nb, Q_DIM), ys, p_sample[0].reshape(nb, -1), wts,
                           nb).reshape(nb, 1, d)
    nconv_s = jnp.concatenate([state_conv[0][:, 1:], xbc_s[:, None, :]], axis=1)

    kv5 = lambda t: t.reshape(1, t.shape[0], window, N_KV_HEADS, HEAD_DIM)
    return (y_prompt, y_sample,
            kv5(nk_p), kv5(nv_p),
            nssm_p.reshape(1, bsz, SSM_HEADS, SSM_HEAD_DIM, SSM_STATE),
            nconv_p[None, :, 8 - (CONV_WIDTH - 1):, :],
            kv5(nk_s), kv5(nv_s),
            nssm_s.reshape(1, nb, SSM_HEADS, SSM_HEAD_DIM, SSM_STATE),
            nconv_s[None])
```

```python
import functools

import numpy as np
import jax
import jax.numpy as jnp
from jax import lax
from jax.experimental import pallas as pl
from jax.experimental.pallas import tpu as pltpu

F32 = jnp.float32
BF16 = jnp.bfloat16

HEAD_DIM = 64
N_HEADS = 16
N_KV_HEADS = 4
WINDOW = 128
ROPE_DIM = 16
ROPE_THETA = 500000.0
SSM_HEADS = 16
SSM_HEAD_DIM = 64
SSM_GROUPS = 4
SSM_STATE = 128
CONV_WIDTH = 4
SSD_CHUNK = 128
RMS_EPS = 1e-6
PAST_LEN = 16384
LOG2E = 1.4426950408889634

Q_DIM = N_HEADS * HEAD_DIM
KV_DIM = N_KV_HEADS * HEAD_DIM
SSM_INNER = SSM_HEADS * SSM_HEAD_DIM
BC_DIM = SSM_GROUPS * SSM_STATE
CONV_DIM = SSM_INNER + 2 * BC_DIM
LANES = 128
HEADS_PAD = LANES

COL_Q = 0
COL_K = Q_DIM
COL_V = COL_K + KV_DIM
COL_Z = COL_V + KV_DIM
COL_XBC = COL_Z + SSM_INNER
PROJ_DIM = COL_XBC + CONV_DIM
PIECE = 512

INPROJ_TN = Q_DIM + 2 * KV_DIM
Q_SCALE = HEAD_DIM ** -0.5 * LOG2E
FFN_TM, FFN_TH = 1024, 512
ATTN_BLOCKS = 8
SSD_CHUNKS = 8
VMEM_LIMIT = 56 * 1024 * 1024


def _cparams(n_axes):
    return pltpu.CompilerParams(
        dimension_semantics=("arbitrary",) * n_axes, vmem_limit_bytes=VMEM_LIMIT)


def _rms(x):
    return x * lax.rsqrt(jnp.mean(x * x, axis=-1, keepdims=True) + RMS_EPS)


def _sigmoid(x):
    return 1.0 / (1.0 + jnp.exp2(x * -LOG2E))


def _silu(x):
    return x * _sigmoid(x)


def _dot(a, b):
    return jnp.dot(a, b, preferred_element_type=F32)


def _dot_nt(a, b):
    return lax.dot_general(a, b, (((1,), (1,)), ((), ())), preferred_element_type=F32)


def _dot_tn(a, b):
    return lax.dot_general(a, b, (((0,), (0,)), ((), ())), preferred_element_type=F32)


def _split3(x):
    hi = x.astype(BF16)
    r = x - hi.astype(F32)
    mid = r.astype(BF16)
    lo = (r - mid.astype(F32)).astype(BF16)
    return hi, mid, lo


def _dot_exact_lhs01(m01, x):
    hi, mid, lo = _split3(x)
    return _dot(m01, hi) + _dot(m01, mid) + _dot(m01, lo)


def _dot_exact_rhs01(x, m01):
    hi, mid, lo = _split3(x)
    return _dot(hi, m01) + _dot(mid, m01) + _dot(lo, m01)


def _inproj_kernel(xa_ref, xb_ref, g_ref, w_ref, wdt_ref, tab_ref, o_ref, odt_ref, u_ref):
    j = pl.program_id(1)

    @pl.when(j == 0)
    def _():
        rp = xa_ref.shape[0]
        for p, x_ref in enumerate((xa_ref, xb_ref)):
            rows = slice(p * rp, (p + 1) * rp)
            u = (_rms(x_ref[...]) * g_ref[...]).astype(BF16)
            u_ref[rows, :] = u
            odt_ref[rows, :] = _dot(u, wdt_ref[...])
            res = _dot(u, w_ref[...])
            tab = tab_ref[rows, :]
            for c in range((Q_DIM + KV_DIM) // LANES):
                cols = slice(c * LANES, (c + 1) * LANES)
                r = _rope(res[:, cols], tab)
                o_ref[rows, cols] = r * Q_SCALE if c < Q_DIM // LANES else r
            o_ref[rows, Q_DIM + KV_DIM:] = res[:, Q_DIM + KV_DIM:]

    @pl.when(j > 0)
    def _():
        o_ref[...] = _dot(u_ref[...], w_ref[...])


def _inproj(x, g, w, wdt, tab, tm):
    t, d = x.shape
    n = PROJ_DIM
    tn = INPROJ_TN
    nt = tab.shape[0] // tm
    ni, nj = t // tm, n // tn
    early = lambda i, j: jnp.minimum(i + (j == nj - 1).astype(jnp.int32), ni - 1)
    return pl.pallas_call(
        _inproj_kernel,
        grid=(ni, nj),
        in_specs=[
            pl.BlockSpec((tm // 2, d), lambda i, j: (2 * i, 0)),
            pl.BlockSpec((tm // 2, d), lambda i, j: (2 * early(i, j) + 1, 0)),
            pl.BlockSpec((1, d), lambda i, j: (0, 0)),
            pl.BlockSpec((d, tn), lambda i, j: (0, j)),
            pl.BlockSpec((d, HEADS_PAD), lambda i, j: (0, 0)),
            pl.BlockSpec((tm, 3 * LANES), lambda i, j: (i % nt, 0)),
        ],
        out_specs=[
            pl.BlockSpec((tm, tn), lambda i, j: (i, j)),
            pl.BlockSpec((tm, HEADS_PAD), lambda i, j: (i, 0)),
        ],
        out_shape=[
            jax.ShapeDtypeStruct((t, n), F32),
            jax.ShapeDtypeStruct((t, HEADS_PAD), F32),
        ],
        scratch_shapes=[pltpu.VMEM((tm, d), BF16)],
        compiler_params=_cparams(2),
        name="inproj",
    )(x, x, g, w, wdt, tab)


def _rope_tables(pos):
    half = ROPE_DIM // 2
    inv = ROPE_THETA ** (-jnp.arange(half, dtype=F32) * (2.0 / ROPE_DIM))
    ang = pos.astype(F32)[:, None] * inv[None, :]
    cs = jnp.concatenate([jnp.cos(ang), jnp.sin(ang)], axis=1)
    expand = np.zeros((2 * half, 3 * LANES), np.float32)
    base = np.zeros((1, 3 * LANES), np.float32)
    for lane in range(LANES):
        m = lane % HEAD_DIM
        if m >= ROPE_DIM:
            base[0, lane] = 1.0
            continue
        expand[m % half, lane] = 1.0
        if m < half:
            expand[half + m, LANES + lane] = -1.0
        else:
            expand[m, 2 * LANES + lane] = 1.0
    cs3 = jnp.concatenate(_split3(cs), axis=1)
    expand3 = jnp.asarray(np.concatenate([expand] * 3, axis=0), BF16)
    return jnp.dot(cs3, expand3, preferred_element_type=F32) + base


def _rope(x, tab):
    half = ROPE_DIM // 2
    c, sa, sb = tab[:, :LANES], tab[:, LANES:2 * LANES], tab[:, 2 * LANES:]
    return x * c + pltpu.roll(x, LANES - half, 1) * sa + pltpu.roll(x, half, 1) * sb


def _softmax_fold(s, band, prev_bias, sink2):
    sp = s[:, :WINDOW] if prev_bias is None else s[:, :WINDOW] + prev_bias
    t = jnp.where(band, sp, s[:, WINDOW:])
    m = jnp.maximum(jnp.max(t, axis=-1, keepdims=True), sink2)
    e = jnp.exp2(t - m)
    den = jnp.sum(e, axis=-1, keepdims=True) + jnp.exp2(sink2 - m)
    p = jnp.concatenate([jnp.where(band, e, 0.0), jnp.where(band, 0.0, e)], axis=1)
    return p.astype(BF16), den


def _attn_prompt_kernel(sink_ref, q_ref, kc_ref, kp_ref, vc_ref, vp_ref,
                        o_ref, nk_ref, nv_ref, *, nq):
    i = pl.program_id(1)
    nsteps = pl.num_programs(1)
    w = WINDOW
    kcr = kc_ref[...]
    kpr = kp_ref[...]
    vc = vc_ref[...]

    @pl.when(i == nsteps - 1)
    def _():
        nk_ref[...] = kcr[(nq - 1) * w:]
        nv_ref[...] = vc[(nq - 1) * w:]

    kall = jnp.concatenate([kpr, kcr], axis=0)
    vall = jnp.concatenate([vp_ref[...], vc], axis=0)
    lo = lax.broadcasted_iota(jnp.int32, ((nq + 1) * w, LANES), 1) < HEAD_DIM
    lo_q = lax.broadcasted_iota(jnp.int32, (w, LANES), 1) < HEAD_DIM
    band = (lax.broadcasted_iota(jnp.int32, (w, w), 1) > lax.broadcasted_iota(jnp.int32, (w, w), 0))
    first_bias = jnp.where(i == 0, -jnp.inf, 0.0)

    for g in range(N_KV_HEADS):
        col, odd = g // 2, g % 2
        kg = kall[:, col * LANES:(col + 1) * LANES]
        vg = vall[:, col * LANES:(col + 1) * LANES]
        kg_sw = pltpu.roll(kg, HEAD_DIM, 1)
        vg_sw = pltpu.roll(vg, HEAD_DIM, 1)
        k_lo = jnp.where(lo, kg_sw if odd else kg, 0.0).astype(BF16)
        k_hi = jnp.where(lo, 0.0, kg if odd else kg_sw).astype(BF16)
        v_lo = jnp.where(lo, vg_sw if odd else vg, 0.0).astype(BF16)
        v_hi = jnp.where(lo, 0.0, vg if odd else vg_sw).astype(BF16)
        sinks2 = [sink_ref[4 * g + r] * LOG2E for r in range(4)]
        def scores(s, g=g, k_lo=k_lo, k_hi=k_hi):
            rows = slice(s * w, (s + 1) * w)
            keys = slice(s * w, (s + 2) * w)
            qst = jnp.concatenate([q_ref[rows, (2 * g) * LANES:(2 * g + 1) * LANES],
                                   q_ref[rows, (2 * g + 1) * LANES:(2 * g + 2) * LANES]],
                                  axis=0).astype(BF16)
            return _dot_nt(qst, k_lo[keys]), _dot_nt(qst, k_hi[keys])

        ahead = scores(0)
        for s in range(nq):
            rows = slice(s * w, (s + 1) * w)
            keys = slice(s * w, (s + 2) * w)
            pb = first_bias if s == 0 else None
            s_lo, s_hi = ahead
            if s + 1 < nq:
                ahead = scores(s + 1)
            e0, d0 = _softmax_fold(s_lo[:w], band, pb, sinks2[0])
            e1, d1 = _softmax_fold(s_hi[:w], band, pb, sinks2[1])
            e2, d2 = _softmax_fold(s_lo[w:], band, pb, sinks2[2])
            e3, d3 = _softmax_fold(s_hi[w:], band, pb, sinks2[3])
            p = jnp.concatenate([jnp.concatenate([e0, e1], axis=1),
                                 jnp.concatenate([e2, e3], axis=1)], axis=0)
            vcat = jnp.concatenate([v_lo[keys], v_hi[keys]], axis=0)
            o = _dot(p, vcat)
            oa = o[:w] / jnp.where(lo_q, d0, d1)
            ob = o[w:] / jnp.where(lo_q, d2, d3)
            o_ref[rows, (2 * g) * LANES:(2 * g + 1) * LANES] = oa.astype(o_ref.dtype)
            o_ref[rows, (2 * g + 1) * LANES:(2 * g + 2) * LANES] = ob.astype(o_ref.dtype)


def _attn_prompt(proj, sinks, nq):
    b, l, _ = proj.shape
    w = WINDOW
    nsteps = l // (nq * w)
    kcol, vcol = COL_K // KV_DIM, COL_V // KV_DIM
    prev = lambda bi, i: jnp.maximum(nq * i - 1, 0)
    return pl.pallas_call(
        functools.partial(_attn_prompt_kernel, nq=nq),
        grid=(b, nsteps),
        in_specs=[
            pl.BlockSpec(memory_space=pltpu.SMEM),
            pl.BlockSpec((None, nq * w, Q_DIM), lambda bi, i: (bi, i, COL_Q // Q_DIM)),
            pl.BlockSpec((None, nq * w, KV_DIM), lambda bi, i: (bi, i, kcol)),
            pl.BlockSpec((None, w, KV_DIM), lambda bi, i: (bi, prev(bi, i), kcol)),
            pl.BlockSpec((None, nq * w, KV_DIM), lambda bi, i: (bi, i, vcol)),
            pl.BlockSpec((None, w, KV_DIM), lambda bi, i: (bi, prev(bi, i), vcol)),
        ],
        out_specs=[
            pl.BlockSpec((None, nq * w, Q_DIM), lambda bi, i: (bi, i, 0)),
            pl.BlockSpec((None, w, KV_DIM), lambda bi, i: (bi, 0, 0)),
            pl.BlockSpec((None, w, KV_DIM), lambda bi, i: (bi, 0, 0)),
        ],
        out_shape=[
            jax.ShapeDtypeStruct((b, l, Q_DIM), BF16),
            jax.ShapeDtypeStruct((b, w, KV_DIM), F32),
            jax.ShapeDtypeStruct((b, w, KV_DIM), F32),
        ],
        compiler_params=_cparams(2),
        name="attn_prompt",
    )(sinks, proj, proj, proj, proj, proj)


def _softplus(v):
    return jnp.maximum(v, 0.0) + jnp.log1p(jnp.exp(-jnp.abs(v)))


def _head_expand(vals, ex2_ref):
    hi = vals.astype(BF16)
    mid = (vals - hi.astype(F32)).astype(BF16)
    return _dot(jnp.concatenate([hi, mid], axis=1), ex2_ref[...])


def _ssd_prompt_kernel(*refs, nsub):
    nz, nx = SSM_INNER // PIECE, CONV_DIM // PIECE
    z_refs, x_refs = refs[:nz], refs[nz:nz + nx]
    (dt_ref, cw_ref, cb_ref, dtb_ref, alog_ref, dsk_ref, gn_ref, ex2_ref,
     y_ref, nssm_ref, nconv_ref, state_ref, carry_ref) = refs[nz + nx:]
    i = pl.program_id(1)
    nc = pl.num_programs(1)
    q = SSD_CHUNK

    @pl.when(i == 0)
    def _():
        state_ref[...] = jnp.zeros_like(state_ref)
        carry_ref[...] = jnp.zeros_like(carry_ref)

    dtp = dt_ref[:q, :]
    for s in range(1, nsub):
        dtp = dtp + pltpu.roll(dt_ref[s * q:(s + 1) * q, :], s * SSM_HEADS, 1)
    dt = _softplus(dtp + dtb_ref[...])
    da = dt * (-jnp.exp(alog_ref[...]))
    tri = (lax.broadcasted_iota(jnp.int32, (q, q), 0) >= lax.broadcasted_iota(jnp.int32, (q, q), 1))
    cs = _dot_exact_lhs01(tri.astype(BF16), da)
    cs2 = cs * LOG2E
    cs2_t = cs2.T
    sc = dict(
        tri=tri, cs2=cs2, cs2_t=cs2_t, dt_t=dt.T,
        cdec_t=jnp.exp2(cs2_t[:, q - 1:q]),
        ecs_x=_head_expand(jnp.exp(cs), ex2_ref),
        wgt_x=_head_expand(dt * jnp.exp(cs[q - 1:q, :] - cs), ex2_ref))

    rows_of = lambda rs, sl: jnp.concatenate([r[sl, :] for r in rs], axis=1)
    ahead = _ssd_conv(0, rows_of, x_refs, cw_ref, cb_ref, carry_ref)
    for s in range(nsub):
        xc = ahead
        if s + 1 < nsub:
            ahead = _ssd_conv(s + 1, rows_of, x_refs, cw_ref, cb_ref, carry_ref)
        _ssd_chunk(s, sc, xc, rows_of, z_refs, dsk_ref, gn_ref, y_ref, state_ref)
    tail = rows_of(x_refs, slice(nsub * q - 8, nsub * q))
    carry_ref[...] = tail

    @pl.when(i == nc - 1)
    def _():
        nconv_ref[...] = tail
        nssm_ref[...] = state_ref[...]


def _ssd_conv(s, rows_of, x_refs, cw_ref, cb_ref, carry_ref):
    q = SSD_CHUNK
    x = rows_of(x_refs, slice(s * q, (s + 1) * q))
    prev = carry_ref[...] if s == 0 else rows_of(x_refs, slice(s * q - 8, s * q))
    row8 = lax.broadcasted_iota(jnp.int32, (8, CONV_DIM), 0)

    def shifted(v, k, before):
        r = pltpu.roll(v, k, 0)
        head = jnp.where(row8 < k, pltpu.roll(before, k, 0), r[:8])
        return jnp.concatenate([head, r[8:]], axis=0)

    assert CONV_WIDTH == 4
    w0, w1, w2, w3 = (cw_ref[k:k + 1, :] for k in range(CONV_WIDTH))
    x1 = shifted(x, 1, prev)
    p = x * w1 + x1 * w0
    p_prev = prev * w1 + pltpu.roll(prev, 1, 0) * w0
    conv = x * w3 + cb_ref[...] + x1 * w2 + shifted(p, 2, p_prev)
    return _silu(conv)


def _ssd_chunk(s, sc, xc, rows_of, z_refs, dsk_ref, gn_ref, y_ref, state_ref):
    q = SSD_CHUNK
    tri, cs2, cs2_t, dt_t, cdec_t = sc["tri"], sc["cs2"], sc["cs2_t"], sc["dt_t"], sc["cdec_t"]
    hoff = s * SSM_HEADS
    xoff = s * SSM_INNER
    trows = slice(s * q, (s + 1) * q)
    z = rows_of(z_refs, trows)
    xs = xc[:, :SSM_INNER]
    bm = xc[:, SSM_INNER:SSM_INNER + BC_DIM]
    cm = xc[:, SSM_INNER + BC_DIM:]

    lo = lax.broadcasted_iota(jnp.int32, (q, LANES), 1) < SSM_HEAD_DIM

    hpg = SSM_HEADS // SSM_GROUPS
    gw = hpg * SSM_HEAD_DIM
    for g in range(SSM_GROUPS):
        bg = bm[:, g * SSM_STATE:(g + 1) * SSM_STATE].astype(BF16)
        cg = cm[:, g * SSM_STATE:(g + 1) * SSM_STATE].astype(BF16)
        cb = _dot_nt(cg, bg)
        st = state_ref[g * gw:(g + 1) * gw, :]
        yoff = _dot_nt(cg, st.astype(BF16))
        ys = []
        for pr in range(2):
            pair = 2 * g + pr
            ms = []
            for h in (hoff + 2 * pair, hoff + 2 * pair + 1):
                diff = cs2[:, h:h + 1] - cs2_t[h:h + 1, :]
                lm = jnp.exp2(jnp.where(tri, diff, -jnp.inf))
                ms.append((cb * lm * dt_t[h:h + 1, :]).astype(BF16))
            xp = xs[:, pair * LANES:(pair + 1) * LANES]
            x2 = jnp.concatenate([jnp.where(lo, xp, 0.0), jnp.where(lo, 0.0, xp)],
                                 axis=0).astype(BF16)
            yd = _dot(jnp.concatenate(ms, axis=1), x2)
            yo = yoff[:, pr * LANES:(pr + 1) * LANES] * sc["ecs_x"][
                :, xoff + pair * LANES:xoff + (pair + 1) * LANES]
            ys.append(yd + yo + xp * dsk_ref[:, pair * LANES:(pair + 1) * LANES])
        yg = jnp.concatenate(ys, axis=1)
        wx = xs[:, g * gw:(g + 1) * gw] * sc["wgt_x"][:, xoff + g * gw:xoff + (g + 1) * gw]
        s_new = _dot_tn(wx.astype(BF16), bg)
        for r in range(hpg):
            h = hoff + hpg * g + r
            rows = slice(g * gw + r * SSM_HEAD_DIM, g * gw + (r + 1) * SSM_HEAD_DIM)
            state_ref[rows, :] = (st[r * SSM_HEAD_DIM:(r + 1) * SSM_HEAD_DIM, :] * cdec_t[h:h + 1, :]
                                  + s_new[r * SSM_HEAD_DIM:(r + 1) * SSM_HEAD_DIM, :])
        hg = yg * _silu(z[:, g * gw:(g + 1) * gw])
        y_ref[trows, g * gw:(g + 1) * gw] = (_rms(hg) * gn_ref[:, g * gw:(g + 1) * gw]
                                            ).astype(y_ref.dtype)


def _ssd_prompt(proj, dtraw, conv_w, conv_b, dt_bias_t, a_log_t, d_skip_x, norm_g, nsub):
    assert nsub * SSM_HEADS <= LANES
    src = np.arange(nsub * SSM_INNER) // SSM_HEAD_DIM
    ex = (np.arange(LANES)[:, None] == src[None, :]).astype(np.float32)
    ex2 = jnp.asarray(np.concatenate([ex, ex], axis=0), BF16)
    b, l, _ = proj.shape
    q = SSD_CHUNK * nsub
    nc = l // q
    const = lambda bi, i: (0, 0)
    piece = lambda c: pl.BlockSpec((None, q, PIECE), lambda bi, i: (bi, i, c))
    n_pieces = (SSM_INNER + CONV_DIM) // PIECE
    return pl.pallas_call(
        functools.partial(_ssd_prompt_kernel, nsub=nsub),
        grid=(b, nc),
        in_specs=[piece(COL_Z // PIECE + c) for c in range(SSM_INNER // PIECE)] + [
            piece(COL_XBC // PIECE + c) for c in range(CONV_DIM // PIECE)] + [
            pl.BlockSpec((None, q, HEADS_PAD), lambda bi, i: (bi, i, 0)),
            pl.BlockSpec((CONV_WIDTH, CONV_DIM), const),
            pl.BlockSpec((1, CONV_DIM), const),
            pl.BlockSpec((1, HEADS_PAD), const),
            pl.BlockSpec((1, HEADS_PAD), const),
            pl.BlockSpec((1, SSM_INNER), const),
            pl.BlockSpec((1, SSM_INNER), const),
            pl.BlockSpec((2 * LANES, nsub * SSM_INNER), const),
        ],
        out_specs=[
            pl.BlockSpec((None, q, SSM_INNER), lambda bi, i: (bi, i, 0)),
            pl.BlockSpec((None, SSM_INNER, SSM_STATE), lambda bi, i: (bi, 0, 0)),
            pl.BlockSpec((None, 8, CONV_DIM), lambda bi, i: (bi, 0, 0)),
        ],
        out_shape=[
            jax.ShapeDtypeStruct((b, l, SSM_INNER), BF16),
            jax.ShapeDtypeStruct((b, SSM_INNER, SSM_STATE), F32),
            jax.ShapeDtypeStruct((b, 8, CONV_DIM), F32),
        ],
        scratch_shapes=[pltpu.VMEM((SSM_INNER, SSM_STATE), F32),
                        pltpu.VMEM((8, CONV_DIM), F32)],
        compiler_params=_cparams(2),
        name="ssd_prompt",
    )(*([proj] * n_pieces), dtraw, conv_w, conv_b, dt_bias_t, a_log_t, d_skip_x, norm_g, ex2)


def _outproj_kernel(x_ref, a_ref, y_ref, wa_ref, wy_ref, o_ref):
    o_ref[...] = (x_ref[...] + _dot(a_ref[...].astype(BF16), wa_ref[...])
                  + _dot(y_ref[...].astype(BF16), wy_ref[...]))


def _outproj(x, attn, y, w_out, tm):
    t, d = x.shape
    half = w_out.shape[0] // 2
    return pl.pallas_call(
        _outproj_kernel,
        grid=(t // tm,),
        in_specs=[
            pl.BlockSpec((tm, d), lambda i: (i, 0)),
            pl.BlockSpec((tm, half), lambda i: (i, 0)),
            pl.BlockSpec((tm, half), lambda i: (i, 0)),
            pl.BlockSpec((half, d), lambda i: (0, 0)),
            pl.BlockSpec((half, d), lambda i: (1, 0)),
        ],
        out_specs=pl.BlockSpec((tm, d), lambda i: (i, 0)),
        out_shape=jax.ShapeDtypeStruct((t, d), F32),
        compiler_params=_cparams(1),
        name="outproj",
    )(x, attn, y, w_out, w_out)


def _ffn_kernel(h_ref, g_ref, wgu_ref, wd_ref, o_ref, f_ref):
    d = o_ref.shape[1]
    th = wd_ref.shape[0]
    tn = min(d, 512)

    def add_delta(f, base_ref):
        gu = _dot(f, wgu_ref[...])
        gate = gu[:, :th]
        hid = (gate * (1.0 / (1.0 + jnp.exp(-gate))) * gu[:, th:]).astype(BF16)
        for c in range(d // tn):
            cols = slice(c * tn, (c + 1) * tn)
            o_ref[:, cols] = base_ref[:, cols] + _dot(hid, wd_ref[:, cols])

    @pl.when(pl.program_id(1) == 0)
    def _():
        f = (_rms(h_ref[...]) * g_ref[...]).astype(BF16)
        f_ref[...] = f
        add_delta(f, h_ref)

    @pl.when(pl.program_id(1) > 0)
    def _():
        add_delta(f_ref[...], o_ref)


def _ffn(h, g, wgu, wd, tm):
    t, d = h.shape
    nj, _, th2 = wgu.shape
    th = th2 // 2
    return pl.pallas_call(
        _ffn_kernel,
        grid=(t // tm, nj),
        in_specs=[
            pl.BlockSpec((tm, d), lambda i, j: (i, 0)),
            pl.BlockSpec((1, d), lambda i, j: (0, 0)),
            pl.BlockSpec((None, d, th2), lambda i, j: (j, 0, 0)),
            pl.BlockSpec((th, d), lambda i, j: (j, 0)),
        ],
        out_specs=pl.BlockSpec((tm, d), lambda i, j: (i, 0)),
        out_shape=jax.ShapeDtypeStruct((t, d), F32),
        scratch_shapes=[pltpu.VMEM((tm, d), BF16)],
        compiler_params=_cparams(2),
        name="ffn",
    )(h, g, wgu, wd)


def _ple_kernel(h_ref, p_ref, gp_ref, wp_ref, wg_ref, gf_ref, o_ref, *, tn, row_parts):
    tm, d = h_ref.shape
    rp = tm // row_parts
    for r in range(row_parts):
        rows = slice(r * rp, (r + 1) * rp)
        n = (_rms(h_ref[rows, :]) * gp_ref[...]).astype(BF16)
        pb = p_ref[rows, :].astype(BF16)
        ss = jnp.zeros((rp, 1), F32)
        for c in range(d // tn):
            cols = slice(c * tn, (c + 1) * tn)
            gate = _dot(n, wg_ref[:, cols])
            h3 = h_ref[rows, cols] + _dot(pb, wp_ref[:, cols]) * _sigmoid(gate)
            o_ref[rows, cols] = h3
            ss = ss + jnp.sum(h3 * h3, axis=-1, keepdims=True)
        inv = lax.rsqrt(ss * (1.0 / d) + RMS_EPS)
        o_ref[rows, :] = o_ref[rows, :] * inv * gf_ref[...]


def _ple_final(h, p, g_ple, w_ple, w_gate, g_final, tm):
    t, d = h.shape
    pd = p.shape[1]
    const = lambda i: (0, 0)
    resident = dict(pipeline_mode=pl.Buffered(1))
    return pl.pallas_call(
        functools.partial(_ple_kernel, tn=512, row_parts=max(tm // 256, 1)),
        grid=(t // tm,),
        in_specs=[
            pl.BlockSpec((tm, d), lambda i: (i, 0)),
            pl.BlockSpec((tm, pd), lambda i: (i, 0)),
            pl.BlockSpec((1, d), const),
            pl.BlockSpec((pd, d), const, **resident),
            pl.BlockSpec((d, d), const, **resident),
            pl.BlockSpec((1, d), const),
        ],
        out_specs=pl.BlockSpec((tm, d), lambda i: (i, 0)),
        out_shape=jax.ShapeDtypeStruct((t, d), F32),
        compiler_params=_cparams(1),
        name="ple_final",
    )(h, p, g_ple, w_ple, w_gate, g_final)


def _sample_pre_kernel(x_ref, dt_ref, sc_ref, cw_ref, cb_ref, dtb_ref,
                       alog_ref, dsk_ref, exp_ref,
                       xdt_ref, b_ref, c_ref, dec_ref, yp_ref, dech_ref):
    conv = x_ref[...] * cw_ref[CONV_WIDTH - 1:CONV_WIDTH, :] + cb_ref[...]
    for k in range(CONV_WIDTH - 1):
        conv = conv + sc_ref[k] * cw_ref[k:k + 1, :]
    xc = _silu(conv)
    xs = xc[:, :SSM_INNER]
    bm = xc[:, SSM_INNER:SSM_INNER + BC_DIM]
    cm = xc[:, SSM_INNER + BC_DIM:]
    b_ref[...] = bm
    c_ref[...] = cm
    dt = _softplus(dt_ref[...] + dtb_ref[...])
    dec = jnp.exp(dt * (-jnp.exp(alog_ref[...])))
    ex = exp_ref[...]
    dtx = _dot_exact_rhs01(dt, ex)
    dec_ref[...] = _dot_exact_rhs01(dec, ex)
    dech_ref[...] = dec
    xdt = xs * dtx
    xdt_ref[...] = xdt
    gw = SSM_INNER // SSM_GROUPS
    cbs = []
    for g in range(SSM_GROUPS):
        prod = cm[:, g * SSM_STATE:(g + 1) * SSM_STATE] * bm[:, g * SSM_STATE:(g + 1) * SSM_STATE]
        cbs.append(jnp.broadcast_to(jnp.sum(prod, axis=-1, keepdims=True), (prod.shape[0], gw)))
    yp_ref[...] = xdt * jnp.concatenate(cbs, axis=1) + xs * dsk_ref[...]


def _sample_pre(xbc, dtraw, sconv_t, conv_w, conv_b, dt_bias, a_log, d_skip_x, expand):
    nb = xbc.shape[0]
    shapes = [(nb, SSM_INNER), (nb, BC_DIM), (nb, BC_DIM),
              (nb, SSM_INNER), (nb, SSM_INNER), (nb, HEADS_PAD)]
    return pl.pallas_call(
        _sample_pre_kernel,
        out_shape=[jax.ShapeDtypeStruct(s, F32) for s in shapes],
        compiler_params=pltpu.CompilerParams(vmem_limit_bytes=VMEM_LIMIT),
        name="sample_pre",
    )(xbc, dtraw, sconv_t, conv_w, conv_b, dt_bias, a_log, d_skip_x, expand)


def _attn_sample_kernel(sink_ref, q_ref, kn_ref, vn_ref, ck_ref, cv_ref, o_ref, nk_ref, nv_ref, *, bb):
    w = WINDOW
    row = lax.broadcasted_iota(jnp.int32, (w, KV_DIM), 0)
    hrow = lax.broadcasted_iota(jnp.int32, (N_HEADS, KV_DIM), 0) // (N_HEADS // N_KV_HEADS)
    hgrp = lax.broadcasted_iota(jnp.int32, (N_HEADS, KV_DIM), 1) // HEAD_DIM
    own = hrow == hgrp
    sink = sink_ref[...] * LOG2E
    kks, vvs, scores, probs, dens = [], [], [], [], []
    for b in range(bb):
        kk = jnp.where(row == w - 1, kn_ref[b:b + 1, :], pltpu.roll(ck_ref[b], w - 1, 0))
        vv = jnp.where(row == w - 1, vn_ref[b:b + 1, :], pltpu.roll(cv_ref[b], w - 1, 0))
        nk_ref[b] = kk
        nv_ref[b] = vv
        kks.append(kk.astype(BF16))
        vvs.append(vv.astype(BF16))
    for b in range(bb):
        qb = q_ref[b]
        qrow = jnp.where(own, jnp.concatenate([qb] * N_KV_HEADS, axis=1), 0.0)
        scores.append(_dot_nt(qrow.astype(BF16), kks[b]))
    for b in range(bb):
        m = jnp.maximum(jnp.max(scores[b], axis=-1, keepdims=True), sink)
        e = jnp.exp2(scores[b] - m)
        dens.append(jnp.sum(e, axis=-1, keepdims=True) + jnp.exp2(sink - m))
        probs.append(e.astype(BF16))
    outs = [jnp.where(own, _dot(probs[b], vvs[b]), 0.0) for b in range(bb)]
    for b in range(bb):
        o = outs[b][:, :HEAD_DIM]
        for g in range(1, N_KV_HEADS):
            o = o + outs[b][:, g * HEAD_DIM:(g + 1) * HEAD_DIM]
        o_ref[b] = o / dens[b]


def _attn_sample(sinks_col, q3, knew, vnew, cache_k, cache_v, bb):
    nb = q3.shape[0]
    w = WINDOW
    return pl.pallas_call(
        functools.partial(_attn_sample_kernel, bb=bb),
        grid=(nb // bb,),
        in_specs=[
            pl.BlockSpec((N_HEADS, 1), lambda i: (0, 0)),
            pl.BlockSpec((bb, N_HEADS, HEAD_DIM), lambda i: (i, 0, 0)),
            pl.BlockSpec((bb, KV_DIM), lambda i: (i, 0)),
            pl.BlockSpec((bb, KV_DIM), lambda i: (i, 0)),
            pl.BlockSpec((bb, w, KV_DIM), lambda i: (i, 0, 0)),
            pl.BlockSpec((bb, w, KV_DIM), lambda i: (i, 0, 0)),
        ],
        out_specs=[
            pl.BlockSpec((bb, N_HEADS, HEAD_DIM), lambda i: (i, 0, 0)),
            pl.BlockSpec((bb, w, KV_DIM), lambda i: (i, 0, 0)),
            pl.BlockSpec((bb, w, KV_DIM), lambda i: (i, 0, 0)),
        ],
        out_shape=[
            jax.ShapeDtypeStruct((nb, N_HEADS, HEAD_DIM), F32),
            jax.ShapeDtypeStruct((nb, w, KV_DIM), F32),
            jax.ShapeDtypeStruct((nb, w, KV_DIM), F32),
        ],
        compiler_params=_cparams(1),
        name="attn_sample",
    )(sinks_col, q3, knew, vnew, cache_k, cache_v)


def _ssd_sample_kernel(st_ref, xdt_ref, b_ref, c_ref, dec_ref, dech_ref, yp_ref, z_ref, gn_ref,
                       y_ref, ns_ref, *, bb):
    gw = SSM_INNER // SSM_GROUPS
    grow = lax.broadcasted_iota(jnp.int32, (8, SSM_INNER), 0)
    glane = lax.broadcasted_iota(jnp.int32, (8, SSM_INNER), 1) // gw
    own = grow == glane
    pad = jnp.zeros((8 - SSM_GROUPS, SSM_STATE), F32)
    yoffs = []
    for b in range(bb):
        st = st_ref[b]
        cmat = jnp.concatenate([c_ref[b], pad], axis=0).astype(BF16)
        bmat = jnp.concatenate([b_ref[b], pad], axis=0).astype(BF16)
        r = _dot_nt(cmat, st.astype(BF16))
        yoffs.append(jnp.sum(jnp.where(own, r, 0.0), axis=0, keepdims=True))
        amat = jnp.where(own, jnp.broadcast_to(xdt_ref[b:b + 1, :], (8, SSM_INNER)), 0.0)
        outer = _dot_tn(amat.astype(BF16), bmat)
        for h in range(SSM_HEADS):
            rows = slice(h * SSM_HEAD_DIM, (h + 1) * SSM_HEAD_DIM)
            ns_ref[b, rows, :] = st[rows, :] * dech_ref[b:b + 1, h:h + 1] + outer[rows, :]
    y = yp_ref[...] + jnp.concatenate(yoffs, axis=0) * dec_ref[...]
    hg = y * _silu(z_ref[...])
    outs = []
    for g in range(SSM_GROUPS):
        outs.append(_rms(hg[:, g * gw:(g + 1) * gw]))
    y_ref[...] = jnp.concatenate(outs, axis=1) * gn_ref[...]


def _ssd_sample(state, xdt, b3, c3, decx, dech, ypart, z, norm_g, bb):
    nb = state.shape[0]
    row = lambda i: (i, 0)
    return pl.pallas_call(
        functools.partial(_ssd_sample_kernel, bb=bb),
        grid=(nb // bb,),
        in_specs=[
            pl.BlockSpec((bb, SSM_INNER, SSM_STATE), lambda i: (i, 0, 0)),
            pl.BlockSpec((bb, SSM_INNER), row),
            pl.BlockSpec((bb, SSM_GROUPS, SSM_STATE), lambda i: (i, 0, 0)),
            pl.BlockSpec((bb, SSM_GROUPS, SSM_STATE), lambda i: (i, 0, 0)),
            pl.BlockSpec((bb, SSM_INNER), row),
            pl.BlockSpec((bb, HEADS_PAD), row),
            pl.BlockSpec((bb, SSM_INNER), row),
            pl.BlockSpec((bb, SSM_INNER), row),
            pl.BlockSpec((1, SSM_INNER), lambda i: (0, 0)),
        ],
        out_specs=[
            pl.BlockSpec((bb, SSM_INNER), row),
            pl.BlockSpec((bb, SSM_INNER, SSM_STATE), lambda i: (i, 0, 0)),
        ],
        out_shape=[
            jax.ShapeDtypeStruct((nb, SSM_INNER), F32),
            jax.ShapeDtypeStruct((nb, SSM_INNER, SSM_STATE), F32),
        ],
        compiler_params=_cparams(1),
        name="ssd_sample",
    )(state, xdt, b3, c3, decx, dech, ypart, z, norm_g)


def _row_tile(t, want):
    return want if t % want == 0 else t


def _dense_tail(x, attn, y, p, wts, tm):
    h1 = _outproj(x, attn, y, wts["w_out"], tm)
    h2 = _ffn(h1, wts["g_ffn"], wts["w_gate_up"], wts["w_down"], _row_tile(x.shape[0], FFN_TM))
    return _ple_final(h2, p, wts["g_ple"], wts["w_ple"], wts["w_ple_gate"], wts["g_final"],
                      _row_tile(x.shape[0], 1024))


def kernel(x_prompt, x_sample, cache_k, cache_v, state_ssm, state_conv, p_prompt, p_sample, w_in, conv_w, conv_b, dt_bias, a_log, d_skip, ssm_norm_g, attn_sinks, w_out, g_mix, g_ffn, w_ffn_gate, w_ffn_up, w_ffn_down, g_ple, w_ple, w_ple_gate, g_final):
    depth = w_in.shape[0]
    assert depth == 1, "single-layer step only"
    bsz, seq, d = x_prompt.shape
    nb, dseq, _ = x_sample.shape
    assert dseq == 1 and seq % SSD_CHUNK == 0 and seq % WINDOW == 0

    w = w_in[0]
    w_main = w.astype(BF16)
    w_dt = jnp.pad(w[:, PROJ_DIM:], ((0, 0), (0, HEADS_PAD - SSM_HEADS))).astype(BF16)
    padh = lambda v: jnp.pad(v, (0, HEADS_PAD - SSM_HEADS)).reshape(1, HEADS_PAD)
    tileh = lambda v: jnp.tile(v, HEADS_PAD // SSM_HEADS).reshape(1, HEADS_PAD)
    dtb, alog = padh(dt_bias[0]), padh(a_log[0])
    dsk_x = jnp.repeat(d_skip[0], SSM_HEAD_DIM).reshape(1, SSM_INNER)
    gn = ssm_norm_g[0].reshape(1, SSM_INNER)
    cw, cb = conv_w[0], conv_b[0].reshape(1, CONV_DIM)
    wts = dict(
        w_out=w_out[0].astype(BF16), g_ffn=g_ffn[0].reshape(1, d),
        w_gate_up=jnp.concatenate(
            [w_ffn_gate[0].astype(BF16).reshape(d, -1, FFN_TH),
             w_ffn_up[0].astype(BF16).reshape(d, -1, FFN_TH)], axis=2).transpose(1, 0, 2),
        w_down=w_ffn_down[0].astype(BF16), g_ple=g_ple[0].reshape(1, d),
        w_ple=w_ple[0].astype(BF16), w_ple_gate=w_ple_gate[0].astype(BF16),
        g_final=g_final.reshape(1, d))
    gmix = g_mix[0].reshape(1, d)
    sinks = attn_sinks[0]

    tp = bsz * seq
    xp = x_prompt.reshape(tp, d)
    tm_in = _row_tile(seq, 1024)
    proj, dtraw = _inproj(xp, gmix, w_main, w_dt, _rope_tables(jnp.arange(seq)), tm_in)
    proj3 = proj.reshape(bsz, seq, PROJ_DIM)
    attn, nk_p, nv_p = _attn_prompt(proj3, sinks, ATTN_BLOCKS)
    yp, nssm_p, nconv_p = _ssd_prompt(proj3, dtraw.reshape(bsz, seq, HEADS_PAD), cw, cb,
                                      tileh(dt_bias[0]), tileh(a_log[0]), dsk_x, gn, SSD_CHUNKS)
    tm = _row_tile(tp, 512)
    y_prompt = _dense_tail(xp, attn.reshape(tp, Q_DIM), yp.reshape(tp, SSM_INNER),
                           p_prompt[0].reshape(tp, -1), wts, tm).reshape(bsz, seq, d)

    xs = x_sample.reshape(nb, d)
    tab_s = _rope_tables(jnp.full((nb,), PAST_LEN, jnp.int32))
    proj_s, dtraw_s = _inproj(xs, gmix, w_main, w_dt, tab_s, nb)
    window = cache_k.shape[2]
    sconv_t = jnp.transpose(state_conv[0], (1, 0, 2))
    expand = (jnp.arange(HEADS_PAD)[:, None] == (jnp.arange(SSM_INNER) // SSM_HEAD_DIM)[None, :]
              ).astype(BF16)
    q_s = proj_s[:, COL_Q:COL_Q + Q_DIM]
    k_s = proj_s[:, COL_K:COL_K + KV_DIM]
    v_s = proj_s[:, COL_V:COL_V + KV_DIM]
    z_s = proj_s[:, COL_Z:COL_Z + SSM_INNER]
    xbc_s = proj_s[:, COL_XBC:COL_XBC + CONV_DIM]
    xdt, bm, cm, decx, ypart, dech = _sample_pre(
        xbc_s, dtraw_s, sconv_t, cw, cb, dtb, alog, dsk_x, expand)
    bb = 8 if nb % 8 == 0 else nb
    attn_s, nk_s, nv_s = _attn_sample(
        sinks.reshape(N_HEADS, 1), q_s.reshape(nb, N_HEADS, HEAD_DIM), k_s, v_s,
        cache_k[0].reshape(nb, window, KV_DIM), cache_v[0].reshape(nb, window, KV_DIM), bb)
    ys, nssm_s = _ssd_sample(
        state_ssm[0].reshape(nb, SSM_INNER, SSM_STATE), xdt,
        bm.reshape(nb, SSM_GROUPS, SSM_STATE), cm.reshape(nb, SSM_GROUPS, SSM_STATE),
        decx, dech, ypart, z_s, gn, bb)
    y_sample = _dense_tail(xs, attn_s.reshape(nb, Q_DIM), ys, p_sample[0].reshape(nb, -1), wts,
                           nb).reshape(nb, 1, d)
    nconv_s = jnp.concatenate([state_conv[0][:, 1:], xbc_s[:, None, :]], axis=1)

    kv5 = lambda t: t.reshape(1, t.shape[0], window, N_KV_HEADS, HEAD_DIM)
    return (y_prompt, y_sample,
            kv5(nk_p), kv5(nv_p),
            nssm_p.reshape(1, bsz, SSM_HEADS, SSM_HEAD_DIM, SSM_STATE),
            nconv_p[None, :, 8 - (CONV_WIDTH - 1):, :],
            kv5(nk_s), kv5(nv_s),
            nssm_s.reshape(1, nb, SSM_HEADS, SSM_HEAD_DIM, SSM_STATE),
            nconv_s[None])
```

```python
import functools

import numpy as np
import jax
import jax.numpy as jnp
from jax import lax
from jax.experimental import pallas as pl
from jax.experimental.pallas import tpu as pltpu

F32 = jnp.float32
BF16 = jnp.bfloat16

HEAD_DIM = 64
N_HEADS = 16
N_KV_HEADS = 4
WINDOW = 128
ROPE_DIM = 16
ROPE_THETA = 500000.0
SSM_HEADS = 16
SSM_HEAD_DIM = 64
SSM_GROUPS = 4
SSM_STATE = 128
CONV_WIDTH = 4
SSD_CHUNK = 128
RMS_EPS = 1e-6
PAST_LEN = 16384
LOG2E = 1.4426950408889634

Q_DIM = N_HEADS * HEAD_DIM
KV_DIM = N_KV_HEADS * HEAD_DIM
SSM_INNER = SSM_HEADS * SSM_HEAD_DIM
BC_DIM = SSM_GROUPS * SSM_STATE
CONV_DIM = SSM_INNER + 2 * BC_DIM
LANES = 128
HEADS_PAD = LANES

COL_Q = 0
COL_K = Q_DIM
COL_V = COL_K + KV_DIM
COL_Z = COL_V + KV_DIM
COL_XBC = COL_Z + SSM_INNER
PROJ_DIM = COL_XBC + CONV_DIM
PIECE = 512

INPROJ_TN = Q_DIM + 2 * KV_DIM
Q_SCALE = HEAD_DIM ** -0.5 * LOG2E
FFN_TM, FFN_TH = 1024, 512
CONV_PITCH = 3
ATTN_BLOCKS = 8
SSD_CHUNKS = 8
VMEM_LIMIT = 56 * 1024 * 1024


def _cparams(n_axes):
    return pltpu.CompilerParams(
        dimension_semantics=("arbitrary",) * n_axes, vmem_limit_bytes=VMEM_LIMIT)


def _rms(x):
    return x * lax.rsqrt(jnp.mean(x * x, axis=-1, keepdims=True) + RMS_EPS)


def _sigmoid(x):
    return 1.0 / (1.0 + jnp.exp2(x * -LOG2E))


def _silu(x):
    return x * _sigmoid(x)


def _dot(a, b):
    return jnp.dot(a, b, preferred_element_type=F32)


def _dot_nt(a, b):
    return lax.dot_general(a, b, (((1,), (1,)), ((), ())), preferred_element_type=F32)


def _dot_tn(a, b):
    return lax.dot_general(a, b, (((0,), (0,)), ((), ())), preferred_element_type=F32)


def _split3(x):
    hi = x.astype(BF16)
    r = x - hi.astype(F32)
    mid = r.astype(BF16)
    lo = (r - mid.astype(F32)).astype(BF16)
    return hi, mid, lo


def _dot_exact_lhs01(m01, x):
    hi, mid, lo = _split3(x)
    return _dot(m01, hi) + _dot(m01, mid) + _dot(m01, lo)


def _dot_exact_rhs01(x, m01):
    hi, mid, lo = _split3(x)
    return _dot(hi, m01) + _dot(mid, m01) + _dot(lo, m01)


def _inproj_kernel(xa_ref, xb_ref, g_ref, w_ref, wdt_ref, tab_ref, o_ref, odt_ref, u_ref):
    j = pl.program_id(1)

    @pl.when(j == 0)
    def _():
        rp = xa_ref.shape[0]
        for p, x_ref in enumerate((xa_ref, xb_ref)):
            rows = slice(p * rp, (p + 1) * rp)
            u = (_rms(x_ref[...]) * g_ref[...]).astype(BF16)
            u_ref[rows, :] = u
            odt_ref[rows, :] = _dot(u, wdt_ref[...])
            res = _dot(u, w_ref[...])
            tab = tab_ref[rows, :]
            for c in range((Q_DIM + KV_DIM) // LANES):
                cols = slice(c * LANES, (c + 1) * LANES)
                r = _rope(res[:, cols], tab)
                o_ref[rows, cols] = r * Q_SCALE if c < Q_DIM // LANES else r
            o_ref[rows, Q_DIM + KV_DIM:] = res[:, Q_DIM + KV_DIM:]

    @pl.when(j > 0)
    def _():
        o_ref[...] = _dot(u_ref[...], w_ref[...])


def _inproj(x, g, w, wdt, tab, tm):
    t, d = x.shape
    n = PROJ_DIM
    tn = INPROJ_TN
    nt = tab.shape[0] // tm
    ni, nj = t // tm, n // tn
    early = lambda i, j: jnp.minimum(i + (j == nj - 1).astype(jnp.int32), ni - 1)
    return pl.pallas_call(
        _inproj_kernel,
        grid=(ni, nj),
        in_specs=[
            pl.BlockSpec((tm // 2, d), lambda i, j: (2 * i, 0)),
            pl.BlockSpec((tm // 2, d), lambda i, j: (2 * early(i, j) + 1, 0)),
            pl.BlockSpec((1, d), lambda i, j: (0, 0)),
            pl.BlockSpec((d, tn), lambda i, j: (0, j)),
            pl.BlockSpec((d, HEADS_PAD), lambda i, j: (0, 0)),
            pl.BlockSpec((tm, 3 * LANES), lambda i, j: (i % nt, 0)),
        ],
        out_specs=[
            pl.BlockSpec((tm, tn), lambda i, j: (i, j)),
            pl.BlockSpec((tm, HEADS_PAD), lambda i, j: (i, 0)),
        ],
        out_shape=[
            jax.ShapeDtypeStruct((t, n), F32),
            jax.ShapeDtypeStruct((t, HEADS_PAD), F32),
        ],
        scratch_shapes=[pltpu.VMEM((tm, d), BF16)],
        compiler_params=_cparams(2),
        name="inproj",
    )(x, x, g, w, wdt, tab)


def _rope_tables(pos):
    half = ROPE_DIM // 2
    inv = ROPE_THETA ** (-jnp.arange(half, dtype=F32) * (2.0 / ROPE_DIM))
    ang = pos.astype(F32)[:, None] * inv[None, :]
    cs = jnp.concatenate([jnp.cos(ang), jnp.sin(ang)], axis=1)
    expand = np.zeros((2 * half, 3 * LANES), np.float32)
    base = np.zeros((1, 3 * LANES), np.float32)
    for lane in range(LANES):
        m = lane % HEAD_DIM
        if m >= ROPE_DIM:
            base[0, lane] = 1.0
            continue
        expand[m % half, lane] = 1.0
        if m < half:
            expand[half + m, LANES + lane] = -1.0
        else:
            expand[m, 2 * LANES + lane] = 1.0
    cs3 = jnp.concatenate(_split3(cs), axis=1)
    expand3 = jnp.asarray(np.concatenate([expand] * 3, axis=0), BF16)
    return jnp.dot(cs3, expand3, preferred_element_type=F32) + base


def _rope(x, tab):
    half = ROPE_DIM // 2
    c, sa, sb = tab[:, :LANES], tab[:, LANES:2 * LANES], tab[:, 2 * LANES:]
    return x * c + pltpu.roll(x, LANES - half, 1) * sa + pltpu.roll(x, half, 1) * sb


def _softmax_fold(s, band, prev_bias, sink2):
    sp = s[:, :WINDOW] if prev_bias is None else s[:, :WINDOW] + prev_bias
    t = jnp.where(band, sp, s[:, WINDOW:])
    m = jnp.maximum(jnp.max(t, axis=-1, keepdims=True), sink2)
    e = jnp.exp2(t - m)
    den = jnp.sum(e, axis=-1, keepdims=True) + jnp.exp2(sink2 - m)
    p = jnp.concatenate([jnp.where(band, e, 0.0), jnp.where(band, 0.0, e)], axis=1)
    return p.astype(BF16), den


def _attn_prompt_kernel(sink_ref, q_ref, kc_ref, kp_ref, vc_ref, vp_ref,
                        o_ref, nk_ref, nv_ref, *, nq):
    i = pl.program_id(1)
    nsteps = pl.num_programs(1)
    w = WINDOW
    kcr = kc_ref[...]
    kpr = kp_ref[...]
    vc = vc_ref[...]

    @pl.when(i == nsteps - 1)
    def _():
        nk_ref[...] = kcr[(nq - 1) * w:]
        nv_ref[...] = vc[(nq - 1) * w:]

    kall = jnp.concatenate([kpr, kcr], axis=0)
    vall = jnp.concatenate([vp_ref[...], vc], axis=0)
    lo = lax.broadcasted_iota(jnp.int32, ((nq + 1) * w, LANES), 1) < HEAD_DIM
    lo_q = lax.broadcasted_iota(jnp.int32, (w, LANES), 1) < HEAD_DIM
    band = (lax.broadcasted_iota(jnp.int32, (w, w), 1) > lax.broadcasted_iota(jnp.int32, (w, w), 0))
    first_bias = jnp.where(i == 0, -jnp.inf, 0.0)

    for g in range(N_KV_HEADS):
        col, odd = g // 2, g % 2
        kg = kall[:, col * LANES:(col + 1) * LANES]
        vg = vall[:, col * LANES:(col + 1) * LANES]
        kg_sw = pltpu.roll(kg, HEAD_DIM, 1)
        vg_sw = pltpu.roll(vg, HEAD_DIM, 1)
        k_lo = jnp.where(lo, kg_sw if odd else kg, 0.0).astype(BF16)
        k_hi = jnp.where(lo, 0.0, kg if odd else kg_sw).astype(BF16)
        v_lo = jnp.where(lo, vg_sw if odd else vg, 0.0).astype(BF16)
        v_hi = jnp.where(lo, 0.0, vg if odd else vg_sw).astype(BF16)
        sinks2 = [sink_ref[4 * g + r] * LOG2E for r in range(4)]
        def scores(s, g=g, k_lo=k_lo, k_hi=k_hi):
            rows = slice(s * w, (s + 1) * w)
            keys = slice(s * w, (s + 2) * w)
            qst = jnp.concatenate([q_ref[rows, (2 * g) * LANES:(2 * g + 1) * LANES],
                                   q_ref[rows, (2 * g + 1) * LANES:(2 * g + 2) * LANES]],
                                  axis=0).astype(BF16)
            return _dot_nt(qst, k_lo[keys]), _dot_nt(qst, k_hi[keys])

        ahead = scores(0)
        for s in range(nq):
            rows = slice(s * w, (s + 1) * w)
            keys = slice(s * w, (s + 2) * w)
            pb = first_bias if s == 0 else None
            s_lo, s_hi = ahead
            if s + 1 < nq:
                ahead = scores(s + 1)
            e0, d0 = _softmax_fold(s_lo[:w], band, pb, sinks2[0])
            e1, d1 = _softmax_fold(s_hi[:w], band, pb, sinks2[1])
            e2, d2 = _softmax_fold(s_lo[w:], band, pb, sinks2[2])
            e3, d3 = _softmax_fold(s_hi[w:], band, pb, sinks2[3])
            p = jnp.concatenate([jnp.concatenate([e0, e1], axis=1),
                                 jnp.concatenate([e2, e3], axis=1)], axis=0)
            vcat = jnp.concatenate([v_lo[keys], v_hi[keys]], axis=0)
            o = _dot(p, vcat)
            oa = o[:w] / jnp.where(lo_q, d0, d1)
            ob = o[w:] / jnp.where(lo_q, d2, d3)
            o_ref[rows, (2 * g) * LANES:(2 * g + 1) * LANES] = oa.astype(o_ref.dtype)
            o_ref[rows, (2 * g + 1) * LANES:(2 * g + 2) * LANES] = ob.astype(o_ref.dtype)


def _attn_prompt(proj, sinks, nq):
    b, l, _ = proj.shape
    w = WINDOW
    nsteps = l // (nq * w)
    kcol, vcol = COL_K // KV_DIM, COL_V // KV_DIM
    prev = lambda bi, i: jnp.maximum(nq * i - 1, 0)
    return pl.pallas_call(
        functools.partial(_attn_prompt_kernel, nq=nq),
        grid=(b, nsteps),
        in_specs=[
            pl.BlockSpec(memory_space=pltpu.SMEM),
            pl.BlockSpec((None, nq * w, Q_DIM), lambda bi, i: (bi, i, COL_Q // Q_DIM)),
            pl.BlockSpec((None, nq * w, KV_DIM), lambda bi, i: (bi, i, kcol)),
            pl.BlockSpec((None, w, KV_DIM), lambda bi, i: (bi, prev(bi, i), kcol)),
            pl.BlockSpec((None, nq * w, KV_DIM), lambda bi, i: (bi, i, vcol)),
            pl.BlockSpec((None, w, KV_DIM), lambda bi, i: (bi, prev(bi, i), vcol)),
        ],
        out_specs=[
            pl.BlockSpec((None, nq * w, Q_DIM), lambda bi, i: (bi, i, 0)),
            pl.BlockSpec((None, w, KV_DIM), lambda bi, i: (bi, 0, 0)),
            pl.BlockSpec((None, w, KV_DIM), lambda bi, i: (bi, 0, 0)),
        ],
        out_shape=[
            jax.ShapeDtypeStruct((b, l, Q_DIM), BF16),
            jax.ShapeDtypeStruct((b, w, KV_DIM), F32),
            jax.ShapeDtypeStruct((b, w, KV_DIM), F32),
        ],
        compiler_params=_cparams(2),
        name="attn_prompt",
    )(sinks, proj, proj, proj, proj, proj)


def _softplus(v):
    return jnp.maximum(v, 0.0) + jnp.log1p(jnp.exp(-jnp.abs(v)))


def _head_expand(vals, ex2_ref):
    hi = vals.astype(BF16)
    mid = (vals - hi.astype(F32)).astype(BF16)
    return _dot(jnp.concatenate([hi, mid], axis=1), ex2_ref[...])


def _ssd_prompt_kernel(*refs, nsub):
    nz, nx = SSM_INNER // PIECE, CONV_DIM // PIECE
    z_refs, x_refs = refs[:nz], refs[nz:nz + nx]
    (dt_ref, cw_ref, cb_ref, dtb_ref, alog_ref, dsk_ref, gn_ref, ex2_ref,
     y_ref, nssm_ref, nconv_ref, state_ref, carry_ref, xst_ref) = refs[nz + nx:]
    i = pl.program_id(1)
    nc = pl.num_programs(1)
    q = SSD_CHUNK

    @pl.when(i == 0)
    def _():
        state_ref[...] = jnp.zeros_like(state_ref)
        carry_ref[...] = jnp.zeros_like(carry_ref)

    dtp = dt_ref[:q, :]
    for s in range(1, nsub):
        dtp = dtp + pltpu.roll(dt_ref[s * q:(s + 1) * q, :], s * SSM_HEADS, 1)
    dt = _softplus(dtp + dtb_ref[...])
    da = dt * (-jnp.exp(alog_ref[...]))
    tri = (lax.broadcasted_iota(jnp.int32, (q, q), 0) >= lax.broadcasted_iota(jnp.int32, (q, q), 1))
    cs = _dot_exact_lhs01(tri.astype(BF16), da)
    cs2 = cs * LOG2E
    cs2_t = cs2.T
    sc = dict(
        tri=tri, cs2=cs2, cs2_t=cs2_t, dt_t=dt.T,
        cdec_t=jnp.exp2(cs2_t[:, q - 1:q]),
        ecs_x=_head_expand(jnp.exp(cs), ex2_ref),
        wgt_x=_head_expand(dt * jnp.exp(cs[q - 1:q, :] - cs), ex2_ref))

    rows_of = lambda rs, sl: jnp.concatenate([r[sl, :] for r in rs], axis=1)
    ahead = _ssd_conv(0, rows_of, x_refs, cw_ref, cb_ref, carry_ref, xst_ref)
    for s in range(nsub):
        xc = ahead
        if s + 1 < nsub:
            ahead = _ssd_conv(s + 1, rows_of, x_refs, cw_ref, cb_ref, carry_ref, xst_ref)
        _ssd_chunk(s, sc, xc, rows_of, z_refs, dsk_ref, gn_ref, y_ref, state_ref)
    tail = rows_of(x_refs, slice(nsub * q - 8, nsub * q))
    carry_ref[...] = tail

    @pl.when(i == nc - 1)
    def _():
        nconv_ref[...] = tail
        nssm_ref[...] = state_ref[...]


def _ssd_conv(s, rows_of, x_refs, cw_ref, cb_ref, carry_ref, xst_ref):
    q = SSD_CHUNK
    x = rows_of(x_refs, slice(s * q, (s + 1) * q))
    prev = carry_ref[...] if s == 0 else rows_of(x_refs, slice(s * q - 8, s * q))
    pitch = CONV_PITCH
    outs = []
    for c in range(CONV_DIM // LANES):
        cols = slice(c * LANES, (c + 1) * LANES)
        buf = xst_ref.at[s % 2, c]
        xcol = x[:, cols]
        buf[pl.ds(0, 8, stride=pitch), :] = prev[:, cols]
        buf[pl.ds(8 * pitch, q, stride=pitch), :] = xcol
        acc = xcol * cw_ref[CONV_WIDTH - 1:CONV_WIDTH, cols] + cb_ref[:, cols]
        for k in range(1, CONV_WIDTH):
            acc = acc + (buf[pl.ds((8 - k) * pitch, q, stride=pitch), :]
                         * cw_ref[CONV_WIDTH - 1 - k:CONV_WIDTH - k, cols])
        outs.append(acc)
    return _silu(jnp.concatenate(outs, axis=1))


def _ssd_chunk(s, sc, xc, rows_of, z_refs, dsk_ref, gn_ref, y_ref, state_ref):
    q = SSD_CHUNK
    tri, cs2, cs2_t, dt_t, cdec_t = sc["tri"], sc["cs2"], sc["cs2_t"], sc["dt_t"], sc["cdec_t"]
    hoff = s * SSM_HEADS
    xoff = s * SSM_INNER
    trows = slice(s * q, (s + 1) * q)
    z = rows_of(z_refs, trows)
    xs = xc[:, :SSM_INNER]
    bm = xc[:, SSM_INNER:SSM_INNER + BC_DIM]
    cm = xc[:, SSM_INNER + BC_DIM:]

    lo = lax.broadcasted_iota(jnp.int32, (q, LANES), 1) < SSM_HEAD_DIM

    hpg = SSM_HEADS // SSM_GROUPS
    gw = hpg * SSM_HEAD_DIM
    for g in range(SSM_GROUPS):
        bg = bm[:, g * SSM_STATE:(g + 1) * SSM_STATE].astype(BF16)
        cg = cm[:, g * SSM_STATE:(g + 1) * SSM_STATE].astype(BF16)
        cb = _dot_nt(cg, bg)
        st = state_ref[g * gw:(g + 1) * gw, :]
        yoff = _dot_nt(cg, st.astype(BF16))
        ys = []
        for pr in range(2):
            pair = 2 * g + pr
            ms = []
            for h in (hoff + 2 * pair, hoff + 2 * pair + 1):
                diff = cs2[:, h:h + 1] - cs2_t[h:h + 1, :]
                lm = jnp.exp2(jnp.where(tri, diff, -jnp.inf))
                ms.append((cb * lm * dt_t[h:h + 1, :]).astype(BF16))
            xp = xs[:, pair * LANES:(pair + 1) * LANES]
            x2 = jnp.concatenate([jnp.where(lo, xp, 0.0), jnp.where(lo, 0.0, xp)],
                                 axis=0).astype(BF16)
            yd = _dot(jnp.concatenate(ms, axis=1), x2)
            yo = yoff[:, pr * LANES:(pr + 1) * LANES] * sc["ecs_x"][
                :, xoff + pair * LANES:xoff + (pair + 1) * LANES]
            ys.append(yd + yo + xp * dsk_ref[:, pair * LANES:(pair + 1) * LANES])
        yg = jnp.concatenate(ys, axis=1)
        wx = xs[:, g * gw:(g + 1) * gw] * sc["wgt_x"][:, xoff + g * gw:xoff + (g + 1) * gw]
        s_new = _dot_tn(wx.astype(BF16), bg)
        for r in range(hpg):
            h = hoff + hpg * g + r
            rows = slice(g * gw + r * SSM_HEAD_DIM, g * gw + (r + 1) * SSM_HEAD_DIM)
            state_ref[rows, :] = (st[r * SSM_HEAD_DIM:(r + 1) * SSM_HEAD_DIM, :] * cdec_t[h:h + 1, :]
                                  + s_new[r * SSM_HEAD_DIM:(r + 1) * SSM_HEAD_DIM, :])
        hg = yg * _silu(z[:, g * gw:(g + 1) * gw])
        y_ref[trows, g * gw:(g + 1) * gw] = (_rms(hg) * gn_ref[:, g * gw:(g + 1) * gw]
                                            ).astype(y_ref.dtype)


def _ssd_prompt(proj, dtraw, conv_w, conv_b, dt_bias_t, a_log_t, d_skip_x, norm_g, nsub):
    assert nsub * SSM_HEADS <= LANES
    src = np.arange(nsub * SSM_INNER) // SSM_HEAD_DIM
    ex = (np.arange(LANES)[:, None] == src[None, :]).astype(np.float32)
    ex2 = jnp.asarray(np.concatenate([ex, ex], axis=0), BF16)
    b, l, _ = proj.shape
    q = SSD_CHUNK * nsub
    nc = l // q
    const = lambda bi, i: (0, 0)
    piece = lambda c: pl.BlockSpec((None, q, PIECE), lambda bi, i: (bi, i, c))
    n_pieces = (SSM_INNER + CONV_DIM) // PIECE
    return pl.pallas_call(
        functools.partial(_ssd_prompt_kernel, nsub=nsub),
        grid=(b, nc),
        in_specs=[piece(COL_Z // PIECE + c) for c in range(SSM_INNER // PIECE)] + [
            piece(COL_XBC // PIECE + c) for c in range(CONV_DIM // PIECE)] + [
            pl.BlockSpec((None, q, HEADS_PAD), lambda bi, i: (bi, i, 0)),
            pl.BlockSpec((CONV_WIDTH, CONV_DIM), const),
            pl.BlockSpec((1, CONV_DIM), const),
            pl.BlockSpec((1, HEADS_PAD), const),
            pl.BlockSpec((1, HEADS_PAD), const),
            pl.BlockSpec((1, SSM_INNER), const),
            pl.BlockSpec((1, SSM_INNER), const),
            pl.BlockSpec((2 * LANES, nsub * SSM_INNER), const),
        ],
        out_specs=[
            pl.BlockSpec((None, q, SSM_INNER), lambda bi, i: (bi, i, 0)),
            pl.BlockSpec((None, SSM_INNER, SSM_STATE), lambda bi, i: (bi, 0, 0)),
            pl.BlockSpec((None, 8, CONV_DIM), lambda bi, i: (bi, 0, 0)),
        ],
        out_shape=[
            jax.ShapeDtypeStruct((b, l, SSM_INNER), BF16),
            jax.ShapeDtypeStruct((b, SSM_INNER, SSM_STATE), F32),
            jax.ShapeDtypeStruct((b, 8, CONV_DIM), F32),
        ],
        scratch_shapes=[pltpu.VMEM((SSM_INNER, SSM_STATE), F32),
                        pltpu.VMEM((8, CONV_DIM), F32),
                        pltpu.VMEM((2, CONV_DIM // LANES, (8 + SSD_CHUNK) * CONV_PITCH, LANES), F32)],
        compiler_params=_cparams(2),
        name="ssd_prompt",
    )(*([proj] * n_pieces), dtraw, conv_w, conv_b, dt_bias_t, a_log_t, d_skip_x, norm_g, ex2)


def _outproj_kernel(x_ref, a_ref, y_ref, wa_ref, wy_ref, o_ref):
    o_ref[...] = (x_ref[...] + _dot(a_ref[...].astype(BF16), wa_ref[...])
                  + _dot(y_ref[...].astype(BF16), wy_ref[...]))


def _outproj(x, attn, y, w_out, tm):
    t, d = x.shape
    half = w_out.shape[0] // 2
    return pl.pallas_call(
        _outproj_kernel,
        grid=(t // tm,),
        in_specs=[
            pl.BlockSpec((tm, d), lambda i: (i, 0)),
            pl.BlockSpec((tm, half), lambda i: (i, 0)),
            pl.BlockSpec((tm, half), lambda i: (i, 0)),
            pl.BlockSpec((half, d), lambda i: (0, 0)),
            pl.BlockSpec((half, d), lambda i: (1, 0)),
        ],
        out_specs=pl.BlockSpec((tm, d), lambda i: (i, 0)),
        out_shape=jax.ShapeDtypeStruct((t, d), F32),
        compiler_params=_cparams(1),
        name="outproj",
    )(x, attn, y, w_out, w_out)


def _ffn_kernel(h_ref, g_ref, wg_ref, wu_ref, wd_ref, o_ref, f_ref):
    d = o_ref.shape[1]
    tn = min(d, 512)

    def add_delta(f, base_ref):
        gate = _dot(f, wg_ref[...])
        hid = (gate * (1.0 / (1.0 + jnp.exp(-gate))) * _dot(f, wu_ref[...])).astype(BF16)
        for c in range(d // tn):
            cols = slice(c * tn, (c + 1) * tn)
            o_ref[:, cols] = base_ref[:, cols] + _dot(hid, wd_ref[:, cols])

    @pl.when(pl.program_id(1) == 0)
    def _():
        f = (_rms(h_ref[...]) * g_ref[...]).astype(BF16)
        f_ref[...] = f
        add_delta(f, h_ref)

    @pl.when(pl.program_id(1) > 0)
    def _():
        add_delta(f_ref[...], o_ref)


def _ffn(h, g, wg, wu, wd, tm, th):
    t, d = h.shape
    hidden = wd.shape[0]
    return pl.pallas_call(
        _ffn_kernel,
        grid=(t // tm, hidden // th),
        in_specs=[
            pl.BlockSpec((tm, d), lambda i, j: (i, 0)),
            pl.BlockSpec((1, d), lambda i, j: (0, 0)),
            pl.BlockSpec((d, th), lambda i, j: (0, j)),
            pl.BlockSpec((d, th), lambda i, j: (0, j)),
            pl.BlockSpec((th, d), lambda i, j: (j, 0)),
        ],
        out_specs=pl.BlockSpec((tm, d), lambda i, j: (i, 0)),
        out_shape=jax.ShapeDtypeStruct((t, d), F32),
        scratch_shapes=[pltpu.VMEM((tm, d), BF16)],
        compiler_params=_cparams(2),
        name="ffn",
    )(h, g, wg, wu, wd)


def _ple_kernel(h_ref, p_ref, gp_ref, wp_ref, wg_ref, gf_ref, o_ref, *, tn, row_parts):
    tm, d = h_ref.shape
    rp = tm // row_parts
    for r in range(row_parts):
        rows = slice(r * rp, (r + 1) * rp)
        n = (_rms(h_ref[rows, :]) * gp_ref[...]).astype(BF16)
        pb = p_ref[rows, :].astype(BF16)
        ss = jnp.zeros((rp, 1), F32)
        for c in range(d // tn):
            cols = slice(c * tn, (c + 1) * tn)
            gate = _dot(n, wg_ref[:, cols])
            h3 = h_ref[rows, cols] + _dot(pb, wp_ref[:, cols]) * _sigmoid(gate)
            o_ref[rows, cols] = h3
            ss = ss + jnp.sum(h3 * h3, axis=-1, keepdims=True)
        inv = lax.rsqrt(ss * (1.0 / d) + RMS_EPS)
        o_ref[rows, :] = o_ref[rows, :] * inv * gf_ref[...]


def _ple_final(h, p, g_ple, w_ple, w_gate, g_final, tm):
    t, d = h.shape
    pd = p.shape[1]
    const = lambda i: (0, 0)
    resident = dict(pipeline_mode=pl.Buffered(1))
    return pl.pallas_call(
        functools.partial(_ple_kernel, tn=512, row_parts=max(tm // 256, 1)),
        grid=(t // tm,),
        in_specs=[
            pl.BlockSpec((tm, d), lambda i: (i, 0)),
            pl.BlockSpec((tm, pd), lambda i: (i, 0)),
            pl.BlockSpec((1, d), const),
            pl.BlockSpec((pd, d), const, **resident),
            pl.BlockSpec((d, d), const, **resident),
            pl.BlockSpec((1, d), const),
        ],
        out_specs=pl.BlockSpec((tm, d), lambda i: (i, 0)),
        out_shape=jax.ShapeDtypeStruct((t, d), F32),
        compiler_params=_cparams(1),
        name="ple_final",
    )(h, p, g_ple, w_ple, w_gate, g_final)


def _sample_pre_kernel(x_ref, dt_ref, sc_ref, cw_ref, cb_ref, dtb_ref,
                       alog_ref, dsk_ref, exp_ref,
                       xdt_ref, b_ref, c_ref, dec_ref, yp_ref, dech_ref):
    conv = x_ref[...] * cw_ref[CONV_WIDTH - 1:CONV_WIDTH, :] + cb_ref[...]
    for k in range(CONV_WIDTH - 1):
        conv = conv + sc_ref[k] * cw_ref[k:k + 1, :]
    xc = _silu(conv)
    xs = xc[:, :SSM_INNER]
    bm = xc[:, SSM_INNER:SSM_INNER + BC_DIM]
    cm = xc[:, SSM_INNER + BC_DIM:]
    b_ref[...] = bm
    c_ref[...] = cm
    dt = _softplus(dt_ref[...] + dtb_ref[...])
    dec = jnp.exp(dt * (-jnp.exp(alog_ref[...])))
    ex = exp_ref[...]
    dtx = _dot_exact_rhs01(dt, ex)
    dec_ref[...] = _dot_exact_rhs01(dec, ex)
    dech_ref[...] = dec
    xdt = xs * dtx
    xdt_ref[...] = xdt
    gw = SSM_INNER // SSM_GROUPS
    cbs = []
    for g in range(SSM_GROUPS):
        prod = cm[:, g * SSM_STATE:(g + 1) * SSM_STATE] * bm[:, g * SSM_STATE:(g + 1) * SSM_STATE]
        cbs.append(jnp.broadcast_to(jnp.sum(prod, axis=-1, keepdims=True), (prod.shape[0], gw)))
    yp_ref[...] = xdt * jnp.concatenate(cbs, axis=1) + xs * dsk_ref[...]


def _sample_pre(xbc, dtraw, sconv_t, conv_w, conv_b, dt_bias, a_log, d_skip_x, expand):
    nb = xbc.shape[0]
    shapes = [(nb, SSM_INNER), (nb, BC_DIM), (nb, BC_DIM),
              (nb, SSM_INNER), (nb, SSM_INNER), (nb, HEADS_PAD)]
    return pl.pallas_call(
        _sample_pre_kernel,
        out_shape=[jax.ShapeDtypeStruct(s, F32) for s in shapes],
        compiler_params=pltpu.CompilerParams(vmem_limit_bytes=VMEM_LIMIT),
        name="sample_pre",
    )(xbc, dtraw, sconv_t, conv_w, conv_b, dt_bias, a_log, d_skip_x, expand)


def _attn_sample_kernel(sink_ref, q_ref, kn_ref, vn_ref, ck_ref, cv_ref, o_ref, nk_ref, nv_ref, *, bb):
    w = WINDOW
    row = lax.broadcasted_iota(jnp.int32, (w, KV_DIM), 0)
    hrow = lax.broadcasted_iota(jnp.int32, (N_HEADS, KV_DIM), 0) // (N_HEADS // N_KV_HEADS)
    hgrp = lax.broadcasted_iota(jnp.int32, (N_HEADS, KV_DIM), 1) // HEAD_DIM
    own = hrow == hgrp
    sink = sink_ref[...] * LOG2E
    kks, vvs, scores, probs, dens = [], [], [], [], []
    for b in range(bb):
        kk = jnp.where(row == w - 1, kn_ref[b:b + 1, :], pltpu.roll(ck_ref[b], w - 1, 0))
        vv = jnp.where(row == w - 1, vn_ref[b:b + 1, :], pltpu.roll(cv_ref[b], w - 1, 0))
        nk_ref[b] = kk
        nv_ref[b] = vv
        kks.append(kk.astype(BF16))
        vvs.append(vv.astype(BF16))
    for b in range(bb):
        qb = q_ref[b]
        qrow = jnp.where(own, jnp.concatenate([qb] * N_KV_HEADS, axis=1), 0.0)
        scores.append(_dot_nt(qrow.astype(BF16), kks[b]))
    for b in range(bb):
        m = jnp.maximum(jnp.max(scores[b], axis=-1, keepdims=True), sink)
        e = jnp.exp2(scores[b] - m)
        dens.append(jnp.sum(e, axis=-1, keepdims=True) + jnp.exp2(sink - m))
        probs.append(e.astype(BF16))
    outs = [jnp.where(own, _dot(probs[b], vvs[b]), 0.0) for b in range(bb)]
    for b in range(bb):
        o = outs[b][:, :HEAD_DIM]
        for g in range(1, N_KV_HEADS):
            o = o + outs[b][:, g * HEAD_DIM:(g + 1) * HEAD_DIM]
        o_ref[b] = o / dens[b]


def _attn_sample(sinks_col, q3, knew, vnew, cache_k, cache_v, bb):
    nb = q3.shape[0]
    w = WINDOW
    return pl.pallas_call(
        functools.partial(_attn_sample_kernel, bb=bb),
        grid=(nb // bb,),
        in_specs=[
            pl.BlockSpec((N_HEADS, 1), lambda i: (0, 0)),
            pl.BlockSpec((bb, N_HEADS, HEAD_DIM), lambda i: (i, 0, 0)),
            pl.BlockSpec((bb, KV_DIM), lambda i: (i, 0)),
            pl.BlockSpec((bb, KV_DIM), lambda i: (i, 0)),
            pl.BlockSpec((bb, w, KV_DIM), lambda i: (i, 0, 0)),
            pl.BlockSpec((bb, w, KV_DIM), lambda i: (i, 0, 0)),
        ],
        out_specs=[
            pl.BlockSpec((bb, N_HEADS, HEAD_DIM), lambda i: (i, 0, 0)),
            pl.BlockSpec((bb, w, KV_DIM), lambda i: (i, 0, 0)),
            pl.BlockSpec((bb, w, KV_DIM), lambda i: (i, 0, 0)),
        ],
        out_shape=[
            jax.ShapeDtypeStruct((nb, N_HEADS, HEAD_DIM), F32),
            jax.ShapeDtypeStruct((nb, w, KV_DIM), F32),
            jax.ShapeDtypeStruct((nb, w, KV_DIM), F32),
        ],
        compiler_params=_cparams(1),
        name="attn_sample",
    )(sinks_col, q3, knew, vnew, cache_k, cache_v)


def _ssd_sample_kernel(st_ref, xdt_ref, b_ref, c_ref, dec_ref, dech_ref, yp_ref, z_ref, gn_ref,
                       y_ref, ns_ref, *, bb):
    gw = SSM_INNER // SSM_GROUPS
    grow = lax.broadcasted_iota(jnp.int32, (8, SSM_INNER), 0)
    glane = lax.broadcasted_iota(jnp.int32, (8, SSM_INNER), 1) // gw
    own = grow == glane
    pad = jnp.zeros((8 - SSM_GROUPS, SSM_STATE), F32)
    yoffs = []
    for b in range(bb):
        st = st_ref[b]
        cmat = jnp.concatenate([c_ref[b], pad], axis=0).astype(BF16)
        bmat = jnp.concatenate([b_ref[b], pad], axis=0).astype(BF16)
        r = _dot_nt(cmat, st.astype(BF16))
        yoffs.append(jnp.sum(jnp.where(own, r, 0.0), axis=0, keepdims=True))
        amat = jnp.where(own, jnp.broadcast_to(xdt_ref[b:b + 1, :], (8, SSM_INNER)), 0.0)
        outer = _dot_tn(amat.astype(BF16), bmat)
        for h in range(SSM_HEADS):
            rows = slice(h * SSM_HEAD_DIM, (h + 1) * SSM_HEAD_DIM)
            ns_ref[b, rows, :] = st[rows, :] * dech_ref[b:b + 1, h:h + 1] + outer[rows, :]
    y = yp_ref[...] + jnp.concatenate(yoffs, axis=0) * dec_ref[...]
    hg = y * _silu(z_ref[...])
    outs = []
    for g in range(SSM_GROUPS):
        outs.append(_rms(hg[:, g * gw:(g + 1) * gw]))
    y_ref[...] = jnp.concatenate(outs, axis=1) * gn_ref[...]


def _ssd_sample(state, xdt, b3, c3, decx, dech, ypart, z, norm_g, bb):
    nb = state.shape[0]
    row = lambda i: (i, 0)
    return pl.pallas_call(
        functools.partial(_ssd_sample_kernel, bb=bb),
        grid=(nb // bb,),
        in_specs=[
            pl.BlockSpec((bb, SSM_INNER, SSM_STATE), lambda i: (i, 0, 0)),
            pl.BlockSpec((bb, SSM_INNER), row),
            pl.BlockSpec((bb, SSM_GROUPS, SSM_STATE), lambda i: (i, 0, 0)),
            pl.BlockSpec((bb, SSM_GROUPS, SSM_STATE), lambda i: (i, 0, 0)),
            pl.BlockSpec((bb, SSM_INNER), row),
            pl.BlockSpec((bb, HEADS_PAD), row),
            pl.BlockSpec((bb, SSM_INNER), row),
            pl.BlockSpec((bb, SSM_INNER), row),
            pl.BlockSpec((1, SSM_INNER), lambda i: (0, 0)),
        ],
        out_specs=[
            pl.BlockSpec((bb, SSM_INNER), row),
            pl.BlockSpec((bb, SSM_INNER, SSM_STATE), lambda i: (i, 0, 0)),
        ],
        out_shape=[
            jax.ShapeDtypeStruct((nb, SSM_INNER), F32),
            jax.ShapeDtypeStruct((nb, SSM_INNER, SSM_STATE), F32),
        ],
        compiler_params=_cparams(1),
        name="ssd_sample",
    )(state, xdt, b3, c3, decx, dech, ypart, z, norm_g)


def _row_tile(t, want):
    return want if t % want == 0 else t


def _dense_tail(x, attn, y, p, wts, tm):
    h1 = _outproj(x, attn, y, wts["w_out"], tm)
    h2 = _ffn(h1, wts["g_ffn"], wts["w_gate"], wts["w_up"], wts["w_down"],
              _row_tile(x.shape[0], FFN_TM), FFN_TH)
    return _ple_final(h2, p, wts["g_ple"], wts["w_ple"], wts["w_ple_gate"], wts["g_final"],
                      _row_tile(x.shape[0], 1024))


def kernel(x_prompt, x_sample, cache_k, cache_v, state_ssm, state_conv, p_prompt, p_sample, w_in, conv_w, conv_b, dt_bias, a_log, d_skip, ssm_norm_g, attn_sinks, w_out, g_mix, g_ffn, w_ffn_gate, w_ffn_up, w_ffn_down, g_ple, w_ple, w_ple_gate, g_final):
    depth = w_in.shape[0]
    assert depth == 1, "single-layer step only"
    bsz, seq, d = x_prompt.shape
    nb, dseq, _ = x_sample.shape
    assert dseq == 1 and seq % SSD_CHUNK == 0 and seq % WINDOW == 0

    w = w_in[0]
    w_main = w.astype(BF16)
    w_dt = jnp.pad(w[:, PROJ_DIM:], ((0, 0), (0, HEADS_PAD - SSM_HEADS))).astype(BF16)
    padh = lambda v: jnp.pad(v, (0, HEADS_PAD - SSM_HEADS)).reshape(1, HEADS_PAD)
    tileh = lambda v: jnp.tile(v, HEADS_PAD // SSM_HEADS).reshape(1, HEADS_PAD)
    dtb, alog = padh(dt_bias[0]), padh(a_log[0])
    dsk_x = jnp.repeat(d_skip[0], SSM_HEAD_DIM).reshape(1, SSM_INNER)
    gn = ssm_norm_g[0].reshape(1, SSM_INNER)
    cw, cb = conv_w[0], conv_b[0].reshape(1, CONV_DIM)
    wts = dict(
        w_out=w_out[0].astype(BF16), g_ffn=g_ffn[0].reshape(1, d),
        w_gate=w_ffn_gate[0].astype(BF16), w_up=w_ffn_up[0].astype(BF16),
        w_down=w_ffn_down[0].astype(BF16), g_ple=g_ple[0].reshape(1, d),
        w_ple=w_ple[0].astype(BF16), w_ple_gate=w_ple_gate[0].astype(BF16),
        g_final=g_final.reshape(1, d))
    gmix = g_mix[0].reshape(1, d)
    sinks = attn_sinks[0]

    tp = bsz * seq
    xp = x_prompt.reshape(tp, d)
    tm_in = _row_tile(seq, 1024)
    proj, dtraw = _inproj(xp, gmix, w_main, w_dt, _rope_tables(jnp.arange(seq)), tm_in)
    proj3 = proj.reshape(bsz, seq, PROJ_DIM)
    attn, nk_p, nv_p = _attn_prompt(proj3, sinks, ATTN_BLOCKS)
    yp, nssm_p, nconv_p = _ssd_prompt(proj3, dtraw.reshape(bsz, seq, HEADS_PAD), cw, cb,
                                      tileh(dt_bias[0]), tileh(a_log[0]), dsk_x, gn, SSD_CHUNKS)
    tm = _row_tile(tp, 512)
    y_prompt = _dense_tail(xp, attn.reshape(tp, Q_DIM), yp.reshape(tp, SSM_INNER),
                           p_prompt[0].reshape(tp, -1), wts, tm).reshape(bsz, seq, d)

    xs = x_sample.reshape(nb, d)
    tab_s = _rope_tables(jnp.full((nb,), PAST_LEN, jnp.int32))
    proj_s, dtraw_s = _inproj(xs, gmix, w_main, w_dt, tab_s, nb)
    window = cache_k.shape[2]
    sconv_t = jnp.transpose(state_conv[0], (1, 0, 2))
    expand = (jnp.arange(HEADS_PAD)[:, None] == (jnp.arange(SSM_INNER) // SSM_HEAD_DIM)[None, :]
              ).astype(BF16)
    q_s = proj_s[:, COL_Q:COL_Q + Q_DIM]
    k_s = proj_s[:, COL_K:COL_K + KV_DIM]
    v_s = proj_s[:, COL_V:COL_V + KV_DIM]
    z_s = proj_s[:, COL_Z:COL_Z + SSM_INNER]
    xbc_s = proj_s[:, COL_XBC:COL_XBC + CONV_DIM]
    xdt, bm, cm, decx, ypart, dech = _sample_pre(
        xbc_s, dtraw_s, sconv_t, cw, cb, dtb, alog, dsk_x, expand)
    bb = 8 if nb % 8 == 0 else nb
    attn_s, nk_s, nv_s = _attn_sample(
        sinks.reshape(N_HEADS, 1), q_s.reshape(nb, N_HEADS, HEAD_DIM), k_s, v_s,
        cache_k[0].reshape(nb, window, KV_DIM), cache_v[0].reshape(nb, window, KV_DIM), bb)
    ys, nssm_s = _ssd_sample(
        state_ssm[0].reshape(nb, SSM_INNER, SSM_STATE), xdt,
        bm.reshape(nb, SSM_GROUPS, SSM_STATE), cm.reshape(nb, SSM_GROUPS, SSM_STATE),
        decx, dech, ypart, z_s, gn, bb)
    y_sample = _dense_tail(xs, attn_s.reshape(nb, Q_DIM), ys, p_sample[0].reshape(nb, -1), wts,
                           nb).reshape(nb, 1, d)
    nconv_s = jnp.concatenate([state_conv[0][:, 1:], xbc_s[:, None, :]], axis=1)

    kv5 = lambda t: t.reshape(1, t.shape[0], window, N_KV_HEADS, HEAD_DIM)
    return (y_prompt, y_sample,
            kv5(nk_p), kv5(nv_p),
            nssm_p.reshape(1, bsz, SSM_HEADS, SSM_HEAD_DIM, SSM_STATE),
            nconv_p[None, :, 8 - (CONV_WIDTH - 1):, :],
            kv5(nk_s), kv5(nv_s),
            nssm_s.reshape(1, nb, SSM_HEADS, SSM_HEAD_DIM, SSM_STATE),
            nconv_s[None])
```

```python
import functools

import numpy as np
import jax
import jax.numpy as jnp
from jax import lax
from jax.experimental import pallas as pl
from jax.experimental.pallas import tpu as pltpu

F32 = jnp.float32
BF16 = jnp.bfloat16

HEAD_DIM = 64
N_HEADS = 16
N_KV_HEADS = 4
WINDOW = 128
ROPE_DIM = 16
ROPE_THETA = 500000.0
SSM_HEADS = 16
SSM_HEAD_DIM = 64
SSM_GROUPS = 4
SSM_STATE = 128
CONV_WIDTH = 4
SSD_CHUNK = 128
RMS_EPS = 1e-6
PAST_LEN = 16384
LOG2E = 1.4426950408889634

Q_DIM = N_HEADS * HEAD_DIM
KV_DIM = N_KV_HEADS * HEAD_DIM
SSM_INNER = SSM_HEADS * SSM_HEAD_DIM
BC_DIM = SSM_GROUPS * SSM_STATE
CONV_DIM = SSM_INNER + 2 * BC_DIM
LANES = 128
HEADS_PAD = LANES

COL_Q = 0
COL_K = Q_DIM
COL_V = COL_K + KV_DIM
COL_Z = COL_V + KV_DIM
COL_XBC = COL_Z + SSM_INNER
PROJ_DIM = COL_XBC + CONV_DIM
PIECE = 512

INPROJ_TN = Q_DIM + 2 * KV_DIM
Q_SCALE = HEAD_DIM ** -0.5 * LOG2E
FFN_TM, FFN_TH = 1024, 512
CONV_PITCH = 3
ATTN_BLOCKS = 8
SSD_CHUNKS = 8
VMEM_LIMIT = 56 * 1024 * 1024


def _cparams(n_axes):
    return pltpu.CompilerParams(
        dimension_semantics=("arbitrary",) * n_axes, vmem_limit_bytes=VMEM_LIMIT)


def _rms(x):
    return x * lax.rsqrt(jnp.mean(x * x, axis=-1, keepdims=True) + RMS_EPS)


def _sigmoid(x):
    return 1.0 / (1.0 + jnp.exp2(x * -LOG2E))


def _silu(x):
    return x * _sigmoid(x)


def _dot(a, b):
    return jnp.dot(a, b, preferred_element_type=F32)


def _dot_nt(a, b):
    return lax.dot_general(a, b, (((1,), (1,)), ((), ())), preferred_element_type=F32)


def _dot_tn(a, b):
    return lax.dot_general(a, b, (((0,), (0,)), ((), ())), preferred_element_type=F32)


def _split3(x):
    hi = x.astype(BF16)
    r = x - hi.astype(F32)
    mid = r.astype(BF16)
    lo = (r - mid.astype(F32)).astype(BF16)
    return hi, mid, lo


def _dot_exact_lhs01(m01, x):
    hi, mid, lo = _split3(x)
    return _dot(m01, hi) + _dot(m01, mid) + _dot(m01, lo)


def _dot_exact_rhs01(x, m01):
    hi, mid, lo = _split3(x)
    return _dot(hi, m01) + _dot(mid, m01) + _dot(lo, m01)


def _inproj_kernel(xa_ref, xb_ref, g_ref, w_ref, wdt_ref, tab_ref, o_ref, odt_ref, u_ref):
    j = pl.program_id(1)

    @pl.when(j == 0)
    def _():
        rp = xa_ref.shape[0]
        for p, x_ref in enumerate((xa_ref, xb_ref)):
            rows = slice(p * rp, (p + 1) * rp)
            u = (_rms(x_ref[...]) * g_ref[...]).astype(BF16)
            u_ref[rows, :] = u
            odt_ref[rows, :] = _dot(u, wdt_ref[...])
            res = _dot(u, w_ref[...])
            tab = tab_ref[rows, :]
            for c in range((Q_DIM + KV_DIM) // LANES):
                cols = slice(c * LANES, (c + 1) * LANES)
                r = _rope(res[:, cols], tab)
                o_ref[rows, cols] = r * Q_SCALE if c < Q_DIM // LANES else r
            o_ref[rows, Q_DIM + KV_DIM:] = res[:, Q_DIM + KV_DIM:]

    @pl.when(j > 0)
    def _():
        o_ref[...] = _dot(u_ref[...], w_ref[...])


def _inproj(x, g, w, wdt, tab, tm):
    t, d = x.shape
    n = PROJ_DIM
    tn = INPROJ_TN
    nt = tab.shape[0] // tm
    ni, nj = t // tm, n // tn
    early = lambda i, j: jnp.minimum(i + (j == nj - 1).astype(jnp.int32), ni - 1)
    return pl.pallas_call(
        _inproj_kernel,
        grid=(ni, nj),
        in_specs=[
            pl.BlockSpec((tm // 2, d), lambda i, j: (2 * i, 0)),
            pl.BlockSpec((tm // 2, d), lambda i, j: (2 * early(i, j) + 1, 0)),
            pl.BlockSpec((1, d), lambda i, j: (0, 0)),
            pl.BlockSpec((d, tn), lambda i, j: (0, j)),
            pl.BlockSpec((d, HEADS_PAD), lambda i, j: (0, 0)),
            pl.BlockSpec((tm, 3 * LANES), lambda i, j: (i % nt, 0)),
        ],
        out_specs=[
            pl.BlockSpec((tm, tn), lambda i, j: (i, j)),
            pl.BlockSpec((tm, HEADS_PAD), lambda i, j: (i, 0)),
        ],
        out_shape=[
            jax.ShapeDtypeStruct((t, n), F32),
            jax.ShapeDtypeStruct((t, HEADS_PAD), F32),
        ],
        scratch_shapes=[pltpu.VMEM((tm, d), BF16)],
        compiler_params=_cparams(2),
        name="inproj",
    )(x, x, g, w, wdt, tab)


def _rope_tables(pos):
    half = ROPE_DIM // 2
    inv = ROPE_THETA ** (-jnp.arange(half, dtype=F32) * (2.0 / ROPE_DIM))
    ang = pos.astype(F32)[:, None] * inv[None, :]
    cs = jnp.concatenate([jnp.cos(ang), jnp.sin(ang)], axis=1)
    expand = np.zeros((2 * half, 3 * LANES), np.float32)
    base = np.zeros((1, 3 * LANES), np.float32)
    for lane in range(LANES):
        m = lane % HEAD_DIM
        if m >= ROPE_DIM:
            base[0, lane] = 1.0
            continue
        expand[m % half, lane] = 1.0
        if m < half:
            expand[half + m, LANES + lane] = -1.0
        else:
            expand[m, 2 * LANES + lane] = 1.0
    cs3 = jnp.concatenate(_split3(cs), axis=1)
    expand3 = jnp.asarray(np.concatenate([expand] * 3, axis=0), BF16)
    return jnp.dot(cs3, expand3, preferred_element_type=F32) + base


def _rope(x, tab):
    half = ROPE_DIM // 2
    c, sa, sb = tab[:, :LANES], tab[:, LANES:2 * LANES], tab[:, 2 * LANES:]
    return x * c + pltpu.roll(x, LANES - half, 1) * sa + pltpu.roll(x, half, 1) * sb


def _softmax_fold(s, band, prev_bias, sink2):
    sp = s[:, :WINDOW] if prev_bias is None else s[:, :WINDOW] + prev_bias
    t = jnp.where(band, sp, s[:, WINDOW:])
    m = jnp.maximum(jnp.max(t, axis=-1, keepdims=True), sink2)
    e = jnp.exp2(t - m)
    den = jnp.sum(e, axis=-1, keepdims=True) + jnp.exp2(sink2 - m)
    p = jnp.concatenate([jnp.where(band, e, 0.0), jnp.where(band, 0.0, e)], axis=1)
    return p.astype(BF16), den


def _cast_slabs(in_refs, out_refs):
    for src, dst in zip(in_refs, out_refs):
        dst[...] = src[...].astype(dst.dtype)


def _cast_specs(weights, n_steps, step_of):
    if any(w.shape[0] % n_steps or (w.shape[0] // n_steps) % 16 for w in weights):
        return None
    specs = [pl.BlockSpec((w.shape[0] // n_steps, w.shape[1]), lambda *g: (step_of(*g), 0))
             for w in weights]
    shapes = [jax.ShapeDtypeStruct(w.shape, BF16) for w in weights]
    return specs, shapes


def _attn_prompt_kernel(sink_ref, q_ref, kc_ref, kp_ref, vc_ref, vp_ref, *rest, nq):
    ncast = (len(rest) - 3) // 2
    o_ref, nk_ref, nv_ref = rest[ncast:ncast + 3]
    _cast_slabs(rest[:ncast], rest[ncast + 3:])
    i = pl.program_id(1)
    nsteps = pl.num_programs(1)
    w = WINDOW
    kcr = kc_ref[...]
    kpr = kp_ref[...]
    vc = vc_ref[...]

    @pl.when(i == nsteps - 1)
    def _():
        nk_ref[...] = kcr[(nq - 1) * w:]
        nv_ref[...] = vc[(nq - 1) * w:]

    kall = jnp.concatenate([kpr, kcr], axis=0)
    vall = jnp.concatenate([vp_ref[...], vc], axis=0)
    lo = lax.broadcasted_iota(jnp.int32, ((nq + 1) * w, LANES), 1) < HEAD_DIM
    lo_q = lax.broadcasted_iota(jnp.int32, (w, LANES), 1) < HEAD_DIM
    band = (lax.broadcasted_iota(jnp.int32, (w, w), 1) > lax.broadcasted_iota(jnp.int32, (w, w), 0))
    first_bias = jnp.where(i == 0, -jnp.inf, 0.0)

    for g in range(N_KV_HEADS):
        col, odd = g // 2, g % 2
        kg = kall[:, col * LANES:(col + 1) * LANES]
        vg = vall[:, col * LANES:(col + 1) * LANES]
        kg_sw = pltpu.roll(kg, HEAD_DIM, 1)
        vg_sw = pltpu.roll(vg, HEAD_DIM, 1)
        k_lo = jnp.where(lo, kg_sw if odd else kg, 0.0).astype(BF16)
        k_hi = jnp.where(lo, 0.0, kg if odd else kg_sw).astype(BF16)
        v_lo = jnp.where(lo, vg_sw if odd else vg, 0.0).astype(BF16)
        v_hi = jnp.where(lo, 0.0, vg if odd else vg_sw).astype(BF16)
        sinks2 = [sink_ref[4 * g + r] * LOG2E for r in range(4)]
        def scores(s, g=g, k_lo=k_lo, k_hi=k_hi):
            rows = slice(s * w, (s + 1) * w)
            keys = slice(s * w, (s + 2) * w)
            qst = jnp.concatenate([q_ref[rows, (2 * g) * LANES:(2 * g + 1) * LANES],
                                   q_ref[rows, (2 * g + 1) * LANES:(2 * g + 2) * LANES]],
                                  axis=0).astype(BF16)
            return _dot_nt(qst, k_lo[keys]), _dot_nt(qst, k_hi[keys])

        ahead = scores(0)
        for s in range(nq):
            rows = slice(s * w, (s + 1) * w)
            keys = slice(s * w, (s + 2) * w)
            pb = first_bias if s == 0 else None
            s_lo, s_hi = ahead
            if s + 1 < nq:
                ahead = scores(s + 1)
            e0, d0 = _softmax_fold(s_lo[:w], band, pb, sinks2[0])
            e1, d1 = _softmax_fold(s_hi[:w], band, pb, sinks2[1])
            e2, d2 = _softmax_fold(s_lo[w:], band, pb, sinks2[2])
            e3, d3 = _softmax_fold(s_hi[w:], band, pb, sinks2[3])
            p = jnp.concatenate([jnp.concatenate([e0, e1], axis=1),
                                 jnp.concatenate([e2, e3], axis=1)], axis=0)
            vcat = jnp.concatenate([v_lo[keys], v_hi[keys]], axis=0)
            o = _dot(p, vcat)
            oa = o[:w] / jnp.where(lo_q, d0, d1)
            ob = o[w:] / jnp.where(lo_q, d2, d3)
            o_ref[rows, (2 * g) * LANES:(2 * g + 1) * LANES] = oa.astype(o_ref.dtype)
            o_ref[rows, (2 * g + 1) * LANES:(2 * g + 2) * LANES] = ob.astype(o_ref.dtype)


def _attn_prompt(proj, sinks, nq, cast_weights=()):
    b, l, _ = proj.shape
    w = WINDOW
    nsteps = l // (nq * w)
    kcol, vcol = COL_K // KV_DIM, COL_V // KV_DIM
    prev = lambda bi, i: jnp.maximum(nq * i - 1, 0)
    cast = _cast_specs(cast_weights, b * nsteps, lambda bi, i: bi * nsteps + i)
    if cast is None:
        res = _attn_prompt(proj, sinks, nq)
        return res[:3] + tuple(cw.astype(BF16) for cw in cast_weights)
    cast_specs, cast_shapes = cast
    return pl.pallas_call(
        functools.partial(_attn_prompt_kernel, nq=nq),
        grid=(b, nsteps),
        in_specs=[
            pl.BlockSpec(memory_space=pltpu.SMEM),
            pl.BlockSpec((None, nq * w, Q_DIM), lambda bi, i: (bi, i, COL_Q // Q_DIM)),
            pl.BlockSpec((None, nq * w, KV_DIM), lambda bi, i: (bi, i, kcol)),
            pl.BlockSpec((None, w, KV_DIM), lambda bi, i: (bi, prev(bi, i), kcol)),
            pl.BlockSpec((None, nq * w, KV_DIM), lambda bi, i: (bi, i, vcol)),
            pl.BlockSpec((None, w, KV_DIM), lambda bi, i: (bi, prev(bi, i), vcol)),
        ] + cast_specs,
        out_specs=[
            pl.BlockSpec((None, nq * w, Q_DIM), lambda bi, i: (bi, i, 0)),
            pl.BlockSpec((None, w, KV_DIM), lambda bi, i: (bi, 0, 0)),
            pl.BlockSpec((None, w, KV_DIM), lambda bi, i: (bi, 0, 0)),
        ] + cast_specs,
        out_shape=[
            jax.ShapeDtypeStruct((b, l, Q_DIM), BF16),
            jax.ShapeDtypeStruct((b, w, KV_DIM), F32),
            jax.ShapeDtypeStruct((b, w, KV_DIM), F32),
        ] + cast_shapes,
        compiler_params=_cparams(2),
        name="attn_prompt",
    )(sinks, proj, proj, proj, proj, proj, *cast_weights)


def _softplus(v):
    return jnp.maximum(v, 0.0) + jnp.log1p(jnp.exp(-jnp.abs(v)))


def _head_expand(vals, ex2_ref):
    hi = vals.astype(BF16)
    mid = (vals - hi.astype(F32)).astype(BF16)
    return _dot(jnp.concatenate([hi, mid], axis=1), ex2_ref[...])


def _ssd_prompt_kernel(*refs, nsub):
    nz, nx = SSM_INNER // PIECE, CONV_DIM // PIECE
    z_refs, x_refs = refs[:nz], refs[nz:nz + nx]
    dt_ref, cw_ref, cb_ref, dtb_ref, alog_ref, dsk_ref, gn_ref, ex2_ref = refs[nz + nx:nz + nx + 8]
    rest = refs[nz + nx + 8:]
    ncast = (len(rest) - 6) // 2
    y_ref, nssm_ref, nconv_ref = rest[ncast:ncast + 3]
    state_ref, carry_ref, xst_ref = rest[2 * ncast + 3:]
    _cast_slabs(rest[:ncast], rest[ncast + 3:2 * ncast + 3])
    i = pl.program_id(1)
    nc = pl.num_programs(1)
    q = SSD_CHUNK

    @pl.when(i == 0)
    def _():
        state_ref[...] = jnp.zeros_like(state_ref)
        carry_ref[...] = jnp.zeros_like(carry_ref)

    dtp = dt_ref[:q, :]
    for s in range(1, nsub):
        dtp = dtp + pltpu.roll(dt_ref[s * q:(s + 1) * q, :], s * SSM_HEADS, 1)
    dt = _softplus(dtp + dtb_ref[...])
    da = dt * (-jnp.exp(alog_ref[...]))
    tri = (lax.broadcasted_iota(jnp.int32, (q, q), 0) >= lax.broadcasted_iota(jnp.int32, (q, q), 1))
    cs = _dot_exact_lhs01(tri.astype(BF16), da)
    cs2 = cs * LOG2E
    cs2_t = cs2.T
    sc = dict(
        tri=tri, cs2=cs2, cs2_t=cs2_t, dt_t=dt.T,
        cdec_t=jnp.exp2(cs2_t[:, q - 1:q]),
        ecs_x=_head_expand(jnp.exp(cs), ex2_ref),
        wgt_x=_head_expand(dt * jnp.exp(cs[q - 1:q, :] - cs), ex2_ref))

    rows_of = lambda rs, sl: jnp.concatenate([r[sl, :] for r in rs], axis=1)
    ahead = _ssd_conv(0, rows_of, x_refs, cw_ref, cb_ref, carry_ref, xst_ref)
    for s in range(nsub):
        xc = ahead
        if s + 1 < nsub:
            ahead = _ssd_conv(s + 1, rows_of, x_refs, cw_ref, cb_ref, carry_ref, xst_ref)
        _ssd_chunk(s, sc, xc, rows_of, z_refs, dsk_ref, gn_ref, y_ref, state_ref)
    tail = rows_of(x_refs, slice(nsub * q - 8, nsub * q))
    carry_ref[...] = tail

    @pl.when(i == nc - 1)
    def _():
        nconv_ref[...] = tail
        nssm_ref[...] = state_ref[...]


def _ssd_conv(s, rows_of, x_refs, cw_ref, cb_ref, carry_ref, xst_ref):
    q = SSD_CHUNK
    x = rows_of(x_refs, slice(s * q, (s + 1) * q))
    prev = carry_ref[...] if s == 0 else rows_of(x_refs, slice(s * q - 8, s * q))
    pitch = CONV_PITCH
    outs = []
    for c in range(CONV_DIM // LANES):
        cols = slice(c * LANES, (c + 1) * LANES)
        buf = xst_ref.at[s % 2, c]
        xcol = x[:, cols]
        buf[pl.ds(0, 8, stride=pitch), :] = prev[:, cols]
        buf[pl.ds(8 * pitch, q, stride=pitch), :] = xcol
        acc = xcol * cw_ref[CONV_WIDTH - 1:CONV_WIDTH, cols] + cb_ref[:, cols]
        for k in range(1, CONV_WIDTH):
            acc = acc + (buf[pl.ds((8 - k) * pitch, q, stride=pitch), :]
                         * cw_ref[CONV_WIDTH - 1 - k:CONV_WIDTH - k, cols])
        outs.append(acc)
    return _silu(jnp.concatenate(outs, axis=1))


def _ssd_chunk(s, sc, xc, rows_of, z_refs, dsk_ref, gn_ref, y_ref, state_ref):
    q = SSD_CHUNK
    tri, cs2, cs2_t, dt_t, cdec_t = sc["tri"], sc["cs2"], sc["cs2_t"], sc["dt_t"], sc["cdec_t"]
    hoff = s * SSM_HEADS
    xoff = s * SSM_INNER
    trows = slice(s * q, (s + 1) * q)
    z = rows_of(z_refs, trows)
    xs = xc[:, :SSM_INNER]
    bm = xc[:, SSM_INNER:SSM_INNER + BC_DIM]
    cm = xc[:, SSM_INNER + BC_DIM:]

    lo = lax.broadcasted_iota(jnp.int32, (q, LANES), 1) < SSM_HEAD_DIM

    hpg = SSM_HEADS // SSM_GROUPS
    gw = hpg * SSM_HEAD_DIM
    for g in range(SSM_GROUPS):
        bg = bm[:, g * SSM_STATE:(g + 1) * SSM_STATE].astype(BF16)
        cg = cm[:, g * SSM_STATE:(g + 1) * SSM_STATE].astype(BF16)
        cb = _dot_nt(cg, bg)
        st = state_ref[g * gw:(g + 1) * gw, :]
        yoff = _dot_nt(cg, st.astype(BF16))
        ys = []
        for pr in range(2):
            pair = 2 * g + pr
            ms = []
            for h in (hoff + 2 * pair, hoff + 2 * pair + 1):
                diff = cs2[:, h:h + 1] - cs2_t[h:h + 1, :]
                lm = jnp.exp2(jnp.where(tri, diff, -jnp.inf))
                ms.append((cb * lm * dt_t[h:h + 1, :]).astype(BF16))
            xp = xs[:, pair * LANES:(pair + 1) * LANES]
            x2 = jnp.concatenate([jnp.where(lo, xp, 0.0), jnp.where(lo, 0.0, xp)],
                                 axis=0).astype(BF16)
            yd = _dot(jnp.concatenate(ms, axis=1), x2)
            yo = yoff[:, pr * LANES:(pr + 1) * LANES] * sc["ecs_x"][
                :, xoff + pair * LANES:xoff + (pair + 1) * LANES]
            ys.append(yd + yo + xp * dsk_ref[:, pair * LANES:(pair + 1) * LANES])
        yg = jnp.concatenate(ys, axis=1)
        wx = xs[:, g * gw:(g + 1) * gw] * sc["wgt_x"][:, xoff + g * gw:xoff + (g + 1) * gw]
        s_new = _dot_tn(wx.astype(BF16), bg)
        for r in range(hpg):
            h = hoff + hpg * g + r
            rows = slice(g * gw + r * SSM_HEAD_DIM, g * gw + (r + 1) * SSM_HEAD_DIM)
            state_ref[rows, :] = (st[r * SSM_HEAD_DIM:(r + 1) * SSM_HEAD_DIM, :] * cdec_t[h:h + 1, :]
                                  + s_new[r * SSM_HEAD_DIM:(r + 1) * SSM_HEAD_DIM, :])
        hg = yg * _silu(z[:, g * gw:(g + 1) * gw])
        y_ref[trows, g * gw:(g + 1) * gw] = (_rms(hg) * gn_ref[:, g * gw:(g + 1) * gw]
                                            ).astype(y_ref.dtype)


def _ssd_prompt(proj, dtraw, conv_w, conv_b, dt_bias_t, a_log_t, d_skip_x, norm_g, nsub,
                cast_weights=()):
    assert nsub * SSM_HEADS <= LANES
    nsteps = proj.shape[1] // (SSD_CHUNK * nsub)
    cast = _cast_specs(cast_weights, proj.shape[0] * nsteps, lambda bi, i: bi * nsteps + i)
    if cast is None:
        res = _ssd_prompt(proj, dtraw, conv_w, conv_b, dt_bias_t, a_log_t, d_skip_x, norm_g, nsub)
        return tuple(res[:3]) + tuple(cw.astype(BF16) for cw in cast_weights)
    cast_specs, cast_shapes = cast
    src = np.arange(nsub * SSM_INNER) // SSM_HEAD_DIM
    ex = (np.arange(LANES)[:, None] == src[None, :]).astype(np.float32)
    ex2 = jnp.asarray(np.concatenate([ex, ex], axis=0), BF16)
    b, l, _ = proj.shape
    q = SSD_CHUNK * nsub
    nc = l // q
    const = lambda bi, i: (0, 0)
    piece = lambda c: pl.BlockSpec((None, q, PIECE), lambda bi, i: (bi, i, c))
    n_pieces = (SSM_INNER + CONV_DIM) // PIECE
    return pl.pallas_call(
        functools.partial(_ssd_prompt_kernel, nsub=nsub),
        grid=(b, nc),
        in_specs=[piece(COL_Z // PIECE + c) for c in range(SSM_INNER // PIECE)] + [
            piece(COL_XBC // PIECE + c) for c in range(CONV_DIM // PIECE)] + [
            pl.BlockSpec((None, q, HEADS_PAD), lambda bi, i: (bi, i, 0)),
            pl.BlockSpec((CONV_WIDTH, CONV_DIM), const),
            pl.BlockSpec((1, CONV_DIM), const),
            pl.BlockSpec((1, HEADS_PAD), const),
            pl.BlockSpec((1, HEADS_PAD), const),
            pl.BlockSpec((1, SSM_INNER), const),
            pl.BlockSpec((1, SSM_INNER), const),
            pl.BlockSpec((2 * LANES, nsub * SSM_INNER), const, pipeline_mode=pl.Buffered(1)),
        ] + cast_specs,
        out_specs=[
            pl.BlockSpec((None, q, SSM_INNER), lambda bi, i: (bi, i, 0)),
            pl.BlockSpec((None, SSM_INNER, SSM_STATE), lambda bi, i: (bi, 0, 0)),
            pl.BlockSpec((None, 8, CONV_DIM), lambda bi, i: (bi, 0, 0)),
        ] + cast_specs,
        out_shape=[
            jax.ShapeDtypeStruct((b, l, SSM_INNER), BF16),
            jax.ShapeDtypeStruct((b, SSM_INNER, SSM_STATE), F32),
            jax.ShapeDtypeStruct((b, 8, CONV_DIM), F32),
        ] + cast_shapes,
        scratch_shapes=[pltpu.VMEM((SSM_INNER, SSM_STATE), F32),
                        pltpu.VMEM((8, CONV_DIM), F32),
                        pltpu.VMEM((2, CONV_DIM // LANES, (8 + SSD_CHUNK) * CONV_PITCH, LANES), F32)],
        compiler_params=_cparams(2),
        name="ssd_prompt",
    )(*([proj] * n_pieces), dtraw, conv_w, conv_b, dt_bias_t, a_log_t, d_skip_x, norm_g, ex2,
      *cast_weights)


def _outproj_kernel(x_ref, a_ref, y_ref, wa_ref, wy_ref, o_ref):
    o_ref[...] = (x_ref[...] + _dot(a_ref[...].astype(BF16), wa_ref[...])
                  + _dot(y_ref[...].astype(BF16), wy_ref[...]))


def _outproj(x, attn, y, w_out, tm):
    t, d = x.shape
    half = w_out.shape[0] // 2
    return pl.pallas_call(
        _outproj_kernel,
        grid=(t // tm,),
        in_specs=[
            pl.BlockSpec((tm, d), lambda i: (i, 0)),
            pl.BlockSpec((tm, half), lambda i: (i, 0)),
            pl.BlockSpec((tm, half), lambda i: (i, 0)),
            pl.BlockSpec((half, d), lambda i: (0, 0)),
            pl.BlockSpec((half, d), lambda i: (1, 0)),
        ],
        out_specs=pl.BlockSpec((tm, d), lambda i: (i, 0)),
        out_shape=jax.ShapeDtypeStruct((t, d), F32),
        compiler_params=_cparams(1),
        name="outproj",
    )(x, attn, y, w_out, w_out)


def _ffn_kernel(h_ref, g_ref, wg_ref, wu_ref, wd_ref, o_ref, f_ref):
    d = o_ref.shape[1]
    tn = min(d, 512)

    def add_delta(f, base_ref):
        gate = _dot(f, wg_ref[...])
        hid = (gate * (1.0 / (1.0 + jnp.exp(-gate))) * _dot(f, wu_ref[...])).astype(BF16)
        for c in range(d // tn):
            cols = slice(c * tn, (c + 1) * tn)
            o_ref[:, cols] = base_ref[:, cols] + _dot(hid, wd_ref[:, cols])

    @pl.when(pl.program_id(1) == 0)
    def _():
        f = (_rms(h_ref[...]) * g_ref[...]).astype(BF16)
        f_ref[...] = f
        add_delta(f, h_ref)

    @pl.when(pl.program_id(1) > 0)
    def _():
        add_delta(f_ref[...], o_ref)


def _ffn(h, g, wg, wu, wd, tm, th):
    t, d = h.shape
    hidden = wd.shape[0]
    return pl.pallas_call(
        _ffn_kernel,
        grid=(t // tm, hidden // th),
        in_specs=[
            pl.BlockSpec((tm, d), lambda i, j: (i, 0)),
            pl.BlockSpec((1, d), lambda i, j: (0, 0)),
            pl.BlockSpec((d, th), lambda i, j: (0, j)),
            pl.BlockSpec((d, th), lambda i, j: (0, j)),
            pl.BlockSpec((th, d), lambda i, j: (j, 0)),
        ],
        out_specs=pl.BlockSpec((tm, d), lambda i, j: (i, 0)),
        out_shape=jax.ShapeDtypeStruct((t, d), F32),
        scratch_shapes=[pltpu.VMEM((tm, d), BF16)],
        compiler_params=_cparams(2),
        name="ffn",
    )(h, g, wg, wu, wd)


def _ple_kernel(h_ref, p_ref, gp_ref, wp_ref, wg_ref, gf_ref, o_ref, *, tn, row_parts):
    tm, d = h_ref.shape
    rp = tm // row_parts
    for r in range(row_parts):
        rows = slice(r * rp, (r + 1) * rp)
        n = (_rms(h_ref[rows, :]) * gp_ref[...]).astype(BF16)
        pb = p_ref[rows, :].astype(BF16)
        ss = jnp.zeros((rp, 1), F32)
        for c in range(d // tn):
            cols = slice(c * tn, (c + 1) * tn)
            gate = _dot(n, wg_ref[:, cols])
            h3 = h_ref[rows, cols] + _dot(pb, wp_ref[:, cols]) * _sigmoid(gate)
            o_ref[rows, cols] = h3
            ss = ss + jnp.sum(h3 * h3, axis=-1, keepdims=True)
        inv = lax.rsqrt(ss * (1.0 / d) + RMS_EPS)
        o_ref[rows, :] = o_ref[rows, :] * inv * gf_ref[...]


def _ple_final(h, p, g_ple, w_ple, w_gate, g_final, tm):
    t, d = h.shape
    pd = p.shape[1]
    const = lambda i: (0, 0)
    resident = dict(pipeline_mode=pl.Buffered(1))
    return pl.pallas_call(
        functools.partial(_ple_kernel, tn=512, row_parts=max(tm // 256, 1)),
        grid=(t // tm,),
        in_specs=[
            pl.BlockSpec((tm, d), lambda i: (i, 0)),
            pl.BlockSpec((tm, pd), lambda i: (i, 0)),
            pl.BlockSpec((1, d), const),
            pl.BlockSpec((pd, d), const, **resident),
            pl.BlockSpec((d, d), const, **resident),
            pl.BlockSpec((1, d), const),
        ],
        out_specs=pl.BlockSpec((tm, d), lambda i: (i, 0)),
        out_shape=jax.ShapeDtypeStruct((t, d), F32),
        compiler_params=_cparams(1),
        name="ple_final",
    )(h, p, g_ple, w_ple, w_gate, g_final)


def _sample_pre_kernel(x_ref, dt_ref, sc_ref, cw_ref, cb_ref, dtb_ref,
                       alog_ref, dsk_ref, exp_ref,
                       xdt_ref, b_ref, c_ref, dec_ref, yp_ref, dech_ref):
    conv = x_ref[...] * cw_ref[CONV_WIDTH - 1:CONV_WIDTH, :] + cb_ref[...]
    for k in range(CONV_WIDTH - 1):
        conv = conv + sc_ref[k] * cw_ref[k:k + 1, :]
    xc = _silu(conv)
    xs = xc[:, :SSM_INNER]
    bm = xc[:, SSM_INNER:SSM_INNER + BC_DIM]
    cm = xc[:, SSM_INNER + BC_DIM:]
    b_ref[...] = bm
    c_ref[...] = cm
    dt = _softplus(dt_ref[...] + dtb_ref[...])
    dec = jnp.exp(dt * (-jnp.exp(alog_ref[...])))
    ex = exp_ref[...]
    dtx = _dot_exact_rhs01(dt, ex)
    dec_ref[...] = _dot_exact_rhs01(dec, ex)
    dech_ref[...] = dec
    xdt = xs * dtx
    xdt_ref[...] = xdt
    gw = SSM_INNER // SSM_GROUPS
    cbs = []
    for g in range(SSM_GROUPS):
        prod = cm[:, g * SSM_STATE:(g + 1) * SSM_STATE] * bm[:, g * SSM_STATE:(g + 1) * SSM_STATE]
        cbs.append(jnp.broadcast_to(jnp.sum(prod, axis=-1, keepdims=True), (prod.shape[0], gw)))
    yp_ref[...] = xdt * jnp.concatenate(cbs, axis=1) + xs * dsk_ref[...]


def _sample_pre(xbc, dtraw, sconv_t, conv_w, conv_b, dt_bias, a_log, d_skip_x, expand):
    nb = xbc.shape[0]
    shapes = [(nb, SSM_INNER), (nb, BC_DIM), (nb, BC_DIM),
              (nb, SSM_INNER), (nb, SSM_INNER), (nb, HEADS_PAD)]
    return pl.pallas_call(
        _sample_pre_kernel,
        out_shape=[jax.ShapeDtypeStruct(s, F32) for s in shapes],
        compiler_params=pltpu.CompilerParams(vmem_limit_bytes=VMEM_LIMIT),
        name="sample_pre",
    )(xbc, dtraw, sconv_t, conv_w, conv_b, dt_bias, a_log, d_skip_x, expand)


def _attn_sample_kernel(sink_ref, q_ref, kn_ref, vn_ref, ck_ref, cv_ref, o_ref, nk_ref, nv_ref, *, bb):
    w = WINDOW
    row = lax.broadcasted_iota(jnp.int32, (w, KV_DIM), 0)
    hrow = lax.broadcasted_iota(jnp.int32, (N_HEADS, KV_DIM), 0) // (N_HEADS // N_KV_HEADS)
    hgrp = lax.broadcasted_iota(jnp.int32, (N_HEADS, KV_DIM), 1) // HEAD_DIM
    own = hrow == hgrp
    sink = sink_ref[...] * LOG2E
    kks, vvs, scores, probs, dens = [], [], [], [], []
    for b in range(bb):
        kk = jnp.where(row == w - 1, kn_ref[b:b + 1, :], pltpu.roll(ck_ref[b], w - 1, 0))
        vv = jnp.where(row == w - 1, vn_ref[b:b + 1, :], pltpu.roll(cv_ref[b], w - 1, 0))
        nk_ref[b] = kk
        nv_ref[b] = vv
        kks.append(kk.astype(BF16))
        vvs.append(vv.astype(BF16))
    for b in range(bb):
        qb = q_ref[b]
        qrow = jnp.where(own, jnp.concatenate([qb] * N_KV_HEADS, axis=1), 0.0)
        scores.append(_dot_nt(qrow.astype(BF16), kks[b]))
    for b in range(bb):
        m = jnp.maximum(jnp.max(scores[b], axis=-1, keepdims=True), sink)
        e = jnp.exp2(scores[b] - m)
        dens.append(jnp.sum(e, axis=-1, keepdims=True) + jnp.exp2(sink - m))
        probs.append(e.astype(BF16))
    outs = [jnp.where(own, _dot(probs[b], vvs[b]), 0.0) for b in range(bb)]
    for b in range(bb):
        o = outs[b][:, :HEAD_DIM]
        for g in range(1, N_KV_HEADS):
            o = o + outs[b][:, g * HEAD_DIM:(g + 1) * HEAD_DIM]
        o_ref[b] = o / dens[b]


def _attn_sample(sinks_col, q3, knew, vnew, cache_k, cache_v, bb):
    nb = q3.shape[0]
    w = WINDOW
    return pl.pallas_call(
        functools.partial(_attn_sample_kernel, bb=bb),
        grid=(nb // bb,),
        in_specs=[
            pl.BlockSpec((N_HEADS, 1), lambda i: (0, 0)),
            pl.BlockSpec((bb, N_HEADS, HEAD_DIM), lambda i: (i, 0, 0)),
            pl.BlockSpec((bb, KV_DIM), lambda i: (i, 0)),
            pl.BlockSpec((bb, KV_DIM), lambda i: (i, 0)),
            pl.BlockSpec((bb, w, KV_DIM), lambda i: (i, 0, 0)),
            pl.BlockSpec((bb, w, KV_DIM), lambda i: (i, 0, 0)),
        ],
        out_specs=[
            pl.BlockSpec((bb, N_HEADS, HEAD_DIM), lambda i: (i, 0, 0)),
            pl.BlockSpec((bb, w, KV_DIM), lambda i: (i, 0, 0)),
            pl.BlockSpec((bb, w, KV_DIM), lambda i: (i, 0, 0)),
        ],
        out_shape=[
            jax.ShapeDtypeStruct((nb, N_HEADS, HEAD_DIM), F32),
            jax.ShapeDtypeStruct((nb, w, KV_DIM), F32),
            jax.ShapeDtypeStruct((nb, w, KV_DIM), F32),
        ],
        compiler_params=_cparams(1),
        name="attn_sample",
    )(sinks_col, q3, knew, vnew, cache_k, cache_v)


def _ssd_sample_kernel(st_ref, xdt_ref, b_ref, c_ref, dec_ref, dech_ref, yp_ref, z_ref, gn_ref,
                       y_ref, ns_ref, *, bb):
    gw = SSM_INNER // SSM_GROUPS
    grow = lax.broadcasted_iota(jnp.int32, (8, SSM_INNER), 0)
    glane = lax.broadcasted_iota(jnp.int32, (8, SSM_INNER), 1) // gw
    own = grow == glane
    pad = jnp.zeros((8 - SSM_GROUPS, SSM_STATE), F32)
    yoffs = []
    for b in range(bb):
        st = st_ref[b]
        cmat = jnp.concatenate([c_ref[b], pad], axis=0).astype(BF16)
        bmat = jnp.concatenate([b_ref[b], pad], axis=0).astype(BF16)
        r = _dot_nt(cmat, st.astype(BF16))
        yoffs.append(jnp.sum(jnp.where(own, r, 0.0), axis=0, keepdims=True))
        amat = jnp.where(own, jnp.broadcast_to(xdt_ref[b:b + 1, :], (8, SSM_INNER)), 0.0)
        outer = _dot_tn(amat.astype(BF16), bmat)
        for h in range(SSM_HEADS):
            rows = slice(h * SSM_HEAD_DIM, (h + 1) * SSM_HEAD_DIM)
            ns_ref[b, rows, :] = st[rows, :] * dech_ref[b:b + 1, h:h + 1] + outer[rows, :]
    y = yp_ref[...] + jnp.concatenate(yoffs, axis=0) * dec_ref[...]
    hg = y * _silu(z_ref[...])
    outs = []
    for g in range(SSM_GROUPS):
        outs.append(_rms(hg[:, g * gw:(g + 1) * gw]))
    y_ref[...] = jnp.concatenate(outs, axis=1) * gn_ref[...]


def _ssd_sample(state, xdt, b3, c3, decx, dech, ypart, z, norm_g, bb):
    nb = state.shape[0]
    row = lambda i: (i, 0)
    return pl.pallas_call(
        functools.partial(_ssd_sample_kernel, bb=bb),
        grid=(nb // bb,),
        in_specs=[
            pl.BlockSpec((bb, SSM_INNER, SSM_STATE), lambda i: (i, 0, 0)),
            pl.BlockSpec((bb, SSM_INNER), row),
            pl.BlockSpec((bb, SSM_GROUPS, SSM_STATE), lambda i: (i, 0, 0)),
            pl.BlockSpec((bb, SSM_GROUPS, SSM_STATE), lambda i: (i, 0, 0)),
            pl.BlockSpec((bb, SSM_INNER), row),
            pl.BlockSpec((bb, HEADS_PAD), row),
            pl.BlockSpec((bb, SSM_INNER), row),
            pl.BlockSpec((bb, SSM_INNER), row),
            pl.BlockSpec((1, SSM_INNER), lambda i: (0, 0)),
        ],
        out_specs=[
            pl.BlockSpec((bb, SSM_INNER), row),
            pl.BlockSpec((bb, SSM_INNER, SSM_STATE), lambda i: (i, 0, 0)),
        ],
        out_shape=[
            jax.ShapeDtypeStruct((nb, SSM_INNER), F32),
            jax.ShapeDtypeStruct((nb, SSM_INNER, SSM_STATE), F32),
        ],
        compiler_params=_cparams(1),
        name="ssd_sample",
    )(state, xdt, b3, c3, decx, dech, ypart, z, norm_g)


def _row_tile(t, want):
    return want if t % want == 0 else t


def _dense_tail(x, attn, y, p, wts, tm):
    h1 = _outproj(x, attn, y, wts["w_out"], tm)
    h2 = _ffn(h1, wts["g_ffn"], wts["w_gate"], wts["w_up"], wts["w_down"],
              _row_tile(x.shape[0], FFN_TM), FFN_TH)
    return _ple_final(h2, p, wts["g_ple"], wts["w_ple"], wts["w_ple_gate"], wts["g_final"],
                      _row_tile(x.shape[0], 1024))


def kernel(x_prompt, x_sample, cache_k, cache_v, state_ssm, state_conv, p_prompt, p_sample, w_in, conv_w, conv_b, dt_bias, a_log, d_skip, ssm_norm_g, attn_sinks, w_out, g_mix, g_ffn, w_ffn_gate, w_ffn_up, w_ffn_down, g_ple, w_ple, w_ple_gate, g_final):
    depth = w_in.shape[0]
    assert depth == 1, "single-layer step only"
    bsz, seq, d = x_prompt.shape
    nb, dseq, _ = x_sample.shape
    assert dseq == 1 and seq % SSD_CHUNK == 0 and seq % WINDOW == 0

    w = w_in[0]
    w_main = w.astype(BF16)
    w_dt = jnp.pad(w[:, PROJ_DIM:], ((0, 0), (0, HEADS_PAD - SSM_HEADS))).astype(BF16)
    padh = lambda v: jnp.pad(v, (0, HEADS_PAD - SSM_HEADS)).reshape(1, HEADS_PAD)
    tileh = lambda v: jnp.tile(v, HEADS_PAD // SSM_HEADS).reshape(1, HEADS_PAD)
    dtb, alog = padh(dt_bias[0]), padh(a_log[0])
    dsk_x = jnp.repeat(d_skip[0], SSM_HEAD_DIM).reshape(1, SSM_INNER)
    gn = ssm_norm_g[0].reshape(1, SSM_INNER)
    cw, cb = conv_w[0], conv_b[0].reshape(1, CONV_DIM)
    wts = dict(g_ffn=g_ffn[0].reshape(1, d), g_ple=g_ple[0].reshape(1, d),
               g_final=g_final.reshape(1, d))
    gmix = g_mix[0].reshape(1, d)
    sinks = attn_sinks[0]

    tp = bsz * seq
    xp = x_prompt.reshape(tp, d)
    tm_in = _row_tile(seq, 1024)
    proj, dtraw = _inproj(xp, gmix, w_main, w_dt, _rope_tables(jnp.arange(seq)), tm_in)
    proj3 = proj.reshape(bsz, seq, PROJ_DIM)
    (attn, nk_p, nv_p, wts["w_gate"], wts["w_up"], wts["w_down"], wts["w_out"], wts["w_ple_gate"],
     wts["w_ple"]) = _attn_prompt(
        proj3, sinks, ATTN_BLOCKS,
        (w_ffn_gate[0], w_ffn_up[0], w_ffn_down[0], w_out[0], w_ple_gate[0], w_ple[0]))
    yp, nssm_p, nconv_p = _ssd_prompt(
        proj3, dtraw.reshape(bsz, seq, HEADS_PAD), cw, cb, tileh(dt_bias[0]), tileh(a_log[0]),
        dsk_x, gn, SSD_CHUNKS)
    tm = _row_tile(tp, 512)
    y_prompt = _dense_tail(xp, attn.reshape(tp, Q_DIM), yp.reshape(tp, SSM_INNER),
                           p_prompt[0].reshape(tp, -1), wts, tm).reshape(bsz, seq, d)

    xs = x_sample.reshape(nb, d)
    tab_s = _rope_tables(jnp.full((nb,), PAST_LEN, jnp.int32))
    proj_s, dtraw_s = _inproj(xs, gmix, w_main, w_dt, tab_s, nb)
    window = cache_k.shape[2]
    sconv_t = jnp.transpose(state_conv[0], (1, 0, 2))
    expand = (jnp.arange(HEADS_PAD)[:, None] == (jnp.arange(SSM_INNER) // SSM_HEAD_DIM)[None, :]
              ).astype(BF16)
    q_s = proj_s[:, COL_Q:COL_Q + Q_DIM]
    k_s = proj_s[:, COL_K:COL_K + KV_DIM]
    v_s = proj_s[:, COL_V:COL_V + KV_DIM]
    z_s = proj_s[:, COL_Z:COL_Z + SSM_INNER]
    xbc_s = proj_s[:, COL_XBC:COL_XBC + CONV_DIM]
    xdt, bm, cm, decx, ypart, dech = _sample_pre(
        xbc_s, dtraw_s, sconv_t, cw, cb, dtb, alog, dsk_x, expand)
    bb = 8 if nb % 8 == 0 else nb
    attn_s, nk_s, nv_s = _attn_sample(
        sinks.reshape(N_HEADS, 1), q_s.reshape(nb, N_HEADS, HEAD_DIM), k_s, v_s,
        cache_k[0].reshape(nb, window, KV_DIM), cache_v[0].reshape(nb, window, KV_DIM), bb)
    ys, nssm_s = _ssd_sample(
        state_ssm[0].reshape(nb, SSM_INNER, SSM_STATE), xdt,
        bm.reshape(nb, SSM_GROUPS, SSM_STATE), cm.reshape(nb, SSM_GROUPS, SSM_STATE),
        decx, dech, ypart, z_s, gn, bb)
    y_sample = _dense_tail(xs, attn_s.reshape(nb, Q_DIM), ys, p_sample[0].reshape(nb, -1), wts,
                           nb).reshape(nb, 1, d)
    nconv_s = jnp.concatenate([state_conv[0][:, 1:], xbc_s[:, None, :]], axis=1)

    kv5 = lambda t: t.reshape(1, t.shape[0], window, N_KV_HEADS, HEAD_DIM)
    return (y_prompt, y_sample,
            kv5(nk_p), kv5(nv_p),
            nssm_p.reshape(1, bsz, SSM_HEADS, SSM_HEAD_DIM, SSM_STATE),
            nconv_p[None, :, 8 - (CONV_WIDTH - 1):, :],
            kv5(nk_s), kv5(nv_s),
            nssm_s.reshape(1, nb, SSM_HEADS, SSM_HEAD_DIM, SSM_STATE),
            nconv_s[None])
```

```python
import functools

import numpy as np
import jax
import jax.numpy as jnp
from jax import lax
from jax.experimental import pallas as pl
from jax.experimental.pallas import tpu as pltpu

F32 = jnp.float32
BF16 = jnp.bfloat16

HEAD_DIM = 64
N_HEADS = 16
N_KV_HEADS = 4
WINDOW = 128
ROPE_DIM = 16
ROPE_THETA = 500000.0
SSM_HEADS = 16
SSM_HEAD_DIM = 64
SSM_GROUPS = 4
SSM_STATE = 128
CONV_WIDTH = 4
SSD_CHUNK = 128
RMS_EPS = 1e-6
PAST_LEN = 16384
LOG2E = 1.4426950408889634

Q_DIM = N_HEADS * HEAD_DIM
KV_DIM = N_KV_HEADS * HEAD_DIM
SSM_INNER = SSM_HEADS * SSM_HEAD_DIM
BC_DIM = SSM_GROUPS * SSM_STATE
CONV_DIM = SSM_INNER + 2 * BC_DIM
LANES = 128
HEADS_PAD = LANES

COL_Q = 0
COL_K = Q_DIM
COL_V = COL_K + KV_DIM
COL_Z = COL_V + KV_DIM
COL_XBC = COL_Z + SSM_INNER
PROJ_DIM = COL_XBC + CONV_DIM
PIECE = 512

INPROJ_TN = Q_DIM + 2 * KV_DIM
Q_SCALE = HEAD_DIM ** -0.5 * LOG2E
FFN_TM, FFN_TH = 1024, 512
CONV_PITCH = 3
DECODE_BB = 16
ATTN_BLOCKS = 8
SSD_CHUNKS = 8
VMEM_LIMIT = 56 * 1024 * 1024


def _cparams(n_axes):
    return pltpu.CompilerParams(
        dimension_semantics=("arbitrary",) * n_axes, vmem_limit_bytes=VMEM_LIMIT)


def _rms(x):
    return x * lax.rsqrt(jnp.mean(x * x, axis=-1, keepdims=True) + RMS_EPS)


def _sigmoid(x):
    return 1.0 / (1.0 + jnp.exp2(x * -LOG2E))


def _silu(x):
    return x * _sigmoid(x)


def _dot(a, b):
    return jnp.dot(a, b, preferred_element_type=F32)


def _dot_nt(a, b):
    return lax.dot_general(a, b, (((1,), (1,)), ((), ())), preferred_element_type=F32)


def _dot_tn(a, b):
    return lax.dot_general(a, b, (((0,), (0,)), ((), ())), preferred_element_type=F32)


def _split3(x):
    hi = x.astype(BF16)
    r = x - hi.astype(F32)
    mid = r.astype(BF16)
    lo = (r - mid.astype(F32)).astype(BF16)
    return hi, mid, lo


def _dot_exact_lhs01(m01, x):
    hi, mid, lo = _split3(x)
    return _dot(m01, hi) + _dot(m01, mid) + _dot(m01, lo)


def _dot_exact_rhs01(x, m01):
    hi, mid, lo = _split3(x)
    return _dot(hi, m01) + _dot(mid, m01) + _dot(lo, m01)


def _inproj_kernel(xa_ref, xb_ref, g_ref, w_ref, wdt_ref, tab_ref, o_ref, odt_ref, u_ref):
    j = pl.program_id(1)

    @pl.when(j == 0)
    def _():
        rp = xa_ref.shape[0]
        for p, x_ref in enumerate((xa_ref, xb_ref)):
            rows = slice(p * rp, (p + 1) * rp)
            u = (_rms(x_ref[...]) * g_ref[...]).astype(BF16)
            u_ref[rows, :] = u
            odt_ref[rows, :] = _dot(u, wdt_ref[...])
            res = _dot(u, w_ref[...])
            tab = tab_ref[rows, :]
            for c in range((Q_DIM + KV_DIM) // LANES):
                cols = slice(c * LANES, (c + 1) * LANES)
                r = _rope(res[:, cols], tab)
                o_ref[rows, cols] = r * Q_SCALE if c < Q_DIM // LANES else r
            o_ref[rows, Q_DIM + KV_DIM:] = res[:, Q_DIM + KV_DIM:]

    @pl.when(j > 0)
    def _():
        o_ref[...] = _dot(u_ref[...], w_ref[...])


def _inproj(x, g, w, wdt, tab, tm):
    t, d = x.shape
    n = PROJ_DIM
    tn = INPROJ_TN
    nt = tab.shape[0] // tm
    ni, nj = t // tm, n // tn
    early = lambda i, j: jnp.minimum(i + (j == nj - 1).astype(jnp.int32), ni - 1)
    return pl.pallas_call(
        _inproj_kernel,
        grid=(ni, nj),
        in_specs=[
            pl.BlockSpec((tm // 2, d), lambda i, j: (2 * i, 0)),
            pl.BlockSpec((tm // 2, d), lambda i, j: (2 * early(i, j) + 1, 0)),
            pl.BlockSpec((1, d), lambda i, j: (0, 0)),
            pl.BlockSpec((d, tn), lambda i, j: (0, j)),
            pl.BlockSpec((d, HEADS_PAD), lambda i, j: (0, 0)),
            pl.BlockSpec((tm, 3 * LANES), lambda i, j: (i % nt, 0)),
        ],
        out_specs=[
            pl.BlockSpec((tm, tn), lambda i, j: (i, j)),
            pl.BlockSpec((tm, HEADS_PAD), lambda i, j: (i, 0)),
        ],
        out_shape=[
            jax.ShapeDtypeStruct((t, n), F32),
            jax.ShapeDtypeStruct((t, HEADS_PAD), F32),
        ],
        scratch_shapes=[pltpu.VMEM((tm, d), BF16)],
        compiler_params=_cparams(2),
        name="inproj",
    )(x, x, g, w, wdt, tab)


def _rope_tables(pos):
    half = ROPE_DIM // 2
    inv = ROPE_THETA ** (-jnp.arange(half, dtype=F32) * (2.0 / ROPE_DIM))
    ang = pos.astype(F32)[:, None] * inv[None, :]
    cs = jnp.concatenate([jnp.cos(ang), jnp.sin(ang)], axis=1)
    expand = np.zeros((2 * half, 3 * LANES), np.float32)
    base = np.zeros((1, 3 * LANES), np.float32)
    for lane in range(LANES):
        m = lane % HEAD_DIM
        if m >= ROPE_DIM:
            base[0, lane] = 1.0
            continue
        expand[m % half, lane] = 1.0
        if m < half:
            expand[half + m, LANES + lane] = -1.0
        else:
            expand[m, 2 * LANES + lane] = 1.0
    cs3 = jnp.concatenate(_split3(cs), axis=1)
    expand3 = jnp.asarray(np.concatenate([expand] * 3, axis=0), BF16)
    return jnp.dot(cs3, expand3, preferred_element_type=F32) + base


def _rope(x, tab):
    half = ROPE_DIM // 2
    c, sa, sb = tab[:, :LANES], tab[:, LANES:2 * LANES], tab[:, 2 * LANES:]
    return x * c + pltpu.roll(x, LANES - half, 1) * sa + pltpu.roll(x, half, 1) * sb


def _softmax_fold(s, band, prev_bias, sink2):
    sp = s[:, :WINDOW] if prev_bias is None else s[:, :WINDOW] + prev_bias
    t = jnp.where(band, sp, s[:, WINDOW:])
    m = jnp.maximum(jnp.max(t, axis=-1, keepdims=True), sink2)
    e = jnp.exp2(t - m)
    den = jnp.sum(e, axis=-1, keepdims=True) + jnp.exp2(sink2 - m)
    p = jnp.concatenate([jnp.where(band, e, 0.0), jnp.where(band, 0.0, e)], axis=1)
    return p.astype(BF16), den


def _cast_slabs(in_refs, out_refs):
    for src, dst in zip(in_refs, out_refs):
        dst[...] = src[...].astype(dst.dtype)


def _cast_cols_kernel(w_ref, o_ref):
    o_ref[...] = w_ref[:, :o_ref.shape[1]].astype(o_ref.dtype)


def _cast_cols(w, ncols, row_block=256):
    rows, cols = w.shape
    rb = row_block if rows % row_block == 0 else rows
    return pl.pallas_call(
        _cast_cols_kernel,
        grid=(rows // rb,),
        in_specs=[pl.BlockSpec((rb, cols), lambda i: (i, 0))],
        out_specs=pl.BlockSpec((rb, ncols), lambda i: (i, 0)),
        out_shape=jax.ShapeDtypeStruct((rows, ncols), BF16),
        compiler_params=_cparams(1),
        name="cast_w_in",
    )(w)


def _cast_specs(weights, n_steps, step_of):
    if any(w.shape[0] % n_steps or (w.shape[0] // n_steps) % 16 for w in weights):
        return None
    specs = [pl.BlockSpec((w.shape[0] // n_steps, w.shape[1]), lambda *g: (step_of(*g), 0))
             for w in weights]
    shapes = [jax.ShapeDtypeStruct(w.shape, BF16) for w in weights]
    return specs, shapes


def _attn_prompt_kernel(sink_ref, q_ref, kc_ref, kp_ref, vc_ref, vp_ref, *rest, nq):
    ncast = (len(rest) - 3) // 2
    o_ref, nk_ref, nv_ref = rest[ncast:ncast + 3]
    _cast_slabs(rest[:ncast], rest[ncast + 3:])
    i = pl.program_id(1)
    nsteps = pl.num_programs(1)
    w = WINDOW
    kcr = kc_ref[...]
    kpr = kp_ref[...]
    vc = vc_ref[...]

    @pl.when(i == nsteps - 1)
    def _():
        nk_ref[...] = kcr[(nq - 1) * w:]
        nv_ref[...] = vc[(nq - 1) * w:]

    kall = jnp.concatenate([kpr, kcr], axis=0)
    vall = jnp.concatenate([vp_ref[...], vc], axis=0)
    lo = lax.broadcasted_iota(jnp.int32, ((nq + 1) * w, LANES), 1) < HEAD_DIM
    lo_q = lax.broadcasted_iota(jnp.int32, (w, LANES), 1) < HEAD_DIM
    band = (lax.broadcasted_iota(jnp.int32, (w, w), 1) > lax.broadcasted_iota(jnp.int32, (w, w), 0))
    first_bias = jnp.where(i == 0, -jnp.inf, 0.0)

    for g in range(N_KV_HEADS):
        col, odd = g // 2, g % 2
        kg = kall[:, col * LANES:(col + 1) * LANES]
        vg = vall[:, col * LANES:(col + 1) * LANES]
        kg_sw = pltpu.roll(kg, HEAD_DIM, 1)
        vg_sw = pltpu.roll(vg, HEAD_DIM, 1)
        k_lo = jnp.where(lo, kg_sw if odd else kg, 0.0).astype(BF16)
        k_hi = jnp.where(lo, 0.0, kg if odd else kg_sw).astype(BF16)
        v_lo = jnp.where(lo, vg_sw if odd else vg, 0.0).astype(BF16)
        v_hi = jnp.where(lo, 0.0, vg if odd else vg_sw).astype(BF16)
        sinks2 = [sink_ref[4 * g + r] * LOG2E for r in range(4)]
        def scores(s, g=g, k_lo=k_lo, k_hi=k_hi):
            rows = slice(s * w, (s + 1) * w)
            keys = slice(s * w, (s + 2) * w)
            qst = jnp.concatenate([q_ref[rows, (2 * g) * LANES:(2 * g + 1) * LANES],
                                   q_ref[rows, (2 * g + 1) * LANES:(2 * g + 2) * LANES]],
                                  axis=0).astype(BF16)
            return _dot_nt(qst, k_lo[keys]), _dot_nt(qst, k_hi[keys])

        ahead = scores(0)
        for s in range(nq):
            rows = slice(s * w, (s + 1) * w)
            keys = slice(s * w, (s + 2) * w)
            pb = first_bias if s == 0 else None
            s_lo, s_hi = ahead
            if s + 1 < nq:
                ahead = scores(s + 1)
            e0, d0 = _softmax_fold(s_lo[:w], band, pb, sinks2[0])
            e1, d1 = _softmax_fold(s_hi[:w], band, pb, sinks2[1])
            e2, d2 = _softmax_fold(s_lo[w:], band, pb, sinks2[2])
            e3, d3 = _softmax_fold(s_hi[w:], band, pb, sinks2[3])
            p = jnp.concatenate([jnp.concatenate([e0, e1], axis=1),
                                 jnp.concatenate([e2, e3], axis=1)], axis=0)
            vcat = jnp.concatenate([v_lo[keys], v_hi[keys]], axis=0)
            o = _dot(p, vcat)
            oa = o[:w] / jnp.where(lo_q, d0, d1)
            ob = o[w:] / jnp.where(lo_q, d2, d3)
            o_ref[rows, (2 * g) * LANES:(2 * g + 1) * LANES] = oa.astype(o_ref.dtype)
            o_ref[rows, (2 * g + 1) * LANES:(2 * g + 2) * LANES] = ob.astype(o_ref.dtype)


def _attn_prompt(proj, sinks, nq, cast_weights=()):
    b, l, _ = proj.shape
    w = WINDOW
    nsteps = l // (nq * w)
    kcol, vcol = COL_K // KV_DIM, COL_V // KV_DIM
    prev = lambda bi, i: jnp.maximum(nq * i - 1, 0)
    cast = _cast_specs(cast_weights, b * nsteps, lambda bi, i: bi * nsteps + i)
    if cast is None:
        res = _attn_prompt(proj, sinks, nq)
        return res[:3] + tuple(cw.astype(BF16) for cw in cast_weights)
    cast_specs, cast_shapes = cast
    return pl.pallas_call(
        functools.partial(_attn_prompt_kernel, nq=nq),
        grid=(b, nsteps),
        in_specs=[
            pl.BlockSpec(memory_space=pltpu.SMEM),
            pl.BlockSpec((None, nq * w, Q_DIM), lambda bi, i: (bi, i, COL_Q // Q_DIM)),
            pl.BlockSpec((None, nq * w, KV_DIM), lambda bi, i: (bi, i, kcol)),
            pl.BlockSpec((None, w, KV_DIM), lambda bi, i: (bi, prev(bi, i), kcol)),
            pl.BlockSpec((None, nq * w, KV_DIM), lambda bi, i: (bi, i, vcol)),
            pl.BlockSpec((None, w, KV_DIM), lambda bi, i: (bi, prev(bi, i), vcol)),
        ] + cast_specs,
        out_specs=[
            pl.BlockSpec((None, nq * w, Q_DIM), lambda bi, i: (bi, i, 0)),
            pl.BlockSpec((None, w, KV_DIM), lambda bi, i: (bi, 0, 0)),
            pl.BlockSpec((None, w, KV_DIM), lambda bi, i: (bi, 0, 0)),
        ] + cast_specs,
        out_shape=[
            jax.ShapeDtypeStruct((b, l, Q_DIM), BF16),
            jax.ShapeDtypeStruct((b, w, KV_DIM), F32),
            jax.ShapeDtypeStruct((b, w, KV_DIM), F32),
        ] + cast_shapes,
        compiler_params=_cparams(2),
        name="attn_prompt",
    )(sinks, proj, proj, proj, proj, proj, *cast_weights)


def _softplus(v):
    return jnp.maximum(v, 0.0) + jnp.log1p(jnp.exp(-jnp.abs(v)))


def _head_expand(vals, ex2_ref):
    hi = vals.astype(BF16)
    mid = (vals - hi.astype(F32)).astype(BF16)
    return _dot(jnp.concatenate([hi, mid], axis=1), ex2_ref[...])


def _ssd_prompt_kernel(*refs, nsub):
    nz, nx = SSM_INNER // PIECE, CONV_DIM // PIECE
    z_refs, x_refs = refs[:nz], refs[nz:nz + nx]
    dt_ref, cw_ref, cb_ref, dtb_ref, alog_ref, dsk_ref, gn_ref, ex2_ref = refs[nz + nx:nz + nx + 8]
    rest = refs[nz + nx + 8:]
    ncast = (len(rest) - 6) // 2
    y_ref, nssm_ref, nconv_ref = rest[ncast:ncast + 3]
    state_ref, carry_ref, xst_ref = rest[2 * ncast + 3:]
    _cast_slabs(rest[:ncast], rest[ncast + 3:2 * ncast + 3])
    i = pl.program_id(1)
    nc = pl.num_programs(1)
    q = SSD_CHUNK

    @pl.when(i == 0)
    def _():
        state_ref[...] = jnp.zeros_like(state_ref)
        carry_ref[...] = jnp.zeros_like(carry_ref)

    dtp = dt_ref[:q, :]
    for s in range(1, nsub):
        dtp = dtp + pltpu.roll(dt_ref[s * q:(s + 1) * q, :], s * SSM_HEADS, 1)
    dt = _softplus(dtp + dtb_ref[...])
    da = dt * (-jnp.exp(alog_ref[...]))
    tri = (lax.broadcasted_iota(jnp.int32, (q, q), 0) >= lax.broadcasted_iota(jnp.int32, (q, q), 1))
    cs = _dot_exact_lhs01(tri.astype(BF16), da)
    cs2 = cs * LOG2E
    cs2_t = cs2.T
    sc = dict(
        tri=tri, cs2=cs2, cs2_t=cs2_t, dt_t=dt.T,
        cdec_t=jnp.exp2(cs2_t[:, q - 1:q]),
        ecs_x=_head_expand(jnp.exp(cs), ex2_ref),
        wgt_x=_head_expand(dt * jnp.exp(cs[q - 1:q, :] - cs), ex2_ref))

    rows_of = lambda rs, sl: jnp.concatenate([r[sl, :] for r in rs], axis=1)
    ahead = _ssd_conv(0, rows_of, x_refs, cw_ref, cb_ref, carry_ref, xst_ref)
    for s in range(nsub):
        xc = ahead
        if s + 1 < nsub:
            ahead = _ssd_conv(s + 1, rows_of, x_refs, cw_ref, cb_ref, carry_ref, xst_ref)
        _ssd_chunk(s, sc, xc, rows_of, z_refs, dsk_ref, gn_ref, y_ref, state_ref)
    tail = rows_of(x_refs, slice(nsub * q - 8, nsub * q))
    carry_ref[...] = tail

    @pl.when(i == nc - 1)
    def _():
        nconv_ref[...] = tail
        nssm_ref[...] = state_ref[...]


def _ssd_conv(s, rows_of, x_refs, cw_ref, cb_ref, carry_ref, xst_ref):
    q = SSD_CHUNK
    x = rows_of(x_refs, slice(s * q, (s + 1) * q))
    prev = carry_ref[...] if s == 0 else rows_of(x_refs, slice(s * q - 8, s * q))
    pitch = CONV_PITCH
    outs = []
    for c in range(CONV_DIM // LANES):
        cols = slice(c * LANES, (c + 1) * LANES)
        buf = xst_ref.at[s % 2, c]
        xcol = x[:, cols]
        buf[pl.ds(0, 8, stride=pitch), :] = prev[:, cols]
        buf[pl.ds(8 * pitch, q, stride=pitch), :] = xcol
        acc = xcol * cw_ref[CONV_WIDTH - 1:CONV_WIDTH, cols] + cb_ref[:, cols]
        for k in range(1, CONV_WIDTH):
            acc = acc + (buf[pl.ds((8 - k) * pitch, q, stride=pitch), :]
                         * cw_ref[CONV_WIDTH - 1 - k:CONV_WIDTH - k, cols])
        outs.append(acc)
    return _silu(jnp.concatenate(outs, axis=1))


def _ssd_chunk(s, sc, xc, rows_of, z_refs, dsk_ref, gn_ref, y_ref, state_ref):
    q = SSD_CHUNK
    tri, cs2, cs2_t, dt_t, cdec_t = sc["tri"], sc["cs2"], sc["cs2_t"], sc["dt_t"], sc["cdec_t"]
    hoff = s * SSM_HEADS
    xoff = s * SSM_INNER
    trows = slice(s * q, (s + 1) * q)
    z = rows_of(z_refs, trows)
    xs = xc[:, :SSM_INNER]
    bm = xc[:, SSM_INNER:SSM_INNER + BC_DIM]
    cm = xc[:, SSM_INNER + BC_DIM:]

    lo = lax.broadcasted_iota(jnp.int32, (q, LANES), 1) < SSM_HEAD_DIM

    hpg = SSM_HEADS // SSM_GROUPS
    gw = hpg * SSM_HEAD_DIM
    for g in range(SSM_GROUPS):
        bg = bm[:, g * SSM_STATE:(g + 1) * SSM_STATE].astype(BF16)
        cg = cm[:, g * SSM_STATE:(g + 1) * SSM_STATE].astype(BF16)
        cb = _dot_nt(cg, bg)
        st = state_ref[g * gw:(g + 1) * gw, :]
        yoff = _dot_nt(cg, st.astype(BF16))
        ys = []
        for pr in range(2):
            pair = 2 * g + pr
            ms = []
            for h in (hoff + 2 * pair, hoff + 2 * pair + 1):
                diff = cs2[:, h:h + 1] - cs2_t[h:h + 1, :]
                lm = jnp.exp2(jnp.where(tri, diff, -jnp.inf))
                ms.append((cb * lm * dt_t[h:h + 1, :]).astype(BF16))
            xp = xs[:, pair * LANES:(pair + 1) * LANES]
            x2 = jnp.concatenate([jnp.where(lo, xp, 0.0), jnp.where(lo, 0.0, xp)],
                                 axis=0).astype(BF16)
            yd = _dot(jnp.concatenate(ms, axis=1), x2)
            yo = yoff[:, pr * LANES:(pr + 1) * LANES] * sc["ecs_x"][
                :, xoff + pair * LANES:xoff + (pair + 1) * LANES]
            ys.append(yd + yo + xp * dsk_ref[:, pair * LANES:(pair + 1) * LANES])
        yg = jnp.concatenate(ys, axis=1)
        wx = xs[:, g * gw:(g + 1) * gw] * sc["wgt_x"][:, xoff + g * gw:xoff + (g + 1) * gw]
        s_new = _dot_tn(wx.astype(BF16), bg)
        for r in range(hpg):
            h = hoff + hpg * g + r
            rows = slice(g * gw + r * SSM_HEAD_DIM, g * gw + (r + 1) * SSM_HEAD_DIM)
            state_ref[rows, :] = (st[r * SSM_HEAD_DIM:(r + 1) * SSM_HEAD_DIM, :] * cdec_t[h:h + 1, :]
                                  + s_new[r * SSM_HEAD_DIM:(r + 1) * SSM_HEAD_DIM, :])
        hg = yg * _silu(z[:, g * gw:(g + 1) * gw])
        y_ref[trows, g * gw:(g + 1) * gw] = (_rms(hg) * gn_ref[:, g * gw:(g + 1) * gw]
                                            ).astype(y_ref.dtype)


def _ssd_prompt(proj, dtraw, conv_w, conv_b, dt_bias_t, a_log_t, d_skip_x, norm_g, nsub,
                cast_weights=()):
    assert nsub * SSM_HEADS <= LANES
    nsteps = proj.shape[1] // (SSD_CHUNK * nsub)
    cast = _cast_specs(cast_weights, proj.shape[0] * nsteps, lambda bi, i: bi * nsteps + i)
    if cast is None:
        res = _ssd_prompt(proj, dtraw, conv_w, conv_b, dt_bias_t, a_log_t, d_skip_x, norm_g, nsub)
        return tuple(res[:3]) + tuple(cw.astype(BF16) for cw in cast_weights)
    cast_specs, cast_shapes = cast
    src = np.arange(nsub * SSM_INNER) // SSM_HEAD_DIM
    ex = (np.arange(LANES)[:, None] == src[None, :]).astype(np.float32)
    ex2 = jnp.asarray(np.concatenate([ex, ex], axis=0), BF16)
    b, l, _ = proj.shape
    q = SSD_CHUNK * nsub
    nc = l // q
    const = lambda bi, i: (0, 0)
    piece = lambda c: pl.BlockSpec((None, q, PIECE), lambda bi, i: (bi, i, c))
    n_pieces = (SSM_INNER + CONV_DIM) // PIECE
    return pl.pallas_call(
        functools.partial(_ssd_prompt_kernel, nsub=nsub),
        grid=(b, nc),
        in_specs=[piece(COL_Z // PIECE + c) for c in range(SSM_INNER // PIECE)] + [
            piece(COL_XBC // PIECE + c) for c in range(CONV_DIM // PIECE)] + [
            pl.BlockSpec((None, q, HEADS_PAD), lambda bi, i: (bi, i, 0)),
            pl.BlockSpec((CONV_WIDTH, CONV_DIM), const),
            pl.BlockSpec((1, CONV_DIM), const),
            pl.BlockSpec((1, HEADS_PAD), const),
            pl.BlockSpec((1, HEADS_PAD), const),
            pl.BlockSpec((1, SSM_INNER), const),
            pl.BlockSpec((1, SSM_INNER), const),
            pl.BlockSpec((2 * LANES, nsub * SSM_INNER), const, pipeline_mode=pl.Buffered(1)),
        ] + cast_specs,
        out_specs=[
            pl.BlockSpec((None, q, SSM_INNER), lambda bi, i: (bi, i, 0)),
            pl.BlockSpec((None, SSM_INNER, SSM_STATE), lambda bi, i: (bi, 0, 0)),
            pl.BlockSpec((None, 8, CONV_DIM), lambda bi, i: (bi, 0, 0)),
        ] + cast_specs,
        out_shape=[
            jax.ShapeDtypeStruct((b, l, SSM_INNER), BF16),
            jax.ShapeDtypeStruct((b, SSM_INNER, SSM_STATE), F32),
            jax.ShapeDtypeStruct((b, 8, CONV_DIM), F32),
        ] + cast_shapes,
        scratch_shapes=[pltpu.VMEM((SSM_INNER, SSM_STATE), F32),
                        pltpu.VMEM((8, CONV_DIM), F32),
                        pltpu.VMEM((2, CONV_DIM // LANES, (8 + SSD_CHUNK) * CONV_PITCH, LANES), F32)],
        compiler_params=_cparams(2),
        name="ssd_prompt",
    )(*([proj] * n_pieces), dtraw, conv_w, conv_b, dt_bias_t, a_log_t, d_skip_x, norm_g, ex2,
      *cast_weights)


def _outproj_kernel(x_ref, a_ref, y_ref, wa_ref, wy_ref, o_ref):
    o_ref[...] = (x_ref[...] + _dot(a_ref[...].astype(BF16), wa_ref[...])
                  + _dot(y_ref[...].astype(BF16), wy_ref[...]))


def _outproj(x, attn, y, w_out, tm):
    t, d = x.shape
    half = w_out.shape[0] // 2
    return pl.pallas_call(
        _outproj_kernel,
        grid=(t // tm,),
        in_specs=[
            pl.BlockSpec((tm, d), lambda i: (i, 0)),
            pl.BlockSpec((tm, half), lambda i: (i, 0)),
            pl.BlockSpec((tm, half), lambda i: (i, 0)),
            pl.BlockSpec((half, d), lambda i: (0, 0)),
            pl.BlockSpec((half, d), lambda i: (1, 0)),
        ],
        out_specs=pl.BlockSpec((tm, d), lambda i: (i, 0)),
        out_shape=jax.ShapeDtypeStruct((t, d), F32),
        compiler_params=_cparams(1),
        name="outproj",
    )(x, attn, y, w_out, w_out)


def _ffn_kernel(h_ref, g_ref, wg_ref, wu_ref, wd_ref, o_ref, f_ref):
    d = o_ref.shape[1]
    tn = min(d, 512)

    def add_delta(f, base_ref):
        gate = _dot(f, wg_ref[...])
        hid = (gate * (1.0 / (1.0 + jnp.exp(-gate))) * _dot(f, wu_ref[...])).astype(BF16)
        for c in range(d // tn):
            cols = slice(c * tn, (c + 1) * tn)
            o_ref[:, cols] = base_ref[:, cols] + _dot(hid, wd_ref[:, cols])

    @pl.when(pl.program_id(1) == 0)
    def _():
        f = (_rms(h_ref[...]) * g_ref[...]).astype(BF16)
        f_ref[...] = f
        add_delta(f, h_ref)

    @pl.when(pl.program_id(1) > 0)
    def _():
        add_delta(f_ref[...], o_ref)


def _ffn(h, g, wg, wu, wd, tm, th):
    t, d = h.shape
    hidden = wd.shape[0]
    return pl.pallas_call(
        _ffn_kernel,
        grid=(t // tm, hidden // th),
        in_specs=[
            pl.BlockSpec((tm, d), lambda i, j: (i, 0)),
            pl.BlockSpec((1, d), lambda i, j: (0, 0)),
            pl.BlockSpec((d, th), lambda i, j: (0, j)),
            pl.BlockSpec((d, th), lambda i, j: (0, j)),
            pl.BlockSpec((th, d), lambda i, j: (j, 0)),
        ],
        out_specs=pl.BlockSpec((tm, d), lambda i, j: (i, 0)),
        out_shape=jax.ShapeDtypeStruct((t, d), F32),
        scratch_shapes=[pltpu.VMEM((tm, d), BF16)],
        compiler_params=_cparams(2),
        name="ffn",
    )(h, g, wg, wu, wd)


def _ple_kernel(h_ref, p_ref, gp_ref, wp_ref, wg_ref, gf_ref, o_ref, *, tn, row_parts):
    tm, d = h_ref.shape
    rp = tm // row_parts
    for r in range(row_parts):
        rows = slice(r * rp, (r + 1) * rp)
        n = (_rms(h_ref[rows, :]) * gp_ref[...]).astype(BF16)
        pb = p_ref[rows, :].astype(BF16)
        ss = jnp.zeros((rp, 1), F32)
        for c in range(d // tn):
            cols = slice(c * tn, (c + 1) * tn)
            gate = _dot(n, wg_ref[:, cols])
            h3 = h_ref[rows, cols] + _dot(pb, wp_ref[:, cols]) * _sigmoid(gate)
            o_ref[rows, cols] = h3
            ss = ss + jnp.sum(h3 * h3, axis=-1, keepdims=True)
        inv = lax.rsqrt(ss * (1.0 / d) + RMS_EPS)
        o_ref[rows, :] = o_ref[rows, :] * inv * gf_ref[...]


def _ple_final(h, p, g_ple, w_ple, w_gate, g_final, tm):
    t, d = h.shape
    pd = p.shape[1]
    const = lambda i: (0, 0)
    resident = dict(pipeline_mode=pl.Buffered(1))
    return pl.pallas_call(
        functools.partial(_ple_kernel, tn=512, row_parts=max(tm // 256, 1)),
        grid=(t // tm,),
        in_specs=[
            pl.BlockSpec((tm, d), lambda i: (i, 0)),
            pl.BlockSpec((tm, pd), lambda i: (i, 0)),
            pl.BlockSpec((1, d), const),
            pl.BlockSpec((pd, d), const, **resident),
            pl.BlockSpec((d, d), const, **resident),
            pl.BlockSpec((1, d), const),
        ],
        out_specs=pl.BlockSpec((tm, d), lambda i: (i, 0)),
        out_shape=jax.ShapeDtypeStruct((t, d), F32),
        compiler_params=_cparams(1),
        name="ple_final",
    )(h, p, g_ple, w_ple, w_gate, g_final)


def _sample_pre_kernel(x_ref, dt_ref, sc_ref, cw_ref, cb_ref, dtb_ref,
                       alog_ref, dsk_ref, exp_ref,
                       xdt_ref, b_ref, c_ref, dec_ref, yp_ref, dech_ref):
    conv = x_ref[...] * cw_ref[CONV_WIDTH - 1:CONV_WIDTH, :] + cb_ref[...]
    for k in range(CONV_WIDTH - 1):
        conv = conv + sc_ref[k] * cw_ref[k:k + 1, :]
    xc = _silu(conv)
    xs = xc[:, :SSM_INNER]
    bm = xc[:, SSM_INNER:SSM_INNER + BC_DIM]
    cm = xc[:, SSM_INNER + BC_DIM:]
    b_ref[...] = bm
    c_ref[...] = cm
    dt = _softplus(dt_ref[...] + dtb_ref[...])
    dec = jnp.exp(dt * (-jnp.exp(alog_ref[...])))
    ex = exp_ref[...]
    dtx = _dot_exact_rhs01(dt, ex)
    dec_ref[...] = _dot_exact_rhs01(dec, ex)
    dech_ref[...] = dec
    xdt = xs * dtx
    xdt_ref[...] = xdt
    gw = SSM_INNER // SSM_GROUPS
    cbs = []
    for g in range(SSM_GROUPS):
        prod = cm[:, g * SSM_STATE:(g + 1) * SSM_STATE] * bm[:, g * SSM_STATE:(g + 1) * SSM_STATE]
        cbs.append(jnp.broadcast_to(jnp.sum(prod, axis=-1, keepdims=True), (prod.shape[0], gw)))
    yp_ref[...] = xdt * jnp.concatenate(cbs, axis=1) + xs * dsk_ref[...]


def _sample_pre(xbc, dtraw, sconv_t, conv_w, conv_b, dt_bias, a_log, d_skip_x, expand):
    nb = xbc.shape[0]
    shapes = [(nb, SSM_INNER), (nb, BC_DIM), (nb, BC_DIM),
              (nb, SSM_INNER), (nb, SSM_INNER), (nb, HEADS_PAD)]
    return pl.pallas_call(
        _sample_pre_kernel,
        out_shape=[jax.ShapeDtypeStruct(s, F32) for s in shapes],
        compiler_params=pltpu.CompilerParams(vmem_limit_bytes=VMEM_LIMIT),
        name="sample_pre",
    )(xbc, dtraw, sconv_t, conv_w, conv_b, dt_bias, a_log, d_skip_x, expand)


def _attn_sample_kernel(sink_ref, q_ref, kn_ref, vn_ref, ck_ref, cv_ref, o_ref, nk_ref, nv_ref, *, bb):
    w = WINDOW
    row = lax.broadcasted_iota(jnp.int32, (w, KV_DIM), 0)
    hrow = lax.broadcasted_iota(jnp.int32, (N_HEADS, KV_DIM), 0) // (N_HEADS // N_KV_HEADS)
    hgrp = lax.broadcasted_iota(jnp.int32, (N_HEADS, KV_DIM), 1) // HEAD_DIM
    own = hrow == hgrp
    sink = sink_ref[...] * LOG2E
    kks, vvs, scores, probs, dens = [], [], [], [], []
    for b in range(bb):
        kk = jnp.where(row == w - 1, kn_ref[b:b + 1, :], pltpu.roll(ck_ref[b], w - 1, 0))
        vv = jnp.where(row == w - 1, vn_ref[b:b + 1, :], pltpu.roll(cv_ref[b], w - 1, 0))
        nk_ref[b] = kk
        nv_ref[b] = vv
        kks.append(kk.astype(BF16))
        vvs.append(vv.astype(BF16))
    for b in range(bb):
        qb = q_ref[b]
        qrow = jnp.where(own, jnp.concatenate([qb] * N_KV_HEADS, axis=1), 0.0)
        scores.append(_dot_nt(qrow.astype(BF16), kks[b]))
    for b in range(bb):
        m = jnp.maximum(jnp.max(scores[b], axis=-1, keepdims=True), sink)
        e = jnp.exp2(scores[b] - m)
        dens.append(jnp.sum(e, axis=-1, keepdims=True) + jnp.exp2(sink - m))
        probs.append(e.astype(BF16))
    outs = [jnp.where(own, _dot(probs[b], vvs[b]), 0.0) for b in range(bb)]
    for b in range(bb):
        o = outs[b][:, :HEAD_DIM]
        for g in range(1, N_KV_HEADS):
            o = o + outs[b][:, g * HEAD_DIM:(g + 1) * HEAD_DIM]
        o_ref[b] = o / dens[b]


def _attn_sample(sinks_col, q3, knew, vnew, cache_k, cache_v, bb):
    nb = q3.shape[0]
    w = WINDOW
    return pl.pallas_call(
        functools.partial(_attn_sample_kernel, bb=bb),
        grid=(nb // bb,),
        in_specs=[
            pl.BlockSpec((N_HEADS, 1), lambda i: (0, 0)),
            pl.BlockSpec((bb, N_HEADS, HEAD_DIM), lambda i: (i, 0, 0)),
            pl.BlockSpec((bb, KV_DIM), lambda i: (i, 0)),
            pl.BlockSpec((bb, KV_DIM), lambda i: (i, 0)),
            pl.BlockSpec((bb, w, KV_DIM), lambda i: (i, 0, 0)),
            pl.BlockSpec((bb, w, KV_DIM), lambda i: (i, 0, 0)),
        ],
        out_specs=[
            pl.BlockSpec((bb, N_HEADS, HEAD_DIM), lambda i: (i, 0, 0)),
            pl.BlockSpec((bb, w, KV_DIM), lambda i: (i, 0, 0)),
            pl.BlockSpec((bb, w, KV_DIM), lambda i: (i, 0, 0)),
        ],
        out_shape=[
            jax.ShapeDtypeStruct((nb, N_HEADS, HEAD_DIM), F32),
            jax.ShapeDtypeStruct((nb, w, KV_DIM), F32),
            jax.ShapeDtypeStruct((nb, w, KV_DIM), F32),
        ],
        compiler_params=_cparams(1),
        name="attn_sample",
    )(sinks_col, q3, knew, vnew, cache_k, cache_v)


def _ssd_sample_kernel(st_ref, xdt_ref, b_ref, c_ref, dec_ref, dech_ref, yp_ref, z_ref, gn_ref,
                       y_ref, ns_ref, *, bb):
    gw = SSM_INNER // SSM_GROUPS
    grow = lax.broadcasted_iota(jnp.int32, (8, SSM_INNER), 0)
    glane = lax.broadcasted_iota(jnp.int32, (8, SSM_INNER), 1) // gw
    own = grow == glane
    pad = jnp.zeros((8 - SSM_GROUPS, SSM_STATE), F32)
    yoffs = []
    for b in range(bb):
        st = st_ref[b]
        cmat = jnp.concatenate([c_ref[b], pad], axis=0).astype(BF16)
        bmat = jnp.concatenate([b_ref[b], pad], axis=0).astype(BF16)
        r = _dot_nt(cmat, st.astype(BF16))
        yoffs.append(jnp.sum(jnp.where(own, r, 0.0), axis=0, keepdims=True))
        amat = jnp.where(own, jnp.broadcast_to(xdt_ref[b:b + 1, :], (8, SSM_INNER)), 0.0)
        outer = _dot_tn(amat.astype(BF16), bmat)
        for h in range(SSM_HEADS):
            rows = slice(h * SSM_HEAD_DIM, (h + 1) * SSM_HEAD_DIM)
            ns_ref[b, rows, :] = st[rows, :] * dech_ref[b:b + 1, h:h + 1] + outer[rows, :]
    y = yp_ref[...] + jnp.concatenate(yoffs, axis=0) * dec_ref[...]
    hg = y * _silu(z_ref[...])
    outs = []
    for g in range(SSM_GROUPS):
        outs.append(_rms(hg[:, g * gw:(g + 1) * gw]))
    y_ref[...] = jnp.concatenate(outs, axis=1) * gn_ref[...]


def _ssd_sample(state, xdt, b3, c3, decx, dech, ypart, z, norm_g, bb):
    nb = state.shape[0]
    row = lambda i: (i, 0)
    return pl.pallas_call(
        functools.partial(_ssd_sample_kernel, bb=bb),
        grid=(nb // bb,),
        in_specs=[
            pl.BlockSpec((bb, SSM_INNER, SSM_STATE), lambda i: (i, 0, 0)),
            pl.BlockSpec((bb, SSM_INNER), row),
            pl.BlockSpec((bb, SSM_GROUPS, SSM_STATE), lambda i: (i, 0, 0)),
            pl.BlockSpec((bb, SSM_GROUPS, SSM_STATE), lambda i: (i, 0, 0)),
            pl.BlockSpec((bb, SSM_INNER), row),
            pl.BlockSpec((bb, HEADS_PAD), row),
            pl.BlockSpec((bb, SSM_INNER), row),
            pl.BlockSpec((bb, SSM_INNER), row),
            pl.BlockSpec((1, SSM_INNER), lambda i: (0, 0)),
        ],
        out_specs=[
            pl.BlockSpec((bb, SSM_INNER), row),
            pl.BlockSpec((bb, SSM_INNER, SSM_STATE), lambda i: (i, 0, 0)),
        ],
        out_shape=[
            jax.ShapeDtypeStruct((nb, SSM_INNER), F32),
            jax.ShapeDtypeStruct((nb, SSM_INNER, SSM_STATE), F32),
        ],
        compiler_params=_cparams(1),
        name="ssd_sample",
    )(state, xdt, b3, c3, decx, dech, ypart, z, norm_g)


def _row_tile(t, want):
    return want if t % want == 0 else t


def _dense_tail(x, attn, y, p, wts, tm):
    h1 = _outproj(x, attn, y, wts["w_out"], tm)
    h2 = _ffn(h1, wts["g_ffn"], wts["w_gate"], wts["w_up"], wts["w_down"],
              _row_tile(x.shape[0], FFN_TM), FFN_TH)
    return _ple_final(h2, p, wts["g_ple"], wts["w_ple"], wts["w_ple_gate"], wts["g_final"],
                      _row_tile(x.shape[0], 1024))


def kernel(x_prompt, x_sample, cache_k, cache_v, state_ssm, state_conv, p_prompt, p_sample, w_in, conv_w, conv_b, dt_bias, a_log, d_skip, ssm_norm_g, attn_sinks, w_out, g_mix, g_ffn, w_ffn_gate, w_ffn_up, w_ffn_down, g_ple, w_ple, w_ple_gate, g_final):
    depth = w_in.shape[0]
    assert depth == 1, "single-layer step only"
    bsz, seq, d = x_prompt.shape
    nb, dseq, _ = x_sample.shape
    assert dseq == 1 and seq % SSD_CHUNK == 0 and seq % WINDOW == 0

    w = w_in[0]
    w_main = _cast_cols(w, PROJ_DIM)
    w_dt = jnp.pad(w[:, PROJ_DIM:], ((0, 0), (0, HEADS_PAD - SSM_HEADS))).astype(BF16)
    padh = lambda v: jnp.pad(v, (0, HEADS_PAD - SSM_HEADS)).reshape(1, HEADS_PAD)
    tileh = lambda v: jnp.tile(v, HEADS_PAD // SSM_HEADS).reshape(1, HEADS_PAD)
    dtb, alog = padh(dt_bias[0]), padh(a_log[0])
    dsk_x = jnp.repeat(d_skip[0], SSM_HEAD_DIM).reshape(1, SSM_INNER)
    gn = ssm_norm_g[0].reshape(1, SSM_INNER)
    cw, cb = conv_w[0], conv_b[0].reshape(1, CONV_DIM)
    wts = dict(g_ffn=g_ffn[0].reshape(1, d), g_ple=g_ple[0].reshape(1, d),
               g_final=g_final.reshape(1, d))
    gmix = g_mix[0].reshape(1, d)
    sinks = attn_sinks[0]

    tp = bsz * seq
    xp = x_prompt.reshape(tp, d)
    tm_in = _row_tile(seq, 1024)
    proj, dtraw = _inproj(xp, gmix, w_main, w_dt, _rope_tables(jnp.arange(seq)), tm_in)
    proj3 = proj.reshape(bsz, seq, PROJ_DIM)
    (attn, nk_p, nv_p, wts["w_gate"], wts["w_up"], wts["w_down"], wts["w_out"], wts["w_ple_gate"],
     wts["w_ple"]) = _attn_prompt(
        proj3, sinks, ATTN_BLOCKS,
        (w_ffn_gate[0], w_ffn_up[0], w_ffn_down[0], w_out[0], w_ple_gate[0], w_ple[0]))
    yp, nssm_p, nconv_p = _ssd_prompt(
        proj3, dtraw.reshape(bsz, seq, HEADS_PAD), cw, cb, tileh(dt_bias[0]), tileh(a_log[0]),
        dsk_x, gn, SSD_CHUNKS)
    tm = _row_tile(tp, 512)
    y_prompt = _dense_tail(xp, attn.reshape(tp, Q_DIM), yp.reshape(tp, SSM_INNER),
                           p_prompt[0].reshape(tp, -1), wts, tm).reshape(bsz, seq, d)

    xs = x_sample.reshape(nb, d)
    tab_s = _rope_tables(jnp.full((nb,), PAST_LEN, jnp.int32))
    proj_s, dtraw_s = _inproj(xs, gmix, w_main, w_dt, tab_s, nb)
    window = cache_k.shape[2]
    sconv_t = jnp.transpose(state_conv[0], (1, 0, 2))
    expand = (jnp.arange(HEADS_PAD)[:, None] == (jnp.arange(SSM_INNER) // SSM_HEAD_DIM)[None, :]
              ).astype(BF16)
    q_s = proj_s[:, COL_Q:COL_Q + Q_DIM]
    k_s = proj_s[:, COL_K:COL_K + KV_DIM]
    v_s = proj_s[:, COL_V:COL_V + KV_DIM]
    z_s = proj_s[:, COL_Z:COL_Z + SSM_INNER]
    xbc_s = proj_s[:, COL_XBC:COL_XBC + CONV_DIM]
    xdt, bm, cm, decx, ypart, dech = _sample_pre(
        xbc_s, dtraw_s, sconv_t, cw, cb, dtb, alog, dsk_x, expand)
    bb = DECODE_BB if nb % DECODE_BB == 0 else nb
    attn_s, nk_s, nv_s = _attn_sample(
        sinks.reshape(N_HEADS, 1), q_s.reshape(nb, N_HEADS, HEAD_DIM), k_s, v_s,
        cache_k[0].reshape(nb, window, KV_DIM), cache_v[0].reshape(nb, window, KV_DIM), bb)
    ys, nssm_s = _ssd_sample(
        state_ssm[0].reshape(nb, SSM_INNER, SSM_STATE), xdt,
        bm.reshape(nb, SSM_GROUPS, SSM_STATE), cm.reshape(nb, SSM_GROUPS, SSM_STATE),
        decx, dech, ypart, z_s, gn, bb)
    y_sample = _dense_tail(xs, attn_s.reshape(nb, Q_DIM), ys, p_sample[0].reshape(nb, -1), wts,
                           nb).reshape(nb, 1, d)
    nconv_s = jnp.concatenate([state_conv[0][:, 1:], xbc_s[:, None, :]], axis=1)

    kv5 = lambda t: t.reshape(1, t.shape[0], window, N_KV_HEADS, HEAD_DIM)
    return (y_prompt, y_sample,
            kv5(nk_p), kv5(nv_p),
            nssm_p.reshape(1, bsz, SSM_HEADS, SSM_HEAD_DIM, SSM_STATE),
            nconv_p[None, :, 8 - (CONV_WIDTH - 1):, :],
            kv5(nk_s), kv5(nv_s),
            nssm_s.reshape(1, nb, SSM_HEADS, SSM_HEAD_DIM, SSM_STATE),
            nconv_s[None])
```

```python
import functools

import numpy as np
import jax
import jax.numpy as jnp
from jax import lax
from jax.experimental import pallas as pl
from jax.experimental.pallas import tpu as pltpu

F32 = jnp.float32
BF16 = jnp.bfloat16

HEAD_DIM = 64
N_HEADS = 16
N_KV_HEADS = 4
WINDOW = 128
ROPE_DIM = 16
ROPE_THETA = 500000.0
SSM_HEADS = 16
SSM_HEAD_DIM = 64
SSM_GROUPS = 4
SSM_STATE = 128
CONV_WIDTH = 4
SSD_CHUNK = 128
RMS_EPS = 1e-6
PAST_LEN = 16384
LOG2E = 1.4426950408889634

Q_DIM = N_HEADS * HEAD_DIM
KV_DIM = N_KV_HEADS * HEAD_DIM
SSM_INNER = SSM_HEADS * SSM_HEAD_DIM
BC_DIM = SSM_GROUPS * SSM_STATE
CONV_DIM = SSM_INNER + 2 * BC_DIM
LANES = 128
HEADS_PAD = LANES

COL_Q = 0
COL_K = Q_DIM
COL_V = COL_K + KV_DIM
COL_Z = COL_V + KV_DIM
COL_XBC = COL_Z + SSM_INNER
PROJ_DIM = COL_XBC + CONV_DIM
PIECE = 512

INPROJ_TN = Q_DIM + 2 * KV_DIM
Q_SCALE = HEAD_DIM ** -0.5 * LOG2E
FFN_TM, FFN_TH = 1024, 512
CONV_PITCH = 3
DECODE_BB = 16
ATTN_BLOCKS = 8
SSD_CHUNKS = 8
VMEM_LIMIT = 56 * 1024 * 1024


def _cparams(n_axes):
    return pltpu.CompilerParams(
        dimension_semantics=("arbitrary",) * n_axes, vmem_limit_bytes=VMEM_LIMIT)


def _rms(x):
    return x * lax.rsqrt(jnp.mean(x * x, axis=-1, keepdims=True) + RMS_EPS)


def _sigmoid(x):
    return 1.0 / (1.0 + jnp.exp2(x * -LOG2E))


def _silu(x):
    return x * _sigmoid(x)


def _dot(a, b):
    return jnp.dot(a, b, preferred_element_type=F32)


def _dot_nt(a, b):
    return lax.dot_general(a, b, (((1,), (1,)), ((), ())), preferred_element_type=F32)


def _dot_tn(a, b):
    return lax.dot_general(a, b, (((0,), (0,)), ((), ())), preferred_element_type=F32)


def _split3(x):
    hi = x.astype(BF16)
    r = x - hi.astype(F32)
    mid = r.astype(BF16)
    lo = (r - mid.astype(F32)).astype(BF16)
    return hi, mid, lo


def _dot_exact_lhs01(m01, x):
    hi, mid, lo = _split3(x)
    return _dot(m01, hi) + _dot(m01, mid) + _dot(m01, lo)


def _dot_exact_rhs01(x, m01):
    hi, mid, lo = _split3(x)
    return _dot(hi, m01) + _dot(mid, m01) + _dot(lo, m01)


def _inproj_kernel(xa_ref, xb_ref, g_ref, w_ref, wdt_ref, tab_ref, o_ref, odt_ref, u_ref):
    j = pl.program_id(1)

    @pl.when(j == 0)
    def _():
        rp = xa_ref.shape[0]
        for p, x_ref in enumerate((xa_ref, xb_ref)):
            rows = slice(p * rp, (p + 1) * rp)
            u = (_rms(x_ref[...]) * g_ref[...]).astype(BF16)
            u_ref[rows, :] = u
            odt_ref[rows, :] = _dot(u, wdt_ref[...])
            res = _dot(u, w_ref[...])
            tab = tab_ref[rows, :]
            for c in range((Q_DIM + KV_DIM) // LANES):
                cols = slice(c * LANES, (c + 1) * LANES)
                r = _rope(res[:, cols], tab)
                o_ref[rows, cols] = r * Q_SCALE if c < Q_DIM // LANES else r
            o_ref[rows, Q_DIM + KV_DIM:] = res[:, Q_DIM + KV_DIM:]

    @pl.when(j > 0)
    def _():
        o_ref[...] = _dot(u_ref[...], w_ref[...])


def _inproj(x, g, w, wdt, tab, tm):
    t, d = x.shape
    n = PROJ_DIM
    tn = INPROJ_TN
    nt = tab.shape[0] // tm
    ni, nj = t // tm, n // tn
    early = lambda i, j: jnp.minimum(i + (j == nj - 1).astype(jnp.int32), ni - 1)
    return pl.pallas_call(
        _inproj_kernel,
        grid=(ni, nj),
        in_specs=[
            pl.BlockSpec((tm // 2, d), lambda i, j: (2 * i, 0)),
            pl.BlockSpec((tm // 2, d), lambda i, j: (2 * early(i, j) + 1, 0)),
            pl.BlockSpec((1, d), lambda i, j: (0, 0)),
            pl.BlockSpec((d, tn), lambda i, j: (0, j)),
            pl.BlockSpec((d, HEADS_PAD), lambda i, j: (0, 0)),
            pl.BlockSpec((tm, 3 * LANES), lambda i, j: (i % nt, 0)),
        ],
        out_specs=[
            pl.BlockSpec((tm, tn), lambda i, j: (i, j)),
            pl.BlockSpec((tm, HEADS_PAD), lambda i, j: (i, 0)),
        ],
        out_shape=[
            jax.ShapeDtypeStruct((t, n), F32),
            jax.ShapeDtypeStruct((t, HEADS_PAD), F32),
        ],
        scratch_shapes=[pltpu.VMEM((tm, d), BF16)],
        compiler_params=_cparams(2),
        name="inproj",
    )(x, x, g, w, wdt, tab)


def _rope_tables(pos):
    half = ROPE_DIM // 2
    inv = ROPE_THETA ** (-jnp.arange(half, dtype=F32) * (2.0 / ROPE_DIM))
    ang = pos.astype(F32)[:, None] * inv[None, :]
    cs = jnp.concatenate([jnp.cos(ang), jnp.sin(ang)], axis=1)
    expand = np.zeros((2 * half, 3 * LANES), np.float32)
    base = np.zeros((1, 3 * LANES), np.float32)
    for lane in range(LANES):
        m = lane % HEAD_DIM
        if m >= ROPE_DIM:
            base[0, lane] = 1.0
            continue
        expand[m % half, lane] = 1.0
        if m < half:
            expand[half + m, LANES + lane] = -1.0
        else:
            expand[m, 2 * LANES + lane] = 1.0
    cs3 = jnp.concatenate(_split3(cs), axis=1)
    expand3 = jnp.asarray(np.concatenate([expand] * 3, axis=0), BF16)
    return jnp.dot(cs3, expand3, preferred_element_type=F32) + base


def _rope(x, tab):
    half = ROPE_DIM // 2
    c, sa, sb = tab[:, :LANES], tab[:, LANES:2 * LANES], tab[:, 2 * LANES:]
    return x * c + pltpu.roll(x, LANES - half, 1) * sa + pltpu.roll(x, half, 1) * sb


def _softmax_fold(s, band, prev_bias, sink2):
    sp = s[:, :WINDOW] if prev_bias is None else s[:, :WINDOW] + prev_bias
    t = jnp.where(band, sp, s[:, WINDOW:])
    m = jnp.maximum(jnp.max(t, axis=-1, keepdims=True), sink2)
    e = jnp.exp2(t - m)
    den = jnp.sum(e, axis=-1, keepdims=True) + jnp.exp2(sink2 - m)
    p = jnp.concatenate([jnp.where(band, e, 0.0), jnp.where(band, 0.0, e)], axis=1)
    return p.astype(BF16), den


def _cast_slabs(in_refs, out_refs):
    for src, dst in zip(in_refs, out_refs):
        dst[...] = src[...].astype(dst.dtype)


def _cast_specs(weights, n_steps, step_of):
    if any(w.shape[0] % n_steps or (w.shape[0] // n_steps) % 16 for w in weights):
        return None
    specs = [pl.BlockSpec((w.shape[0] // n_steps, w.shape[1]), lambda *g: (step_of(*g), 0))
             for w in weights]
    shapes = [jax.ShapeDtypeStruct(w.shape, BF16) for w in weights]
    return specs, shapes


def _attn_prompt_kernel(sink_ref, q_ref, kc_ref, kp_ref, vc_ref, vp_ref, *rest, nq):
    ncast = (len(rest) - 3) // 2
    o_ref, nk_ref, nv_ref = rest[ncast:ncast + 3]
    _cast_slabs(rest[:ncast], rest[ncast + 3:])
    i = pl.program_id(1)
    nsteps = pl.num_programs(1)
    w = WINDOW
    kcr = kc_ref[...]
    kpr = kp_ref[...]
    vc = vc_ref[...]

    @pl.when(i == nsteps - 1)
    def _():
        nk_ref[...] = kcr[(nq - 1) * w:]
        nv_ref[...] = vc[(nq - 1) * w:]

    kall = jnp.concatenate([kpr, kcr], axis=0)
    vall = jnp.concatenate([vp_ref[...], vc], axis=0)
    lo = lax.broadcasted_iota(jnp.int32, ((nq + 1) * w, LANES), 1) < HEAD_DIM
    lo_q = lax.broadcasted_iota(jnp.int32, (w, LANES), 1) < HEAD_DIM
    band = (lax.broadcasted_iota(jnp.int32, (w, w), 1) > lax.broadcasted_iota(jnp.int32, (w, w), 0))
    first_bias = jnp.where(i == 0, -jnp.inf, 0.0)

    for g in range(N_KV_HEADS):
        col, odd = g // 2, g % 2
        kg = kall[:, col * LANES:(col + 1) * LANES]
        vg = vall[:, col * LANES:(col + 1) * LANES]
        kg_sw = pltpu.roll(kg, HEAD_DIM, 1)
        vg_sw = pltpu.roll(vg, HEAD_DIM, 1)
        k_lo = jnp.where(lo, kg_sw if odd else kg, 0.0).astype(BF16)
        k_hi = jnp.where(lo, 0.0, kg if odd else kg_sw).astype(BF16)
        v_lo = jnp.where(lo, vg_sw if odd else vg, 0.0).astype(BF16)
        v_hi = jnp.where(lo, 0.0, vg if odd else vg_sw).astype(BF16)
        sinks2 = [sink_ref[4 * g + r] * LOG2E for r in range(4)]
        def scores(s, g=g, k_lo=k_lo, k_hi=k_hi):
            rows = slice(s * w, (s + 1) * w)
            keys = slice(s * w, (s + 2) * w)
            qst = jnp.concatenate([q_ref[rows, (2 * g) * LANES:(2 * g + 1) * LANES],
                                   q_ref[rows, (2 * g + 1) * LANES:(2 * g + 2) * LANES]],
                                  axis=0).astype(BF16)
            return _dot_nt(qst, k_lo[keys]), _dot_nt(qst, k_hi[keys])

        ahead = scores(0)
        for s in range(nq):
            rows = slice(s * w, (s + 1) * w)
            keys = slice(s * w, (s + 2) * w)
            pb = first_bias if s == 0 else None
            s_lo, s_hi = ahead
            if s + 1 < nq:
                ahead = scores(s + 1)
            e0, d0 = _softmax_fold(s_lo[:w], band, pb, sinks2[0])
            e1, d1 = _softmax_fold(s_hi[:w], band, pb, sinks2[1])
            e2, d2 = _softmax_fold(s_lo[w:], band, pb, sinks2[2])
            e3, d3 = _softmax_fold(s_hi[w:], band, pb, sinks2[3])
            p = jnp.concatenate([jnp.concatenate([e0, e1], axis=1),
                                 jnp.concatenate([e2, e3], axis=1)], axis=0)
            vcat = jnp.concatenate([v_lo[keys], v_hi[keys]], axis=0)
            o = _dot(p, vcat)
            oa = o[:w] / jnp.where(lo_q, d0, d1)
            ob = o[w:] / jnp.where(lo_q, d2, d3)
            o_ref[rows, (2 * g) * LANES:(2 * g + 1) * LANES] = oa.astype(o_ref.dtype)
            o_ref[rows, (2 * g + 1) * LANES:(2 * g + 2) * LANES] = ob.astype(o_ref.dtype)


def _attn_prompt(proj, sinks, nq, cast_weights=()):
    b, l, _ = proj.shape
    w = WINDOW
    nsteps = l // (nq * w)
    kcol, vcol = COL_K // KV_DIM, COL_V // KV_DIM
    prev = lambda bi, i: jnp.maximum(nq * i - 1, 0)
    cast = _cast_specs(cast_weights, b * nsteps, lambda bi, i: bi * nsteps + i)
    if cast is None:
        res = _attn_prompt(proj, sinks, nq)
        return res[:3] + tuple(cw.astype(BF16) for cw in cast_weights)
    cast_specs, cast_shapes = cast
    return pl.pallas_call(
        functools.partial(_attn_prompt_kernel, nq=nq),
        grid=(b, nsteps),
        in_specs=[
            pl.BlockSpec(memory_space=pltpu.SMEM),
            pl.BlockSpec((None, nq * w, Q_DIM), lambda bi, i: (bi, i, COL_Q // Q_DIM)),
            pl.BlockSpec((None, nq * w, KV_DIM), lambda bi, i: (bi, i, kcol)),
            pl.BlockSpec((None, w, KV_DIM), lambda bi, i: (bi, prev(bi, i), kcol)),
            pl.BlockSpec((None, nq * w, KV_DIM), lambda bi, i: (bi, i, vcol)),
            pl.BlockSpec((None, w, KV_DIM), lambda bi, i: (bi, prev(bi, i), vcol)),
        ] + cast_specs,
        out_specs=[
            pl.BlockSpec((None, nq * w, Q_DIM), lambda bi, i: (bi, i, 0)),
            pl.BlockSpec((None, w, KV_DIM), lambda bi, i: (bi, 0, 0)),
            pl.BlockSpec((None, w, KV_DIM), lambda bi, i: (bi, 0, 0)),
        ] + cast_specs,
        out_shape=[
            jax.ShapeDtypeStruct((b, l, Q_DIM), BF16),
            jax.ShapeDtypeStruct((b, w, KV_DIM), F32),
            jax.ShapeDtypeStruct((b, w, KV_DIM), F32),
        ] + cast_shapes,
        compiler_params=_cparams(2),
        name="attn_prompt",
    )(sinks, proj, proj, proj, proj, proj, *cast_weights)


def _softplus(v):
    return jnp.maximum(v, 0.0) + jnp.log1p(jnp.exp(-jnp.abs(v)))


def _head_expand(vals, ex2_ref):
    hi = vals.astype(BF16)
    mid = (vals - hi.astype(F32)).astype(BF16)
    return _dot(jnp.concatenate([hi, mid], axis=1), ex2_ref[...])


def _ssd_prompt_kernel(*refs, nsub):
    nz, nx = SSM_INNER // PIECE, CONV_DIM // PIECE
    z_refs, x_refs = refs[:nz], refs[nz:nz + nx]
    dt_ref, cw_ref, cb_ref, dtb_ref, alog_ref, dsk_ref, gn_ref, ex2_ref = refs[nz + nx:nz + nx + 8]
    rest = refs[nz + nx + 8:]
    ncast = (len(rest) - 6) // 2
    y_ref, nssm_ref, nconv_ref = rest[ncast:ncast + 3]
    state_ref, carry_ref, xst_ref = rest[2 * ncast + 3:]
    _cast_slabs(rest[:ncast], rest[ncast + 3:2 * ncast + 3])
    i = pl.program_id(1)
    nc = pl.num_programs(1)
    q = SSD_CHUNK

    @pl.when(i == 0)
    def _():
        state_ref[...] = jnp.zeros_like(state_ref)
        carry_ref[...] = jnp.zeros_like(carry_ref)

    dtp = dt_ref[:q, :]
    for s in range(1, nsub):
        dtp = dtp + pltpu.roll(dt_ref[s * q:(s + 1) * q, :], s * SSM_HEADS, 1)
    dt = _softplus(dtp + dtb_ref[...])
    da = dt * (-jnp.exp(alog_ref[...]))
    tri = (lax.broadcasted_iota(jnp.int32, (q, q), 0) >= lax.broadcasted_iota(jnp.int32, (q, q), 1))
    cs = _dot_exact_lhs01(tri.astype(BF16), da)
    cs2 = cs * LOG2E
    cs2_t = cs2.T
    sc = dict(
        tri=tri, cs2=cs2, cs2_t=cs2_t, dt_t=dt.T,
        cdec_t=jnp.exp2(cs2_t[:, q - 1:q]),
        ecs_x=_head_expand(jnp.exp(cs), ex2_ref),
        wgt_x=_head_expand(dt * jnp.exp(cs[q - 1:q, :] - cs), ex2_ref))

    rows_of = lambda rs, sl: jnp.concatenate([r[sl, :] for r in rs], axis=1)
    ahead = _ssd_conv(0, rows_of, x_refs, cw_ref, cb_ref, carry_ref, xst_ref)
    for s in range(nsub):
        xc = ahead
        if s + 1 < nsub:
            ahead = _ssd_conv(s + 1, rows_of, x_refs, cw_ref, cb_ref, carry_ref, xst_ref)
        _ssd_chunk(s, sc, xc, rows_of, z_refs, dsk_ref, gn_ref, y_ref, state_ref)
    tail = rows_of(x_refs, slice(nsub * q - 8, nsub * q))
    carry_ref[...] = tail

    @pl.when(i == nc - 1)
    def _():
        nconv_ref[...] = tail
        nssm_ref[...] = state_ref[...]


def _ssd_conv(s, rows_of, x_refs, cw_ref, cb_ref, carry_ref, xst_ref):
    q = SSD_CHUNK
    x = rows_of(x_refs, slice(s * q, (s + 1) * q))
    prev = carry_ref[...] if s == 0 else rows_of(x_refs, slice(s * q - 8, s * q))
    pitch = CONV_PITCH
    outs = []
    for c in range(CONV_DIM // LANES):
        cols = slice(c * LANES, (c + 1) * LANES)
        buf = xst_ref.at[s % 2, c]
        xcol = x[:, cols]
        buf[pl.ds(0, 8, stride=pitch), :] = prev[:, cols]
        buf[pl.ds(8 * pitch, q, stride=pitch), :] = xcol
        acc = xcol * cw_ref[CONV_WIDTH - 1:CONV_WIDTH, cols] + cb_ref[:, cols]
        for k in range(1, CONV_WIDTH):
            acc = acc + (buf[pl.ds((8 - k) * pitch, q, stride=pitch), :]
                         * cw_ref[CONV_WIDTH - 1 - k:CONV_WIDTH - k, cols])
        outs.append(acc)
    return _silu(jnp.concatenate(outs, axis=1))


def _ssd_chunk(s, sc, xc, rows_of, z_refs, dsk_ref, gn_ref, y_ref, state_ref):
    q = SSD_CHUNK
    tri, cs2, cs2_t, dt_t, cdec_t = sc["tri"], sc["cs2"], sc["cs2_t"], sc["dt_t"], sc["cdec_t"]
    hoff = s * SSM_HEADS
    xoff = s * SSM_INNER
    trows = slice(s * q, (s + 1) * q)
    z = rows_of(z_refs, trows)
    xs = xc[:, :SSM_INNER]
    bm = xc[:, SSM_INNER:SSM_INNER + BC_DIM]
    cm = xc[:, SSM_INNER + BC_DIM:]

    lo = lax.broadcasted_iota(jnp.int32, (q, LANES), 1) < SSM_HEAD_DIM

    hpg = SSM_HEADS // SSM_GROUPS
    gw = hpg * SSM_HEAD_DIM
    for g in range(SSM_GROUPS):
        bg = bm[:, g * SSM_STATE:(g + 1) * SSM_STATE].astype(BF16)
        cg = cm[:, g * SSM_STATE:(g + 1) * SSM_STATE].astype(BF16)
        cb = _dot_nt(cg, bg)
        st = state_ref[g * gw:(g + 1) * gw, :]
        yoff = _dot_nt(cg, st.astype(BF16))
        ys = []
        for pr in range(2):
            pair = 2 * g + pr
            ms = []
            for h in (hoff + 2 * pair, hoff + 2 * pair + 1):
                diff = cs2[:, h:h + 1] - cs2_t[h:h + 1, :]
                lm = jnp.exp2(jnp.where(tri, diff, -jnp.inf))
                ms.append((cb * lm * dt_t[h:h + 1, :]).astype(BF16))
            xp = xs[:, pair * LANES:(pair + 1) * LANES]
            x2 = jnp.concatenate([jnp.where(lo, xp, 0.0), jnp.where(lo, 0.0, xp)],
                                 axis=0).astype(BF16)
            yd = _dot(jnp.concatenate(ms, axis=1), x2)
            yo = yoff[:, pr * LANES:(pr + 1) * LANES] * sc["ecs_x"][
                :, xoff + pair * LANES:xoff + (pair + 1) * LANES]
            ys.append(yd + yo + xp * dsk_ref[:, pair * LANES:(pair + 1) * LANES])
        yg = jnp.concatenate(ys, axis=1)
        wx = xs[:, g * gw:(g + 1) * gw] * sc["wgt_x"][:, xoff + g * gw:xoff + (g + 1) * gw]
        s_new = _dot_tn(wx.astype(BF16), bg)
        for r in range(hpg):
            h = hoff + hpg * g + r
            rows = slice(g * gw + r * SSM_HEAD_DIM, g * gw + (r + 1) * SSM_HEAD_DIM)
            state_ref[rows, :] = (st[r * SSM_HEAD_DIM:(r + 1) * SSM_HEAD_DIM, :] * cdec_t[h:h + 1, :]
                                  + s_new[r * SSM_HEAD_DIM:(r + 1) * SSM_HEAD_DIM, :])
        hg = yg * _silu(z[:, g * gw:(g + 1) * gw])
        y_ref[trows, g * gw:(g + 1) * gw] = (_rms(hg) * gn_ref[:, g * gw:(g + 1) * gw]
                                            ).astype(y_ref.dtype)


def _ssd_prompt(proj, dtraw, conv_w, conv_b, dt_bias_t, a_log_t, d_skip_x, norm_g, nsub,
                cast_weights=()):
    assert nsub * SSM_HEADS <= LANES
    nsteps = proj.shape[1] // (SSD_CHUNK * nsub)
    cast = _cast_specs(cast_weights, proj.shape[0] * nsteps, lambda bi, i: bi * nsteps + i)
    if cast is None:
        res = _ssd_prompt(proj, dtraw, conv_w, conv_b, dt_bias_t, a_log_t, d_skip_x, norm_g, nsub)
        return tuple(res[:3]) + tuple(cw.astype(BF16) for cw in cast_weights)
    cast_specs, cast_shapes = cast
    src = np.arange(nsub * SSM_INNER) // SSM_HEAD_DIM
    ex = (np.arange(LANES)[:, None] == src[None, :]).astype(np.float32)
    ex2 = jnp.asarray(np.concatenate([ex, ex], axis=0), BF16)
    b, l, _ = proj.shape
    q = SSD_CHUNK * nsub
    nc = l // q
    const = lambda bi, i: (0, 0)
    piece = lambda c: pl.BlockSpec((None, q, PIECE), lambda bi, i: (bi, i, c))
    n_pieces = (SSM_INNER + CONV_DIM) // PIECE
    return pl.pallas_call(
        functools.partial(_ssd_prompt_kernel, nsub=nsub),
        grid=(b, nc),
        in_specs=[piece(COL_Z // PIECE + c) for c in range(SSM_INNER // PIECE)] + [
            piece(COL_XBC // PIECE + c) for c in range(CONV_DIM // PIECE)] + [
            pl.BlockSpec((None, q, HEADS_PAD), lambda bi, i: (bi, i, 0)),
            pl.BlockSpec((CONV_WIDTH, CONV_DIM), const),
            pl.BlockSpec((1, CONV_DIM), const),
            pl.BlockSpec((1, HEADS_PAD), const),
            pl.BlockSpec((1, HEADS_PAD), const),
            pl.BlockSpec((1, SSM_INNER), const),
            pl.BlockSpec((1, SSM_INNER), const),
            pl.BlockSpec((2 * LANES, nsub * SSM_INNER), const, pipeline_mode=pl.Buffered(1)),
        ] + cast_specs,
        out_specs=[
            pl.BlockSpec((None, q, SSM_INNER), lambda bi, i: (bi, i, 0)),
            pl.BlockSpec((None, SSM_INNER, SSM_STATE), lambda bi, i: (bi, 0, 0)),
            pl.BlockSpec((None, 8, CONV_DIM), lambda bi, i: (bi, 0, 0)),
        ] + cast_specs,
        out_shape=[
            jax.ShapeDtypeStruct((b, l, SSM_INNER), BF16),
            jax.ShapeDtypeStruct((b, SSM_INNER, SSM_STATE), F32),
            jax.ShapeDtypeStruct((b, 8, CONV_DIM), F32),
        ] + cast_shapes,
        scratch_shapes=[pltpu.VMEM((SSM_INNER, SSM_STATE), F32),
                        pltpu.VMEM((8, CONV_DIM), F32),
                        pltpu.VMEM((2, CONV_DIM // LANES, (8 + SSD_CHUNK) * CONV_PITCH, LANES), F32)],
        compiler_params=_cparams(2),
        name="ssd_prompt",
    )(*([proj] * n_pieces), dtraw, conv_w, conv_b, dt_bias_t, a_log_t, d_skip_x, norm_g, ex2,
      *cast_weights)


def _outproj_kernel(x_ref, a_ref, y_ref, wa_ref, wy_ref, o_ref):
    o_ref[...] = (x_ref[...] + _dot(a_ref[...].astype(BF16), wa_ref[...])
                  + _dot(y_ref[...].astype(BF16), wy_ref[...]))


def _outproj(x, attn, y, w_out, tm):
    t, d = x.shape
    half = w_out.shape[0] // 2
    return pl.pallas_call(
        _outproj_kernel,
        grid=(t // tm,),
        in_specs=[
            pl.BlockSpec((tm, d), lambda i: (i, 0)),
            pl.BlockSpec((tm, half), lambda i: (i, 0)),
            pl.BlockSpec((tm, half), lambda i: (i, 0)),
            pl.BlockSpec((half, d), lambda i: (0, 0)),
            pl.BlockSpec((half, d), lambda i: (1, 0)),
        ],
        out_specs=pl.BlockSpec((tm, d), lambda i: (i, 0)),
        out_shape=jax.ShapeDtypeStruct((t, d), F32),
        compiler_params=_cparams(1),
        name="outproj",
    )(x, attn, y, w_out, w_out)


def _ffn_kernel(h_ref, g_ref, wg_ref, wu_ref, wd_ref, o_ref, f_ref):
    d = o_ref.shape[1]
    tn = min(d, 512)

    def add_delta(f, base_ref):
        gate = _dot(f, wg_ref[...])
        hid = (gate * (1.0 / (1.0 + jnp.exp(-gate))) * _dot(f, wu_ref[...])).astype(BF16)
        for c in range(d // tn):
            cols = slice(c * tn, (c + 1) * tn)
            o_ref[:, cols] = base_ref[:, cols] + _dot(hid, wd_ref[:, cols])

    @pl.when(pl.program_id(1) == 0)
    def _():
        f = (_rms(h_ref[...]) * g_ref[...]).astype(BF16)
        f_ref[...] = f
        add_delta(f, h_ref)

    @pl.when(pl.program_id(1) > 0)
    def _():
        add_delta(f_ref[...], o_ref)


def _ffn(h, g, wg, wu, wd, tm, th):
    t, d = h.shape
    hidden = wd.shape[0]
    return pl.pallas_call(
        _ffn_kernel,
        grid=(t // tm, hidden // th),
        in_specs=[
            pl.BlockSpec((tm, d), lambda i, j: (i, 0)),
            pl.BlockSpec((1, d), lambda i, j: (0, 0)),
            pl.BlockSpec((d, th), lambda i, j: (0, j)),
            pl.BlockSpec((d, th), lambda i, j: (0, j)),
            pl.BlockSpec((th, d), lambda i, j: (j, 0)),
        ],
        out_specs=pl.BlockSpec((tm, d), lambda i, j: (i, 0)),
        out_shape=jax.ShapeDtypeStruct((t, d), F32),
        scratch_shapes=[pltpu.VMEM((tm, d), BF16)],
        compiler_params=_cparams(2),
        name="ffn",
    )(h, g, wg, wu, wd)


def _ple_kernel(h_ref, p_ref, gp_ref, wp_ref, wg_ref, gf_ref, o_ref, *, tn, row_parts):
    tm, d = h_ref.shape
    rp = tm // row_parts
    for r in range(row_parts):
        rows = slice(r * rp, (r + 1) * rp)
        n = (_rms(h_ref[rows, :]) * gp_ref[...]).astype(BF16)
        pb = p_ref[rows, :].astype(BF16)
        ss = jnp.zeros((rp, 1), F32)
        for c in range(d // tn):
            cols = slice(c * tn, (c + 1) * tn)
            gate = _dot(n, wg_ref[:, cols])
            h3 = h_ref[rows, cols] + _dot(pb, wp_ref[:, cols]) * _sigmoid(gate)
            o_ref[rows, cols] = h3
            ss = ss + jnp.sum(h3 * h3, axis=-1, keepdims=True)
        inv = lax.rsqrt(ss * (1.0 / d) + RMS_EPS)
        o_ref[rows, :] = o_ref[rows, :] * inv * gf_ref[...]


def _ple_final(h, p, g_ple, w_ple, w_gate, g_final, tm):
    t, d = h.shape
    pd = p.shape[1]
    const = lambda i: (0, 0)
    resident = dict(pipeline_mode=pl.Buffered(1))
    return pl.pallas_call(
        functools.partial(_ple_kernel, tn=512, row_parts=max(tm // 256, 1)),
        grid=(t // tm,),
        in_specs=[
            pl.BlockSpec((tm, d), lambda i: (i, 0)),
            pl.BlockSpec((tm, pd), lambda i: (i, 0)),
            pl.BlockSpec((1, d), const),
            pl.BlockSpec((pd, d), const, **resident),
            pl.BlockSpec((d, d), const, **resident),
            pl.BlockSpec((1, d), const),
        ],
        out_specs=pl.BlockSpec((tm, d), lambda i: (i, 0)),
        out_shape=jax.ShapeDtypeStruct((t, d), F32),
        compiler_params=_cparams(1),
        name="ple_final",
    )(h, p, g_ple, w_ple, w_gate, g_final)


def _sample_pre_kernel(x_ref, dt_ref, sc_ref, cw_ref, cb_ref, dtb_ref,
                       alog_ref, dsk_ref, exp_ref,
                       xdt_ref, b_ref, c_ref, dec_ref, yp_ref, dech_ref):
    conv = x_ref[...] * cw_ref[CONV_WIDTH - 1:CONV_WIDTH, :] + cb_ref[...]
    for k in range(CONV_WIDTH - 1):
        conv = conv + sc_ref[k] * cw_ref[k:k + 1, :]
    xc = _silu(conv)
    xs = xc[:, :SSM_INNER]
    bm = xc[:, SSM_INNER:SSM_INNER + BC_DIM]
    cm = xc[:, SSM_INNER + BC_DIM:]
    b_ref[...] = bm
    c_ref[...] = cm
    dt = _softplus(dt_ref[...] + dtb_ref[...])
    dec = jnp.exp(dt * (-jnp.exp(alog_ref[...])))
    ex = exp_ref[...]
    dtx = _dot_exact_rhs01(dt, ex)
    dec_ref[...] = _dot_exact_rhs01(dec, ex)
    dech_ref[...] = dec
    xdt = xs * dtx
    xdt_ref[...] = xdt
    gw = SSM_INNER // SSM_GROUPS
    cbs = []
    for g in range(SSM_GROUPS):
        prod = cm[:, g * SSM_STATE:(g + 1) * SSM_STATE] * bm[:, g * SSM_STATE:(g + 1) * SSM_STATE]
        cbs.append(jnp.broadcast_to(jnp.sum(prod, axis=-1, keepdims=True), (prod.shape[0], gw)))
    yp_ref[...] = xdt * jnp.concatenate(cbs, axis=1) + xs * dsk_ref[...]


def _sample_pre(xbc, dtraw, sconv_t, conv_w, conv_b, dt_bias, a_log, d_skip_x, expand):
    nb = xbc.shape[0]
    shapes = [(nb, SSM_INNER), (nb, BC_DIM), (nb, BC_DIM),
              (nb, SSM_INNER), (nb, SSM_INNER), (nb, HEADS_PAD)]
    return pl.pallas_call(
        _sample_pre_kernel,
        out_shape=[jax.ShapeDtypeStruct(s, F32) for s in shapes],
        compiler_params=pltpu.CompilerParams(vmem_limit_bytes=VMEM_LIMIT),
        name="sample_pre",
    )(xbc, dtraw, sconv_t, conv_w, conv_b, dt_bias, a_log, d_skip_x, expand)


def _attn_sample_kernel(sink_ref, q_ref, kn_ref, vn_ref, ck_ref, cv_ref, o_ref, nk_ref, nv_ref, *, bb):
    w = WINDOW
    row = lax.broadcasted_iota(jnp.int32, (w, KV_DIM), 0)
    hrow = lax.broadcasted_iota(jnp.int32, (N_HEADS, KV_DIM), 0) // (N_HEADS // N_KV_HEADS)
    hgrp = lax.broadcasted_iota(jnp.int32, (N_HEADS, KV_DIM), 1) // HEAD_DIM
    own = hrow == hgrp
    sink = sink_ref[...] * LOG2E
    kks, vvs, scores, probs, dens = [], [], [], [], []
    for b in range(bb):
        kk = jnp.where(row == w - 1, kn_ref[b:b + 1, :], pltpu.roll(ck_ref[b], w - 1, 0))
        vv = jnp.where(row == w - 1, vn_ref[b:b + 1, :], pltpu.roll(cv_ref[b], w - 1, 0))
        nk_ref[b] = kk
        nv_ref[b] = vv
        kks.append(kk.astype(BF16))
        vvs.append(vv.astype(BF16))
    for b in range(bb):
        qb = q_ref[b]
        qrow = jnp.where(own, jnp.concatenate([qb] * N_KV_HEADS, axis=1), 0.0)
        scores.append(_dot_nt(qrow.astype(BF16), kks[b]))
    for b in range(bb):
        m = jnp.maximum(jnp.max(scores[b], axis=-1, keepdims=True), sink)
        e = jnp.exp2(scores[b] - m)
        dens.append(jnp.sum(e, axis=-1, keepdims=True) + jnp.exp2(sink - m))
        probs.append(e.astype(BF16))
    outs = [jnp.where(own, _dot(probs[b], vvs[b]), 0.0) for b in range(bb)]
    for b in range(bb):
        o = outs[b][:, :HEAD_DIM]
        for g in range(1, N_KV_HEADS):
            o = o + outs[b][:, g * HEAD_DIM:(g + 1) * HEAD_DIM]
        o_ref[b] = o / dens[b]


def _attn_sample(sinks_col, q3, knew, vnew, cache_k, cache_v, bb):
    nb = q3.shape[0]
    w = WINDOW
    return pl.pallas_call(
        functools.partial(_attn_sample_kernel, bb=bb),
        grid=(nb // bb,),
        in_specs=[
            pl.BlockSpec((N_HEADS, 1), lambda i: (0, 0)),
            pl.BlockSpec((bb, N_HEADS, HEAD_DIM), lambda i: (i, 0, 0)),
            pl.BlockSpec((bb, KV_DIM), lambda i: (i, 0)),
            pl.BlockSpec((bb, KV_DIM), lambda i: (i, 0)),
            pl.BlockSpec((bb, w, KV_DIM), lambda i: (i, 0, 0)),
            pl.BlockSpec((bb, w, KV_DIM), lambda i: (i, 0, 0)),
        ],
        out_specs=[
            pl.BlockSpec((bb, N_HEADS, HEAD_DIM), lambda i: (i, 0, 0)),
            pl.BlockSpec((bb, w, KV_DIM), lambda i: (i, 0, 0)),
            pl.BlockSpec((bb, w, KV_DIM), lambda i: (i, 0, 0)),
        ],
        out_shape=[
            jax.ShapeDtypeStruct((nb, N_HEADS, HEAD_DIM), F32),
            jax.ShapeDtypeStruct((nb, w, KV_DIM), F32),
            jax.ShapeDtypeStruct((nb, w, KV_DIM), F32),
        ],
        compiler_params=_cparams(1),
        name="attn_sample",
    )(sinks_col, q3, knew, vnew, cache_k, cache_v)


def _ssd_sample_kernel(st_ref, xdt_ref, b_ref, c_ref, dec_ref, dech_ref, yp_ref, z_ref, gn_ref,
                       y_ref, ns_ref, *, bb):
    gw = SSM_INNER // SSM_GROUPS
    grow = lax.broadcasted_iota(jnp.int32, (8, SSM_INNER), 0)
    glane = lax.broadcasted_iota(jnp.int32, (8, SSM_INNER), 1) // gw
    own = grow == glane
    pad = jnp.zeros((8 - SSM_GROUPS, SSM_STATE), F32)
    yoffs = []
    for b in range(bb):
        st = st_ref[b]
        cmat = jnp.concatenate([c_ref[b], pad], axis=0).astype(BF16)
        bmat = jnp.concatenate([b_ref[b], pad], axis=0).astype(BF16)
        r = _dot_nt(cmat, st.astype(BF16))
        yoffs.append(jnp.sum(jnp.where(own, r, 0.0), axis=0, keepdims=True))
        amat = jnp.where(own, jnp.broadcast_to(xdt_ref[b:b + 1, :], (8, SSM_INNER)), 0.0)
        outer = _dot_tn(amat.astype(BF16), bmat)
        for h in range(SSM_HEADS):
            rows = slice(h * SSM_HEAD_DIM, (h + 1) * SSM_HEAD_DIM)
            ns_ref[b, rows, :] = st[rows, :] * dech_ref[b:b + 1, h:h + 1] + outer[rows, :]
    y = yp_ref[...] + jnp.concatenate(yoffs, axis=0) * dec_ref[...]
    hg = y * _silu(z_ref[...])
    outs = []
    for g in range(SSM_GROUPS):
        outs.append(_rms(hg[:, g * gw:(g + 1) * gw]))
    y_ref[...] = jnp.concatenate(outs, axis=1) * gn_ref[...]


def _ssd_sample(state, xdt, b3, c3, decx, dech, ypart, z, norm_g, bb):
    nb = state.shape[0]
    row = lambda i: (i, 0)
    return pl.pallas_call(
        functools.partial(_ssd_sample_kernel, bb=bb),
        grid=(nb // bb,),
        in_specs=[
            pl.BlockSpec((bb, SSM_INNER, SSM_STATE), lambda i: (i, 0, 0)),
            pl.BlockSpec((bb, SSM_INNER), row),
            pl.BlockSpec((bb, SSM_GROUPS, SSM_STATE), lambda i: (i, 0, 0)),
            pl.BlockSpec((bb, SSM_GROUPS, SSM_STATE), lambda i: (i, 0, 0)),
            pl.BlockSpec((bb, SSM_INNER), row),
            pl.BlockSpec((bb, HEADS_PAD), row),
            pl.BlockSpec((bb, SSM_INNER), row),
            pl.BlockSpec((bb, SSM_INNER), row),
            pl.BlockSpec((1, SSM_INNER), lambda i: (0, 0)),
        ],
        out_specs=[
            pl.BlockSpec((bb, SSM_INNER), row),
            pl.BlockSpec((bb, SSM_INNER, SSM_STATE), lambda i: (i, 0, 0)),
        ],
        out_shape=[
            jax.ShapeDtypeStruct((nb, SSM_INNER), F32),
            jax.ShapeDtypeStruct((nb, SSM_INNER, SSM_STATE), F32),
        ],
        compiler_params=_cparams(1),
        name="ssd_sample",
    )(state, xdt, b3, c3, decx, dech, ypart, z, norm_g)


def _row_tile(t, want):
    return want if t % want == 0 else t


def _dense_tail(x, attn, y, p, wts, tm):
    h1 = _outproj(x, attn, y, wts["w_out"], tm)
    h2 = _ffn(h1, wts["g_ffn"], wts["w_gate"], wts["w_up"], wts["w_down"],
              _row_tile(x.shape[0], FFN_TM), FFN_TH)
    return _ple_final(h2, p, wts["g_ple"], wts["w_ple"], wts["w_ple_gate"], wts["g_final"],
                      _row_tile(x.shape[0], 1024))


def kernel(x_prompt, x_sample, cache_k, cache_v, state_ssm, state_conv, p_prompt, p_sample, w_in, conv_w, conv_b, dt_bias, a_log, d_skip, ssm_norm_g, attn_sinks, w_out, g_mix, g_ffn, w_ffn_gate, w_ffn_up, w_ffn_down, g_ple, w_ple, w_ple_gate, g_final):
    depth = w_in.shape[0]
    assert depth == 1, "single-layer step only"
    bsz, seq, d = x_prompt.shape
    nb, dseq, _ = x_sample.shape
    assert dseq == 1 and seq % SSD_CHUNK == 0 and seq % WINDOW == 0

    w = w_in[0]
    w_main = w.astype(BF16)
    w_dt = jnp.pad(w[:, PROJ_DIM:], ((0, 0), (0, HEADS_PAD - SSM_HEADS))).astype(BF16)
    padh = lambda v: jnp.pad(v, (0, HEADS_PAD - SSM_HEADS)).reshape(1, HEADS_PAD)
    tileh = lambda v: jnp.tile(v, HEADS_PAD // SSM_HEADS).reshape(1, HEADS_PAD)
    dtb, alog = padh(dt_bias[0]), padh(a_log[0])
    dsk_x = jnp.repeat(d_skip[0], SSM_HEAD_DIM).reshape(1, SSM_INNER)
    gn = ssm_norm_g[0].reshape(1, SSM_INNER)
    cw, cb = conv_w[0], conv_b[0].reshape(1, CONV_DIM)
    wts = dict(g_ffn=g_ffn[0].reshape(1, d), g_ple=g_ple[0].reshape(1, d),
               g_final=g_final.reshape(1, d))
    gmix = g_mix[0].reshape(1, d)
    sinks = attn_sinks[0]

    tp = bsz * seq
    xp = x_prompt.reshape(tp, d)
    tm_in = _row_tile(seq, 1024)
    proj, dtraw = _inproj(xp, gmix, w_main, w_dt, _rope_tables(jnp.arange(seq)), tm_in)
    proj3 = proj.reshape(bsz, seq, PROJ_DIM)
    (attn, nk_p, nv_p, wts["w_gate"], wts["w_up"], wts["w_down"], wts["w_out"], wts["w_ple_gate"],
     wts["w_ple"]) = _attn_prompt(
        proj3, sinks, ATTN_BLOCKS,
        (w_ffn_gate[0], w_ffn_up[0], w_ffn_down[0], w_out[0], w_ple_gate[0], w_ple[0]))
    yp, nssm_p, nconv_p = _ssd_prompt(
        proj3, dtraw.reshape(bsz, seq, HEADS_PAD), cw, cb, tileh(dt_bias[0]), tileh(a_log[0]),
        dsk_x, gn, SSD_CHUNKS)
    tm = _row_tile(tp, 512)
    y_prompt = _dense_tail(xp, attn.reshape(tp, Q_DIM), yp.reshape(tp, SSM_INNER),
                           p_prompt[0].reshape(tp, -1), wts, tm).reshape(bsz, seq, d)

    xs = x_sample.reshape(nb, d)
    tab_s = _rope_tables(jnp.full((nb,), PAST_LEN, jnp.int32))
    proj_s, dtraw_s = _inproj(xs, gmix, w_main, w_dt, tab_s, nb)
    window = cache_k.shape[2]
    sconv_t = jnp.transpose(state_conv[0], (1, 0, 2))
    expand = (jnp.arange(HEADS_PAD)[:, None] == (jnp.arange(SSM_INNER) // SSM_HEAD_DIM)[None, :]
              ).astype(BF16)
    q_s = proj_s[:, COL_Q:COL_Q + Q_DIM]
    k_s = proj_s[:, COL_K:COL_K + KV_DIM]
    v_s = proj_s[:, COL_V:COL_V + KV_DIM]
    z_s = proj_s[:, COL_Z:COL_Z + SSM_INNER]
    xbc_s = proj_s[:, COL_XBC:COL_XBC + CONV_DIM]
    xdt, bm, cm, decx, ypart, dech = _sample_pre(
        xbc_s, dtraw_s, sconv_t, cw, cb, dtb, alog, dsk_x, expand)
    bb = DECODE_BB if nb % DECODE_BB == 0 else nb
    attn_s, nk_s, nv_s = _attn_sample(
        sinks.reshape(N_HEADS, 1), q_s.reshape(nb, N_HEADS, HEAD_DIM), k_s, v_s,
        cache_k[0].reshape(nb, window, KV_DIM), cache_v[0].reshape(nb, window, KV_DIM), bb)
    ys, nssm_s = _ssd_sample(
        state_ssm[0].reshape(nb, SSM_INNER, SSM_STATE), xdt,
        bm.reshape(nb, SSM_GROUPS, SSM_STATE), cm.reshape(nb, SSM_GROUPS, SSM_STATE),
        decx, dech, ypart, z_s, gn, bb)
    y_sample = _dense_tail(xs, attn_s.reshape(nb, Q_DIM), ys, p_sample[0].reshape(nb, -1), wts,
                           nb).reshape(nb, 1, d)
    nconv_s = jnp.concatenate([state_conv[0][:, 1:], xbc_s[:, None, :]], axis=1)

    kv5 = lambda t: t.reshape(1, t.shape[0], window, N_KV_HEADS, HEAD_DIM)
    return (y_prompt, y_sample,
            kv5(nk_p), kv5(nv_p),
            nssm_p.reshape(1, bsz, SSM_HEADS, SSM_HEAD_DIM, SSM_STATE),
            nconv_p[None, :, 8 - (CONV_WIDTH - 1):, :],
            kv5(nk_s), kv5(nv_s),
            nssm_s.reshape(1, nb, SSM_HEADS, SSM_HEAD_DIM, SSM_STATE),
            nconv_s[None])
```

```python
import functools

import numpy as np
import jax
import jax.numpy as jnp
from jax import lax
from jax.experimental import pallas as pl
from jax.experimental.pallas import tpu as pltpu

F32 = jnp.float32
BF16 = jnp.bfloat16

HEAD_DIM = 64
N_HEADS = 16
N_KV_HEADS = 4
WINDOW = 128
ROPE_DIM = 16
ROPE_THETA = 500000.0
SSM_HEADS = 16
SSM_HEAD_DIM = 64
SSM_GROUPS = 4
SSM_STATE = 128
CONV_WIDTH = 4
SSD_CHUNK = 128
RMS_EPS = 1e-6
PAST_LEN = 16384
LOG2E = 1.4426950408889634

Q_DIM = N_HEADS * HEAD_DIM
KV_DIM = N_KV_HEADS * HEAD_DIM
SSM_INNER = SSM_HEADS * SSM_HEAD_DIM
BC_DIM = SSM_GROUPS * SSM_STATE
CONV_DIM = SSM_INNER + 2 * BC_DIM
LANES = 128
HEADS_PAD = LANES

COL_Q = 0
COL_K = Q_DIM
COL_V = COL_K + KV_DIM
COL_Z = COL_V + KV_DIM
COL_XBC = COL_Z + SSM_INNER
PROJ_DIM = COL_XBC + CONV_DIM
PIECE = 512

INPROJ_TN = Q_DIM + 2 * KV_DIM
Q_SCALE = HEAD_DIM ** -0.5 * LOG2E
FFN_TM, FFN_TH = 1024, 512
CONV_PITCH = 3
DECODE_BB = 16
ATTN_BLOCKS = 8
SSD_CHUNKS = 8
VMEM_LIMIT = 56 * 1024 * 1024


def _cparams(n_axes):
    return pltpu.CompilerParams(
        dimension_semantics=("arbitrary",) * n_axes, vmem_limit_bytes=VMEM_LIMIT)


def _rms(x):
    return x * lax.rsqrt(jnp.mean(x * x, axis=-1, keepdims=True) + RMS_EPS)


def _sigmoid(x):
    return 1.0 / (1.0 + jnp.exp2(x * -LOG2E))


def _silu(x):
    return x * _sigmoid(x)


def _dot(a, b):
    return jnp.dot(a, b, preferred_element_type=F32)


def _dot_nt(a, b):
    return lax.dot_general(a, b, (((1,), (1,)), ((), ())), preferred_element_type=F32)


def _dot_tn(a, b):
    return lax.dot_general(a, b, (((0,), (0,)), ((), ())), preferred_element_type=F32)


def _split3(x):
    hi = x.astype(BF16)
    r = x - hi.astype(F32)
    mid = r.astype(BF16)
    lo = (r - mid.astype(F32)).astype(BF16)
    return hi, mid, lo


def _dot_exact_lhs01(m01, x):
    hi, mid, lo = _split3(x)
    return _dot(m01, hi) + _dot(m01, mid) + _dot(m01, lo)


def _dot_exact_rhs01(x, m01):
    hi, mid, lo = _split3(x)
    return _dot(hi, m01) + _dot(mid, m01) + _dot(lo, m01)


def _inproj_kernel(xa_ref, xb_ref, g_ref, w_ref, wdt_ref, tab_ref, o_ref, odt_ref, u_ref):
    j = pl.program_id(1)

    @pl.when(j == 0)
    def _():
        rp = xa_ref.shape[0]
        for p, x_ref in enumerate((xa_ref, xb_ref)):
            rows = slice(p * rp, (p + 1) * rp)
            u = (_rms(x_ref[...]) * g_ref[...]).astype(BF16)
            u_ref[rows, :] = u
            odt_ref[rows, :] = _dot(u, wdt_ref[...])
            res = _dot(u, w_ref[...])
            tab = tab_ref[rows, :]
            for c in range((Q_DIM + KV_DIM) // LANES):
                cols = slice(c * LANES, (c + 1) * LANES)
                r = _rope(res[:, cols], tab)
                o_ref[rows, cols] = r * Q_SCALE if c < Q_DIM // LANES else r
            o_ref[rows, Q_DIM + KV_DIM:] = res[:, Q_DIM + KV_DIM:]

    @pl.when(j > 0)
    def _():
        o_ref[...] = _dot(u_ref[...], w_ref[...])


def _inproj(x, g, w, wdt, tab, tm):
    t, d = x.shape
    n = PROJ_DIM
    tn = INPROJ_TN
    nt = tab.shape[0] // tm
    ni, nj = t // tm, n // tn
    early = lambda i, j: jnp.minimum(i + (j == nj - 1).astype(jnp.int32), ni - 1)
    return pl.pallas_call(
        _inproj_kernel,
        grid=(ni, nj),
        in_specs=[
            pl.BlockSpec((tm // 2, d), lambda i, j: (2 * i, 0)),
            pl.BlockSpec((tm // 2, d), lambda i, j: (2 * early(i, j) + 1, 0)),
            pl.BlockSpec((1, d), lambda i, j: (0, 0)),
            pl.BlockSpec((d, tn), lambda i, j: (0, j)),
            pl.BlockSpec((d, HEADS_PAD), lambda i, j: (0, 0)),
            pl.BlockSpec((tm, 3 * LANES), lambda i, j: (i % nt, 0)),
        ],
        out_specs=[
            pl.BlockSpec((tm, tn), lambda i, j: (i, j)),
            pl.BlockSpec((tm, HEADS_PAD), lambda i, j: (i, 0)),
        ],
        out_shape=[
            jax.ShapeDtypeStruct((t, n), F32),
            jax.ShapeDtypeStruct((t, HEADS_PAD), F32),
        ],
        scratch_shapes=[pltpu.VMEM((tm, d), BF16)],
        compiler_params=_cparams(2),
        name="inproj",
    )(x, x, g, w, wdt, tab)


def _rope_tables(pos):
    half = ROPE_DIM // 2
    inv = ROPE_THETA ** (-jnp.arange(half, dtype=F32) * (2.0 / ROPE_DIM))
    ang = pos.astype(F32)[:, None] * inv[None, :]
    cs = jnp.concatenate([jnp.cos(ang), jnp.sin(ang)], axis=1)
    expand = np.zeros((2 * half, 3 * LANES), np.float32)
    base = np.zeros((1, 3 * LANES), np.float32)
    for lane in range(LANES):
        m = lane % HEAD_DIM
        if m >= ROPE_DIM:
            base[0, lane] = 1.0
            continue
        expand[m % half, lane] = 1.0
        if m < half:
            expand[half + m, LANES + lane] = -1.0
        else:
            expand[m, 2 * LANES + lane] = 1.0
    cs3 = jnp.concatenate(_split3(cs), axis=1)
    expand3 = jnp.asarray(np.concatenate([expand] * 3, axis=0), BF16)
    return jnp.dot(cs3, expand3, preferred_element_type=F32) + base


def _rope(x, tab):
    half = ROPE_DIM // 2
    c, sa, sb = tab[:, :LANES], tab[:, LANES:2 * LANES], tab[:, 2 * LANES:]
    return x * c + pltpu.roll(x, LANES - half, 1) * sa + pltpu.roll(x, half, 1) * sb


def _softmax_fold(s, band, prev_bias, sink2):
    sp = s[:, :WINDOW] if prev_bias is None else s[:, :WINDOW] + prev_bias
    t = jnp.where(band, sp, s[:, WINDOW:])
    m = jnp.maximum(jnp.max(t, axis=-1, keepdims=True), sink2)
    e = jnp.exp2(t - m)
    den = jnp.sum(e, axis=-1, keepdims=True) + jnp.exp2(sink2 - m)
    p = jnp.concatenate([jnp.where(band, e, 0.0), jnp.where(band, 0.0, e)], axis=1)
    return p.astype(BF16), den


def _cast_slabs(in_refs, out_refs):
    for src, dst in zip(in_refs, out_refs):
        dst[...] = src[...].astype(dst.dtype)


def _cast_specs(weights, n_steps, step_of):
    if any(w.shape[0] % n_steps or (w.shape[0] // n_steps) % 16 for w in weights):
        return None
    specs = [pl.BlockSpec((w.shape[0] // n_steps, w.shape[1]), lambda *g: (step_of(*g), 0))
             for w in weights]
    shapes = [jax.ShapeDtypeStruct(w.shape, BF16) for w in weights]
    return specs, shapes


def _attn_prompt_kernel(sink_ref, q_ref, kc_ref, kp_ref, vc_ref, vp_ref, *rest, nq):
    ncast = (len(rest) - 3) // 2
    o_ref, nk_ref, nv_ref = rest[ncast:ncast + 3]
    _cast_slabs(rest[:ncast], rest[ncast + 3:])
    i = pl.program_id(1)
    nsteps = pl.num_programs(1)
    w = WINDOW
    kcr = kc_ref[...]
    kpr = kp_ref[...]
    vc = vc_ref[...]

    @pl.when(i == nsteps - 1)
    def _():
        nk_ref[...] = kcr[(nq - 1) * w:]
        nv_ref[...] = vc[(nq - 1) * w:]

    kall = jnp.concatenate([kpr, kcr], axis=0)
    vall = jnp.concatenate([vp_ref[...], vc], axis=0)
    lo = lax.broadcasted_iota(jnp.int32, ((nq + 1) * w, LANES), 1) < HEAD_DIM
    lo_q = lax.broadcasted_iota(jnp.int32, (w, LANES), 1) < HEAD_DIM
    band = (lax.broadcasted_iota(jnp.int32, (w, w), 1) > lax.broadcasted_iota(jnp.int32, (w, w), 0))
    first_bias = jnp.where(i == 0, -jnp.inf, 0.0)

    for g in range(N_KV_HEADS):
        col, odd = g // 2, g % 2
        kg = kall[:, col * LANES:(col + 1) * LANES]
        vg = vall[:, col * LANES:(col + 1) * LANES]
        kg_sw = pltpu.roll(kg, HEAD_DIM, 1)
        vg_sw = pltpu.roll(vg, HEAD_DIM, 1)
        k_lo = jnp.where(lo, kg_sw if odd else kg, 0.0).astype(BF16)
        k_hi = jnp.where(lo, 0.0, kg if odd else kg_sw).astype(BF16)
        v_lo = jnp.where(lo, vg_sw if odd else vg, 0.0).astype(BF16)
        v_hi = jnp.where(lo, 0.0, vg if odd else vg_sw).astype(BF16)
        sinks2 = [sink_ref[4 * g + r] * LOG2E for r in range(4)]
        def scores(s, g=g, k_lo=k_lo, k_hi=k_hi):
            rows = slice(s * w, (s + 1) * w)
            keys = slice(s * w, (s + 2) * w)
            qst = jnp.concatenate([q_ref[rows, (2 * g) * LANES:(2 * g + 1) * LANES],
                                   q_ref[rows, (2 * g + 1) * LANES:(2 * g + 2) * LANES]],
                                  axis=0).astype(BF16)
            return _dot_nt(qst, k_lo[keys]), _dot_nt(qst, k_hi[keys])

        ahead = scores(0)
        for s in range(nq):
            rows = slice(s * w, (s + 1) * w)
            keys = slice(s * w, (s + 2) * w)
            pb = first_bias if s == 0 else None
            s_lo, s_hi = ahead
            if s + 1 < nq:
                ahead = scores(s + 1)
            e0, d0 = _softmax_fold(s_lo[:w], band, pb, sinks2[0])
            e1, d1 = _softmax_fold(s_hi[:w], band, pb, sinks2[1])
            e2, d2 = _softmax_fold(s_lo[w:], band, pb, sinks2[2])
            e3, d3 = _softmax_fold(s_hi[w:], band, pb, sinks2[3])
            p = jnp.concatenate([jnp.concatenate([e0, e1], axis=1),
                                 jnp.concatenate([e2, e3], axis=1)], axis=0)
            vcat = jnp.concatenate([v_lo[keys], v_hi[keys]], axis=0)
            o = _dot(p, vcat)
            oa = o[:w] / jnp.where(lo_q, d0, d1)
            ob = o[w:] / jnp.where(lo_q, d2, d3)
            o_ref[rows, (2 * g) * LANES:(2 * g + 1) * LANES] = oa.astype(o_ref.dtype)
            o_ref[rows, (2 * g + 1) * LANES:(2 * g + 2) * LANES] = ob.astype(o_ref.dtype)


def _attn_prompt(proj, sinks, nq, cast_weights=()):
    b, l, _ = proj.shape
    w = WINDOW
    nsteps = l // (nq * w)
    kcol, vcol = COL_K // KV_DIM, COL_V // KV_DIM
    prev = lambda bi, i: jnp.maximum(nq * i - 1, 0)
    cast = _cast_specs(cast_weights, b * nsteps, lambda bi, i: bi * nsteps + i)
    if cast is None:
        res = _attn_prompt(proj, sinks, nq)
        return res[:3] + tuple(cw.astype(BF16) for cw in cast_weights)
    cast_specs, cast_shapes = cast
    return pl.pallas_call(
        functools.partial(_attn_prompt_kernel, nq=nq),
        grid=(b, nsteps),
        in_specs=[
            pl.BlockSpec(memory_space=pltpu.SMEM),
            pl.BlockSpec((None, nq * w, Q_DIM), lambda bi, i: (bi, i, COL_Q // Q_DIM)),
            pl.BlockSpec((None, nq * w, KV_DIM), lambda bi, i: (bi, i, kcol)),
            pl.BlockSpec((None, w, KV_DIM), lambda bi, i: (bi, prev(bi, i), kcol)),
            pl.BlockSpec((None, nq * w, KV_DIM), lambda bi, i: (bi, i, vcol)),
            pl.BlockSpec((None, w, KV_DIM), lambda bi, i: (bi, prev(bi, i), vcol)),
        ] + cast_specs,
        out_specs=[
            pl.BlockSpec((None, nq * w, Q_DIM), lambda bi, i: (bi, i, 0)),
            pl.BlockSpec((None, w, KV_DIM), lambda bi, i: (bi, 0, 0)),
            pl.BlockSpec((None, w, KV_DIM), lambda bi, i: (bi, 0, 0)),
        ] + cast_specs,
        out_shape=[
            jax.ShapeDtypeStruct((b, l, Q_DIM), BF16),
            jax.ShapeDtypeStruct((b, w, KV_DIM), F32),
            jax.ShapeDtypeStruct((b, w, KV_DIM), F32),
        ] + cast_shapes,
        compiler_params=_cparams(2),
        name="attn_prompt",
    )(sinks, proj, proj, proj, proj, proj, *cast_weights)


def _softplus(v):
    return jnp.maximum(v, 0.0) + jnp.log1p(jnp.exp(-jnp.abs(v)))


def _head_expand(vals, ex2_ref):
    hi = vals.astype(BF16)
    mid = (vals - hi.astype(F32)).astype(BF16)
    return _dot(jnp.concatenate([hi, mid], axis=1), ex2_ref[...])


def _ssd_prompt_kernel(*refs, nsub):
    nz, nx = SSM_INNER // PIECE, CONV_DIM // PIECE
    z_refs, x_refs = refs[:nz], refs[nz:nz + nx]
    (dt_ref, cw_ref, cb_ref, dtb_ref, alog_ref, dsk_ref, gn_ref, ex2_ref,
     y_ref, nssm_ref, nconv_ref, state_ref, carry_ref, xst_ref) = refs[nz + nx:]
    i = pl.program_id(1)
    nc = pl.num_programs(1)
    q = SSD_CHUNK

    @pl.when(i == 0)
    def _():
        state_ref[...] = jnp.zeros_like(state_ref)
        carry_ref[...] = jnp.zeros_like(carry_ref)

    dtp = dt_ref[:q, :]
    for s in range(1, nsub):
        dtp = dtp + pltpu.roll(dt_ref[s * q:(s + 1) * q, :], s * SSM_HEADS, 1)
    dt = _softplus(dtp + dtb_ref[...])
    da = dt * (-jnp.exp(alog_ref[...]))
    tri = (lax.broadcasted_iota(jnp.int32, (q, q), 0) >= lax.broadcasted_iota(jnp.int32, (q, q), 1))
    cs = _dot_exact_lhs01(tri.astype(BF16), da)
    cs2 = cs * LOG2E
    cs2_t = cs2.T
    sc = dict(
        tri=tri, cs2=cs2, cs2_t=cs2_t, dt_t=dt.T,
        cdec_t=jnp.exp2(cs2_t[:, q - 1:q]),
        ecs_x=_head_expand(jnp.exp(cs), ex2_ref),
        wgt_x=_head_expand(dt * jnp.exp(cs[q - 1:q, :] - cs), ex2_ref))

    rows_of = lambda rs, sl: jnp.concatenate([r[sl, :] for r in rs], axis=1)
    ahead = _ssd_conv(0, rows_of, x_refs, cw_ref, cb_ref, carry_ref, xst_ref)
    for s in range(nsub):
        xc = ahead
        if s + 1 < nsub:
            ahead = _ssd_conv(s + 1, rows_of, x_refs, cw_ref, cb_ref, carry_ref, xst_ref)
        _ssd_chunk(s, sc, xc, rows_of, z_refs, dsk_ref, gn_ref, y_ref, state_ref)
    tail = rows_of(x_refs, slice(nsub * q - 8, nsub * q))
    carry_ref[...] = tail

    @pl.when(i == nc - 1)
    def _():
        nconv_ref[...] = tail
        nssm_ref[...] = state_ref[...]


def _ssd_conv(s, rows_of, x_refs, cw_ref, cb_ref, carry_ref, xst_ref):
    q = SSD_CHUNK
    x = rows_of(x_refs, slice(s * q, (s + 1) * q))
    prev = carry_ref[...] if s == 0 else rows_of(x_refs, slice(s * q - 8, s * q))
    pitch = CONV_PITCH
    outs = []
    for c in range(CONV_DIM // LANES):
        cols = slice(c * LANES, (c + 1) * LANES)
        buf = xst_ref.at[s % 2, c]
        xcol = x[:, cols]
        buf[pl.ds(0, 8, stride=pitch), :] = prev[:, cols]
        buf[pl.ds(8 * pitch, q, stride=pitch), :] = xcol
        acc = xcol * cw_ref[CONV_WIDTH - 1:CONV_WIDTH, cols] + cb_ref[:, cols]
        for k in range(1, CONV_WIDTH):
            acc = acc + (buf[pl.ds((8 - k) * pitch, q, stride=pitch), :]
                         * cw_ref[CONV_WIDTH - 1 - k:CONV_WIDTH - k, cols])
        outs.append(acc)
    return _silu(jnp.concatenate(outs, axis=1))


def _ssd_chunk(s, sc, xc, rows_of, z_refs, dsk_ref, gn_ref, y_ref, state_ref):
    q = SSD_CHUNK
    tri, cs2, cs2_t, dt_t, cdec_t = sc["tri"], sc["cs2"], sc["cs2_t"], sc["dt_t"], sc["cdec_t"]
    hoff = s * SSM_HEADS
    xoff = s * SSM_INNER
    trows = slice(s * q, (s + 1) * q)
    z = rows_of(z_refs, trows)
    xs = xc[:, :SSM_INNER]
    bm = xc[:, SSM_INNER:SSM_INNER + BC_DIM]
    cm = xc[:, SSM_INNER + BC_DIM:]

    lo = lax.broadcasted_iota(jnp.int32, (q, LANES), 1) < SSM_HEAD_DIM

    hpg = SSM_HEADS // SSM_GROUPS
    gw = hpg * SSM_HEAD_DIM
    for g in range(SSM_GROUPS):
        bg = bm[:, g * SSM_STATE:(g + 1) * SSM_STATE].astype(BF16)
        cg = cm[:, g * SSM_STATE:(g + 1) * SSM_STATE].astype(BF16)
        cb = _dot_nt(cg, bg)
        st = state_ref[g * gw:(g + 1) * gw, :]
        yoff = _dot_nt(cg, st.astype(BF16))
        ys = []
        for pr in range(2):
            pair = 2 * g + pr
            ms = []
            for h in (hoff + 2 * pair, hoff + 2 * pair + 1):
                diff = cs2[:, h:h + 1] - cs2_t[h:h + 1, :]
                lm = jnp.exp2(jnp.where(tri, diff, -jnp.inf))
                ms.append((cb * lm * dt_t[h:h + 1, :]).astype(BF16))
            xp = xs[:, pair * LANES:(pair + 1) * LANES]
            x2 = jnp.concatenate([jnp.where(lo, xp, 0.0), jnp.where(lo, 0.0, xp)],
                                 axis=0).astype(BF16)
            yd = _dot(jnp.concatenate(ms, axis=1), x2)
            yo = yoff[:, pr * LANES:(pr + 1) * LANES] * sc["ecs_x"][
                :, xoff + pair * LANES:xoff + (pair + 1) * LANES]
            ys.append(yd + yo + xp * dsk_ref[:, pair * LANES:(pair + 1) * LANES])
        yg = jnp.concatenate(ys, axis=1)
        wx = xs[:, g * gw:(g + 1) * gw] * sc["wgt_x"][:, xoff + g * gw:xoff + (g + 1) * gw]
        s_new = _dot_tn(wx.astype(BF16), bg)
        for r in range(hpg):
            h = hoff + hpg * g + r
            rows = slice(g * gw + r * SSM_HEAD_DIM, g * gw + (r + 1) * SSM_HEAD_DIM)
            state_ref[rows, :] = (st[r * SSM_HEAD_DIM:(r + 1) * SSM_HEAD_DIM, :] * cdec_t[h:h + 1, :]
                                  + s_new[r * SSM_HEAD_DIM:(r + 1) * SSM_HEAD_DIM, :])
        hg = yg * _silu(z[:, g * gw:(g + 1) * gw])
        y_ref[trows, g * gw:(g + 1) * gw] = (_rms(hg) * gn_ref[:, g * gw:(g + 1) * gw]
                                            ).astype(y_ref.dtype)


def _ssd_prompt(proj, dtraw, conv_w, conv_b, dt_bias_t, a_log_t, d_skip_x, norm_g, nsub):
    assert nsub * SSM_HEADS <= LANES
    src = np.arange(nsub * SSM_INNER) // SSM_HEAD_DIM
    ex = (np.arange(LANES)[:, None] == src[None, :]).astype(np.float32)
    ex2 = jnp.asarray(np.concatenate([ex, ex], axis=0), BF16)
    b, l, _ = proj.shape
    q = SSD_CHUNK * nsub
    nc = l // q
    const = lambda bi, i: (0, 0)
    piece = lambda c: pl.BlockSpec((None, q, PIECE), lambda bi, i: (bi, i, c))
    n_pieces = (SSM_INNER + CONV_DIM) // PIECE
    return pl.pallas_call(
        functools.partial(_ssd_prompt_kernel, nsub=nsub),
        grid=(b, nc),
        in_specs=[piece(COL_Z // PIECE + c) for c in range(SSM_INNER // PIECE)] + [
            piece(COL_XBC // PIECE + c) for c in range(CONV_DIM // PIECE)] + [
            pl.BlockSpec((None, q, HEADS_PAD), lambda bi, i: (bi, i, 0)),
            pl.BlockSpec((CONV_WIDTH, CONV_DIM), const),
            pl.BlockSpec((1, CONV_DIM), const),
            pl.BlockSpec((1, HEADS_PAD), const),
            pl.BlockSpec((1, HEADS_PAD), const),
            pl.BlockSpec((1, SSM_INNER), const),
            pl.BlockSpec((1, SSM_INNER), const),
            pl.BlockSpec((2 * LANES, nsub * SSM_INNER), const, pipeline_mode=pl.Buffered(1)),
        ],
        out_specs=[
            pl.BlockSpec((None, q, SSM_INNER), lambda bi, i: (bi, i, 0)),
            pl.BlockSpec((None, SSM_INNER, SSM_STATE), lambda bi, i: (bi, 0, 0)),
            pl.BlockSpec((None, 8, CONV_DIM), lambda bi, i: (bi, 0, 0)),
        ],
        out_shape=[
            jax.ShapeDtypeStruct((b, l, SSM_INNER), BF16),
            jax.ShapeDtypeStruct((b, SSM_INNER, SSM_STATE), F32),
            jax.ShapeDtypeStruct((b, 8, CONV_DIM), F32),
        ],
        scratch_shapes=[pltpu.VMEM((SSM_INNER, SSM_STATE), F32),
                        pltpu.VMEM((8, CONV_DIM), F32),
                        pltpu.VMEM((2, CONV_DIM // LANES, (8 + SSD_CHUNK) * CONV_PITCH, LANES), F32)],
        compiler_params=_cparams(2),
        name="ssd_prompt",
    )(*([proj] * n_pieces), dtraw, conv_w, conv_b, dt_bias_t, a_log_t, d_skip_x, norm_g, ex2)


def _outproj_kernel(x_ref, a_ref, y_ref, wa_ref, wy_ref, o_ref):
    o_ref[...] = (x_ref[...] + _dot(a_ref[...].astype(BF16), wa_ref[...])
                  + _dot(y_ref[...].astype(BF16), wy_ref[...]))


def _outproj(x, attn, y, w_out, tm):
    t, d = x.shape
    half = w_out.shape[0] // 2
    return pl.pallas_call(
        _outproj_kernel,
        grid=(t // tm,),
        in_specs=[
            pl.BlockSpec((tm, d), lambda i: (i, 0)),
            pl.BlockSpec((tm, half), lambda i: (i, 0)),
            pl.BlockSpec((tm, half), lambda i: (i, 0)),
            pl.BlockSpec((half, d), lambda i: (0, 0)),
            pl.BlockSpec((half, d), lambda i: (1, 0)),
        ],
        out_specs=pl.BlockSpec((tm, d), lambda i: (i, 0)),
        out_shape=jax.ShapeDtypeStruct((t, d), F32),
        compiler_params=_cparams(1),
        name="outproj",
    )(x, attn, y, w_out, w_out)


def _ffn_kernel(h_ref, g_ref, wg_ref, wu_ref, wd_ref, o_ref, f_ref):
    d = o_ref.shape[1]
    tn = min(d, 512)

    def add_delta(f, base_ref):
        gate = _dot(f, wg_ref[...])
        hid = (gate * (1.0 / (1.0 + jnp.exp(-gate))) * _dot(f, wu_ref[...])).astype(BF16)
        for c in range(d // tn):
            cols = slice(c * tn, (c + 1) * tn)
            o_ref[:, cols] = base_ref[:, cols] + _dot(hid, wd_ref[:, cols])

    @pl.when(pl.program_id(1) == 0)
    def _():
        f = (_rms(h_ref[...]) * g_ref[...]).astype(BF16)
        f_ref[...] = f
        add_delta(f, h_ref)

    @pl.when(pl.program_id(1) > 0)
    def _():
        add_delta(f_ref[...], o_ref)


def _ffn(h, g, wg, wu, wd, tm, th):
    t, d = h.shape
    hidden = wd.shape[0]
    return pl.pallas_call(
        _ffn_kernel,
        grid=(t // tm, hidden // th),
        in_specs=[
            pl.BlockSpec((tm, d), lambda i, j: (i, 0)),
            pl.BlockSpec((1, d), lambda i, j: (0, 0)),
            pl.BlockSpec((d, th), lambda i, j: (0, j)),
            pl.BlockSpec((d, th), lambda i, j: (0, j)),
            pl.BlockSpec((th, d), lambda i, j: (j, 0)),
        ],
        out_specs=pl.BlockSpec((tm, d), lambda i, j: (i, 0)),
        out_shape=jax.ShapeDtypeStruct((t, d), F32),
        scratch_shapes=[pltpu.VMEM((tm, d), BF16)],
        compiler_params=_cparams(2),
        name="ffn",
    )(h, g, wg, wu, wd)


def _ple_kernel(h_ref, p_ref, gp_ref, wp_ref, wg_ref, gf_ref, o_ref, *, tn, row_parts):
    tm, d = h_ref.shape
    rp = tm // row_parts
    for r in range(row_parts):
        rows = slice(r * rp, (r + 1) * rp)
        n = (_rms(h_ref[rows, :]) * gp_ref[...]).astype(BF16)
        pb = p_ref[rows, :].astype(BF16)
        ss = jnp.zeros((rp, 1), F32)
        for c in range(d // tn):
            cols = slice(c * tn, (c + 1) * tn)
            gate = _dot(n, wg_ref[:, cols])
            h3 = h_ref[rows, cols] + _dot(pb, wp_ref[:, cols]) * _sigmoid(gate)
            o_ref[rows, cols] = h3
            ss = ss + jnp.sum(h3 * h3, axis=-1, keepdims=True)
        inv = lax.rsqrt(ss * (1.0 / d) + RMS_EPS)
        o_ref[rows, :] = o_ref[rows, :] * inv * gf_ref[...]


def _ple_final(h, p, g_ple, w_ple, w_gate, g_final, tm):
    t, d = h.shape
    pd = p.shape[1]
    const = lambda i: (0, 0)
    resident = dict(pipeline_mode=pl.Buffered(1))
    return pl.pallas_call(
        functools.partial(_ple_kernel, tn=512, row_parts=max(tm // 256, 1)),
        grid=(t // tm,),
        in_specs=[
            pl.BlockSpec((tm, d), lambda i: (i, 0)),
            pl.BlockSpec((tm, pd), lambda i: (i, 0)),
            pl.BlockSpec((1, d), const),
            pl.BlockSpec((pd, d), const, **resident),
            pl.BlockSpec((d, d), const, **resident),
            pl.BlockSpec((1, d), const),
        ],
        out_specs=pl.BlockSpec((tm, d), lambda i: (i, 0)),
        out_shape=jax.ShapeDtypeStruct((t, d), F32),
        compiler_params=_cparams(1),
        name="ple_final",
    )(h, p, g_ple, w_ple, w_gate, g_final)


def _sample_pre_kernel(x_ref, dt_ref, sc_ref, cw_ref, cb_ref, dtb_ref,
                       alog_ref, dsk_ref, exp_ref,
                       xdt_ref, b_ref, c_ref, dec_ref, yp_ref, dech_ref):
    conv = x_ref[...] * cw_ref[CONV_WIDTH - 1:CONV_WIDTH, :] + cb_ref[...]
    for k in range(CONV_WIDTH - 1):
        conv = conv + sc_ref[k] * cw_ref[k:k + 1, :]
    xc = _silu(conv)
    xs = xc[:, :SSM_INNER]
    bm = xc[:, SSM_INNER:SSM_INNER + BC_DIM]
    cm = xc[:, SSM_INNER + BC_DIM:]
    b_ref[...] = bm
    c_ref[...] = cm
    dt = _softplus(dt_ref[...] + dtb_ref[...])
    dec = jnp.exp(dt * (-jnp.exp(alog_ref[...])))
    ex = exp_ref[...]
    dtx = _dot_exact_rhs01(dt, ex)
    dec_ref[...] = _dot_exact_rhs01(dec, ex)
    dech_ref[...] = dec
    xdt = xs * dtx
    xdt_ref[...] = xdt
    gw = SSM_INNER // SSM_GROUPS
    cbs = []
    for g in range(SSM_GROUPS):
        prod = cm[:, g * SSM_STATE:(g + 1) * SSM_STATE] * bm[:, g * SSM_STATE:(g + 1) * SSM_STATE]
        cbs.append(jnp.broadcast_to(jnp.sum(prod, axis=-1, keepdims=True), (prod.shape[0], gw)))
    yp_ref[...] = xdt * jnp.concatenate(cbs, axis=1) + xs * dsk_ref[...]


def _sample_pre(xbc, dtraw, sconv_t, conv_w, conv_b, dt_bias, a_log, d_skip_x, expand):
    nb = xbc.shape[0]
    shapes = [(nb, SSM_INNER), (nb, BC_DIM), (nb, BC_DIM),
              (nb, SSM_INNER), (nb, SSM_INNER), (nb, HEADS_PAD)]
    return pl.pallas_call(
        _sample_pre_kernel,
        out_shape=[jax.ShapeDtypeStruct(s, F32) for s in shapes],
        compiler_params=pltpu.CompilerParams(vmem_limit_bytes=VMEM_LIMIT),
        name="sample_pre",
    )(xbc, dtraw, sconv_t, conv_w, conv_b, dt_bias, a_log, d_skip_x, expand)


def _attn_sample_kernel(sink_ref, q_ref, kn_ref, vn_ref, ck_ref, cv_ref, o_ref, nk_ref, nv_ref, *, bb):
    w = WINDOW
    row = lax.broadcasted_iota(jnp.int32, (w, KV_DIM), 0)
    hrow = lax.broadcasted_iota(jnp.int32, (N_HEADS, KV_DIM), 0) // (N_HEADS // N_KV_HEADS)
    hgrp = lax.broadcasted_iota(jnp.int32, (N_HEADS, KV_DIM), 1) // HEAD_DIM
    own = hrow == hgrp
    sink = sink_ref[...] * LOG2E
    kks, vvs, scores, probs, dens = [], [], [], [], []
    for b in range(bb):
        kk = jnp.where(row == w - 1, kn_ref[b:b + 1, :], pltpu.roll(ck_ref[b], w - 1, 0))
        vv = jnp.where(row == w - 1, vn_ref[b:b + 1, :], pltpu.roll(cv_ref[b], w - 1, 0))
        nk_ref[b] = kk
        nv_ref[b] = vv
        kks.append(kk.astype(BF16))
        vvs.append(vv.astype(BF16))
    for b in range(bb):
        qb = q_ref[b]
        qrow = jnp.where(own, jnp.concatenate([qb] * N_KV_HEADS, axis=1), 0.0)
        scores.append(_dot_nt(qrow.astype(BF16), kks[b]))
    for b in range(bb):
        m = jnp.maximum(jnp.max(scores[b], axis=-1, keepdims=True), sink)
        e = jnp.exp2(scores[b] - m)
        dens.append(jnp.sum(e, axis=-1, keepdims=True) + jnp.exp2(sink - m))
        probs.append(e.astype(BF16))
    outs = [jnp.where(own, _dot(probs[b], vvs[b]), 0.0) for b in range(bb)]
    for b in range(bb):
        o = outs[b][:, :HEAD_DIM]
        for g in range(1, N_KV_HEADS):
            o = o + outs[b][:, g * HEAD_DIM:(g + 1) * HEAD_DIM]
        o_ref[b] = o / dens[b]


def _attn_sample(sinks_col, q3, knew, vnew, cache_k, cache_v, bb):
    nb = q3.shape[0]
    w = WINDOW
    return pl.pallas_call(
        functools.partial(_attn_sample_kernel, bb=bb),
        grid=(nb // bb,),
        in_specs=[
            pl.BlockSpec((N_HEADS, 1), lambda i: (0, 0)),
            pl.BlockSpec((bb, N_HEADS, HEAD_DIM), lambda i: (i, 0, 0)),
            pl.BlockSpec((bb, KV_DIM), lambda i: (i, 0)),
            pl.BlockSpec((bb, KV_DIM), lambda i: (i, 0)),
            pl.BlockSpec((bb, w, KV_DIM), lambda i: (i, 0, 0)),
            pl.BlockSpec((bb, w, KV_DIM), lambda i: (i, 0, 0)),
        ],
        out_specs=[
            pl.BlockSpec((bb, N_HEADS, HEAD_DIM), lambda i: (i, 0, 0)),
            pl.BlockSpec((bb, w, KV_DIM), lambda i: (i, 0, 0)),
            pl.BlockSpec((bb, w, KV_DIM), lambda i: (i, 0, 0)),
        ],
        out_shape=[
            jax.ShapeDtypeStruct((nb, N_HEADS, HEAD_DIM), F32),
            jax.ShapeDtypeStruct((nb, w, KV_DIM), F32),
            jax.ShapeDtypeStruct((nb, w, KV_DIM), F32),
        ],
        compiler_params=_cparams(1),
        name="attn_sample",
    )(sinks_col, q3, knew, vnew, cache_k, cache_v)


def _ssd_sample_kernel(st_ref, xdt_ref, b_ref, c_ref, dec_ref, dech_ref, yp_ref, z_ref, gn_ref,
                       y_ref, ns_ref, *, bb):
    gw = SSM_INNER // SSM_GROUPS
    grow = lax.broadcasted_iota(jnp.int32, (8, SSM_INNER), 0)
    glane = lax.broadcasted_iota(jnp.int32, (8, SSM_INNER), 1) // gw
    own = grow == glane
    pad = jnp.zeros((8 - SSM_GROUPS, SSM_STATE), F32)
    yoffs = []
    for b in range(bb):
        st = st_ref[b]
        cmat = jnp.concatenate([c_ref[b], pad], axis=0).astype(BF16)
        bmat = jnp.concatenate([b_ref[b], pad], axis=0).astype(BF16)
        r = _dot_nt(cmat, st.astype(BF16))
        yoffs.append(jnp.sum(jnp.where(own, r, 0.0), axis=0, keepdims=True))
        amat = jnp.where(own, jnp.broadcast_to(xdt_ref[b:b + 1, :], (8, SSM_INNER)), 0.0)
        outer = _dot_tn(amat.astype(BF16), bmat)
        for h in range(SSM_HEADS):
            rows = slice(h * SSM_HEAD_DIM, (h + 1) * SSM_HEAD_DIM)
            ns_ref[b, rows, :] = st[rows, :] * dech_ref[b:b + 1, h:h + 1] + outer[rows, :]
    y = yp_ref[...] + jnp.concatenate(yoffs, axis=0) * dec_ref[...]
    hg = y * _silu(z_ref[...])
    outs = []
    for g in range(SSM_GROUPS):
        outs.append(_rms(hg[:, g * gw:(g + 1) * gw]))
    y_ref[...] = jnp.concatenate(outs, axis=1) * gn_ref[...]


def _ssd_sample(state, xdt, b3, c3, decx, dech, ypart, z, norm_g, bb):
    nb = state.shape[0]
    row = lambda i: (i, 0)
    return pl.pallas_call(
        functools.partial(_ssd_sample_kernel, bb=bb),
        grid=(nb // bb,),
        in_specs=[
            pl.BlockSpec((bb, SSM_INNER, SSM_STATE), lambda i: (i, 0, 0)),
            pl.BlockSpec((bb, SSM_INNER), row),
            pl.BlockSpec((bb, SSM_GROUPS, SSM_STATE), lambda i: (i, 0, 0)),
            pl.BlockSpec((bb, SSM_GROUPS, SSM_STATE), lambda i: (i, 0, 0)),
            pl.BlockSpec((bb, SSM_INNER), row),
            pl.BlockSpec((bb, HEADS_PAD), row),
            pl.BlockSpec((bb, SSM_INNER), row),
            pl.BlockSpec((bb, SSM_INNER), row),
            pl.BlockSpec((1, SSM_INNER), lambda i: (0, 0)),
        ],
        out_specs=[
            pl.BlockSpec((bb, SSM_INNER), row),
            pl.BlockSpec((bb, SSM_INNER, SSM_STATE), lambda i: (i, 0, 0)),
        ],
        out_shape=[
            jax.ShapeDtypeStruct((nb, SSM_INNER), F32),
            jax.ShapeDtypeStruct((nb, SSM_INNER, SSM_STATE), F32),
        ],
        compiler_params=_cparams(1),
        name="ssd_sample",
    )(state, xdt, b3, c3, decx, dech, ypart, z, norm_g)


def _row_tile(t, want):
    return want if t % want == 0 else t


def _dense_tail(x, attn, y, p, wts, tm):
    h1 = _outproj(x, attn, y, wts["w_out"], tm)
    h2 = _ffn(h1, wts["g_ffn"], wts["w_gate"], wts["w_up"], wts["w_down"],
              _row_tile(x.shape[0], FFN_TM), FFN_TH)
    return _ple_final(h2, p, wts["g_ple"], wts["w_ple"], wts["w_ple_gate"], wts["g_final"],
                      _row_tile(x.shape[0], 1024))


def kernel(x_prompt, x_sample, cache_k, cache_v, state_ssm, state_conv, p_prompt, p_sample, w_in, conv_w, conv_b, dt_bias, a_log, d_skip, ssm_norm_g, attn_sinks, w_out, g_mix, g_ffn, w_ffn_gate, w_ffn_up, w_ffn_down, g_ple, w_ple, w_ple_gate, g_final):
    depth = w_in.shape[0]
    assert depth == 1, "single-layer step only"
    bsz, seq, d = x_prompt.shape
    nb, dseq, _ = x_sample.shape
    assert dseq == 1 and seq % SSD_CHUNK == 0 and seq % WINDOW == 0

    w = w_in[0]
    w_main = w.astype(BF16)
    w_dt = jnp.pad(w[:, PROJ_DIM:], ((0, 0), (0, HEADS_PAD - SSM_HEADS))).astype(BF16)
    padh = lambda v: jnp.pad(v, (0, HEADS_PAD - SSM_HEADS)).reshape(1, HEADS_PAD)
    tileh = lambda v: jnp.tile(v, HEADS_PAD // SSM_HEADS).reshape(1, HEADS_PAD)
    dtb, alog = padh(dt_bias[0]), padh(a_log[0])
    dsk_x = jnp.repeat(d_skip[0], SSM_HEAD_DIM).reshape(1, SSM_INNER)
    gn = ssm_norm_g[0].reshape(1, SSM_INNER)
    cw, cb = conv_w[0], conv_b[0].reshape(1, CONV_DIM)
    wts = dict(g_ffn=g_ffn[0].reshape(1, d), g_ple=g_ple[0].reshape(1, d),
               g_final=g_final.reshape(1, d))
    gmix = g_mix[0].reshape(1, d)
    sinks = attn_sinks[0]

    tp = bsz * seq
    xp = x_prompt.reshape(tp, d)
    tm_in = _row_tile(seq, 1024)
    proj, dtraw = _inproj(xp, gmix, w_main, w_dt, _rope_tables(jnp.arange(seq)), tm_in)
    proj3 = proj.reshape(bsz, seq, PROJ_DIM)
    (attn, nk_p, nv_p, wts["w_gate"], wts["w_up"], wts["w_down"], wts["w_out"], wts["w_ple_gate"],
     wts["w_ple"]) = _attn_prompt(
        proj3, sinks, ATTN_BLOCKS,
        (w_ffn_gate[0], w_ffn_up[0], w_ffn_down[0], w_out[0], w_ple_gate[0], w_ple[0]))
    yp, nssm_p, nconv_p = _ssd_prompt(
        proj3, dtraw.reshape(bsz, seq, HEADS_PAD), cw, cb, tileh(dt_bias[0]), tileh(a_log[0]),
        dsk_x, gn, SSD_CHUNKS)
    tm = _row_tile(tp, 512)
    y_prompt = _dense_tail(xp, attn.reshape(tp, Q_DIM), yp.reshape(tp, SSM_INNER),
                           p_prompt[0].reshape(tp, -1), wts, tm).reshape(bsz, seq, d)

    xs = x_sample.reshape(nb, d)
    tab_s = _rope_tables(jnp.full((nb,), PAST_LEN, jnp.int32))
    proj_s, dtraw_s = _inproj(xs, gmix, w_main, w_dt, tab_s, nb)
    window = cache_k.shape[2]
    sconv_t = jnp.transpose(state_conv[0], (1, 0, 2))
    expand = (jnp.arange(HEADS_PAD)[:, None] == (jnp.arange(SSM_INNER) // SSM_HEAD_DIM)[None, :]
              ).astype(BF16)
    q_s = proj_s[:, COL_Q:COL_Q + Q_DIM]
    k_s = proj_s[:, COL_K:COL_K + KV_DIM]
    v_s = proj_s[:, COL_V:COL_V + KV_DIM]
    z_s = proj_s[:, COL_Z:COL_Z + SSM_INNER]
    xbc_s = proj_s[:, COL_XBC:COL_XBC + CONV_DIM]
    xdt, bm, cm, decx, ypart, dech = _sample_pre(
        xbc_s, dtraw_s, sconv_t, cw, cb, dtb, alog, dsk_x, expand)
    bb = DECODE_BB if nb % DECODE_BB == 0 else nb
    attn_s, nk_s, nv_s = _attn_sample(
        sinks.reshape(N_HEADS, 1), q_s.reshape(nb, N_HEADS, HEAD_DIM), k_s, v_s,
        cache_k[0].reshape(nb, window, KV_DIM), cache_v[0].reshape(nb, window, KV_DIM), bb)
    ys, nssm_s = _ssd_sample(
        state_ssm[0].reshape(nb, SSM_INNER, SSM_STATE), xdt,
        bm.reshape(nb, SSM_GROUPS, SSM_STATE), cm.reshape(nb, SSM_GROUPS, SSM_STATE),
        decx, dech, ypart, z_s, gn, bb)
    y_sample = _dense_tail(xs, attn_s.reshape(nb, Q_DIM), ys, p_sample[0].reshape(nb, -1), wts,
                           nb).reshape(nb, 1, d)
    nconv_s = jnp.concatenate([state_conv[0][:, 1:], xbc_s[:, None, :]], axis=1)

    kv5 = lambda t: t.reshape(1, t.shape[0], window, N_KV_HEADS, HEAD_DIM)
    return (y_prompt, y_sample,
            kv5(nk_p), kv5(nv_p),
            nssm_p.reshape(1, bsz, SSM_HEADS, SSM_HEAD_DIM, SSM_STATE),
            nconv_p[None, :, 8 - (CONV_WIDTH - 1):, :],
            kv5(nk_s), kv5(nv_s),
            nssm_s.reshape(1, nb, SSM_HEADS, SSM_HEAD_DIM, SSM_STATE),
            nconv_s[None])
```
